```python
import math
import jax, jax.numpy as jnp
from jax import lax
import numpy as np


D_MODEL = 1024
BATCH = 2
SEQ = 8192
DEPTH = 4
DEC_BATCH = 128
DEC_SEQ = 8
PAST_LEN = 8192
PAGE_SIZE = 128

N_A_LAYERS = DEPTH // 2
N_B_LAYERS = DEPTH - N_A_LAYERS
FFN_DIM = 2816
CONV_DIM = D_MODEL
CONV_WIDTH = 3
N_HEADS = 16
N_KV_HEADS = 4
HEAD_DIM = 64
GROUP = N_HEADS // N_KV_HEADS
ATTN_DIM = N_HEADS * HEAD_DIM
WINDOW = 128
REL_BUCKETS = 32
REL_MAX_DIST = 128
MEM_TOKENS = 256
MEM_HEADS = 4
MEM_HEAD_DIM = 128
MEM_DIM = MEM_HEADS * MEM_HEAD_DIM
RMS_EPS = 1e-5

kernel_name = 'yoco_shortconv_swa_sink_macaron_memory_step'


def rms_norm(x, g):
    xf = x.astype(jnp.float32)
    y = xf * lax.rsqrt(jnp.mean(xf * xf, axis=-1, keepdims=True) + RMS_EPS)
    return (y * g.astype(jnp.float32)).astype(x.dtype)


def swiglu(h, wg, wu, wd):
    return (jax.nn.silu(h @ wg) * (h @ wu)) @ wd


def t5_bucket(dist):
    n = jnp.maximum(dist, 0)
    exact = REL_BUCKETS // 2
    nf = jnp.maximum(n, 1).astype(jnp.float32)
    large = exact + (jnp.log(nf / exact) / math.log(REL_MAX_DIST / exact)
                     * (REL_BUCKETS - exact)).astype(jnp.int32)
    large = jnp.minimum(large, REL_BUCKETS - 1)
    return jnp.where(n < exact, n, large)


def short_conv(u, prefix, w):
    t = u.shape[1]
    up = jnp.concatenate([prefix.astype(u.dtype), u], axis=1)
    out = sum(w[i].astype(u.dtype) * up[:, i:i + t] for i in range(CONV_WIDTH))
    return out, up[:, -(CONV_WIDTH - 1):]


def memory_kv(mem, g, w):
    b, m, _ = mem.shape
    k, v = jnp.split(rms_norm(mem, g) @ w, 2, axis=-1)
    return (k.reshape(b, m, MEM_HEADS, MEM_HEAD_DIM), v.reshape(b, m, MEM_HEADS, MEM_HEAD_DIM))


def memory_attention(q, mk, mv):
    b, t = q.shape[:2]
    s = jnp.einsum('bthd,bmhd->bhtm', q, mk).astype(jnp.float32) * (MEM_HEAD_DIM ** -0.5)
    p = jax.nn.softmax(s, axis=-1).astype(mv.dtype)
    return jnp.einsum('bhtm,bmhd->bthd', p, mv).reshape(b, t, MEM_DIM)


def sink_window_attention(q, k, v, dist, valid, rel_bias, sinks):
    tq, tk = dist.shape
    s = jnp.einsum('bnqhgd,bnkhd->bnhgqk', q, k).astype(jnp.float32) * (HEAD_DIM ** -0.5)
    bias = rel_bias.astype(jnp.float32)[t5_bucket(dist)]
    bias = bias.reshape(tq, tk, N_KV_HEADS, GROUP).transpose(2, 3, 0, 1)
    mask = valid[None, :, None, None] & ((dist >= 0) & (dist < WINDOW))
    s = jnp.where(mask, s + bias, -jnp.inf)
    sink = sinks.astype(jnp.float32).reshape(N_KV_HEADS, GROUP)[:, :, None, None]
    m = jnp.maximum(jnp.max(s, axis=-1, keepdims=True), sink)
    p = jnp.exp(s - m)
    p = p / (jnp.sum(p, axis=-1, keepdims=True) + jnp.exp(sink - m))
    return jnp.einsum('bnhgqk,bnkhd->bnqhgd', p.astype(v.dtype), v)


def trunk(x, conv_prefix, swa_past_k, swa_past_v, mem_k, mem_v, W):
    b, t, _ = x.shape
    conv_states = []
    for l in range(DEPTH):
        if l == N_A_LAYERS:
            k_new, v_new = jnp.split(rms_norm(x, W['kv_norm']) @ W['w_kv'], 2, axis=-1)
            k_new = k_new.reshape(b, t, N_KV_HEADS, HEAD_DIM)
            v_new = v_new.reshape(b, t, N_KV_HEADS, HEAD_DIM)
            if swa_past_k is None:
                nb = t // WINDOW
                q_block = WINDOW
                kb = k_new.reshape(b, nb, WINDOW, N_KV_HEADS, HEAD_DIM)
                vb = v_new.reshape(b, nb, WINDOW, N_KV_HEADS, HEAD_DIM)
                kprev = jnp.concatenate([jnp.zeros_like(kb[:, :1]), kb[:, :-1]], axis=1)
                vprev = jnp.concatenate([jnp.zeros_like(vb[:, :1]), vb[:, :-1]], axis=1)
                keys = jnp.concatenate([kprev, kb], axis=2)
                vals = jnp.concatenate([vprev, vb], axis=2)
                valid = (jnp.arange(nb)[:, None, None] > 0) | (jnp.arange(2 * WINDOW)[None, None, :] >= WINDOW)
                swa_k_state = k_new[:, -WINDOW:]
                swa_v_state = v_new[:, -WINDOW:]
            else:
                q_block = t
                keys_flat = jnp.concatenate([swa_past_k.astype(k_new.dtype), k_new], axis=1)
                vals_flat = jnp.concatenate([swa_past_v.astype(v_new.dtype), v_new], axis=1)
                keys = keys_flat[:, None]
                vals = vals_flat[:, None]
                valid = jnp.ones((1, 1, WINDOW + t), dtype=bool)
                swa_k_state = keys_flat[:, -WINDOW:]
                swa_v_state = vals_flat[:, -WINDOW:]
            dist = WINDOW + jnp.arange(q_block)[:, None] - jnp.arange(keys.shape[2])[None, :]
        x = x + 0.5 * swiglu(rms_norm(x, W['ffn1_norm'][l]), W['ffn1_wg'][l], W['ffn1_wu'][l], W['ffn1_wd'][l])
        h = rms_norm(x, W['mix_norm'][l])
        if l < N_A_LAYERS:
            z = h @ W['w_in_a'][l]
            b_g, c_g, xin, qm = jnp.split(z, [CONV_DIM, 2 * CONV_DIM, 3 * CONV_DIM], axis=-1)
            conv_out, st = short_conv(c_g * xin, conv_prefix[l], W['conv_w'][l])
            conv_states.append(st)
            y_tok = b_g * conv_out
            w_out = W['w_out_a'][l]
        else:
            j = l - N_A_LAYERS
            z = h @ W['w_in_b'][j]
            qa, qm = jnp.split(z, [ATTN_DIM], axis=-1)
            qa = qa.reshape(b, t // q_block, q_block, N_KV_HEADS, GROUP, HEAD_DIM)
            o = sink_window_attention(qa, keys, vals, dist, valid, W['rel_bias'], W['attn_sinks'][j])
            y_tok = o.reshape(b, t, ATTN_DIM)
            w_out = W['w_out_b'][j]
        y_mem = memory_attention(qm.reshape(b, t, MEM_HEADS, MEM_HEAD_DIM), mem_k[l], mem_v[l])
        x = x + jnp.concatenate([y_tok, y_mem], axis=-1) @ w_out
        x = x + 0.5 * swiglu(rms_norm(x, W['ffn2_norm'][l]), W['ffn2_wg'][l], W['ffn2_wu'][l], W['ffn2_wd'][l])
    return rms_norm(x, W['final_norm']), jnp.stack(conv_states), swa_k_state, swa_v_state


def setup_inputs(seed: int = 0) -> dict:
    key = jax.random.key(seed)
    ks = iter(jax.random.split(key, 40))
    f32 = jnp.float32
    D = D_MODEL

    def nrm(shape, scale):
        return jax.random.normal(next(ks), shape, f32) * scale

    def gain(shape):
        return 1.0 + 0.05 * jax.random.normal(next(ks), shape, f32)

    return {
        'x_prompt': nrm((BATCH, SEQ, D), 1.0),
        'x_sample': nrm((DEC_BATCH, DEC_SEQ, D), 1.0),
        'state_conv': nrm((N_A_LAYERS, DEC_BATCH, CONV_WIDTH - 1, CONV_DIM), 1.0),
        'cache_swa_k': nrm((DEC_BATCH, WINDOW, N_KV_HEADS, HEAD_DIM), 1.0),
        'cache_swa_v': nrm((DEC_BATCH, WINDOW, N_KV_HEADS, HEAD_DIM), 1.0),
        'cache_mem_k': nrm((DEPTH, DEC_BATCH, MEM_TOKENS, MEM_HEADS, MEM_HEAD_DIM), 1.0),
        'cache_mem_v': nrm((DEPTH, DEC_BATCH, MEM_TOKENS, MEM_HEADS, MEM_HEAD_DIM), 1.0),
        'mem_prompt': nrm((BATCH, MEM_TOKENS, D), 1.0),
        'ffn1_norm': gain((DEPTH, D)),
        'ffn1_wg': nrm((DEPTH, D, FFN_DIM), D ** -0.5),
        'ffn1_wu': nrm((DEPTH, D, FFN_DIM), D ** -0.5),
        'ffn1_wd': nrm((DEPTH, FFN_DIM, D), FFN_DIM ** -0.5),
        'mix_norm': gain((DEPTH, D)),
        'w_in_a': nrm((N_A_LAYERS, D, 3 * CONV_DIM + MEM_DIM), D ** -0.5),
        'conv_w': nrm((N_A_LAYERS, CONV_WIDTH, CONV_DIM), CONV_WIDTH ** -0.5),
        'w_out_a': nrm((N_A_LAYERS, CONV_DIM + MEM_DIM, D), (CONV_DIM + MEM_DIM) ** -0.5),
        'kv_norm': gain((D,)),
        'w_kv': nrm((D, 2 * N_KV_HEADS * HEAD_DIM), D ** -0.5),
        'w_in_b': nrm((N_B_LAYERS, D, ATTN_DIM + MEM_DIM), D ** -0.5),
        'attn_sinks': nrm((N_B_LAYERS, N_HEADS), 1.0),
        'rel_bias': nrm((REL_BUCKETS, N_HEADS), 0.5),
        'w_out_b': nrm((N_B_LAYERS, ATTN_DIM + MEM_DIM, D), (ATTN_DIM + MEM_DIM) ** -0.5),
        'mem_norm': gain((DEPTH, D)),
        'w_mem_kv': nrm((DEPTH, D, 2 * MEM_DIM), D ** -0.5),
        'ffn2_norm': gain((DEPTH, D)),
        'ffn2_wg': nrm((DEPTH, D, FFN_DIM), D ** -0.5),
        'ffn2_wu': nrm((DEPTH, D, FFN_DIM), D ** -0.5),
        'ffn2_wd': nrm((DEPTH, FFN_DIM, D), FFN_DIM ** -0.5),
        'final_norm': gain((D,)),
    }


def reference(x_prompt, x_sample, state_conv, cache_swa_k, cache_swa_v, cache_mem_k, cache_mem_v,
              mem_prompt, ffn1_norm, ffn1_wg, ffn1_wu, ffn1_wd, mix_norm, w_in_a, conv_w, w_out_a,
              kv_norm, w_kv, w_in_b, attn_sinks, rel_bias, w_out_b, mem_norm, w_mem_kv,
              ffn2_norm, ffn2_wg, ffn2_wu, ffn2_wd, final_norm):
    W = dict(ffn1_norm=ffn1_norm, ffn1_wg=ffn1_wg, ffn1_wu=ffn1_wu, ffn1_wd=ffn1_wd,
             mix_norm=mix_norm, w_in_a=w_in_a, conv_w=conv_w, w_out_a=w_out_a,
             kv_norm=kv_norm, w_kv=w_kv, w_in_b=w_in_b, attn_sinks=attn_sinks,
             rel_bias=rel_bias, w_out_b=w_out_b,
             ffn2_norm=ffn2_norm, ffn2_wg=ffn2_wg, ffn2_wu=ffn2_wu, ffn2_wd=ffn2_wd,
             final_norm=final_norm)
    mk_list, mv_list = [], []
    for l in range(DEPTH):
        mk, mv = memory_kv(mem_prompt, mem_norm[l], w_mem_kv[l])
        mk_list.append(mk)
        mv_list.append(mv)
    mem_k_prompt = jnp.stack(mk_list)
    mem_v_prompt = jnp.stack(mv_list)
    conv_zero = jnp.zeros((N_A_LAYERS, x_prompt.shape[0], CONV_WIDTH - 1, CONV_DIM), x_prompt.dtype)
    y_prompt, conv_state_prompt, swa_k_prompt, swa_v_prompt = trunk(
        x_prompt, conv_zero, None, None, mem_k_prompt, mem_v_prompt, W)
    y_sample, conv_state_sample, swa_k_sample, swa_v_sample = trunk(
        x_sample, state_conv, cache_swa_k, cache_swa_v, cache_mem_k, cache_mem_v, W)
    return (y_prompt, y_sample, conv_state_prompt, conv_state_sample,
            swa_k_prompt, swa_v_prompt, swa_k_sample, swa_v_sample,
            mem_k_prompt, mem_v_prompt)
```

```python
import functools
import math

import numpy as np
import jax
import jax.numpy as jnp
from jax import lax
from jax.experimental import pallas as pl
from jax.experimental.pallas import tpu as pltpu

D_MODEL = 1024
BATCH = 2
SEQ = 8192
DEPTH = 4
DEC_BATCH = 128
DEC_SEQ = 8
N_A_LAYERS = DEPTH // 2
FFN_DIM = 2816
CONV_DIM = D_MODEL
CONV_WIDTH = 3
N_HEADS = 16
N_KV_HEADS = 4
HEAD_DIM = 64
GROUP = N_HEADS // N_KV_HEADS
ATTN_DIM = N_HEADS * HEAD_DIM
WINDOW = 128
REL_BUCKETS = 32
REL_MAX_DIST = 128
MEM_TOKENS = 256
MEM_HEADS = 4
MEM_HEAD_DIM = 128
MEM_DIM = MEM_HEADS * MEM_HEAD_DIM
RMS_EPS = 1e-5

F32 = jnp.float32
BF16 = jnp.bfloat16
NEG_INF = float("-inf")

V7X_LANES = 128
V7X_SUBLANES = 8
V7X_MXU_DIM = 256
V7X_VMEM_BYTES = 64 * 1024 * 1024

ROWS_PROMPT = BATCH * SEQ
ROWS_SAMPLE = DEC_BATCH * DEC_SEQ
ROWS = ROWS_PROMPT + ROWS_SAMPLE
TILE_ROWS = 512
PROMPT_TILES = ROWS_PROMPT // TILE_ROWS
SAMPLE_TILES = ROWS_SAMPLE // TILE_ROWS
ROW_TILES = PROMPT_TILES + SAMPLE_TILES
TILES_PER_SEQ = SEQ // TILE_ROWS
FFN_CHUNK = V7X_MXU_DIM
CONV_CHUNK = V7X_MXU_DIM
BLOCKS_PER_TILE = TILE_ROWS // WINDOW
SAMPLE_BATCH_BLOCK = 8
SAMPLE_BLOCK_ROWS = SAMPLE_BATCH_BLOCK * DEC_SEQ
SAMPLE_KEYS = WINDOW + DEC_SEQ
SAMPLE_KEYS_PADDED = 2 * WINDOW
HALF_LANES = V7X_LANES // 2
MEM_SCALE = MEM_HEAD_DIM ** -0.5
ATTN_SCALE = HEAD_DIM ** -0.5

assert HEAD_DIM == HALF_LANES and MEM_HEAD_DIM == V7X_LANES
assert ROWS_PROMPT % TILE_ROWS == 0 and ROWS_SAMPLE % TILE_ROWS == 0 and SEQ % TILE_ROWS == 0
assert FFN_DIM % FFN_CHUNK == 0 and TILE_ROWS % WINDOW == 0


def _vmem_limit(resident_bytes, streamed_bytes, scratch_bytes, temp_bytes):
    need = resident_bytes + 2 * streamed_bytes + scratch_bytes + temp_bytes
    assert need < V7X_VMEM_BYTES, need
    return int(need)


def _params(vmem_bytes, n_axes=1):
    return pltpu.CompilerParams(
        dimension_semantics=("arbitrary",) * n_axes, vmem_limit_bytes=vmem_bytes)


def _resident(shape):
    zeros = (0,) * len(shape)
    return pl.BlockSpec(shape, lambda *_: zeros, pipeline_mode=pl.Buffered(1))


def _dot(a, b):
    return jnp.dot(a, b, preferred_element_type=F32)


def _dot_nt(a, b):
    return lax.dot_general(a, b, (((1,), (1,)), ((), ())), preferred_element_type=F32)


def _rms(x, g):
    return x * lax.rsqrt(jnp.mean(x * x, axis=-1, keepdims=True) + RMS_EPS) * g


def _ffn_half_step(x, g_ref, wg_ref, wu_ref, wd_ref, act_ref):
    h = _rms(x, g_ref[...]).astype(BF16)
    for c in range(FFN_DIM // FFN_CHUNK):
        sl = slice(c * FFN_CHUNK, (c + 1) * FFN_CHUNK)
        gate = _dot(h, wg_ref[:, sl])
        up = _dot(h, wu_ref[:, sl])
        act_ref[:, sl] = (gate / (1.0 + jnp.exp(-gate)) * up).astype(BF16)
    return x + 0.5 * _dot(act_ref[...], wd_ref[...])


def _t5_bucket_np(dist):
    n = np.maximum(dist, 0)
    exact = REL_BUCKETS // 2
    nf = np.maximum(n, 1).astype(np.float32)
    large = exact + (np.log(nf / np.float32(exact)) / np.float32(math.log(REL_MAX_DIST / exact))
                     * np.float32(REL_BUCKETS - exact)).astype(np.int32)
    large = np.minimum(large, REL_BUCKETS - 1)
    return np.where(n < exact, n, large).astype(np.int32)


def _bucket_tables():
    q = np.arange(WINDOW)[:, None]
    k = np.arange(2 * WINDOW)[None, :]
    dist = WINDOW + q - k
    prompt = np.where((dist >= 0) & (dist < WINDOW), _t5_bucket_np(dist), -1)
    t = (np.arange(N_HEADS * DEC_SEQ) % DEC_SEQ)[:, None]
    k = np.arange(SAMPLE_KEYS_PADDED)[None, :]
    dist = WINDOW + t - k
    ok = (dist >= 0) & (dist < WINDOW) & (k < SAMPLE_KEYS)
    sample = np.where(ok, _t5_bucket_np(dist), -1)
    return prompt.astype(np.int32), sample.astype(np.int32)


def _bias_kernel(rel_ref, bp_ref, bs_ref, op_ref, os_ref):
    j = pl.program_id(0)

    def build(bucket):
        acc = jnp.zeros(bucket.shape, F32)
        for b in range(REL_BUCKETS):
            acc = jnp.where(bucket == b, rel_ref[b, j], acc)
        return jnp.where(bucket < 0, NEG_INF, acc)

    op_ref[0] = build(bp_ref[...])
    os_ref[...] = build(bs_ref[...])


def _bias_tables(rel_bias):
    bp, bs = _bucket_tables()
    return pl.pallas_call(
        _bias_kernel,
        out_shape=(jax.ShapeDtypeStruct((N_HEADS, WINDOW, 2 * WINDOW), F32),
                   jax.ShapeDtypeStruct((N_HEADS * DEC_SEQ, SAMPLE_KEYS_PADDED), F32)),
        grid=(N_HEADS,),
        in_specs=[pl.BlockSpec(memory_space=pltpu.SMEM),
                  pl.BlockSpec((WINDOW, 2 * WINDOW), lambda j: (0, 0)),
                  pl.BlockSpec((DEC_SEQ, SAMPLE_KEYS_PADDED), lambda j: (j, 0))],
        out_specs=(pl.BlockSpec((1, WINDOW, 2 * WINDOW), lambda j: (j, 0, 0)),
                   pl.BlockSpec((DEC_SEQ, SAMPLE_KEYS_PADDED), lambda j: (j, 0))),
        name="bias_tables",
    )(rel_bias, jnp.asarray(bp), jnp.asarray(bs))


def _memkv_kernel(m_ref, g_ref, w_ref, k_ref, v_ref, kb_ref, ve_ref):
    hn = _rms(m_ref[...], g_ref[...]).astype(BF16)
    kv = _dot(hn, w_ref[...])
    k = kv[:, :MEM_DIM]
    v = kv[:, MEM_DIM:]
    k_ref[...] = k
    v_ref[...] = v
    kb_ref[...] = k.astype(BF16)
    ones = jnp.ones((MEM_TOKENS, MEM_HEAD_DIM), BF16)
    for h in range(MEM_HEADS):
        vh = v[:, h * MEM_HEAD_DIM:(h + 1) * MEM_HEAD_DIM].astype(BF16)
        ve_ref[h] = jnp.concatenate([vh, ones], axis=1)


def _memkv(mem_prompt, mem_norm, w_mem_kv_bf):
    shp = (DEPTH, BATCH, MEM_TOKENS, MEM_DIM)
    blk = pl.BlockSpec((None, None, MEM_TOKENS, MEM_DIM), lambda l, b: (l, b, 0, 0))
    return pl.pallas_call(
        _memkv_kernel,
        out_shape=(jax.ShapeDtypeStruct(shp, F32), jax.ShapeDtypeStruct(shp, F32),
                   jax.ShapeDtypeStruct(shp, BF16),
                   jax.ShapeDtypeStruct((DEPTH, BATCH, MEM_HEADS, MEM_TOKENS, 2 * MEM_HEAD_DIM), BF16)),
        grid=(DEPTH, BATCH),
        in_specs=[pl.BlockSpec((None, MEM_TOKENS, D_MODEL), lambda l, b: (b, 0, 0)),
                  pl.BlockSpec((None, 1, D_MODEL), lambda l, b: (l, 0, 0)),
                  pl.BlockSpec((None, D_MODEL, 2 * MEM_DIM), lambda l, b: (l, 0, 0))],
        out_specs=(blk, blk, blk,
                   pl.BlockSpec((None, None, MEM_HEADS, MEM_TOKENS, 2 * MEM_HEAD_DIM),
                                lambda l, b: (l, b, 0, 0, 0))),
        compiler_params=_params(32 * 1024 * 1024, 2),
        name="memkv",
    )(mem_prompt, mem_norm.reshape(DEPTH, 1, D_MODEL), w_mem_kv_bf)


def _ffn_kernel(*refs, attn_proj):
    if attn_proj:
        x_ref, g_ref, wg_ref, wu_ref, wd_ref, gm_ref, win_ref, x1_ref, qa_ref, qm_ref, act_ref = refs
    else:
        x_ref, g_ref, wg_ref, wu_ref, wd_ref, x1_ref, act_ref = refs
    x1 = _ffn_half_step(x_ref[...], g_ref, wg_ref, wu_ref, wd_ref, act_ref)
    x1_ref[...] = x1
    if attn_proj:
        hm = _rms(x1, gm_ref[...]).astype(BF16)
        qa_ref[...] = (_dot(hm, win_ref[:, :ATTN_DIM]) * ATTN_SCALE).astype(BF16)
        qm_ref[...] = _dot(hm, win_ref[:, ATTN_DIM:]).astype(BF16)


_FFN_WEIGHT_BYTES = 3 * D_MODEL * FFN_DIM * 2
_ROW_TILE_F32 = TILE_ROWS * D_MODEL * 4
_ACT_BYTES = TILE_ROWS * FFN_DIM * 2


def _row_spec(width):
    return pl.BlockSpec((TILE_ROWS, width), lambda i: (i, 0))


def _ffn(x, g, wg, wu, wd, gm=None, win=None):
    attn_proj = win is not None
    in_specs = [_row_spec(D_MODEL), _resident((1, D_MODEL)), _resident((D_MODEL, FFN_DIM)),
                _resident((D_MODEL, FFN_DIM)), _resident((FFN_DIM, D_MODEL))]
    args = [x, g.reshape(1, D_MODEL), wg, wu, wd]
    out_shape = [jax.ShapeDtypeStruct((ROWS, D_MODEL), F32)]
    out_specs = [_row_spec(D_MODEL)]
    resident = _FFN_WEIGHT_BYTES
    streamed = 2 * _ROW_TILE_F32
    if attn_proj:
        in_specs += [_resident((1, D_MODEL)), _resident((D_MODEL, ATTN_DIM + MEM_DIM))]
        args += [gm.reshape(1, D_MODEL), win]
        out_shape += [jax.ShapeDtypeStruct((ROWS, ATTN_DIM), BF16), jax.ShapeDtypeStruct((ROWS, MEM_DIM), BF16)]
        out_specs += [_row_spec(ATTN_DIM), _row_spec(MEM_DIM)]
        resident += D_MODEL * (ATTN_DIM + MEM_DIM) * 2
        streamed += TILE_ROWS * (ATTN_DIM + MEM_DIM) * 2
    return pl.pallas_call(
        functools.partial(_ffn_kernel, attn_proj=attn_proj),
        out_shape=tuple(out_shape),
        grid=(ROW_TILES,),
        in_specs=in_specs,
        out_specs=tuple(out_specs),
        scratch_shapes=[pltpu.VMEM((TILE_ROWS, FFN_DIM), BF16)],
        compiler_params=_params(_vmem_limit(resident, streamed, _ACT_BYTES, 6 * _ROW_TILE_F32)),
        name="ffn_attn_proj" if attn_proj else "ffn",
    )(*args)


def _inproj_conv_kernel(x_ref, gm_ref, win_ref, cw_ref, pre_ref,
                        ytok_ref, qm_ref, tail_ref, us_ref, shift_ref):
    i = pl.program_id(0)
    hm = _rms(x_ref[...], gm_ref[...]).astype(BF16)
    qm_ref[...] = _dot(hm, win_ref[:, 3 * CONV_DIM:]).astype(BF16)

    def chunk(cc, prompt):
        sl = slice(cc * CONV_CHUNK, (cc + 1) * CONV_CHUNK)
        c_gate = _dot(hm, win_ref[:, CONV_DIM + cc * CONV_CHUNK:CONV_DIM + (cc + 1) * CONV_CHUNK])
        x_in = _dot(hm, win_ref[:, 2 * CONV_DIM + cc * CONV_CHUNK:2 * CONV_DIM + (cc + 1) * CONV_CHUNK])
        u = c_gate * x_in
        if prompt:
            shift_ref[V7X_SUBLANES:, sl] = u
            u1 = shift_ref[V7X_SUBLANES - 1:V7X_SUBLANES - 1 + TILE_ROWS, sl]
            u2 = shift_ref[V7X_SUBLANES - 2:V7X_SUBLANES - 2 + TILE_ROWS, sl]
            last = u[TILE_ROWS - V7X_SUBLANES:, :]
            shift_ref[:V7X_SUBLANES, sl] = last
            tail_ref[0, :, sl] = last
        else:
            t = lax.broadcasted_iota(jnp.int32, (TILE_ROWS, CONV_CHUNK), 0) % DEC_SEQ
            p2 = pre_ref[:, sl]
            p1 = pltpu.roll(p2, TILE_ROWS - 1, axis=0)
            u1 = jnp.where(t == 0, p1, pltpu.roll(u, 1, axis=0))
            u2 = jnp.where(t < 2, p2, pltpu.roll(u, 2, axis=0))
            us_ref[:, sl] = u
            tail_ref[0, :, sl] = jnp.zeros((V7X_SUBLANES, CONV_CHUNK), F32)
        w = cw_ref[:, sl]
        conv = w[0:1] * u2 + w[1:2] * u1 + w[2:3] * u
        b_gate = _dot(hm, win_ref[:, sl])
        ytok_ref[:, sl] = (b_gate * conv).astype(BF16)

    @pl.when(i < PROMPT_TILES)
    def _():
        @pl.when(i % TILES_PER_SEQ == 0)
        def _():
            shift_ref[:V7X_SUBLANES, :] = jnp.zeros((V7X_SUBLANES, CONV_DIM), F32)
        for cc in range(CONV_DIM // CONV_CHUNK):
            chunk(cc, True)

    @pl.when(i >= PROMPT_TILES)
    def _():
        for cc in range(CONV_DIM // CONV_CHUNK):
            chunk(cc, False)


def _inproj_conv(x1, gm, win, conv_w, prefix_rows):
    sample_idx = lambda i: (jnp.maximum(i - PROMPT_TILES, 0), 0)
    win_bytes = D_MODEL * (3 * CONV_DIM + MEM_DIM) * 2
    return pl.pallas_call(
        _inproj_conv_kernel,
        out_shape=(jax.ShapeDtypeStruct((ROWS, CONV_DIM), BF16),
                   jax.ShapeDtypeStruct((ROWS, MEM_DIM), BF16),
                   jax.ShapeDtypeStruct((ROW_TILES, V7X_SUBLANES, CONV_DIM), F32),
                   jax.ShapeDtypeStruct((ROWS_SAMPLE, CONV_DIM), F32)),
        grid=(ROW_TILES,),
        in_specs=[_row_spec(D_MODEL), _resident((1, D_MODEL)),
                  _resident((D_MODEL, 3 * CONV_DIM + MEM_DIM)), _resident((CONV_WIDTH, CONV_DIM)),
                  pl.BlockSpec((TILE_ROWS, CONV_DIM), sample_idx)],
        out_specs=(_row_spec(CONV_DIM), _row_spec(MEM_DIM),
                   pl.BlockSpec((1, V7X_SUBLANES, CONV_DIM), lambda i: (i, 0, 0)),
                   pl.BlockSpec((TILE_ROWS, CONV_DIM), sample_idx)),
        scratch_shapes=[pltpu.VMEM((TILE_ROWS + V7X_SUBLANES, CONV_DIM), F32)],
        compiler_params=_params(_vmem_limit(
            win_bytes, 3 * _ROW_TILE_F32 + TILE_ROWS * (CONV_DIM + MEM_DIM) * 2,
            _ROW_TILE_F32 + V7X_SUBLANES * CONV_DIM * 4, 6 * _ROW_TILE_F32)),
        name="inproj_conv",
    )(x1, gm.reshape(1, D_MODEL), win, conv_w, prefix_rows)


def _kvproj_kernel(x_ref, g_ref, wk_ref, wkt_ref, wv2_ref, k_ref, kt_ref, v2_ref):
    hk = _rms(x_ref[...], g_ref[...]).astype(BF16)
    k_ref[...] = _dot(hk, wk_ref[...])
    kt_ref[...] = _dot_nt(wkt_ref[...], hk)
    v2_ref[...] = _dot(hk, wv2_ref[...])


def _kvproj(x, g, wk, wkt, wv2):
    kv_dim = N_KV_HEADS * HEAD_DIM
    return pl.pallas_call(
        _kvproj_kernel,
        out_shape=(jax.ShapeDtypeStruct((ROWS, kv_dim), F32),
                   jax.ShapeDtypeStruct((kv_dim, ROWS), F32),
                   jax.ShapeDtypeStruct((ROWS, 2 * kv_dim), F32)),
        grid=(ROW_TILES,),
        in_specs=[_row_spec(D_MODEL), _resident((1, D_MODEL)), _resident((D_MODEL, kv_dim)),
                  _resident((kv_dim, D_MODEL)), _resident((D_MODEL, 2 * kv_dim))],
        out_specs=(_row_spec(kv_dim), pl.BlockSpec((kv_dim, TILE_ROWS), lambda i: (0, i)),
                   _row_spec(2 * kv_dim)),
        compiler_params=_params(_vmem_limit(
            4 * D_MODEL * kv_dim * 2, _ROW_TILE_F32 + TILE_ROWS * 4 * kv_dim * 4, 0, 4 * _ROW_TILE_F32)),
        name="kvproj",
    )(x, g.reshape(1, D_MODEL), wk, wkt, wv2)


def _attn_prompt_kernel(*refs, swa):
    if swa:
        (qm_ref, mk_ref, mve_ref, qa_ref, ktp_ref, ktc_ref, v2p_ref, v2c_ref, bias_ref, sink_ref,
         ymem_ref, ytok_ref, kt_scr, v2_scr) = refs
    else:
        qm_ref, mk_ref, mve_ref, ymem_ref = refs

    for h in range(MEM_HEADS):
        sl = slice(h * MEM_HEAD_DIM, (h + 1) * MEM_HEAD_DIM)
        s = _dot_nt(qm_ref[:, sl], mk_ref[:, sl]) * MEM_SCALE
        p = jnp.exp(s - jnp.max(s, axis=-1, keepdims=True)).astype(BF16)
        oe = _dot(p, mve_ref[h])
        ymem_ref[:, sl] = (oe[:, :MEM_HEAD_DIM] / oe[:, MEM_HEAD_DIM:]).astype(BF16)
    if not swa:
        return

    first_tile = (pl.program_id(0) % TILES_PER_SEQ) == 0
    kt_scr[0] = ktp_ref[...].astype(BF16)
    for m in range(BLOCKS_PER_TILE):
        kt_scr[m + 1] = ktc_ref[:, m * WINDOW:(m + 1) * WINDOW].astype(BF16)
    v2_scr[:WINDOW, :] = v2p_ref[...]
    v2_scr[WINDOW:, :] = v2c_ref[...]

    lo_k = lax.broadcasted_iota(jnp.int32, (2 * WINDOW, V7X_LANES), 1) < HALF_LANES
    ones_lo = jnp.where(lo_k, 1.0, 0.0).astype(BF16)
    ones_hi = jnp.where(lo_k, 0.0, 1.0).astype(BF16)
    lo_q = lax.broadcasted_iota(jnp.int32, (WINDOW, V7X_LANES), 1) < HALF_LANES
    prev_cols = lax.broadcasted_iota(jnp.int32, (WINDOW, 2 * WINDOW), 1) < WINDOW
    zero_k = jnp.zeros((HEAD_DIM, 2 * WINDOW), BF16)

    def block(n, carry):
        r0 = pl.multiple_of(n * WINDOW, WINDOW)
        kill = jnp.logical_and(jnp.logical_and(first_tile, n == 0), prev_cols)
        for h in range(N_KV_HEADS):
            rows = slice(h * HEAD_DIM, (h + 1) * HEAD_DIM)
            kth = jnp.concatenate([kt_scr[n, rows, :], kt_scr[n + 1, rows, :]], axis=1)
            k_sel = (jnp.concatenate([kth, zero_k], axis=0), jnp.concatenate([zero_k, kth], axis=0))
            v2h = v2_scr[pl.ds(r0, 2 * WINDOW), h * V7X_LANES:(h + 1) * V7X_LANES]
            v_sel = (jnp.concatenate([jnp.where(lo_k, v2h, 0.0).astype(BF16), ones_lo], axis=1),
                     jnp.concatenate([jnp.where(lo_k, 0.0, v2h).astype(BF16), ones_hi], axis=1))
            for pr in range(GROUP // 2):
                c0 = (h * (GROUP // 2) + pr) * V7X_LANES
                qp = qa_ref[pl.ds(r0, WINDOW), c0:c0 + V7X_LANES]
                acc = None
                sink_terms = []
                for e in range(2):
                    j = h * GROUP + pr * 2 + e
                    s = _dot(qp, k_sel[e]) + bias_ref[j]
                    s = jnp.where(kill, NEG_INF, s)
                    sink = sink_ref[j]
                    m = jnp.maximum(jnp.max(s, axis=-1, keepdims=True), sink)
                    p = jnp.exp(s - m).astype(BF16)
                    part = _dot(p, v_sel[e])
                    acc = part if acc is None else acc + part
                    sink_terms.append(jnp.exp(sink - m))
                den = acc[:, V7X_LANES:] + jnp.where(lo_q, sink_terms[0], sink_terms[1])
                ytok_ref[pl.ds(r0, WINDOW), c0:c0 + V7X_LANES] = (acc[:, :V7X_LANES] / den).astype(BF16)
        return carry

    lax.fori_loop(0, BLOCKS_PER_TILE, block, 0)


def _attn_prompt(qm, mk_bf, mv_ext, swa_args=None):
    swa = swa_args is not None
    kv_dim = N_KV_HEADS * HEAD_DIM
    batch_of = lambda i: i // TILES_PER_SEQ
    in_specs = [_row_spec(MEM_DIM),
                pl.BlockSpec((None, MEM_TOKENS, MEM_DIM), lambda i: (batch_of(i), 0, 0)),
                pl.BlockSpec((None, MEM_HEADS, MEM_TOKENS, 2 * MEM_HEAD_DIM), lambda i: (batch_of(i), 0, 0, 0))]
    args = [qm, mk_bf, mv_ext]
    out_shape = [jax.ShapeDtypeStruct((ROWS_PROMPT, MEM_DIM), BF16)]
    out_specs = [_row_spec(MEM_DIM)]
    scratch = []
    streamed = TILE_ROWS * MEM_DIM * 4 + MEM_TOKENS * MEM_DIM * 2 * 3
    resident = 0
    scratch_bytes = 0
    if swa:
        qa, kt, v2, bias_p, sinks = swa_args
        prev_blk = lambda i: jnp.maximum(i * BLOCKS_PER_TILE - 1, 0)
        in_specs += [_row_spec(ATTN_DIM),
                     pl.BlockSpec((kv_dim, WINDOW), lambda i: (0, prev_blk(i))),
                     pl.BlockSpec((kv_dim, TILE_ROWS), lambda i: (0, i)),
                     pl.BlockSpec((WINDOW, 2 * kv_dim), lambda i: (prev_blk(i), 0)),
                     pl.BlockSpec((TILE_ROWS, 2 * kv_dim), lambda i: (i, 0)),
                     _resident((N_HEADS, WINDOW, 2 * WINDOW)),
                     pl.BlockSpec(memory_space=pltpu.SMEM)]
        args += [qa, kt, kt, v2, v2, bias_p, sinks]
        out_shape += [jax.ShapeDtypeStruct((ROWS_PROMPT, ATTN_DIM), BF16)]
        out_specs += [_row_spec(ATTN_DIM)]
        scratch = [pltpu.VMEM((BLOCKS_PER_TILE + 1, kv_dim, WINDOW), BF16),
                   pltpu.VMEM((TILE_ROWS + WINDOW, 2 * kv_dim), F32)]
        resident = N_HEADS * WINDOW * 2 * WINDOW * 4
        streamed += 2 * TILE_ROWS * ATTN_DIM * 2 + (TILE_ROWS + WINDOW) * 3 * kv_dim * 4
        scratch_bytes = (TILE_ROWS + WINDOW) * kv_dim * 2 + (TILE_ROWS + WINDOW) * 2 * kv_dim * 4
    return pl.pallas_call(
        functools.partial(_attn_prompt_kernel, swa=swa),
        out_shape=tuple(out_shape),
        grid=(PROMPT_TILES,),
        in_specs=in_specs,
        out_specs=tuple(out_specs),
        scratch_shapes=scratch,
        compiler_params=_params(_vmem_limit(resident, streamed, scratch_bytes, 8 * _ROW_TILE_F32)),
        name="attn_prompt_swa" if swa else "attn_prompt_mem",
    )(*args)


def _attn_sample_kernel(*refs, swa):
    if swa:
        (qm_ref, mk_ref, mv_ref, qa_ref, ck_ref, cv_ref, kn_ref, vn_ref, bias_ref, sink_ref,
         ymem_ref, ytok_ref, qm_scr, ym_scr, qa_scr, yt_scr, kc_scr, vc_scr) = refs
    else:
        qm_ref, mk_ref, mv_ref, ymem_ref, qm_scr, ym_scr = refs

    qm_scr[...] = qm_ref[...].astype(F32)
    if swa:
        qa_scr[...] = qa_ref[...].astype(F32)
        pad = jnp.zeros((SAMPLE_KEYS_PADDED - SAMPLE_KEYS, N_KV_HEADS * HEAD_DIM), F32)
        kc_scr[SAMPLE_KEYS:, :] = pad
        vc_scr[SAMPLE_KEYS:, :] = pad
        lo = lax.broadcasted_iota(jnp.int32, (DEC_SEQ, V7X_LANES), 1) < HALF_LANES
        zero_slab = jnp.zeros((DEC_SEQ, V7X_LANES), F32)

    def one_batch(b, carry):
        r0 = pl.multiple_of(b * DEC_SEQ, DEC_SEQ)
        qm = qm_scr[pl.ds(r0, DEC_SEQ), :]
        for h in range(MEM_HEADS):
            sl = slice(h * MEM_HEAD_DIM, (h + 1) * MEM_HEAD_DIM)
            s = _dot_nt(qm[:, sl], mk_ref[b, :, sl]) * MEM_SCALE
            p = jnp.exp(s - jnp.max(s, axis=-1, keepdims=True))
            o = _dot(p, mv_ref[b, :, sl]) / jnp.sum(p, axis=-1, keepdims=True)
            ym_scr[pl.ds(r0, DEC_SEQ), sl] = o
        if swa:
            qb = qa_scr[pl.ds(r0, DEC_SEQ), :]
            groups = []
            for j in range(N_HEADS):
                kvh = j // GROUP
                slab = qb[:, (j // 2) * V7X_LANES:(j // 2 + 1) * V7X_LANES]
                if j % 2 != kvh % 2:
                    slab = pltpu.roll(slab, HALF_LANES, axis=1)
                slab = jnp.where(lo if kvh % 2 == 0 else jnp.logical_not(lo), slab, 0.0)
                groups.append(jnp.concatenate(
                    [slab, zero_slab] if kvh // 2 == 0 else [zero_slab, slab], axis=1))
            q_bd = jnp.concatenate(groups, axis=0)
            kc_scr[:WINDOW, :] = ck_ref[b]
            kc_scr[WINDOW:SAMPLE_KEYS, :] = kn_ref[pl.ds(r0, DEC_SEQ), :]
            vc_scr[:WINDOW, :] = cv_ref[b]
            vc_scr[WINDOW:SAMPLE_KEYS, :] = vn_ref[pl.ds(r0, DEC_SEQ), :]
            s = _dot_nt(q_bd, kc_scr[...]) + bias_ref[...]
            sink = sink_ref[...]
            m = jnp.maximum(jnp.max(s, axis=-1, keepdims=True), sink)
            p = jnp.exp(s - m)
            den = jnp.sum(p, axis=-1, keepdims=True) + jnp.exp(sink - m)
            o_full = _dot(p, vc_scr[...]) / den
            for pair in range(N_HEADS // 2):
                acc = None
                for e in range(2):
                    j = pair * 2 + e
                    kvh = j // GROUP
                    slab = o_full[j * DEC_SEQ:(j + 1) * DEC_SEQ, (kvh // 2) * V7X_LANES:(kvh // 2 + 1) * V7X_LANES]
                    if e != kvh % 2:
                        slab = pltpu.roll(slab, HALF_LANES, axis=1)
                    slab = jnp.where(lo if e == 0 else jnp.logical_not(lo), slab, 0.0)
                    acc = slab if acc is None else acc + slab
                yt_scr[pl.ds(r0, DEC_SEQ), pair * V7X_LANES:(pair + 1) * V7X_LANES] = acc
        return carry

    lax.fori_loop(0, SAMPLE_BATCH_BLOCK, one_batch, 0)
    ymem_ref[...] = ym_scr[...].astype(BF16)
    if swa:
        ytok_ref[...] = yt_scr[...].astype(BF16)


def _attn_sample(qm, cache_k, cache_v, layer, swa_args=None):
    swa = swa_args is not None
    kv_dim = N_KV_HEADS * HEAD_DIM
    row0 = ROWS_PROMPT // SAMPLE_BLOCK_ROWS
    blk_rows = lambda width: pl.BlockSpec((SAMPLE_BLOCK_ROWS, width), lambda i: (row0 + i, 0))
    out_rows = lambda width: pl.BlockSpec((SAMPLE_BLOCK_ROWS, width), lambda i: (i, 0))
    cache_spec = pl.BlockSpec((None, SAMPLE_BATCH_BLOCK, MEM_TOKENS, MEM_DIM), lambda i: (layer, i, 0, 0))
    in_specs = [blk_rows(MEM_DIM), cache_spec, cache_spec]
    args = [qm, cache_k, cache_v]
    out_shape = [jax.ShapeDtypeStruct((ROWS_SAMPLE, MEM_DIM), BF16)]
    out_specs = [out_rows(MEM_DIM)]
    scratch = [pltpu.VMEM((SAMPLE_BLOCK_ROWS, MEM_DIM), F32), pltpu.VMEM((SAMPLE_BLOCK_ROWS, MEM_DIM), F32)]
    streamed = 2 * SAMPLE_BATCH_BLOCK * MEM_TOKENS * MEM_DIM * 4 + SAMPLE_BLOCK_ROWS * MEM_DIM * 4
    scratch_bytes = 2 * SAMPLE_BLOCK_ROWS * MEM_DIM * 4
    if swa:
        qa, swa_k, swa_v, k_new, v_new, bias_s, sink_col = swa_args
        swa_spec = pl.BlockSpec((SAMPLE_BATCH_BLOCK, WINDOW, kv_dim), lambda i: (i, 0, 0))
        in_specs += [blk_rows(ATTN_DIM), swa_spec, swa_spec, out_rows(kv_dim), out_rows(kv_dim),
                     _resident((N_HEADS * DEC_SEQ, SAMPLE_KEYS_PADDED)), _resident((N_HEADS * DEC_SEQ, 1))]
        args += [qa, swa_k, swa_v, k_new, v_new, bias_s, sink_col]
        out_shape += [jax.ShapeDtypeStruct((ROWS_SAMPLE, ATTN_DIM), BF16)]
        out_specs += [out_rows(ATTN_DIM)]
        scratch += [pltpu.VMEM((SAMPLE_BLOCK_ROWS, ATTN_DIM), F32), pltpu.VMEM((SAMPLE_BLOCK_ROWS, ATTN_DIM), F32),
                    pltpu.VMEM((SAMPLE_KEYS_PADDED, kv_dim), F32), pltpu.VMEM((SAMPLE_KEYS_PADDED, kv_dim), F32)]
        streamed += 2 * SAMPLE_BATCH_BLOCK * WINDOW * kv_dim * 4 + SAMPLE_BLOCK_ROWS * (ATTN_DIM + kv_dim) * 4
        scratch_bytes += 2 * SAMPLE_BLOCK_ROWS * ATTN_DIM * 4 + 2 * SAMPLE_KEYS_PADDED * kv_dim * 4
    return pl.pallas_call(
        functools.partial(_attn_sample_kernel, swa=swa),
        out_shape=tuple(out_shape),
        grid=(DEC_BATCH // SAMPLE_BATCH_BLOCK,),
        in_specs=in_specs,
        out_specs=tuple(out_specs),
        scratch_shapes=scratch,
        compiler_params=_params(_vmem_limit(1 << 20, streamed, scratch_bytes, 4 * _ROW_TILE_F32)),
        name="attn_sample_swa" if swa else "attn_sample_mem",
    )(*args)


def _outffn_kernel(*refs, split_tok, final):
    refs = list(refs)
    x1_ref = refs.pop(0)
    tok_refs = [refs.pop(0) for _ in range(2 if split_tok else 1)]
    ymp_ref, yms_ref, wo_ref, g_ref, wg_ref, wu_ref, wd_ref = [refs.pop(0) for _ in range(7)]
    gf_ref = refs.pop(0) if final else None
    out_ref, act_ref = refs
    is_prompt = pl.program_id(0) < PROMPT_TILES
    y_tok = jnp.where(is_prompt, tok_refs[0][...], tok_refs[1][...]) if split_tok else tok_refs[0][...]
    y_mem = jnp.where(is_prompt, ymp_ref[...], yms_ref[...])
    tok_dim = wo_ref.shape[0] - MEM_DIM
    x2 = x1_ref[...] + _dot(y_tok, wo_ref[:tok_dim, :]) + _dot(y_mem, wo_ref[tok_dim:, :])
    x3 = _ffn_half_step(x2, g_ref, wg_ref, wu_ref, wd_ref, act_ref)
    out_ref[...] = _rms(x3, gf_ref[...]) if final else x3


def _outffn(x1, y_tok, ymem_p, ymem_s, wo, g, wg, wu, wd, final_gain=None):
    split_tok = isinstance(y_tok, tuple)
    final = final_gain is not None
    prompt_idx = lambda i: (jnp.minimum(i, PROMPT_TILES - 1), 0)
    sample_idx = lambda i: (jnp.maximum(i - PROMPT_TILES, 0), 0)
    tok_dim = wo.shape[0] - MEM_DIM
    in_specs = [_row_spec(D_MODEL)]
    args = [x1]
    if split_tok:
        in_specs += [pl.BlockSpec((TILE_ROWS, tok_dim), prompt_idx), pl.BlockSpec((TILE_ROWS, tok_dim), sample_idx)]
        args += list(y_tok)
    else:
        in_specs += [_row_spec(tok_dim)]
        args += [y_tok]
    in_specs += [pl.BlockSpec((TILE_ROWS, MEM_DIM), prompt_idx), pl.BlockSpec((TILE_ROWS, MEM_DIM), sample_idx),
                 _resident((tok_dim + MEM_DIM, D_MODEL)), _resident((1, D_MODEL)),
                 _resident((D_MODEL, FFN_DIM)), _resident((D_MODEL, FFN_DIM)), _resident((FFN_DIM, D_MODEL))]
    args += [ymem_p, ymem_s, wo, g.reshape(1, D_MODEL), wg, wu, wd]
    if final:
        in_specs += [_resident((1, D_MODEL))]
        args += [final_gain.reshape(1, D_MODEL)]
    resident = _FFN_WEIGHT_BYTES + (tok_dim + MEM_DIM) * D_MODEL * 2
    streamed = 2 * _ROW_TILE_F32 + 2 * TILE_ROWS * (tok_dim + MEM_DIM) * 2
    return pl.pallas_call(
        functools.partial(_outffn_kernel, split_tok=split_tok, final=final),
        out_shape=jax.ShapeDtypeStruct((ROWS, D_MODEL), F32),
        grid=(ROW_TILES,),
        in_specs=in_specs,
        out_specs=_row_spec(D_MODEL),
        scratch_shapes=[pltpu.VMEM((TILE_ROWS, FFN_DIM), BF16)],
        compiler_params=_params(_vmem_limit(resident, streamed, _ACT_BYTES, 7 * _ROW_TILE_F32)),
        name="outffn_final" if final else "outffn",
    )(*args)


def kernel(x_prompt, x_sample, state_conv, cache_swa_k, cache_swa_v, cache_mem_k, cache_mem_v, mem_prompt, ffn1_norm, ffn1_wg, ffn1_wu, ffn1_wd, mix_norm, w_in_a, conv_w, w_out_a, kv_norm, w_kv, w_in_b, attn_sinks, rel_bias, w_out_b, mem_norm, w_mem_kv, ffn2_norm, ffn2_wg, ffn2_wu, ffn2_wd, final_norm):
    kv_dim = N_KV_HEADS * HEAD_DIM
    bf = lambda w: w.astype(BF16)
    ffn1 = (bf(ffn1_wg), bf(ffn1_wu), bf(ffn1_wd))
    ffn2 = (bf(ffn2_wg), bf(ffn2_wu), bf(ffn2_wd))
    w_in_a_bf, w_out_a_bf, w_in_b_bf, w_out_b_bf = bf(w_in_a), bf(w_out_a), bf(w_in_b), bf(w_out_b)
    wk = w_kv[:, :kv_dim]
    wv = w_kv[:, kv_dim:]
    wv2 = jnp.broadcast_to(wv.reshape(D_MODEL, N_KV_HEADS, 1, HEAD_DIM),
                           (D_MODEL, N_KV_HEADS, 2, HEAD_DIM)).reshape(D_MODEL, 2 * kv_dim)

    mem_k, mem_v, mem_k_bf, mem_v_ext = _memkv(mem_prompt, mem_norm, bf(w_mem_kv))
    bias_p, bias_s = _bias_tables(rel_bias)
    cache_k = cache_mem_k.reshape(DEPTH, DEC_BATCH, MEM_TOKENS, MEM_DIM)
    cache_v = cache_mem_v.reshape(DEPTH, DEC_BATCH, MEM_TOKENS, MEM_DIM)
    swa_k_cache = cache_swa_k.reshape(DEC_BATCH, WINDOW, kv_dim)
    swa_v_cache = cache_swa_v.reshape(DEC_BATCH, WINDOW, kv_dim)

    x = jnp.concatenate([x_prompt.reshape(ROWS_PROMPT, D_MODEL), x_sample.reshape(ROWS_SAMPLE, D_MODEL)], axis=0)
    tails, sample_us = [], []
    k_rows = kt = v2 = k_new = v_new = None
    for l in range(DEPTH):
        final_gain = final_norm if l == DEPTH - 1 else None
        ffn2_l = (ffn2_norm[l], ffn2[0][l], ffn2[1][l], ffn2[2][l])
        if l < N_A_LAYERS:
            (x1,) = _ffn(x, ffn1_norm[l], ffn1[0][l], ffn1[1][l], ffn1[2][l])
            prefix_rows = jnp.pad(state_conv[l], ((0, 0), (0, DEC_SEQ - (CONV_WIDTH - 1)), (0, 0)))
            y_tok, qm, tail, us = _inproj_conv(x1, mix_norm[l], w_in_a_bf[l], conv_w[l],
                                               prefix_rows.reshape(ROWS_SAMPLE, CONV_DIM))
            tails.append(tail)
            sample_us.append(us)
            (ymem_p,) = _attn_prompt(qm, mem_k_bf[l], mem_v_ext[l])
            (ymem_s,) = _attn_sample(qm, cache_k, cache_v, l)
            x = _outffn(x1, y_tok, ymem_p, ymem_s, w_out_a_bf[l], *ffn2_l, final_gain=final_gain)
        else:
            j = l - N_A_LAYERS
            if j == 0:
                k_rows, kt, v2 = _kvproj(x, kv_norm, bf(wk), bf(wk.T), bf(wv2))
                k_new = k_rows[ROWS_PROMPT:]
                v_new = v2[ROWS_PROMPT:].reshape(ROWS_SAMPLE, N_KV_HEADS, 2, HEAD_DIM)[:, :, 0, :].reshape(ROWS_SAMPLE, kv_dim)
            x1, qa, qm = _ffn(x, ffn1_norm[l], ffn1[0][l], ffn1[1][l], ffn1[2][l], mix_norm[l], w_in_b_bf[j])
            sink_col = jnp.repeat(attn_sinks[j], DEC_SEQ).reshape(N_HEADS * DEC_SEQ, 1)
            ymem_p, ytok_p = _attn_prompt(qm, mem_k_bf[l], mem_v_ext[l], (qa, kt, v2, bias_p, attn_sinks[j]))
            ymem_s, ytok_s = _attn_sample(qm, cache_k, cache_v, l,
                                          (qa, swa_k_cache, swa_v_cache, k_new, v_new, bias_s, sink_col))
            x = _outffn(x1, (ytok_p, ytok_s), ymem_p, ymem_s, w_out_b_bf[j], *ffn2_l, final_gain=final_gain)

    y_prompt = x[:ROWS_PROMPT].reshape(BATCH, SEQ, D_MODEL)
    y_sample = x[ROWS_PROMPT:].reshape(DEC_BATCH, DEC_SEQ, D_MODEL)
    keep = CONV_WIDTH - 1
    last_tiles = np.arange(BATCH) * TILES_PER_SEQ + TILES_PER_SEQ - 1
    conv_state_prompt = jnp.stack([t[last_tiles, V7X_SUBLANES - keep:, :] for t in tails])
    conv_state_sample = jnp.stack([u.reshape(DEC_BATCH, DEC_SEQ, CONV_DIM)[:, DEC_SEQ - keep:, :] for u in sample_us])
    k_p = k_rows[:ROWS_PROMPT].reshape(BATCH, SEQ, N_KV_HEADS, HEAD_DIM)
    v_p = v2[:ROWS_PROMPT].reshape(BATCH, SEQ, N_KV_HEADS, 2, HEAD_DIM)[:, :, :, 0, :]
    swa_k_prompt = k_p[:, SEQ - WINDOW:]
    swa_v_prompt = v_p[:, SEQ - WINDOW:]
    swa_k_sample = jnp.concatenate(
        [cache_swa_k[:, DEC_SEQ:], k_new.reshape(DEC_BATCH, DEC_SEQ, N_KV_HEADS, HEAD_DIM)], axis=1)
    swa_v_sample = jnp.concatenate(
        [cache_swa_v[:, DEC_SEQ:], v_new.reshape(DEC_BATCH, DEC_SEQ, N_KV_HEADS, HEAD_DIM)], axis=1)
    mem_shape = (DEPTH, BATCH, MEM_TOKENS, MEM_HEADS, MEM_HEAD_DIM)
    return (y_prompt, y_sample, conv_state_prompt, conv_state_sample,
            swa_k_prompt, swa_v_prompt, swa_k_sample, swa_v_sample,
            mem_k.reshape(mem_shape), mem_v.reshape(mem_shape))
```

```python
import functools
import math

import numpy as np
import jax
import jax.numpy as jnp
from jax import lax
from jax.experimental import pallas as pl
from jax.experimental.pallas import tpu as pltpu

D_MODEL = 1024
BATCH = 2
SEQ = 8192
DEPTH = 4
DEC_BATCH = 128
DEC_SEQ = 8
N_A_LAYERS = DEPTH // 2
FFN_DIM = 2816
CONV_DIM = D_MODEL
CONV_WIDTH = 3
N_HEADS = 16
N_KV_HEADS = 4
HEAD_DIM = 64
GROUP = N_HEADS // N_KV_HEADS
ATTN_DIM = N_HEADS * HEAD_DIM
KV_DIM = N_KV_HEADS * HEAD_DIM
WINDOW = 128
REL_BUCKETS = 32
REL_MAX_DIST = 128
MEM_TOKENS = 256
MEM_HEADS = 4
MEM_HEAD_DIM = 128
MEM_DIM = MEM_HEADS * MEM_HEAD_DIM
RMS_EPS = 1e-5

F32 = jnp.float32
BF16 = jnp.bfloat16
NEG_INF = float("-inf")

V7X_LANES = 128
V7X_SUBLANES = 8
V7X_MXU_DIM = 256
V7X_VMEM_BYTES = 64 * 1024 * 1024

ROWS_PROMPT = BATCH * SEQ
ROWS_SAMPLE = DEC_BATCH * DEC_SEQ
ROWS = ROWS_PROMPT + ROWS_SAMPLE
TILE_ROWS = 512
PROMPT_TILES = ROWS_PROMPT // TILE_ROWS
SAMPLE_TILES = ROWS_SAMPLE // TILE_ROWS
ROW_TILES = PROMPT_TILES + SAMPLE_TILES
TILES_PER_SEQ = SEQ // TILE_ROWS
FFN_CHUNK = V7X_MXU_DIM
CONV_CHUNK = V7X_MXU_DIM
BLOCKS_PER_TILE = TILE_ROWS // WINDOW
SAMPLE_BATCH_BLOCK = 8
SAMPLE_BLOCK_ROWS = SAMPLE_BATCH_BLOCK * DEC_SEQ
SAMPLE_KEYS = WINDOW + DEC_SEQ
SAMPLE_KEYS_PADDED = 2 * WINDOW
HALF_LANES = V7X_LANES // 2
MEM_SCALE = MEM_HEAD_DIM ** -0.5
ATTN_SCALE = HEAD_DIM ** -0.5

assert HEAD_DIM == HALF_LANES and MEM_HEAD_DIM == V7X_LANES
assert ROWS_PROMPT % TILE_ROWS == 0 and ROWS_SAMPLE % TILE_ROWS == 0 and SEQ % TILE_ROWS == 0
assert FFN_DIM % FFN_CHUNK == 0 and TILE_ROWS % WINDOW == 0


def _vmem_limit(resident_bytes, streamed_bytes, scratch_bytes, temp_bytes):
    need = resident_bytes + 2 * streamed_bytes + scratch_bytes + temp_bytes
    assert need < V7X_VMEM_BYTES, need
    return int(need)


def _params(vmem_bytes, n_axes=1):
    return pltpu.CompilerParams(
        dimension_semantics=("arbitrary",) * n_axes, vmem_limit_bytes=vmem_bytes)


def _resident(shape):
    zeros = (0,) * len(shape)
    return pl.BlockSpec(shape, lambda *_: zeros, pipeline_mode=pl.Buffered(1))


def _resident_layer(shape, layer):
    idx = (layer,) + (0,) * len(shape)
    return pl.BlockSpec((None,) + tuple(shape), lambda *_: idx, pipeline_mode=pl.Buffered(1))


def _row_spec(width, tile0=0):
    return pl.BlockSpec((TILE_ROWS, width), lambda i: (i + tile0, 0))


def _group_specs(width, tile0=0):
    return [pl.BlockSpec((TILE_ROWS, width), lambda i: (jnp.minimum(i + tile0, PROMPT_TILES - 1), 0)),
            pl.BlockSpec((TILE_ROWS, width), lambda i: (jnp.maximum(i + tile0 - PROMPT_TILES, 0), 0))]


def _pick_group(prompt_ref, sample_ref, tile0=0):
    return jnp.where(pl.program_id(0) + tile0 < PROMPT_TILES, prompt_ref[...], sample_ref[...])


def _dot(a, b):
    return jnp.dot(a, b, preferred_element_type=F32)


def _dot_nt(a, b):
    return lax.dot_general(a, b, (((1,), (1,)), ((), ())), preferred_element_type=F32)


def _rms(x, g):
    return x * lax.rsqrt(jnp.mean(x * x, axis=-1, keepdims=True) + RMS_EPS) * g


def _ffn_half_step(x, g_ref, wg_ref, wu_ref, wd_ref, act_ref):
    h = _rms(x, g_ref[...]).astype(BF16)
    for c in range(FFN_DIM // FFN_CHUNK):
        sl = slice(c * FFN_CHUNK, (c + 1) * FFN_CHUNK)
        gate = _dot(h, wg_ref[:, sl])
        up = _dot(h, wu_ref[:, sl])
        act_ref[:, sl] = (gate / (1.0 + jnp.exp(-gate)) * up).astype(BF16)
    return x + 0.5 * _dot(act_ref[...], wd_ref[...])


_FFN_WEIGHT_BYTES = 3 * D_MODEL * FFN_DIM * 2
_ROW_TILE_F32 = TILE_ROWS * D_MODEL * 4
_ACT_BYTES = TILE_ROWS * FFN_DIM * 2


def _ffn_specs(layer):
    return [_resident((1, D_MODEL)), _resident_layer((D_MODEL, FFN_DIM), layer),
            _resident_layer((D_MODEL, FFN_DIM), layer), _resident_layer((FFN_DIM, D_MODEL), layer)]


def _t5_bucket_np(dist):
    n = np.maximum(dist, 0)
    exact = REL_BUCKETS // 2
    nf = np.maximum(n, 1).astype(np.float32)
    large = exact + (np.log(nf / np.float32(exact)) / np.float32(math.log(REL_MAX_DIST / exact))
                     * np.float32(REL_BUCKETS - exact)).astype(np.int32)
    large = np.minimum(large, REL_BUCKETS - 1)
    return np.where(n < exact, n, large).astype(np.int32)


def _bucket_tables():
    q = np.arange(WINDOW)[:, None]
    k = np.arange(2 * WINDOW)[None, :]
    dist = WINDOW + q - k
    prompt = np.where((dist >= 0) & (dist < WINDOW), _t5_bucket_np(dist), -1)
    t = (np.arange(N_HEADS * DEC_SEQ) % DEC_SEQ)[:, None]
    k = np.arange(SAMPLE_KEYS_PADDED)[None, :]
    dist = WINDOW + t - k
    ok = (dist >= 0) & (dist < WINDOW) & (k < SAMPLE_KEYS)
    sample = np.where(ok, _t5_bucket_np(dist), -1)
    return prompt.astype(np.int32), sample.astype(np.int32)


def _bias_kernel(rel_ref, bp_ref, bs_ref, op_ref, os_ref):
    j = pl.program_id(0)

    def build(bucket):
        acc = jnp.zeros(bucket.shape, F32)
        for b in range(REL_BUCKETS):
            acc = jnp.where(bucket == b, rel_ref[b, j], acc)
        return jnp.where(bucket < 0, NEG_INF, acc)

    op_ref[0] = build(bp_ref[...])
    os_ref[...] = build(bs_ref[...])


def _bias_tables(rel_bias):
    bp, bs = _bucket_tables()
    return pl.pallas_call(
        _bias_kernel,
        out_shape=(jax.ShapeDtypeStruct((N_HEADS, WINDOW, 2 * WINDOW), F32),
                   jax.ShapeDtypeStruct((N_HEADS * DEC_SEQ, SAMPLE_KEYS_PADDED), F32)),
        grid=(N_HEADS,),
        in_specs=[pl.BlockSpec(memory_space=pltpu.SMEM),
                  pl.BlockSpec((WINDOW, 2 * WINDOW), lambda j: (0, 0)),
                  pl.BlockSpec((DEC_SEQ, SAMPLE_KEYS_PADDED), lambda j: (j, 0))],
        out_specs=(pl.BlockSpec((1, WINDOW, 2 * WINDOW), lambda j: (j, 0, 0)),
                   pl.BlockSpec((DEC_SEQ, SAMPLE_KEYS_PADDED), lambda j: (j, 0))),
        name="bias_tables",
    )(rel_bias, jnp.asarray(bp), jnp.asarray(bs))


def _memkv_kernel(m_ref, g_ref, w_ref, k_ref, v_ref, kb_ref, ve_ref):
    hn = _rms(m_ref[...], g_ref[...]).astype(BF16)
    kv = _dot(hn, w_ref[...])
    k = kv[:, :MEM_DIM]
    v = kv[:, MEM_DIM:]
    k_ref[...] = k
    v_ref[...] = v
    kb_ref[...] = k.astype(BF16)
    ones = jnp.ones((MEM_TOKENS, MEM_HEAD_DIM), BF16)
    for h in range(MEM_HEADS):
        vh = v[:, h * MEM_HEAD_DIM:(h + 1) * MEM_HEAD_DIM].astype(BF16)
        ve_ref[h] = jnp.concatenate([vh, ones], axis=1)


def _memkv(mem_prompt, mem_norm, w_mem_kv_bf):
    shp = (DEPTH, BATCH, MEM_TOKENS, MEM_DIM)
    blk = pl.BlockSpec((None, None, MEM_TOKENS, MEM_DIM), lambda l, b: (l, b, 0, 0))
    return pl.pallas_call(
        _memkv_kernel,
        out_shape=(jax.ShapeDtypeStruct(shp, F32), jax.ShapeDtypeStruct(shp, F32),
                   jax.ShapeDtypeStruct(shp, BF16),
                   jax.ShapeDtypeStruct((DEPTH, BATCH, MEM_HEADS, MEM_TOKENS, 2 * MEM_HEAD_DIM), BF16)),
        grid=(DEPTH, BATCH),
        in_specs=[pl.BlockSpec((None, MEM_TOKENS, D_MODEL), lambda l, b: (b, 0, 0)),
                  pl.BlockSpec((None, 1, D_MODEL), lambda l, b: (l, 0, 0)),
                  pl.BlockSpec((None, D_MODEL, 2 * MEM_DIM), lambda l, b: (l, 0, 0))],
        out_specs=(blk, blk, blk,
                   pl.BlockSpec((None, None, MEM_HEADS, MEM_TOKENS, 2 * MEM_HEAD_DIM),
                                lambda l, b: (l, b, 0, 0, 0))),
        compiler_params=_params(32 * 1024 * 1024, 2),
        name="memkv",
    )(mem_prompt, mem_norm.reshape(DEPTH, 1, D_MODEL), w_mem_kv_bf)


def _ffn_kernel(*refs, n_x, attn_proj):
    refs = list(refs)
    x_refs = [refs.pop(0) for _ in range(n_x)]
    g_ref, wg_ref, wu_ref, wd_ref = [refs.pop(0) for _ in range(4)]
    if attn_proj:
        gm_ref, win_ref, x1_ref, qa_ref, qm_ref, act_ref = refs
    else:
        x1_ref, act_ref = refs
    x = _pick_group(*x_refs) if n_x == 2 else x_refs[0][...]
    x1 = _ffn_half_step(x, g_ref, wg_ref, wu_ref, wd_ref, act_ref)
    x1_ref[...] = x1
    if attn_proj:
        hm = _rms(x1, gm_ref[...]).astype(BF16)
        qa_ref[...] = (_dot(hm, win_ref[:, :ATTN_DIM]) * ATTN_SCALE).astype(BF16)
        qm_ref[...] = _dot(hm, win_ref[:, ATTN_DIM:]).astype(BF16)


def _ffn(x, g, ffn_w, layer, gm=None, win=None, win_layer=None):
    attn_proj = win is not None
    xs = list(x) if isinstance(x, tuple) else [x]
    in_specs = (_group_specs(D_MODEL) if len(xs) == 2 else [_row_spec(D_MODEL)]) + _ffn_specs(layer)
    args = xs + [g.reshape(1, D_MODEL)] + list(ffn_w)
    out_shape = [jax.ShapeDtypeStruct((ROWS, D_MODEL), F32)]
    out_specs = [_row_spec(D_MODEL)]
    resident = _FFN_WEIGHT_BYTES
    streamed = (1 + len(xs)) * _ROW_TILE_F32
    if attn_proj:
        in_specs += [_resident((1, D_MODEL)), _resident_layer((D_MODEL, ATTN_DIM + MEM_DIM), win_layer)]
        args += [gm.reshape(1, D_MODEL), win]
        out_shape += [jax.ShapeDtypeStruct((ROWS, ATTN_DIM), BF16), jax.ShapeDtypeStruct((ROWS, MEM_DIM), BF16)]
        out_specs += [_row_spec(ATTN_DIM), _row_spec(MEM_DIM)]
        resident += D_MODEL * (ATTN_DIM + MEM_DIM) * 2
        streamed += TILE_ROWS * (ATTN_DIM + MEM_DIM) * 2
    return pl.pallas_call(
        functools.partial(_ffn_kernel, n_x=len(xs), attn_proj=attn_proj),
        out_shape=tuple(out_shape),
        grid=(ROW_TILES,),
        in_specs=in_specs,
        out_specs=tuple(out_specs),
        scratch_shapes=[pltpu.VMEM((TILE_ROWS, FFN_DIM), BF16)],
        compiler_params=_params(_vmem_limit(resident, streamed, _ACT_BYTES, 6 * _ROW_TILE_F32)),
        name="ffn_attn_proj" if attn_proj else "ffn",
    )(*args)


def _inproj_conv_kernel(x_ref, gm_ref, win_ref, cw_ref, pre_ref,
                        ytok_ref, qm_ref, tail_ref, us_ref, shift_ref):
    i = pl.program_id(0)
    hm = _rms(x_ref[...], gm_ref[...]).astype(BF16)
    qm_ref[...] = _dot(hm, win_ref[:, 3 * CONV_DIM:]).astype(BF16)

    def chunk(cc, prompt):
        sl = slice(cc * CONV_CHUNK, (cc + 1) * CONV_CHUNK)
        c_gate = _dot(hm, win_ref[:, CONV_DIM + cc * CONV_CHUNK:CONV_DIM + (cc + 1) * CONV_CHUNK])
        x_in = _dot(hm, win_ref[:, 2 * CONV_DIM + cc * CONV_CHUNK:2 * CONV_DIM + (cc + 1) * CONV_CHUNK])
        u = c_gate * x_in
        if prompt:
            shift_ref[V7X_SUBLANES:, sl] = u
            u1 = shift_ref[V7X_SUBLANES - 1:V7X_SUBLANES - 1 + TILE_ROWS, sl]
            u2 = shift_ref[V7X_SUBLANES - 2:V7X_SUBLANES - 2 + TILE_ROWS, sl]
            last = u[TILE_ROWS - V7X_SUBLANES:, :]
            shift_ref[:V7X_SUBLANES, sl] = last
            tail_ref[0, :, sl] = last
        else:
            t = lax.broadcasted_iota(jnp.int32, (TILE_ROWS, CONV_CHUNK), 0) % DEC_SEQ
            p2 = pre_ref[:, sl]
            p1 = pltpu.roll(p2, TILE_ROWS - 1, axis=0)
            u1 = jnp.where(t == 0, p1, pltpu.roll(u, 1, axis=0))
            u2 = jnp.where(t < 2, p2, pltpu.roll(u, 2, axis=0))
            us_ref[:, sl] = u
            tail_ref[0, :, sl] = jnp.zeros((V7X_SUBLANES, CONV_CHUNK), F32)
        w = cw_ref[:, sl]
        conv = w[0:1] * u2 + w[1:2] * u1 + w[2:3] * u
        b_gate = _dot(hm, win_ref[:, sl])
        ytok_ref[:, sl] = (b_gate * conv).astype(BF16)

    @pl.when(i < PROMPT_TILES)
    def _():
        @pl.when(i % TILES_PER_SEQ == 0)
        def _():
            shift_ref[:V7X_SUBLANES, :] = jnp.zeros((V7X_SUBLANES, CONV_DIM), F32)
        for cc in range(CONV_DIM // CONV_CHUNK):
            chunk(cc, True)

    @pl.when(i >= PROMPT_TILES)
    def _():
        for cc in range(CONV_DIM // CONV_CHUNK):
            chunk(cc, False)


def _inproj_conv(x1, gm, win, conv_w, prefix_rows, layer):
    sample_idx = lambda i: (jnp.maximum(i - PROMPT_TILES, 0), 0)
    win_bytes = D_MODEL * (3 * CONV_DIM + MEM_DIM) * 2
    return pl.pallas_call(
        _inproj_conv_kernel,
        out_shape=(jax.ShapeDtypeStruct((ROWS, CONV_DIM), BF16),
                   jax.ShapeDtypeStruct((ROWS, MEM_DIM), BF16),
                   jax.ShapeDtypeStruct((ROW_TILES, V7X_SUBLANES, CONV_DIM), F32),
                   jax.ShapeDtypeStruct((ROWS_SAMPLE, CONV_DIM), F32)),
        grid=(ROW_TILES,),
        in_specs=[_row_spec(D_MODEL), _resident((1, D_MODEL)),
                  _resident_layer((D_MODEL, 3 * CONV_DIM + MEM_DIM), layer),
                  _resident_layer((CONV_WIDTH, CONV_DIM), layer),
                  pl.BlockSpec((TILE_ROWS, CONV_DIM), sample_idx)],
        out_specs=(_row_spec(CONV_DIM), _row_spec(MEM_DIM),
                   pl.BlockSpec((1, V7X_SUBLANES, CONV_DIM), lambda i: (i, 0, 0)),
                   pl.BlockSpec((TILE_ROWS, CONV_DIM), sample_idx)),
        scratch_shapes=[pltpu.VMEM((TILE_ROWS + V7X_SUBLANES, CONV_DIM), F32)],
        compiler_params=_params(_vmem_limit(
            win_bytes, 3 * _ROW_TILE_F32 + TILE_ROWS * (CONV_DIM + MEM_DIM) * 2,
            _ROW_TILE_F32 + V7X_SUBLANES * CONV_DIM * 4, 6 * _ROW_TILE_F32)),
        name="inproj_conv",
    )(x1, gm.reshape(1, D_MODEL), win, conv_w, prefix_rows)


def _kvproj_kernel(x_ref, g_ref, wk_ref, wkt_ref, wv2_ref, k_ref, kt_ref, v2_ref):
    hk = _rms(x_ref[...], g_ref[...]).astype(BF16)
    k_ref[...] = _dot(hk, wk_ref[...])
    kt_ref[...] = _dot_nt(wkt_ref[...], hk)
    v2_ref[...] = _dot(hk, wv2_ref[...])


def _kvproj(x, g, wk, wkt, wv2):
    return pl.pallas_call(
        _kvproj_kernel,
        out_shape=(jax.ShapeDtypeStruct((ROWS, KV_DIM), F32),
                   jax.ShapeDtypeStruct((KV_DIM, ROWS), F32),
                   jax.ShapeDtypeStruct((ROWS, 2 * KV_DIM), F32)),
        grid=(ROW_TILES,),
        in_specs=[_row_spec(D_MODEL), _resident((1, D_MODEL)), _resident((D_MODEL, KV_DIM)),
                  _resident((KV_DIM, D_MODEL)), _resident((D_MODEL, 2 * KV_DIM))],
        out_specs=(_row_spec(KV_DIM), pl.BlockSpec((KV_DIM, TILE_ROWS), lambda i: (0, i)),
                   _row_spec(2 * KV_DIM)),
        compiler_params=_params(_vmem_limit(
            4 * D_MODEL * KV_DIM * 2, _ROW_TILE_F32 + TILE_ROWS * 4 * KV_DIM * 4, 0, 4 * _ROW_TILE_F32)),
        name="kvproj",
    )(x, g.reshape(1, D_MODEL), wk, wkt, wv2)


def _attn_prompt_kernel(*refs, swa):
    if swa:
        (qm_ref, mk_ref, mve_ref, qa_ref, ktp_ref, ktc_ref, v2p_ref, v2c_ref, bias_ref, sink_ref,
         ymem_ref, ytok_ref, kt_scr, v2_scr) = refs
    else:
        qm_ref, mk_ref, mve_ref, ymem_ref = refs

    for h in range(MEM_HEADS):
        sl = slice(h * MEM_HEAD_DIM, (h + 1) * MEM_HEAD_DIM)
        s = _dot_nt(qm_ref[:, sl], mk_ref[:, sl]) * MEM_SCALE
        p = jnp.exp(s - jnp.max(s, axis=-1, keepdims=True)).astype(BF16)
        oe = _dot(p, mve_ref[h])
        ymem_ref[:, sl] = (oe[:, :MEM_HEAD_DIM] / oe[:, MEM_HEAD_DIM:]).astype(BF16)
    if not swa:
        return

    first_tile = (pl.program_id(0) % TILES_PER_SEQ) == 0
    kt_scr[0] = ktp_ref[...].astype(BF16)
    for m in range(BLOCKS_PER_TILE):
        kt_scr[m + 1] = ktc_ref[:, m * WINDOW:(m + 1) * WINDOW].astype(BF16)
    v2_scr[:WINDOW, :] = v2p_ref[...]
    v2_scr[WINDOW:, :] = v2c_ref[...]

    lo_k = lax.broadcasted_iota(jnp.int32, (2 * WINDOW, V7X_LANES), 1) < HALF_LANES
    ones_lo = jnp.where(lo_k, 1.0, 0.0).astype(BF16)
    ones_hi = jnp.where(lo_k, 0.0, 1.0).astype(BF16)
    lo_q = lax.broadcasted_iota(jnp.int32, (WINDOW, V7X_LANES), 1) < HALF_LANES
    prev_cols = lax.broadcasted_iota(jnp.int32, (WINDOW, 2 * WINDOW), 1) < WINDOW
    zero_k = jnp.zeros((HEAD_DIM, 2 * WINDOW), BF16)

    def block(n, carry):
        r0 = pl.multiple_of(n * WINDOW, WINDOW)
        kill = jnp.logical_and(jnp.logical_and(first_tile, n == 0), prev_cols)
        for h in range(N_KV_HEADS):
            rows = slice(h * HEAD_DIM, (h + 1) * HEAD_DIM)
            kth = jnp.concatenate([kt_scr[n, rows, :], kt_scr[n + 1, rows, :]], axis=1)
            k_sel = (jnp.concatenate([kth, zero_k], axis=0), jnp.concatenate([zero_k, kth], axis=0))
            v2h = v2_scr[pl.ds(r0, 2 * WINDOW), h * V7X_LANES:(h + 1) * V7X_LANES]
            v_sel = (jnp.concatenate([jnp.where(lo_k, v2h, 0.0).astype(BF16), ones_lo], axis=1),
                     jnp.concatenate([jnp.where(lo_k, 0.0, v2h).astype(BF16), ones_hi], axis=1))
            for pr in range(GROUP // 2):
                c0 = (h * (GROUP // 2) + pr) * V7X_LANES
                qp = qa_ref[pl.ds(r0, WINDOW), c0:c0 + V7X_LANES]
                acc = None
                sink_terms = []
                for e in range(2):
                    j = h * GROUP + pr * 2 + e
                    s = _dot(qp, k_sel[e]) + bias_ref[j]
                    s = jnp.where(kill, NEG_INF, s)
                    sink = sink_ref[j]
                    m = jnp.maximum(jnp.max(s, axis=-1, keepdims=True), sink)
                    p = jnp.exp(s - m).astype(BF16)
                    part = _dot(p, v_sel[e])
                    acc = part if acc is None else acc + part
                    sink_terms.append(jnp.exp(sink - m))
                den = acc[:, V7X_LANES:] + jnp.where(lo_q, sink_terms[0], sink_terms[1])
                ytok_ref[pl.ds(r0, WINDOW), c0:c0 + V7X_LANES] = (acc[:, :V7X_LANES] / den).astype(BF16)
        return carry

    lax.fori_loop(0, BLOCKS_PER_TILE, block, 0)


def _attn_prompt(qm, mk_bf, mv_ext, layer, swa_args=None):
    swa = swa_args is not None
    batch_of = lambda i: i // TILES_PER_SEQ
    in_specs = [_row_spec(MEM_DIM),
                pl.BlockSpec((None, None, MEM_TOKENS, MEM_DIM), lambda i: (layer, batch_of(i), 0, 0)),
                pl.BlockSpec((None, None, MEM_HEADS, MEM_TOKENS, 2 * MEM_HEAD_DIM),
                             lambda i: (layer, batch_of(i), 0, 0, 0))]
    args = [qm, mk_bf, mv_ext]
    out_shape = [jax.ShapeDtypeStruct((ROWS_PROMPT, MEM_DIM), BF16)]
    out_specs = [_row_spec(MEM_DIM)]
    scratch = []
    streamed = TILE_ROWS * MEM_DIM * 4 + MEM_TOKENS * MEM_DIM * 2 * 3
    resident = 0
    scratch_bytes = 0
    if swa:
        qa, kt, v2, bias_p, sinks = swa_args
        prev_blk = lambda i: jnp.maximum(i * BLOCKS_PER_TILE - 1, 0)
        in_specs += [_row_spec(ATTN_DIM),
                     pl.BlockSpec((KV_DIM, WINDOW), lambda i: (0, prev_blk(i))),
                     pl.BlockSpec((KV_DIM, TILE_ROWS), lambda i: (0, i)),
                     pl.BlockSpec((WINDOW, 2 * KV_DIM), lambda i: (prev_blk(i), 0)),
                     pl.BlockSpec((TILE_ROWS, 2 * KV_DIM), lambda i: (i, 0)),
                     _resident((N_HEADS, WINDOW, 2 * WINDOW)),
                     pl.BlockSpec(memory_space=pltpu.SMEM)]
        args += [qa, kt, kt, v2, v2, bias_p, sinks]
        out_shape += [jax.ShapeDtypeStruct((ROWS_PROMPT, ATTN_DIM), BF16)]
        out_specs += [_row_spec(ATTN_DIM)]
        scratch = [pltpu.VMEM((BLOCKS_PER_TILE + 1, KV_DIM, WINDOW), BF16),
                   pltpu.VMEM((TILE_ROWS + WINDOW, 2 * KV_DIM), F32)]
        resident = N_HEADS * WINDOW * 2 * WINDOW * 4
        streamed += 2 * TILE_ROWS * ATTN_DIM * 2 + (TILE_ROWS + WINDOW) * 3 * KV_DIM * 4
        scratch_bytes = (TILE_ROWS + WINDOW) * KV_DIM * 2 + (TILE_ROWS + WINDOW) * 2 * KV_DIM * 4
    return pl.pallas_call(
        functools.partial(_attn_prompt_kernel, swa=swa),
        out_shape=tuple(out_shape),
        grid=(PROMPT_TILES,),
        in_specs=in_specs,
        out_specs=tuple(out_specs),
        scratch_shapes=scratch,
        compiler_params=_params(_vmem_limit(resident, streamed, scratch_bytes, 8 * _ROW_TILE_F32)),
        name="attn_prompt_swa" if swa else "attn_prompt_mem",
    )(*args)


def _attn_sample_kernel(*refs, swa):
    if swa:
        (qm_ref, mk_ref, mv_ref, qa_ref, ck_ref, cv_ref, kn_ref, vn_ref, bias_ref, sink_ref,
         ymem_ref, ytok_ref) = refs
    else:
        qm_ref, mk_ref, mv_ref, ymem_ref = refs

    qm_all = qm_ref[...].astype(F32)
    if swa:
        qa_all = qa_ref[...].astype(F32)
        kn_all = kn_ref[...]
        vn_all = vn_ref[...]
        bias = bias_ref[...]
        sink = sink_ref[...]
        pad = jnp.zeros((SAMPLE_KEYS_PADDED - SAMPLE_KEYS, KV_DIM), F32)
        lo = lax.broadcasted_iota(jnp.int32, (DEC_SEQ, V7X_LANES), 1) < HALF_LANES
        hi = jnp.logical_not(lo)
        zero_slab = jnp.zeros((DEC_SEQ, V7X_LANES), F32)

    ymem_rows, ytok_rows = [], []
    for b in range(SAMPLE_BATCH_BLOCK):
        rows = slice(b * DEC_SEQ, (b + 1) * DEC_SEQ)
        qm = qm_all[rows]
        heads = []
        for h in range(MEM_HEADS):
            s = _dot_nt(qm[:, h * MEM_HEAD_DIM:(h + 1) * MEM_HEAD_DIM], mk_ref[b, :, h, :]) * MEM_SCALE
            p = jnp.exp(s - jnp.max(s, axis=-1, keepdims=True))
            heads.append(_dot(p, mv_ref[b, :, h, :]) / jnp.sum(p, axis=-1, keepdims=True))
        ymem_rows.append(jnp.concatenate(heads, axis=1))
        if not swa:
            continue
        qb = qa_all[rows]
        groups = []
        for j in range(N_HEADS):
            kvh = j // GROUP
            slab = qb[:, (j // 2) * V7X_LANES:(j // 2 + 1) * V7X_LANES]
            if j % 2 != kvh % 2:
                slab = pltpu.roll(slab, HALF_LANES, axis=1)
            slab = jnp.where(lo if kvh % 2 == 0 else hi, slab, 0.0)
            groups.append(jnp.concatenate([slab, zero_slab] if kvh // 2 == 0 else [zero_slab, slab], axis=1))
        q_bd = jnp.concatenate(groups, axis=0)
        k_cat = jnp.concatenate([ck_ref[b], kn_all[rows], pad], axis=0)
        v_cat = jnp.concatenate([cv_ref[b], vn_all[rows], pad], axis=0)
        s = _dot_nt(q_bd, k_cat) + bias
        m = jnp.maximum(jnp.max(s, axis=-1, keepdims=True), sink)
        p = jnp.exp(s - m)
        den = jnp.sum(p, axis=-1, keepdims=True) + jnp.exp(sink - m)
        o_full = _dot(p, v_cat) / den
        pairs = []
        for pair in range(N_HEADS // 2):
            acc = None
            for e in range(2):
                j = pair * 2 + e
                kvh = j // GROUP
                slab = o_full[j * DEC_SEQ:(j + 1) * DEC_SEQ, (kvh // 2) * V7X_LANES:(kvh // 2 + 1) * V7X_LANES]
                if e != kvh % 2:
                    slab = pltpu.roll(slab, HALF_LANES, axis=1)
                slab = jnp.where(lo if e == 0 else hi, slab, 0.0)
                acc = slab if acc is None else acc + slab
            pairs.append(acc)
        ytok_rows.append(jnp.concatenate(pairs, axis=1))

    ymem_ref[...] = jnp.concatenate(ymem_rows, axis=0).astype(BF16)
    if swa:
        ytok_ref[...] = jnp.concatenate(ytok_rows, axis=0).astype(BF16)


def _attn_sample(qm, cache_k, cache_v, layer, swa_args=None):
    swa = swa_args is not None
    row0 = ROWS_PROMPT // SAMPLE_BLOCK_ROWS
    blk_rows = lambda width: pl.BlockSpec((SAMPLE_BLOCK_ROWS, width), lambda i: (row0 + i, 0))
    out_rows = lambda width: pl.BlockSpec((SAMPLE_BLOCK_ROWS, width), lambda i: (i, 0))
    cache_spec = pl.BlockSpec((None, SAMPLE_BATCH_BLOCK, MEM_TOKENS, MEM_HEADS, MEM_HEAD_DIM),
                              lambda i: (layer, i, 0, 0, 0))
    in_specs = [blk_rows(MEM_DIM), cache_spec, cache_spec]
    args = [qm, cache_k, cache_v]
    out_shape = [jax.ShapeDtypeStruct((ROWS_SAMPLE, MEM_DIM), BF16)]
    out_specs = [out_rows(MEM_DIM)]
    streamed = 2 * SAMPLE_BATCH_BLOCK * MEM_TOKENS * MEM_DIM * 4 + SAMPLE_BLOCK_ROWS * MEM_DIM * 4
    if swa:
        qa, swa_k, swa_v, k_new, v_new, bias_s, sink_col = swa_args
        swa_spec = pl.BlockSpec((SAMPLE_BATCH_BLOCK, WINDOW, KV_DIM), lambda i: (i, 0, 0))
        in_specs += [blk_rows(ATTN_DIM), swa_spec, swa_spec, out_rows(KV_DIM), out_rows(KV_DIM),
                     _resident((N_HEADS * DEC_SEQ, SAMPLE_KEYS_PADDED)), _resident((N_HEADS * DEC_SEQ, 1))]
        args += [qa, swa_k, swa_v, k_new, v_new, bias_s, sink_col]
        out_shape += [jax.ShapeDtypeStruct((ROWS_SAMPLE, ATTN_DIM), BF16)]
        out_specs += [out_rows(ATTN_DIM)]
        streamed += 2 * SAMPLE_BATCH_BLOCK * WINDOW * KV_DIM * 4 + SAMPLE_BLOCK_ROWS * (ATTN_DIM + KV_DIM) * 4
    return pl.pallas_call(
        functools.partial(_attn_sample_kernel, swa=swa),
        out_shape=tuple(out_shape),
        grid=(DEC_BATCH // SAMPLE_BATCH_BLOCK,),
        in_specs=in_specs,
        out_specs=tuple(out_specs),
        compiler_params=_params(_vmem_limit(1 << 20, streamed, 0, 6 * _ROW_TILE_F32)),
        name="attn_sample_swa" if swa else "attn_sample_mem",
    )(*args)


def _outffn_kernel(*refs, split_tok, final, tile0):
    refs = list(refs)
    x1_ref = refs.pop(0)
    tok_refs = [refs.pop(0) for _ in range(2 if split_tok else 1)]
    ymp_ref, yms_ref, wo_ref, g_ref, wg_ref, wu_ref, wd_ref = [refs.pop(0) for _ in range(7)]
    gf_ref = refs.pop(0) if final else None
    out_ref, act_ref = refs
    y_tok = _pick_group(*tok_refs, tile0=tile0) if split_tok else tok_refs[0][...]
    y_mem = _pick_group(ymp_ref, yms_ref, tile0=tile0)
    tok_dim = wo_ref.shape[0] - MEM_DIM
    x2 = x1_ref[...] + _dot(y_tok, wo_ref[:tok_dim, :]) + _dot(y_mem, wo_ref[tok_dim:, :])
    x3 = _ffn_half_step(x2, g_ref, wg_ref, wu_ref, wd_ref, act_ref)
    out_ref[...] = _rms(x3, gf_ref[...]) if final else x3


def _outffn(x1, y_tok, ymem_p, ymem_s, wo, wo_layer, g, ffn_w, layer, final_gain=None, tiles=(0, ROW_TILES)):
    split_tok = isinstance(y_tok, tuple)
    final = final_gain is not None
    tile0, n_tiles = tiles
    tok_dim = wo.shape[1] - MEM_DIM
    in_specs = [_row_spec(D_MODEL, tile0)]
    args = [x1]
    if split_tok:
        in_specs += _group_specs(tok_dim, tile0)
        args += list(y_tok)
    else:
        in_specs += [_row_spec(tok_dim, tile0)]
        args += [y_tok]
    in_specs += _group_specs(MEM_DIM, tile0) + [_resident_layer((tok_dim + MEM_DIM, D_MODEL), wo_layer)] + _ffn_specs(layer)
    args += [ymem_p, ymem_s, wo, g.reshape(1, D_MODEL)] + list(ffn_w)
    if final:
        in_specs += [_resident((1, D_MODEL))]
        args += [final_gain.reshape(1, D_MODEL)]
    resident = _FFN_WEIGHT_BYTES + (tok_dim + MEM_DIM) * D_MODEL * 2
    streamed = 2 * _ROW_TILE_F32 + 2 * TILE_ROWS * (tok_dim + MEM_DIM) * 2
    return pl.pallas_call(
        functools.partial(_outffn_kernel, split_tok=split_tok, final=final, tile0=tile0),
        out_shape=jax.ShapeDtypeStruct((n_tiles * TILE_ROWS, D_MODEL), F32),
        grid=(n_tiles,),
        in_specs=in_specs,
        out_specs=_row_spec(D_MODEL),
        scratch_shapes=[pltpu.VMEM((TILE_ROWS, FFN_DIM), BF16)],
        compiler_params=_params(_vmem_limit(resident, streamed, _ACT_BYTES, 7 * _ROW_TILE_F32)),
        name="outffn_final" if final else "outffn",
    )(*args)


def kernel(x_prompt, x_sample, state_conv, cache_swa_k, cache_swa_v, cache_mem_k, cache_mem_v, mem_prompt, ffn1_norm, ffn1_wg, ffn1_wu, ffn1_wd, mix_norm, w_in_a, conv_w, w_out_a, kv_norm, w_kv, w_in_b, attn_sinks, rel_bias, w_out_b, mem_norm, w_mem_kv, ffn2_norm, ffn2_wg, ffn2_wu, ffn2_wd, final_norm):
    bf = lambda w: w.astype(BF16)
    ffn1 = (bf(ffn1_wg), bf(ffn1_wu), bf(ffn1_wd))
    ffn2 = (bf(ffn2_wg), bf(ffn2_wu), bf(ffn2_wd))
    w_in_a_bf, w_out_a_bf, w_in_b_bf, w_out_b_bf = bf(w_in_a), bf(w_out_a), bf(w_in_b), bf(w_out_b)
    wk = w_kv[:, :KV_DIM]
    wv = w_kv[:, KV_DIM:]
    wv2 = jnp.broadcast_to(wv.reshape(D_MODEL, N_KV_HEADS, 1, HEAD_DIM),
                           (D_MODEL, N_KV_HEADS, 2, HEAD_DIM)).reshape(D_MODEL, 2 * KV_DIM)

    mem_k, mem_v, mem_k_bf, mem_v_ext = _memkv(mem_prompt, mem_norm, bf(w_mem_kv))
    bias_p, bias_s = _bias_tables(rel_bias)
    swa_k_cache = cache_swa_k.reshape(DEC_BATCH, WINDOW, KV_DIM)
    swa_v_cache = cache_swa_v.reshape(DEC_BATCH, WINDOW, KV_DIM)

    x = (x_prompt.reshape(ROWS_PROMPT, D_MODEL), x_sample.reshape(ROWS_SAMPLE, D_MODEL))
    tails, sample_us = [], []
    k_rows = kt = v2 = k_new = v_new = None
    for l in range(DEPTH):
        last = l == DEPTH - 1
        if l < N_A_LAYERS:
            (x1,) = _ffn(x, ffn1_norm[l], ffn1, l)
            prefix_rows = jnp.pad(state_conv[l], ((0, 0), (0, DEC_SEQ - (CONV_WIDTH - 1)), (0, 0)))
            y_tok, qm, tail, us = _inproj_conv(x1, mix_norm[l], w_in_a_bf, conv_w,
                                               prefix_rows.reshape(ROWS_SAMPLE, CONV_DIM), l)
            tails.append(tail)
            sample_us.append(us)
            (ymem_p,) = _attn_prompt(qm, mem_k_bf, mem_v_ext, l)
            (ymem_s,) = _attn_sample(qm, cache_mem_k, cache_mem_v, l)
            wo, wo_layer = w_out_a_bf, l
        else:
            j = l - N_A_LAYERS
            if j == 0:
                k_rows, kt, v2 = _kvproj(x, kv_norm, bf(wk), bf(wk.T), bf(wv2))
                k_new = k_rows[ROWS_PROMPT:]
                v_new = v2[ROWS_PROMPT:].reshape(ROWS_SAMPLE, N_KV_HEADS, 2, HEAD_DIM)[:, :, 0, :].reshape(ROWS_SAMPLE, KV_DIM)
            x1, qa, qm = _ffn(x, ffn1_norm[l], ffn1, l, mix_norm[l], w_in_b_bf, j)
            sink_col = jnp.repeat(attn_sinks[j], DEC_SEQ).reshape(N_HEADS * DEC_SEQ, 1)
            ymem_p, ytok_p = _attn_prompt(qm, mem_k_bf, mem_v_ext, l, (qa, kt, v2, bias_p, attn_sinks[j]))
            ymem_s, ytok_s = _attn_sample(qm, cache_mem_k, cache_mem_v, l,
                                          (qa, swa_k_cache, swa_v_cache, k_new, v_new, bias_s, sink_col))
            y_tok = (ytok_p, ytok_s)
            wo, wo_layer = w_out_b_bf, j
        if not last:
            x = _outffn(x1, y_tok, ymem_p, ymem_s, wo, wo_layer, ffn2_norm[l], ffn2, l)
        else:
            y_prompt, y_sample = [
                _outffn(x1, y_tok, ymem_p, ymem_s, wo, wo_layer, ffn2_norm[l], ffn2, l,
                        final_gain=final_norm, tiles=t)
                for t in ((0, PROMPT_TILES), (PROMPT_TILES, SAMPLE_TILES))]

    keep = CONV_WIDTH - 1
    last_tiles = np.arange(BATCH) * TILES_PER_SEQ + TILES_PER_SEQ - 1
    conv_state_prompt = jnp.stack([t[last_tiles, V7X_SUBLANES - keep:, :] for t in tails])
    conv_state_sample = jnp.stack([u.reshape(DEC_BATCH, DEC_SEQ, CONV_DIM)[:, DEC_SEQ - keep:, :] for u in sample_us])
    k_tail = jnp.stack([k_rows[(b + 1) * SEQ - WINDOW:(b + 1) * SEQ] for b in range(BATCH)])
    v_tail = jnp.stack([v2[(b + 1) * SEQ - WINDOW:(b + 1) * SEQ] for b in range(BATCH)])
    swa_k_prompt = k_tail.reshape(BATCH, WINDOW, N_KV_HEADS, HEAD_DIM)
    swa_v_prompt = v_tail.reshape(BATCH, WINDOW, N_KV_HEADS, 2, HEAD_DIM)[:, :, :, 0, :]
    swa_k_sample = jnp.concatenate(
        [cache_swa_k[:, DEC_SEQ:], k_new.reshape(DEC_BATCH, DEC_SEQ, N_KV_HEADS, HEAD_DIM)], axis=1)
    swa_v_sample = jnp.concatenate(
        [cache_swa_v[:, DEC_SEQ:], v_new.reshape(DEC_BATCH, DEC_SEQ, N_KV_HEADS, HEAD_DIM)], axis=1)
    mem_shape = (DEPTH, BATCH, MEM_TOKENS, MEM_HEADS, MEM_HEAD_DIM)
    return (y_prompt.reshape(BATCH, SEQ, D_MODEL), y_sample.reshape(DEC_BATCH, DEC_SEQ, D_MODEL),
            conv_state_prompt, conv_state_sample,
            swa_k_prompt, swa_v_prompt, swa_k_sample, swa_v_sample,
            mem_k.reshape(mem_shape), mem_v.reshape(mem_shape))
```

```python
import functools
import math

import numpy as np
import jax
import jax.numpy as jnp
from jax import lax
from jax.experimental import pallas as pl
from jax.experimental.pallas import tpu as pltpu

D_MODEL = 1024
BATCH = 2
SEQ = 8192
DEPTH = 4
DEC_BATCH = 128
DEC_SEQ = 8
N_A_LAYERS = DEPTH // 2
FFN_DIM = 2816
CONV_DIM = D_MODEL
CONV_WIDTH = 3
N_HEADS = 16
N_KV_HEADS = 4
HEAD_DIM = 64
GROUP = N_HEADS // N_KV_HEADS
ATTN_DIM = N_HEADS * HEAD_DIM
KV_DIM = N_KV_HEADS * HEAD_DIM
WINDOW = 128
REL_BUCKETS = 32
REL_MAX_DIST = 128
MEM_TOKENS = 256
MEM_HEADS = 4
MEM_HEAD_DIM = 128
MEM_DIM = MEM_HEADS * MEM_HEAD_DIM
RMS_EPS = 1e-5

F32 = jnp.float32
BF16 = jnp.bfloat16
NEG_INF = float("-inf")

V7X_LANES = 128
V7X_SUBLANES = 8
V7X_MXU_DIM = 256
V7X_VMEM_BYTES = 64 * 1024 * 1024

ROWS_PROMPT = BATCH * SEQ
ROWS_SAMPLE = DEC_BATCH * DEC_SEQ
ROWS = ROWS_PROMPT + ROWS_SAMPLE
TILE_ROWS = 512
PROMPT_TILES = ROWS_PROMPT // TILE_ROWS
SAMPLE_TILES = ROWS_SAMPLE // TILE_ROWS
ROW_TILES = PROMPT_TILES + SAMPLE_TILES
TILES_PER_SEQ = SEQ // TILE_ROWS
FFN_CHUNK = V7X_MXU_DIM
CONV_CHUNK = V7X_MXU_DIM
BLOCKS_PER_TILE = TILE_ROWS // WINDOW
SAMPLE_BATCH_BLOCK = 8
SAMPLE_BLOCK_ROWS = SAMPLE_BATCH_BLOCK * DEC_SEQ
SAMPLE_KEYS = WINDOW + DEC_SEQ
SAMPLE_KEYS_PADDED = 2 * WINDOW
HALF_LANES = V7X_LANES // 2
MEM_SCALE = MEM_HEAD_DIM ** -0.5
ATTN_SCALE = HEAD_DIM ** -0.5

assert HEAD_DIM == HALF_LANES and MEM_HEAD_DIM == V7X_LANES
assert ROWS_PROMPT % TILE_ROWS == 0 and ROWS_SAMPLE % TILE_ROWS == 0 and SEQ % TILE_ROWS == 0
assert FFN_DIM % FFN_CHUNK == 0 and TILE_ROWS % WINDOW == 0


def _vmem_limit(resident_bytes, streamed_bytes, scratch_bytes, temp_bytes):
    need = resident_bytes + 2 * streamed_bytes + scratch_bytes + temp_bytes
    assert need < V7X_VMEM_BYTES, need
    return int(need)


def _params(vmem_bytes, n_axes=1):
    return pltpu.CompilerParams(
        dimension_semantics=("arbitrary",) * n_axes, vmem_limit_bytes=vmem_bytes)


def _resident(shape):
    zeros = (0,) * len(shape)
    return pl.BlockSpec(shape, lambda *_: zeros, pipeline_mode=pl.Buffered(1))


def _resident_layer(shape, layer):
    idx = (layer,) + (0,) * len(shape)
    return pl.BlockSpec((None,) + tuple(shape), lambda *_: idx, pipeline_mode=pl.Buffered(1))


def _row_spec(width, tile0=0):
    return pl.BlockSpec((TILE_ROWS, width), lambda i: (i + tile0, 0))


def _group_specs(width, tile0=0):
    return [pl.BlockSpec((TILE_ROWS, width), lambda i: (jnp.minimum(i + tile0, PROMPT_TILES - 1), 0)),
            pl.BlockSpec((TILE_ROWS, width), lambda i: (jnp.maximum(i + tile0 - PROMPT_TILES, 0), 0))]


def _pick_group(prompt_ref, sample_ref, tile0=0):
    return jnp.where(pl.program_id(0) + tile0 < PROMPT_TILES, prompt_ref[...], sample_ref[...])


def _dot(a, b):
    return jnp.dot(a, b, preferred_element_type=F32)


def _dot_nt(a, b):
    return lax.dot_general(a, b, (((1,), (1,)), ((), ())), preferred_element_type=F32)


def _rms(x, g):
    return x * lax.rsqrt(jnp.mean(x * x, axis=-1, keepdims=True) + RMS_EPS) * g


def _ffn_half_step(x, g_ref, wg_ref, wu_ref, wd_ref, act_ref):
    h = _rms(x, g_ref[...]).astype(BF16)
    for c in range(FFN_DIM // FFN_CHUNK):
        sl = slice(c * FFN_CHUNK, (c + 1) * FFN_CHUNK)
        gate = _dot(h, wg_ref[:, sl])
        up = _dot(h, wu_ref[:, sl])
        act_ref[:, sl] = (gate / (1.0 + jnp.exp(-gate)) * up).astype(BF16)
    return x + 0.5 * _dot(act_ref[...], wd_ref[...])


_FFN_WEIGHT_BYTES = 3 * D_MODEL * FFN_DIM * 2
_ROW_TILE_F32 = TILE_ROWS * D_MODEL * 4
_ACT_BYTES = TILE_ROWS * FFN_DIM * 2


def _ffn_specs(layer):
    return [_resident((1, D_MODEL)), _resident_layer((D_MODEL, FFN_DIM), layer),
            _resident_layer((D_MODEL, FFN_DIM), layer), _resident_layer((FFN_DIM, D_MODEL), layer)]


def _t5_bucket_np(dist):
    n = np.maximum(dist, 0)
    exact = REL_BUCKETS // 2
    nf = np.maximum(n, 1).astype(np.float32)
    large = exact + (np.log(nf / np.float32(exact)) / np.float32(math.log(REL_MAX_DIST / exact))
                     * np.float32(REL_BUCKETS - exact)).astype(np.int32)
    large = np.minimum(large, REL_BUCKETS - 1)
    return np.where(n < exact, n, large).astype(np.int32)


def _bucket_tables():
    q = np.arange(WINDOW)[:, None]
    k = np.arange(2 * WINDOW)[None, :]
    dist = WINDOW + q - k
    prompt = np.where((dist >= 0) & (dist < WINDOW), _t5_bucket_np(dist), -1)
    t = (np.arange(N_HEADS * DEC_SEQ) % DEC_SEQ)[:, None]
    k = np.arange(SAMPLE_KEYS_PADDED)[None, :]
    dist = WINDOW + t - k
    ok = (dist >= 0) & (dist < WINDOW) & (k < SAMPLE_KEYS)
    sample = np.where(ok, _t5_bucket_np(dist), -1)
    return prompt.astype(np.int32), sample.astype(np.int32)


def _bias_kernel(rel_ref, bp_ref, bs_ref, op_ref, os_ref):
    j = pl.program_id(0)

    def build(bucket):
        acc = jnp.zeros(bucket.shape, F32)
        for b in range(REL_BUCKETS):
            acc = jnp.where(bucket == b, rel_ref[b, j], acc)
        return jnp.where(bucket < 0, NEG_INF, acc)

    op_ref[0] = build(bp_ref[...])
    os_ref[...] = build(bs_ref[...])


def _bias_tables(rel_bias):
    bp, bs = _bucket_tables()
    return pl.pallas_call(
        _bias_kernel,
        out_shape=(jax.ShapeDtypeStruct((N_HEADS, WINDOW, 2 * WINDOW), F32),
                   jax.ShapeDtypeStruct((N_HEADS * DEC_SEQ, SAMPLE_KEYS_PADDED), F32)),
        grid=(N_HEADS,),
        in_specs=[pl.BlockSpec(memory_space=pltpu.SMEM),
                  pl.BlockSpec((WINDOW, 2 * WINDOW), lambda j: (0, 0)),
                  pl.BlockSpec((DEC_SEQ, SAMPLE_KEYS_PADDED), lambda j: (j, 0))],
        out_specs=(pl.BlockSpec((1, WINDOW, 2 * WINDOW), lambda j: (j, 0, 0)),
                   pl.BlockSpec((DEC_SEQ, SAMPLE_KEYS_PADDED), lambda j: (j, 0))),
        name="bias_tables",
    )(rel_bias, jnp.asarray(bp), jnp.asarray(bs))


def _memkv_kernel(m_ref, g_ref, w_ref, k_ref, v_ref, kb_ref, ve_ref):
    hn = _rms(m_ref[...], g_ref[...]).astype(BF16)
    kv = _dot(hn, w_ref[...])
    k = kv[:, :MEM_DIM]
    v = kv[:, MEM_DIM:]
    k_ref[...] = k
    v_ref[...] = v
    kb_ref[...] = k.astype(BF16)
    ones = jnp.ones((MEM_TOKENS, MEM_HEAD_DIM), BF16)
    for h in range(MEM_HEADS):
        vh = v[:, h * MEM_HEAD_DIM:(h + 1) * MEM_HEAD_DIM].astype(BF16)
        ve_ref[h] = jnp.concatenate([vh, ones], axis=1)


def _memkv(mem_prompt, mem_norm, w_mem_kv_bf):
    shp = (DEPTH, BATCH, MEM_TOKENS, MEM_DIM)
    blk = pl.BlockSpec((None, None, MEM_TOKENS, MEM_DIM), lambda l, b: (l, b, 0, 0))
    return pl.pallas_call(
        _memkv_kernel,
        out_shape=(jax.ShapeDtypeStruct(shp, F32), jax.ShapeDtypeStruct(shp, F32),
                   jax.ShapeDtypeStruct(shp, BF16),
                   jax.ShapeDtypeStruct((DEPTH, BATCH, MEM_HEADS, MEM_TOKENS, 2 * MEM_HEAD_DIM), BF16)),
        grid=(DEPTH, BATCH),
        in_specs=[pl.BlockSpec((None, MEM_TOKENS, D_MODEL), lambda l, b: (b, 0, 0)),
                  pl.BlockSpec((None, 1, D_MODEL), lambda l, b: (l, 0, 0)),
                  pl.BlockSpec((None, D_MODEL, 2 * MEM_DIM), lambda l, b: (l, 0, 0))],
        out_specs=(blk, blk, blk,
                   pl.BlockSpec((None, None, MEM_HEADS, MEM_TOKENS, 2 * MEM_HEAD_DIM),
                                lambda l, b: (l, b, 0, 0, 0))),
        compiler_params=_params(32 * 1024 * 1024, 2),
        name="memkv",
    )(mem_prompt, mem_norm.reshape(DEPTH, 1, D_MODEL), w_mem_kv_bf)


def _ffn_kernel(*refs, n_x, attn_proj):
    refs = list(refs)
    x_refs = [refs.pop(0) for _ in range(n_x)]
    g_ref, wg_ref, wu_ref, wd_ref = [refs.pop(0) for _ in range(4)]
    if attn_proj:
        gm_ref, win_ref, x1_ref, qa_ref, qm_ref, act_ref = refs
    else:
        x1_ref, act_ref = refs
    x = _pick_group(*x_refs) if n_x == 2 else x_refs[0][...]
    x1 = _ffn_half_step(x, g_ref, wg_ref, wu_ref, wd_ref, act_ref)
    x1_ref[...] = x1
    if attn_proj:
        hm = _rms(x1, gm_ref[...]).astype(BF16)
        qa_ref[...] = (_dot(hm, win_ref[:, :ATTN_DIM]) * ATTN_SCALE).astype(BF16)
        qm_ref[...] = _dot(hm, win_ref[:, ATTN_DIM:]).astype(BF16)


def _ffn(x, g, ffn_w, layer, gm=None, win=None, win_layer=None):
    attn_proj = win is not None
    xs = list(x) if isinstance(x, tuple) else [x]
    in_specs = (_group_specs(D_MODEL) if len(xs) == 2 else [_row_spec(D_MODEL)]) + _ffn_specs(layer)
    args = xs + [g.reshape(1, D_MODEL)] + list(ffn_w)
    out_shape = [jax.ShapeDtypeStruct((ROWS, D_MODEL), F32)]
    out_specs = [_row_spec(D_MODEL)]
    resident = _FFN_WEIGHT_BYTES
    streamed = (1 + len(xs)) * _ROW_TILE_F32
    if attn_proj:
        in_specs += [_resident((1, D_MODEL)), _resident_layer((D_MODEL, ATTN_DIM + MEM_DIM), win_layer)]
        args += [gm.reshape(1, D_MODEL), win]
        out_shape += [jax.ShapeDtypeStruct((ROWS, ATTN_DIM), BF16), jax.ShapeDtypeStruct((ROWS, MEM_DIM), BF16)]
        out_specs += [_row_spec(ATTN_DIM), _row_spec(MEM_DIM)]
        resident += D_MODEL * (ATTN_DIM + MEM_DIM) * 2
        streamed += TILE_ROWS * (ATTN_DIM + MEM_DIM) * 2
    return pl.pallas_call(
        functools.partial(_ffn_kernel, n_x=len(xs), attn_proj=attn_proj),
        out_shape=tuple(out_shape),
        grid=(ROW_TILES,),
        in_specs=in_specs,
        out_specs=tuple(out_specs),
        scratch_shapes=[pltpu.VMEM((TILE_ROWS, FFN_DIM), BF16)],
        compiler_params=_params(_vmem_limit(resident, streamed, _ACT_BYTES, 6 * _ROW_TILE_F32)),
        name="ffn_attn_proj" if attn_proj else "ffn",
    )(*args)


def _inproj_conv_kernel(x_ref, gm_ref, win_ref, cw_ref, pre_ref,
                        ytok_ref, qm_ref, tail_ref, us_ref, shift_ref):
    i = pl.program_id(0)
    hm = _rms(x_ref[...], gm_ref[...]).astype(BF16)
    qm_ref[...] = _dot(hm, win_ref[:, 3 * CONV_DIM:]).astype(BF16)

    def chunk(cc, prompt):
        sl = slice(cc * CONV_CHUNK, (cc + 1) * CONV_CHUNK)
        c_gate = _dot(hm, win_ref[:, CONV_DIM + cc * CONV_CHUNK:CONV_DIM + (cc + 1) * CONV_CHUNK])
        x_in = _dot(hm, win_ref[:, 2 * CONV_DIM + cc * CONV_CHUNK:2 * CONV_DIM + (cc + 1) * CONV_CHUNK])
        u = c_gate * x_in
        if prompt:
            shift_ref[V7X_SUBLANES:, sl] = u
            u1 = shift_ref[V7X_SUBLANES - 1:V7X_SUBLANES - 1 + TILE_ROWS, sl]
            u2 = shift_ref[V7X_SUBLANES - 2:V7X_SUBLANES - 2 + TILE_ROWS, sl]
            last = u[TILE_ROWS - V7X_SUBLANES:, :]
            shift_ref[:V7X_SUBLANES, sl] = last
            tail_ref[0, :, sl] = last
        else:
            t = lax.broadcasted_iota(jnp.int32, (TILE_ROWS, CONV_CHUNK), 0) % DEC_SEQ
            p2 = pre_ref[:, sl]
            p1 = pltpu.roll(p2, TILE_ROWS - 1, axis=0)
            u1 = jnp.where(t == 0, p1, pltpu.roll(u, 1, axis=0))
            u2 = jnp.where(t < 2, p2, pltpu.roll(u, 2, axis=0))
            us_ref[:, sl] = u
            tail_ref[0, :, sl] = jnp.zeros((V7X_SUBLANES, CONV_CHUNK), F32)
        w = cw_ref[:, sl]
        conv = w[0:1] * u2 + w[1:2] * u1 + w[2:3] * u
        b_gate = _dot(hm, win_ref[:, sl])
        ytok_ref[:, sl] = (b_gate * conv).astype(BF16)

    @pl.when(i < PROMPT_TILES)
    def _():
        @pl.when(i % TILES_PER_SEQ == 0)
        def _():
            shift_ref[:V7X_SUBLANES, :] = jnp.zeros((V7X_SUBLANES, CONV_DIM), F32)
        for cc in range(CONV_DIM // CONV_CHUNK):
            chunk(cc, True)

    @pl.when(i >= PROMPT_TILES)
    def _():
        for cc in range(CONV_DIM // CONV_CHUNK):
            chunk(cc, False)


def _inproj_conv(x1, gm, win, conv_w, prefix_rows, layer):
    sample_idx = lambda i: (jnp.maximum(i - PROMPT_TILES, 0), 0)
    win_bytes = D_MODEL * (3 * CONV_DIM + MEM_DIM) * 2
    return pl.pallas_call(
        _inproj_conv_kernel,
        out_shape=(jax.ShapeDtypeStruct((ROWS, CONV_DIM), BF16),
                   jax.ShapeDtypeStruct((ROWS, MEM_DIM), BF16),
                   jax.ShapeDtypeStruct((ROW_TILES, V7X_SUBLANES, CONV_DIM), F32),
                   jax.ShapeDtypeStruct((ROWS_SAMPLE, CONV_DIM), F32)),
        grid=(ROW_TILES,),
        in_specs=[_row_spec(D_MODEL), _resident((1, D_MODEL)),
                  _resident_layer((D_MODEL, 3 * CONV_DIM + MEM_DIM), layer),
                  _resident_layer((CONV_WIDTH, CONV_DIM), layer),
                  pl.BlockSpec((TILE_ROWS, CONV_DIM), sample_idx)],
        out_specs=(_row_spec(CONV_DIM), _row_spec(MEM_DIM),
                   pl.BlockSpec((1, V7X_SUBLANES, CONV_DIM), lambda i: (i, 0, 0)),
                   pl.BlockSpec((TILE_ROWS, CONV_DIM), sample_idx)),
        scratch_shapes=[pltpu.VMEM((TILE_ROWS + V7X_SUBLANES, CONV_DIM), F32)],
        compiler_params=_params(_vmem_limit(
            win_bytes, 3 * _ROW_TILE_F32 + TILE_ROWS * (CONV_DIM + MEM_DIM) * 2,
            _ROW_TILE_F32 + V7X_SUBLANES * CONV_DIM * 4, 6 * _ROW_TILE_F32)),
        name="inproj_conv",
    )(x1, gm.reshape(1, D_MODEL), win, conv_w, prefix_rows)


def _kvproj_kernel(x_ref, g_ref, wk_ref, wkt_ref, wv2_ref, k_ref, kt_ref, v2_ref):
    hk = _rms(x_ref[...], g_ref[...]).astype(BF16)
    k_ref[...] = _dot(hk, wk_ref[...])
    kt_ref[...] = _dot_nt(wkt_ref[...], hk)
    v2_ref[...] = _dot(hk, wv2_ref[...])


def _kvproj(x, g, wk, wkt, wv2):
    return pl.pallas_call(
        _kvproj_kernel,
        out_shape=(jax.ShapeDtypeStruct((ROWS, KV_DIM), F32),
                   jax.ShapeDtypeStruct((KV_DIM, ROWS), F32),
                   jax.ShapeDtypeStruct((ROWS, 2 * KV_DIM), F32)),
        grid=(ROW_TILES,),
        in_specs=[_row_spec(D_MODEL), _resident((1, D_MODEL)), _resident((D_MODEL, KV_DIM)),
                  _resident((KV_DIM, D_MODEL)), _resident((D_MODEL, 2 * KV_DIM))],
        out_specs=(_row_spec(KV_DIM), pl.BlockSpec((KV_DIM, TILE_ROWS), lambda i: (0, i)),
                   _row_spec(2 * KV_DIM)),
        compiler_params=_params(_vmem_limit(
            4 * D_MODEL * KV_DIM * 2, _ROW_TILE_F32 + TILE_ROWS * 4 * KV_DIM * 4, 0, 4 * _ROW_TILE_F32)),
        name="kvproj",
    )(x, g.reshape(1, D_MODEL), wk, wkt, wv2)


def _attn_prompt_kernel(*refs, swa):
    if swa:
        (qm_ref, mk_ref, mve_ref, qa_ref, ktp_ref, ktc_ref, v2p_ref, v2c_ref, bias_ref, sink_ref,
         ymem_ref, ytok_ref, kt_scr, v2_scr) = refs
    else:
        qm_ref, mk_ref, mve_ref, ymem_ref = refs

    for h in range(MEM_HEADS):
        sl = slice(h * MEM_HEAD_DIM, (h + 1) * MEM_HEAD_DIM)
        s = _dot_nt(qm_ref[:, sl], mk_ref[:, sl]) * MEM_SCALE
        p = jnp.exp(s - jnp.max(s, axis=-1, keepdims=True)).astype(BF16)
        oe = _dot(p, mve_ref[h])
        ymem_ref[:, sl] = (oe[:, :MEM_HEAD_DIM] / oe[:, MEM_HEAD_DIM:]).astype(BF16)
    if not swa:
        return

    first_tile = (pl.program_id(0) % TILES_PER_SEQ) == 0
    kt_scr[0] = ktp_ref[...].astype(BF16)
    for m in range(BLOCKS_PER_TILE):
        kt_scr[m + 1] = ktc_ref[:, m * WINDOW:(m + 1) * WINDOW].astype(BF16)
    v2_scr[:WINDOW, :] = v2p_ref[...]
    v2_scr[WINDOW:, :] = v2c_ref[...]

    lo_k = lax.broadcasted_iota(jnp.int32, (2 * WINDOW, V7X_LANES), 1) < HALF_LANES
    ones_lo = jnp.where(lo_k, 1.0, 0.0).astype(BF16)
    ones_hi = jnp.where(lo_k, 0.0, 1.0).astype(BF16)
    lo_q = lax.broadcasted_iota(jnp.int32, (WINDOW, V7X_LANES), 1) < HALF_LANES
    prev_cols = lax.broadcasted_iota(jnp.int32, (WINDOW, 2 * WINDOW), 1) < WINDOW
    zero_k = jnp.zeros((HEAD_DIM, 2 * WINDOW), BF16)

    def block(n, carry):
        r0 = pl.multiple_of(n * WINDOW, WINDOW)
        kill = jnp.logical_and(jnp.logical_and(first_tile, n == 0), prev_cols)
        for h in range(N_KV_HEADS):
            rows = slice(h * HEAD_DIM, (h + 1) * HEAD_DIM)
            kth = jnp.concatenate([kt_scr[n, rows, :], kt_scr[n + 1, rows, :]], axis=1)
            k_sel = (jnp.concatenate([kth, zero_k], axis=0), jnp.concatenate([zero_k, kth], axis=0))
            v2h = v2_scr[pl.ds(r0, 2 * WINDOW), h * V7X_LANES:(h + 1) * V7X_LANES]
            v_sel = (jnp.concatenate([jnp.where(lo_k, v2h, 0.0).astype(BF16), ones_lo], axis=1),
                     jnp.concatenate([jnp.where(lo_k, 0.0, v2h).astype(BF16), ones_hi], axis=1))
            for pr in range(GROUP // 2):
                c0 = (h * (GROUP // 2) + pr) * V7X_LANES
                qp = qa_ref[pl.ds(r0, WINDOW), c0:c0 + V7X_LANES]
                acc = None
                sink_terms = []
                for e in range(2):
                    j = h * GROUP + pr * 2 + e
                    s = _dot(qp, k_sel[e]) + bias_ref[j]
                    s = jnp.where(kill, NEG_INF, s)
                    sink = sink_ref[j]
                    m = jnp.maximum(jnp.max(s, axis=-1, keepdims=True), sink)
                    p = jnp.exp(s - m).astype(BF16)
                    part = _dot(p, v_sel[e])
                    acc = part if acc is None else acc + part
                    sink_terms.append(jnp.exp(sink - m))
                den = acc[:, V7X_LANES:] + jnp.where(lo_q, sink_terms[0], sink_terms[1])
                ytok_ref[pl.ds(r0, WINDOW), c0:c0 + V7X_LANES] = (acc[:, :V7X_LANES] / den).astype(BF16)
        return carry

    lax.fori_loop(0, BLOCKS_PER_TILE, block, 0)


def _attn_prompt(qm, mk_bf, mv_ext, layer, swa_args=None):
    swa = swa_args is not None
    batch_of = lambda i: i // TILES_PER_SEQ
    in_specs = [_row_spec(MEM_DIM),
                pl.BlockSpec((None, None, MEM_TOKENS, MEM_DIM), lambda i: (layer, batch_of(i), 0, 0)),
                pl.BlockSpec((None, None, MEM_HEADS, MEM_TOKENS, 2 * MEM_HEAD_DIM),
                             lambda i: (layer, batch_of(i), 0, 0, 0))]
    args = [qm, mk_bf, mv_ext]
    out_shape = [jax.ShapeDtypeStruct((ROWS_PROMPT, MEM_DIM), BF16)]
    out_specs = [_row_spec(MEM_DIM)]
    scratch = []
    streamed = TILE_ROWS * MEM_DIM * 4 + MEM_TOKENS * MEM_DIM * 2 * 3
    resident = 0
    scratch_bytes = 0
    if swa:
        qa, kt, v2, bias_p, sinks = swa_args
        prev_blk = lambda i: jnp.maximum(i * BLOCKS_PER_TILE - 1, 0)
        in_specs += [_row_spec(ATTN_DIM),
                     pl.BlockSpec((KV_DIM, WINDOW), lambda i: (0, prev_blk(i))),
                     pl.BlockSpec((KV_DIM, TILE_ROWS), lambda i: (0, i)),
                     pl.BlockSpec((WINDOW, 2 * KV_DIM), lambda i: (prev_blk(i), 0)),
                     pl.BlockSpec((TILE_ROWS, 2 * KV_DIM), lambda i: (i, 0)),
                     _resident((N_HEADS, WINDOW, 2 * WINDOW)),
                     pl.BlockSpec(memory_space=pltpu.SMEM)]
        args += [qa, kt, kt, v2, v2, bias_p, sinks]
        out_shape += [jax.ShapeDtypeStruct((ROWS_PROMPT, ATTN_DIM), BF16)]
        out_specs += [_row_spec(ATTN_DIM)]
        scratch = [pltpu.VMEM((BLOCKS_PER_TILE + 1, KV_DIM, WINDOW), BF16),
                   pltpu.VMEM((TILE_ROWS + WINDOW, 2 * KV_DIM), F32)]
        resident = N_HEADS * WINDOW * 2 * WINDOW * 4
        streamed += 2 * TILE_ROWS * ATTN_DIM * 2 + (TILE_ROWS + WINDOW) * 3 * KV_DIM * 4
        scratch_bytes = (TILE_ROWS + WINDOW) * KV_DIM * 2 + (TILE_ROWS + WINDOW) * 2 * KV_DIM * 4
    return pl.pallas_call(
        functools.partial(_attn_prompt_kernel, swa=swa),
        out_shape=tuple(out_shape),
        grid=(PROMPT_TILES,),
        in_specs=in_specs,
        out_specs=tuple(out_specs),
        scratch_shapes=scratch,
        compiler_params=_params(_vmem_limit(resident, streamed, scratch_bytes, 8 * _ROW_TILE_F32)),
        name="attn_prompt_swa" if swa else "attn_prompt_mem",
    )(*args)


def _attn_sample_kernel(*refs, swa):
    if swa:
        (qm_ref, mk_ref, mv_ref, qa_ref, ck_ref, cv_ref, kn_ref, vn_ref, bias_ref, sink_ref,
         ymem_ref, ytok_ref) = refs
    else:
        qm_ref, mk_ref, mv_ref, ymem_ref = refs

    qm_all = qm_ref[...].astype(F32)
    mem_rows = MEM_HEADS * DEC_SEQ
    own_head = (lax.broadcasted_iota(jnp.int32, (mem_rows, MEM_TOKENS * MEM_HEADS), 1) % MEM_HEADS
                == lax.broadcasted_iota(jnp.int32, (mem_rows, MEM_TOKENS * MEM_HEADS), 0) // DEC_SEQ)
    if swa:
        qa_all = qa_ref[...].astype(F32)
        kn_all = kn_ref[...]
        vn_all = vn_ref[...]
        bias = bias_ref[...]
        sink = sink_ref[...]
        pad = jnp.zeros((SAMPLE_KEYS_PADDED - SAMPLE_KEYS, KV_DIM), F32)
        lo = lax.broadcasted_iota(jnp.int32, (DEC_SEQ, V7X_LANES), 1) < HALF_LANES
        hi = jnp.logical_not(lo)
        zero_slab = jnp.zeros((DEC_SEQ, V7X_LANES), F32)

    ymem_rows, ytok_rows = [], []
    for b in range(SAMPLE_BATCH_BLOCK):
        rows = slice(b * DEC_SEQ, (b + 1) * DEC_SEQ)
        qm = qm_all[rows]
        q_heads = jnp.concatenate(
            [qm[:, h * MEM_HEAD_DIM:(h + 1) * MEM_HEAD_DIM] for h in range(MEM_HEADS)], axis=0)
        s = jnp.where(own_head, _dot_nt(q_heads, mk_ref[b]) * MEM_SCALE, NEG_INF)
        p = jnp.exp(s - jnp.max(s, axis=-1, keepdims=True))
        o = _dot(p, mv_ref[b]) / jnp.sum(p, axis=-1, keepdims=True)
        ymem_rows.append(jnp.concatenate(
            [o[h * DEC_SEQ:(h + 1) * DEC_SEQ] for h in range(MEM_HEADS)], axis=1))
        if not swa:
            continue
        qb = qa_all[rows]
        groups = []
        for j in range(N_HEADS):
            kvh = j // GROUP
            slab = qb[:, (j // 2) * V7X_LANES:(j // 2 + 1) * V7X_LANES]
            if j % 2 != kvh % 2:
                slab = pltpu.roll(slab, HALF_LANES, axis=1)
            slab = jnp.where(lo if kvh % 2 == 0 else hi, slab, 0.0)
            groups.append(jnp.concatenate([slab, zero_slab] if kvh // 2 == 0 else [zero_slab, slab], axis=1))
        q_bd = jnp.concatenate(groups, axis=0)
        k_cat = jnp.concatenate([ck_ref[b], kn_all[rows], pad], axis=0)
        v_cat = jnp.concatenate([cv_ref[b], vn_all[rows], pad], axis=0)
        s = _dot_nt(q_bd, k_cat) + bias
        m = jnp.maximum(jnp.max(s, axis=-1, keepdims=True), sink)
        p = jnp.exp(s - m)
        den = jnp.sum(p, axis=-1, keepdims=True) + jnp.exp(sink - m)
        o_full = _dot(p, v_cat) / den
        pairs = []
        for pair in range(N_HEADS // 2):
            acc = None
            for e in range(2):
                j = pair * 2 + e
                kvh = j // GROUP
                slab = o_full[j * DEC_SEQ:(j + 1) * DEC_SEQ, (kvh // 2) * V7X_LANES:(kvh // 2 + 1) * V7X_LANES]
                if e != kvh % 2:
                    slab = pltpu.roll(slab, HALF_LANES, axis=1)
                slab = jnp.where(lo if e == 0 else hi, slab, 0.0)
                acc = slab if acc is None else acc + slab
            pairs.append(acc)
        ytok_rows.append(jnp.concatenate(pairs, axis=1))

    ymem_ref[...] = jnp.concatenate(ymem_rows, axis=0).astype(BF16)
    if swa:
        ytok_ref[...] = jnp.concatenate(ytok_rows, axis=0).astype(BF16)


def _attn_sample(qm, cache_k, cache_v, layer, swa_args=None):
    swa = swa_args is not None
    row0 = ROWS_PROMPT // SAMPLE_BLOCK_ROWS
    blk_rows = lambda width: pl.BlockSpec((SAMPLE_BLOCK_ROWS, width), lambda i: (row0 + i, 0))
    out_rows = lambda width: pl.BlockSpec((SAMPLE_BLOCK_ROWS, width), lambda i: (i, 0))
    cache_spec = pl.BlockSpec((None, SAMPLE_BATCH_BLOCK, MEM_TOKENS * MEM_HEADS, MEM_HEAD_DIM),
                              lambda i: (layer, i, 0, 0))
    in_specs = [blk_rows(MEM_DIM), cache_spec, cache_spec]
    args = [qm, cache_k, cache_v]
    out_shape = [jax.ShapeDtypeStruct((ROWS_SAMPLE, MEM_DIM), BF16)]
    out_specs = [out_rows(MEM_DIM)]
    streamed = 2 * SAMPLE_BATCH_BLOCK * MEM_TOKENS * MEM_DIM * 4 + SAMPLE_BLOCK_ROWS * MEM_DIM * 4
    if swa:
        qa, swa_k, swa_v, k_new, v_new, bias_s, sink_col = swa_args
        swa_spec = pl.BlockSpec((SAMPLE_BATCH_BLOCK, WINDOW, KV_DIM), lambda i: (i, 0, 0))
        in_specs += [blk_rows(ATTN_DIM), swa_spec, swa_spec, out_rows(KV_DIM), out_rows(KV_DIM),
                     _resident((N_HEADS * DEC_SEQ, SAMPLE_KEYS_PADDED)), _resident((N_HEADS * DEC_SEQ, 1))]
        args += [qa, swa_k, swa_v, k_new, v_new, bias_s, sink_col]
        out_shape += [jax.ShapeDtypeStruct((ROWS_SAMPLE, ATTN_DIM), BF16)]
        out_specs += [out_rows(ATTN_DIM)]
        streamed += 2 * SAMPLE_BATCH_BLOCK * WINDOW * KV_DIM * 4 + SAMPLE_BLOCK_ROWS * (ATTN_DIM + KV_DIM) * 4
    return pl.pallas_call(
        functools.partial(_attn_sample_kernel, swa=swa),
        out_shape=tuple(out_shape),
        grid=(DEC_BATCH // SAMPLE_BATCH_BLOCK,),
        in_specs=in_specs,
        out_specs=tuple(out_specs),
        compiler_params=_params(_vmem_limit(1 << 20, streamed, 0, 6 * _ROW_TILE_F32)),
        name="attn_sample_swa" if swa else "attn_sample_mem",
    )(*args)


def _outffn_kernel(*refs, split_tok, final, tile0):
    refs = list(refs)
    x1_ref = refs.pop(0)
    tok_refs = [refs.pop(0) for _ in range(2 if split_tok else 1)]
    ymp_ref, yms_ref, wo_ref, g_ref, wg_ref, wu_ref, wd_ref = [refs.pop(0) for _ in range(7)]
    gf_ref = refs.pop(0) if final else None
    out_ref, act_ref = refs
    y_tok = _pick_group(*tok_refs, tile0=tile0) if split_tok else tok_refs[0][...]
    y_mem = _pick_group(ymp_ref, yms_ref, tile0=tile0)
    tok_dim = wo_ref.shape[0] - MEM_DIM
    x2 = x1_ref[...] + _dot(y_tok, wo_ref[:tok_dim, :]) + _dot(y_mem, wo_ref[tok_dim:, :])
    x3 = _ffn_half_step(x2, g_ref, wg_ref, wu_ref, wd_ref, act_ref)
    out_ref[...] = _rms(x3, gf_ref[...]) if final else x3


def _outffn(x1, y_tok, ymem_p, ymem_s, wo, wo_layer, g, ffn_w, layer, final_gain=None, tiles=(0, ROW_TILES)):
    split_tok = isinstance(y_tok, tuple)
    final = final_gain is not None
    tile0, n_tiles = tiles
    tok_dim = wo.shape[1] - MEM_DIM
    in_specs = [_row_spec(D_MODEL, tile0)]
    args = [x1]
    if split_tok:
        in_specs += _group_specs(tok_dim, tile0)
        args += list(y_tok)
    else:
        in_specs += [_row_spec(tok_dim, tile0)]
        args += [y_tok]
    in_specs += _group_specs(MEM_DIM, tile0) + [_resident_layer((tok_dim + MEM_DIM, D_MODEL), wo_layer)] + _ffn_specs(layer)
    args += [ymem_p, ymem_s, wo, g.reshape(1, D_MODEL)] + list(ffn_w)
    if final:
        in_specs += [_resident((1, D_MODEL))]
        args += [final_gain.reshape(1, D_MODEL)]
    resident = _FFN_WEIGHT_BYTES + (tok_dim + MEM_DIM) * D_MODEL * 2
    streamed = 2 * _ROW_TILE_F32 + 2 * TILE_ROWS * (tok_dim + MEM_DIM) * 2
    return pl.pallas_call(
        functools.partial(_outffn_kernel, split_tok=split_tok, final=final, tile0=tile0),
        out_shape=jax.ShapeDtypeStruct((n_tiles * TILE_ROWS, D_MODEL), F32),
        grid=(n_tiles,),
        in_specs=in_specs,
        out_specs=_row_spec(D_MODEL),
        scratch_shapes=[pltpu.VMEM((TILE_ROWS, FFN_DIM), BF16)],
        compiler_params=_params(_vmem_limit(resident, streamed, _ACT_BYTES, 7 * _ROW_TILE_F32)),
        name="outffn_final" if final else "outffn",
    )(*args)


def kernel(x_prompt, x_sample, state_conv, cache_swa_k, cache_swa_v, cache_mem_k, cache_mem_v, mem_prompt, ffn1_norm, ffn1_wg, ffn1_wu, ffn1_wd, mix_norm, w_in_a, conv_w, w_out_a, kv_norm, w_kv, w_in_b, attn_sinks, rel_bias, w_out_b, mem_norm, w_mem_kv, ffn2_norm, ffn2_wg, ffn2_wu, ffn2_wd, final_norm):
    bf = lambda w: w.astype(BF16)
    ffn1 = (bf(ffn1_wg), bf(ffn1_wu), bf(ffn1_wd))
    ffn2 = (bf(ffn2_wg), bf(ffn2_wu), bf(ffn2_wd))
    w_in_a_bf, w_out_a_bf, w_in_b_bf, w_out_b_bf = bf(w_in_a), bf(w_out_a), bf(w_in_b), bf(w_out_b)
    wk = w_kv[:, :KV_DIM]
    wv = w_kv[:, KV_DIM:]
    wv2 = jnp.broadcast_to(wv.reshape(D_MODEL, N_KV_HEADS, 1, HEAD_DIM),
                           (D_MODEL, N_KV_HEADS, 2, HEAD_DIM)).reshape(D_MODEL, 2 * KV_DIM)

    mem_k, mem_v, mem_k_bf, mem_v_ext = _memkv(mem_prompt, mem_norm, bf(w_mem_kv))
    bias_p, bias_s = _bias_tables(rel_bias)
    swa_k_cache = cache_swa_k.reshape(DEC_BATCH, WINDOW, KV_DIM)
    swa_v_cache = cache_swa_v.reshape(DEC_BATCH, WINDOW, KV_DIM)
    mem_rows_shape = (DEPTH, DEC_BATCH, MEM_TOKENS * MEM_HEADS, MEM_HEAD_DIM)
    cache_k = cache_mem_k.reshape(mem_rows_shape)
    cache_v = cache_mem_v.reshape(mem_rows_shape)

    x = (x_prompt.reshape(ROWS_PROMPT, D_MODEL), x_sample.reshape(ROWS_SAMPLE, D_MODEL))
    tails, sample_us = [], []
    k_rows = kt = v2 = k_new = v_new = None
    for l in range(DEPTH):
        last = l == DEPTH - 1
        if l < N_A_LAYERS:
            (x1,) = _ffn(x, ffn1_norm[l], ffn1, l)
            prefix_rows = jnp.pad(state_conv[l], ((0, 0), (0, DEC_SEQ - (CONV_WIDTH - 1)), (0, 0)))
            y_tok, qm, tail, us = _inproj_conv(x1, mix_norm[l], w_in_a_bf, conv_w,
                                               prefix_rows.reshape(ROWS_SAMPLE, CONV_DIM), l)
            tails.append(tail)
            sample_us.append(us)
            (ymem_p,) = _attn_prompt(qm, mem_k_bf, mem_v_ext, l)
            (ymem_s,) = _attn_sample(qm, cache_k, cache_v, l)
            wo, wo_layer = w_out_a_bf, l
        else:
            j = l - N_A_LAYERS
            if j == 0:
                k_rows, kt, v2 = _kvproj(x, kv_norm, bf(wk), bf(wk.T), bf(wv2))
                k_new = k_rows[ROWS_PROMPT:]
                v_new = v2[ROWS_PROMPT:].reshape(ROWS_SAMPLE, N_KV_HEADS, 2, HEAD_DIM)[:, :, 0, :].reshape(ROWS_SAMPLE, KV_DIM)
            x1, qa, qm = _ffn(x, ffn1_norm[l], ffn1, l, mix_norm[l], w_in_b_bf, j)
            sink_col = jnp.repeat(attn_sinks[j], DEC_SEQ).reshape(N_HEADS * DEC_SEQ, 1)
            ymem_p, ytok_p = _attn_prompt(qm, mem_k_bf, mem_v_ext, l, (qa, kt, v2, bias_p, attn_sinks[j]))
            ymem_s, ytok_s = _attn_sample(qm, cache_k, cache_v, l,
                                          (qa, swa_k_cache, swa_v_cache, k_new, v_new, bias_s, sink_col))
            y_tok = (ytok_p, ytok_s)
            wo, wo_layer = w_out_b_bf, j
        if not last:
            x = _outffn(x1, y_tok, ymem_p, ymem_s, wo, wo_layer, ffn2_norm[l], ffn2, l)
        else:
            y_prompt, y_sample = [
                _outffn(x1, y_tok, ymem_p, ymem_s, wo, wo_layer, ffn2_norm[l], ffn2, l,
                        final_gain=final_norm, tiles=t)
                for t in ((0, PROMPT_TILES), (PROMPT_TILES, SAMPLE_TILES))]

    keep = CONV_WIDTH - 1
    last_tiles = np.arange(BATCH) * TILES_PER_SEQ + TILES_PER_SEQ - 1
    conv_state_prompt = jnp.stack([t[last_tiles, V7X_SUBLANES - keep:, :] for t in tails])
    conv_state_sample = jnp.stack([u.reshape(DEC_BATCH, DEC_SEQ, CONV_DIM)[:, DEC_SEQ - keep:, :] for u in sample_us])
    k_tail = jnp.stack([k_rows[(b + 1) * SEQ - WINDOW:(b + 1) * SEQ] for b in range(BATCH)])
    v_tail = jnp.stack([v2[(b + 1) * SEQ - WINDOW:(b + 1) * SEQ] for b in range(BATCH)])
    swa_k_prompt = k_tail.reshape(BATCH, WINDOW, N_KV_HEADS, HEAD_DIM)
    swa_v_prompt = v_tail.reshape(BATCH, WINDOW, N_KV_HEADS, 2, HEAD_DIM)[:, :, :, 0, :]
    swa_k_sample = jnp.concatenate(
        [cache_swa_k[:, DEC_SEQ:], k_new.reshape(DEC_BATCH, DEC_SEQ, N_KV_HEADS, HEAD_DIM)], axis=1)
    swa_v_sample = jnp.concatenate(
        [cache_swa_v[:, DEC_SEQ:], v_new.reshape(DEC_BATCH, DEC_SEQ, N_KV_HEADS, HEAD_DIM)], axis=1)
    mem_shape = (DEPTH, BATCH, MEM_TOKENS, MEM_HEADS, MEM_HEAD_DIM)
    return (y_prompt.reshape(BATCH, SEQ, D_MODEL), y_sample.reshape(DEC_BATCH, DEC_SEQ, D_MODEL),
            conv_state_prompt, conv_state_sample,
            swa_k_prompt, swa_v_prompt, swa_k_sample, swa_v_sample,
            mem_k.reshape(mem_shape), mem_v.reshape(mem_shape))
```

```python
import functools
import math

import numpy as np
import jax
import jax.numpy as jnp
from jax import lax
from jax.experimental import pallas as pl
from jax.experimental.pallas import tpu as pltpu

D_MODEL = 1024
BATCH = 2
SEQ = 8192
DEPTH = 4
DEC_BATCH = 128
DEC_SEQ = 8
N_A_LAYERS = DEPTH // 2
FFN_DIM = 2816
CONV_DIM = D_MODEL
CONV_WIDTH = 3
N_HEADS = 16
N_KV_HEADS = 4
HEAD_DIM = 64
GROUP = N_HEADS // N_KV_HEADS
ATTN_DIM = N_HEADS * HEAD_DIM
KV_DIM = N_KV_HEADS * HEAD_DIM
WINDOW = 128
REL_BUCKETS = 32
REL_MAX_DIST = 128
MEM_TOKENS = 256
MEM_HEADS = 4
MEM_HEAD_DIM = 128
MEM_DIM = MEM_HEADS * MEM_HEAD_DIM
RMS_EPS = 1e-5

F32 = jnp.float32
BF16 = jnp.bfloat16
NEG_INF = float("-inf")

V7X_LANES = 128
V7X_SUBLANES = 8
V7X_MXU_DIM = 256
V7X_VMEM_BYTES = 64 * 1024 * 1024

ROWS_PROMPT = BATCH * SEQ
ROWS_SAMPLE = DEC_BATCH * DEC_SEQ
ROWS = ROWS_PROMPT + ROWS_SAMPLE
TILE_ROWS = 512
PROMPT_TILES = ROWS_PROMPT // TILE_ROWS
SAMPLE_TILES = ROWS_SAMPLE // TILE_ROWS
ROW_TILES = PROMPT_TILES + SAMPLE_TILES
TILES_PER_SEQ = SEQ // TILE_ROWS
FFN_CHUNK = V7X_MXU_DIM
CONV_CHUNK = V7X_MXU_DIM
BLOCKS_PER_TILE = TILE_ROWS // WINDOW
SAMPLE_BATCH_BLOCK = 8
SAMPLE_BLOCK_ROWS = SAMPLE_BATCH_BLOCK * DEC_SEQ
SAMPLE_KEYS = WINDOW + DEC_SEQ
SAMPLE_KEYS_PADDED = 2 * WINDOW
HALF_LANES = V7X_LANES // 2
LOG2E = math.log2(math.e)
MEM_Q_SCALE = MEM_HEAD_DIM ** -0.5 * LOG2E
ATTN_Q_SCALE = HEAD_DIM ** -0.5 * LOG2E

assert HEAD_DIM == HALF_LANES and MEM_HEAD_DIM == V7X_LANES
assert ROWS_PROMPT % TILE_ROWS == 0 and ROWS_SAMPLE % TILE_ROWS == 0 and SEQ % TILE_ROWS == 0
assert FFN_DIM % FFN_CHUNK == 0 and TILE_ROWS % WINDOW == 0


def _vmem_limit(resident_bytes, streamed_bytes, scratch_bytes, temp_bytes):
    need = resident_bytes + 2 * streamed_bytes + scratch_bytes + temp_bytes
    assert need < V7X_VMEM_BYTES, need
    return int(need)


def _params(vmem_bytes, n_axes=1):
    return pltpu.CompilerParams(
        dimension_semantics=("arbitrary",) * n_axes, vmem_limit_bytes=vmem_bytes)


def _resident(shape):
    zeros = (0,) * len(shape)
    return pl.BlockSpec(shape, lambda *_: zeros, pipeline_mode=pl.Buffered(1))


def _resident_layer(shape, layer):
    idx = (layer,) + (0,) * len(shape)
    return pl.BlockSpec((None,) + tuple(shape), lambda *_: idx, pipeline_mode=pl.Buffered(1))


def _row_spec(width, tile0=0):
    return pl.BlockSpec((TILE_ROWS, width), lambda i: (i + tile0, 0))


def _group_specs(width, tile0=0):
    return [pl.BlockSpec((TILE_ROWS, width), lambda i: (jnp.minimum(i + tile0, PROMPT_TILES - 1), 0)),
            pl.BlockSpec((TILE_ROWS, width), lambda i: (jnp.maximum(i + tile0 - PROMPT_TILES, 0), 0))]


def _pick_group(prompt_ref, sample_ref, tile0=0):
    return jnp.where(pl.program_id(0) + tile0 < PROMPT_TILES, prompt_ref[...], sample_ref[...])


def _dot(a, b):
    return jnp.dot(a, b, preferred_element_type=F32)


def _dot_nt(a, b):
    return lax.dot_general(a, b, (((1,), (1,)), ((), ())), preferred_element_type=F32)


def _rms(x, g):
    return x * lax.rsqrt(jnp.mean(x * x, axis=-1, keepdims=True) + RMS_EPS) * g


def _ffn_half_step(x, g_ref, wg_ref, wu_ref, wd_ref, act_ref):
    h = _rms(x, g_ref[...]).astype(BF16)
    for c in range(FFN_DIM // FFN_CHUNK):
        sl = slice(c * FFN_CHUNK, (c + 1) * FFN_CHUNK)
        gate = _dot(h, wg_ref[:, sl])
        up = _dot(h, wu_ref[:, sl])
        act_ref[:, sl] = (gate / (1.0 + jnp.exp(-gate)) * up).astype(BF16)
    return x + 0.5 * _dot(act_ref[...], wd_ref[...])


_FFN_WEIGHT_BYTES = 3 * D_MODEL * FFN_DIM * 2
_ROW_TILE_F32 = TILE_ROWS * D_MODEL * 4
_ACT_BYTES = TILE_ROWS * FFN_DIM * 2


def _ffn_specs(layer):
    return [_resident((1, D_MODEL)), _resident_layer((D_MODEL, FFN_DIM), layer),
            _resident_layer((D_MODEL, FFN_DIM), layer), _resident_layer((FFN_DIM, D_MODEL), layer)]


def _t5_bucket_np(dist):
    n = np.maximum(dist, 0)
    exact = REL_BUCKETS // 2
    nf = np.maximum(n, 1).astype(np.float32)
    large = exact + (np.log(nf / np.float32(exact)) / np.float32(math.log(REL_MAX_DIST / exact))
                     * np.float32(REL_BUCKETS - exact)).astype(np.int32)
    large = np.minimum(large, REL_BUCKETS - 1)
    return np.where(n < exact, n, large).astype(np.int32)


def _bucket_tables():
    q = np.arange(WINDOW)[:, None]
    k = np.arange(2 * WINDOW)[None, :]
    dist = WINDOW + q - k
    prompt = np.where((dist >= 0) & (dist < WINDOW), _t5_bucket_np(dist), -1)
    t = (np.arange(N_HEADS * DEC_SEQ) % DEC_SEQ)[:, None]
    k = np.arange(SAMPLE_KEYS_PADDED)[None, :]
    dist = WINDOW + t - k
    ok = (dist >= 0) & (dist < WINDOW) & (k < SAMPLE_KEYS)
    sample = np.where(ok, _t5_bucket_np(dist), -1)
    return prompt.astype(np.int32), sample.astype(np.int32)


def _bias_kernel(rel_ref, sink_ref, bp_ref, bs_ref, op_ref, os_ref):
    layer = pl.program_id(0)
    j = pl.program_id(1)
    sink = sink_ref[layer, j] * LOG2E

    def build(bucket):
        acc = jnp.zeros(bucket.shape, F32)
        for b in range(REL_BUCKETS):
            acc = jnp.where(bucket == b, rel_ref[b, j], acc)
        return jnp.where(bucket < 0, NEG_INF, acc * LOG2E)

    col_p = lax.broadcasted_iota(jnp.int32, (WINDOW, 2 * WINDOW), 1)
    table = build(bp_ref[...])
    op_ref[0, 0, 0] = jnp.where(col_p == 0, sink, table)
    op_ref[0, 1, 0] = jnp.where(col_p == 0, sink, jnp.where(col_p < WINDOW, NEG_INF, table))
    col_s = lax.broadcasted_iota(jnp.int32, (DEC_SEQ, SAMPLE_KEYS_PADDED), 1)
    os_ref[0] = jnp.where(col_s == 0, sink, build(bs_ref[...]))


def _bias_tables(rel_bias, attn_sinks):
    bp, bs = _bucket_tables()
    n_layers = attn_sinks.shape[0]
    return pl.pallas_call(
        _bias_kernel,
        out_shape=(jax.ShapeDtypeStruct((n_layers, 2, N_HEADS, WINDOW, 2 * WINDOW), F32),
                   jax.ShapeDtypeStruct((n_layers, N_HEADS * DEC_SEQ, SAMPLE_KEYS_PADDED), F32)),
        grid=(n_layers, N_HEADS),
        in_specs=[pl.BlockSpec(memory_space=pltpu.SMEM), pl.BlockSpec(memory_space=pltpu.SMEM),
                  pl.BlockSpec((WINDOW, 2 * WINDOW), lambda l, j: (0, 0)),
                  pl.BlockSpec((DEC_SEQ, SAMPLE_KEYS_PADDED), lambda l, j: (j, 0))],
        out_specs=(pl.BlockSpec((1, 2, 1, WINDOW, 2 * WINDOW), lambda l, j: (l, 0, j, 0, 0)),
                   pl.BlockSpec((1, DEC_SEQ, SAMPLE_KEYS_PADDED), lambda l, j: (l, j, 0))),
        compiler_params=_params(16 * 1024 * 1024, 2),
        name="bias_tables",
    )(rel_bias, attn_sinks, jnp.asarray(bp), jnp.asarray(bs))


def _memkv_kernel(m_ref, g_ref, w_ref, k_ref, v_ref, kb_ref, ve_ref):
    hn = _rms(m_ref[...], g_ref[...]).astype(BF16)
    kv = _dot(hn, w_ref[...])
    k = kv[:, :MEM_DIM]
    v = kv[:, MEM_DIM:]
    k_ref[...] = k
    v_ref[...] = v
    kb_ref[...] = k.astype(BF16)
    ones = jnp.ones((MEM_TOKENS, MEM_HEAD_DIM), BF16)
    for h in range(MEM_HEADS):
        vh = v[:, h * MEM_HEAD_DIM:(h + 1) * MEM_HEAD_DIM].astype(BF16)
        ve_ref[h] = jnp.concatenate([vh, ones], axis=1)


def _memkv(mem_prompt, mem_norm, w_mem_kv_bf):
    shp = (DEPTH, BATCH, MEM_TOKENS, MEM_DIM)
    blk = pl.BlockSpec((None, None, MEM_TOKENS, MEM_DIM), lambda l, b: (l, b, 0, 0))
    return pl.pallas_call(
        _memkv_kernel,
        out_shape=(jax.ShapeDtypeStruct(shp, F32), jax.ShapeDtypeStruct(shp, F32),
                   jax.ShapeDtypeStruct(shp, BF16),
                   jax.ShapeDtypeStruct((DEPTH, BATCH, MEM_HEADS, MEM_TOKENS, 2 * MEM_HEAD_DIM), BF16)),
        grid=(DEPTH, BATCH),
        in_specs=[pl.BlockSpec((None, MEM_TOKENS, D_MODEL), lambda l, b: (b, 0, 0)),
                  pl.BlockSpec((None, 1, D_MODEL), lambda l, b: (l, 0, 0)),
                  pl.BlockSpec((None, D_MODEL, 2 * MEM_DIM), lambda l, b: (l, 0, 0))],
        out_specs=(blk, blk, blk,
                   pl.BlockSpec((None, None, MEM_HEADS, MEM_TOKENS, 2 * MEM_HEAD_DIM),
                                lambda l, b: (l, b, 0, 0, 0))),
        compiler_params=_params(32 * 1024 * 1024, 2),
        name="memkv",
    )(mem_prompt, mem_norm.reshape(DEPTH, 1, D_MODEL), w_mem_kv_bf)


def _ffn_kernel(*refs, n_x, attn_proj):
    refs = list(refs)
    x_refs = [refs.pop(0) for _ in range(n_x)]
    g_ref, wg_ref, wu_ref, wd_ref = [refs.pop(0) for _ in range(4)]
    if attn_proj:
        gm_ref, win_ref, x1_ref, qa_ref, qm_ref, act_ref = refs
    else:
        x1_ref, act_ref = refs
    x = _pick_group(*x_refs) if n_x == 2 else x_refs[0][...]
    x1 = _ffn_half_step(x, g_ref, wg_ref, wu_ref, wd_ref, act_ref)
    x1_ref[...] = x1
    if attn_proj:
        hm = _rms(x1, gm_ref[...]).astype(BF16)
        qa_ref[...] = (_dot(hm, win_ref[:, :ATTN_DIM]) * ATTN_Q_SCALE).astype(BF16)
        qm_ref[...] = (_dot(hm, win_ref[:, ATTN_DIM:]) * MEM_Q_SCALE).astype(BF16)


def _ffn(x, g, ffn_w, layer, gm=None, win=None, win_layer=None):
    attn_proj = win is not None
    xs = list(x) if isinstance(x, tuple) else [x]
    in_specs = (_group_specs(D_MODEL) if len(xs) == 2 else [_row_spec(D_MODEL)]) + _ffn_specs(layer)
    args = xs + [g.reshape(1, D_MODEL)] + list(ffn_w)
    out_shape = [jax.ShapeDtypeStruct((ROWS, D_MODEL), F32)]
    out_specs = [_row_spec(D_MODEL)]
    resident = _FFN_WEIGHT_BYTES
    streamed = (1 + len(xs)) * _ROW_TILE_F32
    if attn_proj:
        in_specs += [_resident((1, D_MODEL)), _resident_layer((D_MODEL, ATTN_DIM + MEM_DIM), win_layer)]
        args += [gm.reshape(1, D_MODEL), win]
        out_shape += [jax.ShapeDtypeStruct((ROWS, ATTN_DIM), BF16), jax.ShapeDtypeStruct((ROWS, MEM_DIM), BF16)]
        out_specs += [_row_spec(ATTN_DIM), _row_spec(MEM_DIM)]
        resident += D_MODEL * (ATTN_DIM + MEM_DIM) * 2
        streamed += TILE_ROWS * (ATTN_DIM + MEM_DIM) * 2
    return pl.pallas_call(
        functools.partial(_ffn_kernel, n_x=len(xs), attn_proj=attn_proj),
        out_shape=tuple(out_shape),
        grid=(ROW_TILES,),
        in_specs=in_specs,
        out_specs=tuple(out_specs),
        scratch_shapes=[pltpu.VMEM((TILE_ROWS, FFN_DIM), BF16)],
        compiler_params=_params(_vmem_limit(resident, streamed, _ACT_BYTES, 6 * _ROW_TILE_F32)),
        name="ffn_attn_proj" if attn_proj else "ffn",
    )(*args)


def _inproj_conv_kernel(x_ref, gm_ref, win_ref, cw_ref, pre_ref,
                        ytok_ref, qm_ref, tail_ref, us_ref, shift_ref):
    i = pl.program_id(0)
    hm = _rms(x_ref[...], gm_ref[...]).astype(BF16)
    qm_ref[...] = (_dot(hm, win_ref[:, 3 * CONV_DIM:]) * MEM_Q_SCALE).astype(BF16)

    def chunk(cc, prompt):
        sl = slice(cc * CONV_CHUNK, (cc + 1) * CONV_CHUNK)
        c_gate = _dot(hm, win_ref[:, CONV_DIM + cc * CONV_CHUNK:CONV_DIM + (cc + 1) * CONV_CHUNK])
        x_in = _dot(hm, win_ref[:, 2 * CONV_DIM + cc * CONV_CHUNK:2 * CONV_DIM + (cc + 1) * CONV_CHUNK])
        u = c_gate * x_in
        if prompt:
            shift_ref[V7X_SUBLANES:, sl] = u
            u1 = shift_ref[V7X_SUBLANES - 1:V7X_SUBLANES - 1 + TILE_ROWS, sl]
            u2 = shift_ref[V7X_SUBLANES - 2:V7X_SUBLANES - 2 + TILE_ROWS, sl]
            last = u[TILE_ROWS - V7X_SUBLANES:, :]
            shift_ref[:V7X_SUBLANES, sl] = last
            tail_ref[0, :, sl] = last
        else:
            t = lax.broadcasted_iota(jnp.int32, (TILE_ROWS, CONV_CHUNK), 0) % DEC_SEQ
            p2 = pre_ref[:, sl]
            p1 = pltpu.roll(p2, TILE_ROWS - 1, axis=0)
            u1 = jnp.where(t == 0, p1, pltpu.roll(u, 1, axis=0))
            u2 = jnp.where(t < 2, p2, pltpu.roll(u, 2, axis=0))
            us_ref[:, sl] = u
            tail_ref[0, :, sl] = jnp.zeros((V7X_SUBLANES, CONV_CHUNK), F32)
        w = cw_ref[:, sl]
        conv = w[0:1] * u2 + w[1:2] * u1 + w[2:3] * u
        b_gate = _dot(hm, win_ref[:, sl])
        ytok_ref[:, sl] = (b_gate * conv).astype(BF16)

    @pl.when(i < PROMPT_TILES)
    def _():
        @pl.when(i % TILES_PER_SEQ == 0)
        def _():
            shift_ref[:V7X_SUBLANES, :] = jnp.zeros((V7X_SUBLANES, CONV_DIM), F32)
        for cc in range(CONV_DIM // CONV_CHUNK):
            chunk(cc, True)

    @pl.when(i >= PROMPT_TILES)
    def _():
        for cc in range(CONV_DIM // CONV_CHUNK):
            chunk(cc, False)


def _inproj_conv(x1, gm, win, conv_w, prefix_rows, layer):
    sample_idx = lambda i: (jnp.maximum(i - PROMPT_TILES, 0), 0)
    win_bytes = D_MODEL * (3 * CONV_DIM + MEM_DIM) * 2
    return pl.pallas_call(
        _inproj_conv_kernel,
        out_shape=(jax.ShapeDtypeStruct((ROWS, CONV_DIM), BF16),
                   jax.ShapeDtypeStruct((ROWS, MEM_DIM), BF16),
                   jax.ShapeDtypeStruct((ROW_TILES, V7X_SUBLANES, CONV_DIM), F32),
                   jax.ShapeDtypeStruct((ROWS_SAMPLE, CONV_DIM), F32)),
        grid=(ROW_TILES,),
        in_specs=[_row_spec(D_MODEL), _resident((1, D_MODEL)),
                  _resident_layer((D_MODEL, 3 * CONV_DIM + MEM_DIM), layer),
                  _resident_layer((CONV_WIDTH, CONV_DIM), layer),
                  pl.BlockSpec((TILE_ROWS, CONV_DIM), sample_idx)],
        out_specs=(_row_spec(CONV_DIM), _row_spec(MEM_DIM),
                   pl.BlockSpec((1, V7X_SUBLANES, CONV_DIM), lambda i: (i, 0, 0)),
                   pl.BlockSpec((TILE_ROWS, CONV_DIM), sample_idx)),
        scratch_shapes=[pltpu.VMEM((TILE_ROWS + V7X_SUBLANES, CONV_DIM), F32)],
        compiler_params=_params(_vmem_limit(
            win_bytes, 3 * _ROW_TILE_F32 + TILE_ROWS * (CONV_DIM + MEM_DIM) * 2,
            _ROW_TILE_F32 + V7X_SUBLANES * CONV_DIM * 4, 6 * _ROW_TILE_F32)),
        name="inproj_conv",
    )(x1, gm.reshape(1, D_MODEL), win, conv_w, prefix_rows)


def _kvproj_kernel(x_ref, g_ref, wk_ref, wkt_ref, wv_ref, wv2_ref, k_ref, v_ref, ktz_ref, vz_ref):
    hk = _rms(x_ref[...], g_ref[...]).astype(BF16)
    k_ref[...] = _dot(hk, wk_ref[...])
    v_ref[...] = _dot(hk, wv_ref[...])
    kt = _dot_nt(wkt_ref[...], hk)
    v2 = _dot(hk, wv2_ref[...])
    lo = lax.broadcasted_iota(jnp.int32, (TILE_ROWS, V7X_LANES), 1) < HALF_LANES
    ones = (jnp.where(lo, 1.0, 0.0).astype(BF16), jnp.where(lo, 0.0, 1.0).astype(BF16))
    zero_k = jnp.zeros((HEAD_DIM, TILE_ROWS), BF16)
    for h in range(N_KV_HEADS):
        kth = kt[h * HEAD_DIM:(h + 1) * HEAD_DIM, :].astype(BF16)
        ktz_ref[h, 0] = jnp.concatenate([kth, zero_k], axis=0)
        ktz_ref[h, 1] = jnp.concatenate([zero_k, kth], axis=0)
        v2h = v2[:, h * V7X_LANES:(h + 1) * V7X_LANES]
        for e in range(2):
            vals = jnp.where(lo if e == 0 else jnp.logical_not(lo), v2h, 0.0).astype(BF16)
            c0 = (h * 2 + e) * 2 * V7X_LANES
            vz_ref[:, c0:c0 + 2 * V7X_LANES] = jnp.concatenate([vals, ones[e]], axis=1)


VZ_WIDTH = N_KV_HEADS * 2 * 2 * V7X_LANES


def _kvproj(x, g, wk, wkt, wv, wv2):
    return pl.pallas_call(
        _kvproj_kernel,
        out_shape=(jax.ShapeDtypeStruct((ROWS, KV_DIM), F32),
                   jax.ShapeDtypeStruct((ROWS, KV_DIM), F32),
                   jax.ShapeDtypeStruct((N_KV_HEADS, 2, V7X_LANES, ROWS), BF16),
                   jax.ShapeDtypeStruct((ROWS, VZ_WIDTH), BF16)),
        grid=(ROW_TILES,),
        in_specs=[_row_spec(D_MODEL), _resident((1, D_MODEL)), _resident((D_MODEL, KV_DIM)),
                  _resident((KV_DIM, D_MODEL)), _resident((D_MODEL, KV_DIM)), _resident((D_MODEL, 2 * KV_DIM))],
        out_specs=(_row_spec(KV_DIM), _row_spec(KV_DIM),
                   pl.BlockSpec((N_KV_HEADS, 2, V7X_LANES, TILE_ROWS), lambda i: (0, 0, 0, i)),
                   _row_spec(VZ_WIDTH)),
        compiler_params=_params(_vmem_limit(
            5 * D_MODEL * KV_DIM * 2, _ROW_TILE_F32 + TILE_ROWS * (2 * KV_DIM * 4 + 2 * KV_DIM * 2 + VZ_WIDTH * 2),
            0, 6 * _ROW_TILE_F32)),
        name="kvproj",
    )(x, g.reshape(1, D_MODEL), wk, wkt, wv, wv2)


def _attn_prompt_kernel(*refs, swa):
    if swa:
        (qm_ref, mk_ref, mve_ref, qa_ref, ktp_ref, ktc_ref, vzp_ref, vzc_ref, bias_ref,
         ymem_ref, ytok_ref) = refs
    else:
        qm_ref, mk_ref, mve_ref, ymem_ref = refs

    for h in range(MEM_HEADS):
        sl = slice(h * MEM_HEAD_DIM, (h + 1) * MEM_HEAD_DIM)
        s = _dot_nt(qm_ref[:, sl], mk_ref[:, sl])
        p = jnp.exp2(s - jnp.max(s, axis=-1, keepdims=True)).astype(BF16)
        oe = _dot(p, mve_ref[h])
        ymem_ref[:, sl] = (oe[:, :MEM_HEAD_DIM] / oe[:, MEM_HEAD_DIM:]).astype(BF16)
    if not swa:
        return

    no_prev = ((pl.program_id(0) % TILES_PER_SEQ) == 0).astype(jnp.int32)
    key0_col = lax.broadcasted_iota(jnp.int32, (V7X_LANES, WINDOW), 1) == 0
    key0_row = lax.broadcasted_iota(jnp.int32, (WINDOW, V7X_LANES), 0) == 0
    zero_kt = jnp.zeros((V7X_LANES, WINDOW), BF16)
    zero_v = jnp.zeros((WINDOW, V7X_LANES), BF16)

    for n in range(BLOCKS_PER_TILE):
        rows = slice(n * WINDOW, (n + 1) * WINDOW)
        for h in range(N_KV_HEADS):
            k_sel, v_sel = [], []
            for e in range(2):
                k_prev = ktp_ref[h, e] if n == 0 else ktc_ref[h, e, :, (n - 1) * WINDOW:n * WINDOW]
                k_prev = jnp.where(key0_col, zero_kt, k_prev)
                k_sel.append(jnp.concatenate([k_prev, ktc_ref[h, e, :, rows]], axis=1))
                c0 = (h * 2 + e) * 2 * V7X_LANES
                cols = slice(c0, c0 + 2 * V7X_LANES)
                v_prev = vzp_ref[:, cols] if n == 0 else vzc_ref[(n - 1) * WINDOW:n * WINDOW, cols]
                v_prev = jnp.concatenate(
                    [jnp.where(key0_row, zero_v, v_prev[:, :V7X_LANES]), v_prev[:, V7X_LANES:]], axis=1)
                v_sel.append(jnp.concatenate([v_prev, vzc_ref[rows, cols]], axis=0))
            for pr in range(GROUP // 2):
                c0 = (h * (GROUP // 2) + pr) * V7X_LANES
                qp = qa_ref[rows, c0:c0 + V7X_LANES]
                acc = None
                for e in range(2):
                    j = h * GROUP + pr * 2 + e
                    bias = bias_ref[no_prev, j] if n == 0 else bias_ref[0, j]
                    s = _dot(qp, k_sel[e]) + bias
                    p = jnp.exp2(s - jnp.max(s, axis=-1, keepdims=True)).astype(BF16)
                    part = _dot(p, v_sel[e])
                    acc = part if acc is None else acc + part
                ytok_ref[rows, c0:c0 + V7X_LANES] = (acc[:, :V7X_LANES] / acc[:, V7X_LANES:]).astype(BF16)


def _attn_prompt(qm, mk_bf, mv_ext, layer, swa_args=None):
    swa = swa_args is not None
    batch_of = lambda i: i // TILES_PER_SEQ
    in_specs = [_row_spec(MEM_DIM),
                pl.BlockSpec((None, None, MEM_TOKENS, MEM_DIM), lambda i: (layer, batch_of(i), 0, 0)),
                pl.BlockSpec((None, None, MEM_HEADS, MEM_TOKENS, 2 * MEM_HEAD_DIM),
                             lambda i: (layer, batch_of(i), 0, 0, 0))]
    args = [qm, mk_bf, mv_ext]
    out_shape = [jax.ShapeDtypeStruct((ROWS_PROMPT, MEM_DIM), BF16)]
    out_specs = [_row_spec(MEM_DIM)]
    streamed = TILE_ROWS * MEM_DIM * 4 + MEM_TOKENS * MEM_DIM * 2 * 3
    resident = 0
    if swa:
        qa, ktz, vz, bias_p, swa_layer = swa_args
        prev_blk = lambda i: jnp.maximum(i * BLOCKS_PER_TILE - 1, 0)
        in_specs += [_row_spec(ATTN_DIM),
                     pl.BlockSpec((N_KV_HEADS, 2, V7X_LANES, WINDOW), lambda i: (0, 0, 0, prev_blk(i))),
                     pl.BlockSpec((N_KV_HEADS, 2, V7X_LANES, TILE_ROWS), lambda i: (0, 0, 0, i)),
                     pl.BlockSpec((WINDOW, VZ_WIDTH), lambda i: (prev_blk(i), 0)),
                     pl.BlockSpec((TILE_ROWS, VZ_WIDTH), lambda i: (i, 0)),
                     _resident_layer((2, N_HEADS, WINDOW, 2 * WINDOW), swa_layer)]
        args += [qa, ktz, ktz, vz, vz, bias_p]
        out_shape += [jax.ShapeDtypeStruct((ROWS_PROMPT, ATTN_DIM), BF16)]
        out_specs += [_row_spec(ATTN_DIM)]
        resident = 2 * N_HEADS * WINDOW * 2 * WINDOW * 4
        streamed += 2 * TILE_ROWS * ATTN_DIM * 2 + (TILE_ROWS + WINDOW) * (2 * KV_DIM + VZ_WIDTH) * 2
    return pl.pallas_call(
        functools.partial(_attn_prompt_kernel, swa=swa),
        out_shape=tuple(out_shape),
        grid=(PROMPT_TILES,),
        in_specs=in_specs,
        out_specs=tuple(out_specs),
        compiler_params=_params(_vmem_limit(resident, streamed, 0, 8 * _ROW_TILE_F32)),
        name="attn_prompt_swa" if swa else "attn_prompt_mem",
    )(*args)


def _attn_sample_kernel(*refs, swa):
    if swa:
        (qm_ref, mk_ref, mv_ref, qa_ref, ck_ref, cv_ref, kn_ref, vn_ref, bias_ref,
         ymem_ref, ytok_ref) = refs
    else:
        qm_ref, mk_ref, mv_ref, ymem_ref = refs

    qm_all = qm_ref[...].astype(F32)
    mem_rows = MEM_HEADS * DEC_SEQ
    own_head = (lax.broadcasted_iota(jnp.int32, (mem_rows, MEM_TOKENS * MEM_HEADS), 1) % MEM_HEADS
                == lax.broadcasted_iota(jnp.int32, (mem_rows, MEM_TOKENS * MEM_HEADS), 0) // DEC_SEQ)
    if swa:
        qa_all = qa_ref[...].astype(F32)
        kn_all = kn_ref[...]
        vn_all = vn_ref[...]
        bias = bias_ref[...]
        key0 = lax.broadcasted_iota(jnp.int32, (WINDOW, KV_DIM), 0) == 0
        pad = jnp.zeros((SAMPLE_KEYS_PADDED - SAMPLE_KEYS, KV_DIM), F32)
        lo = lax.broadcasted_iota(jnp.int32, (DEC_SEQ, V7X_LANES), 1) < HALF_LANES
        hi = jnp.logical_not(lo)
        zero_slab = jnp.zeros((DEC_SEQ, V7X_LANES), F32)

    ymem_rows, ytok_rows = [], []
    for b in range(SAMPLE_BATCH_BLOCK):
        rows = slice(b * DEC_SEQ, (b + 1) * DEC_SEQ)
        qm = qm_all[rows]
        q_heads = jnp.concatenate(
            [qm[:, h * MEM_HEAD_DIM:(h + 1) * MEM_HEAD_DIM] for h in range(MEM_HEADS)], axis=0)
        s = jnp.where(own_head, _dot_nt(q_heads, mk_ref[b]), NEG_INF)
        p = jnp.exp2(s - jnp.max(s, axis=-1, keepdims=True))
        o = _dot(p, mv_ref[b]) / jnp.sum(p, axis=-1, keepdims=True)
        ymem_rows.append(jnp.concatenate(
            [o[h * DEC_SEQ:(h + 1) * DEC_SEQ] for h in range(MEM_HEADS)], axis=1))
        if not swa:
            continue
        qb = qa_all[rows]
        groups = []
        for j in range(N_HEADS):
            kvh = j // GROUP
            slab = qb[:, (j // 2) * V7X_LANES:(j // 2 + 1) * V7X_LANES]
            if j % 2 != kvh % 2:
                slab = pltpu.roll(slab, HALF_LANES, axis=1)
            slab = jnp.where(lo if kvh % 2 == 0 else hi, slab, 0.0)
            groups.append(jnp.concatenate([slab, zero_slab] if kvh // 2 == 0 else [zero_slab, slab], axis=1))
        q_bd = jnp.concatenate(groups, axis=0)
        k_cat = jnp.concatenate([jnp.where(key0, 0.0, ck_ref[b]), kn_all[rows], pad], axis=0)
        v_cat = jnp.concatenate([jnp.where(key0, 0.0, cv_ref[b]), vn_all[rows], pad], axis=0)
        s = _dot_nt(q_bd, k_cat) + bias
        p = jnp.exp2(s - jnp.max(s, axis=-1, keepdims=True))
        o_full = _dot(p, v_cat) / jnp.sum(p, axis=-1, keepdims=True)
        pairs = []
        for pair in range(N_HEADS // 2):
            acc = None
            for e in range(2):
                j = pair * 2 + e
                kvh = j // GROUP
                slab = o_full[j * DEC_SEQ:(j + 1) * DEC_SEQ, (kvh // 2) * V7X_LANES:(kvh // 2 + 1) * V7X_LANES]
                if e != kvh % 2:
                    slab = pltpu.roll(slab, HALF_LANES, axis=1)
                slab = jnp.where(lo if e == 0 else hi, slab, 0.0)
                acc = slab if acc is None else acc + slab
            pairs.append(acc)
        ytok_rows.append(jnp.concatenate(pairs, axis=1))

    ymem_ref[...] = jnp.concatenate(ymem_rows, axis=0).astype(BF16)
    if swa:
        ytok_ref[...] = jnp.concatenate(ytok_rows, axis=0).astype(BF16)


def _attn_sample(qm, cache_k, cache_v, layer, swa_args=None):
    swa = swa_args is not None
    row0 = ROWS_PROMPT // SAMPLE_BLOCK_ROWS
    blk_rows = lambda width: pl.BlockSpec((SAMPLE_BLOCK_ROWS, width), lambda i: (row0 + i, 0))
    out_rows = lambda width: pl.BlockSpec((SAMPLE_BLOCK_ROWS, width), lambda i: (i, 0))
    cache_spec = pl.BlockSpec((None, SAMPLE_BATCH_BLOCK, MEM_TOKENS * MEM_HEADS, MEM_HEAD_DIM),
                              lambda i: (layer, i, 0, 0))
    in_specs = [blk_rows(MEM_DIM), cache_spec, cache_spec]
    args = [qm, cache_k, cache_v]
    out_shape = [jax.ShapeDtypeStruct((ROWS_SAMPLE, MEM_DIM), BF16)]
    out_specs = [out_rows(MEM_DIM)]
    streamed = 2 * SAMPLE_BATCH_BLOCK * MEM_TOKENS * MEM_DIM * 4 + SAMPLE_BLOCK_ROWS * MEM_DIM * 4
    if swa:
        qa, swa_k, swa_v, k_new, v_new, bias_s, swa_layer = swa_args
        swa_spec = pl.BlockSpec((SAMPLE_BATCH_BLOCK, WINDOW, KV_DIM), lambda i: (i, 0, 0))
        in_specs += [blk_rows(ATTN_DIM), swa_spec, swa_spec, out_rows(KV_DIM), out_rows(KV_DIM),
                     _resident_layer((N_HEADS * DEC_SEQ, SAMPLE_KEYS_PADDED), swa_layer)]
        args += [qa, swa_k, swa_v, k_new, v_new, bias_s]
        out_shape += [jax.ShapeDtypeStruct((ROWS_SAMPLE, ATTN_DIM), BF16)]
        out_specs += [out_rows(ATTN_DIM)]
        streamed += 2 * SAMPLE_BATCH_BLOCK * WINDOW * KV_DIM * 4 + SAMPLE_BLOCK_ROWS * (ATTN_DIM + KV_DIM) * 4
    return pl.pallas_call(
        functools.partial(_attn_sample_kernel, swa=swa),
        out_shape=tuple(out_shape),
        grid=(DEC_BATCH // SAMPLE_BATCH_BLOCK,),
        in_specs=in_specs,
        out_specs=tuple(out_specs),
        compiler_params=_params(_vmem_limit(1 << 20, streamed, 0, 6 * _ROW_TILE_F32)),
        name="attn_sample_swa" if swa else "attn_sample_mem",
    )(*args)


def _outffn_kernel(*refs, split_tok, final, tile0):
    refs = list(refs)
    x1_ref = refs.pop(0)
    tok_refs = [refs.pop(0) for _ in range(2 if split_tok else 1)]
    ymp_ref, yms_ref, wo_ref, g_ref, wg_ref, wu_ref, wd_ref = [refs.pop(0) for _ in range(7)]
    gf_ref = refs.pop(0) if final else None
    out_ref, act_ref = refs
    y_tok = _pick_group(*tok_refs, tile0=tile0) if split_tok else tok_refs[0][...]
    y_mem = _pick_group(ymp_ref, yms_ref, tile0=tile0)
    tok_dim = wo_ref.shape[0] - MEM_DIM
    x2 = x1_ref[...] + _dot(y_tok, wo_ref[:tok_dim, :]) + _dot(y_mem, wo_ref[tok_dim:, :])
    x3 = _ffn_half_step(x2, g_ref, wg_ref, wu_ref, wd_ref, act_ref)
    out_ref[...] = _rms(x3, gf_ref[...]) if final else x3


def _outffn(x1, y_tok, ymem_p, ymem_s, wo, wo_layer, g, ffn_w, layer, final_gain=None, tiles=(0, ROW_TILES)):
    split_tok = isinstance(y_tok, tuple)
    final = final_gain is not None
    tile0, n_tiles = tiles
    tok_dim = wo.shape[1] - MEM_DIM
    in_specs = [_row_spec(D_MODEL, tile0)]
    args = [x1]
    if split_tok:
        in_specs += _group_specs(tok_dim, tile0)
        args += list(y_tok)
    else:
        in_specs += [_row_spec(tok_dim, tile0)]
        args += [y_tok]
    in_specs += _group_specs(MEM_DIM, tile0) + [_resident_layer((tok_dim + MEM_DIM, D_MODEL), wo_layer)] + _ffn_specs(layer)
    args += [ymem_p, ymem_s, wo, g.reshape(1, D_MODEL)] + list(ffn_w)
    if final:
        in_specs += [_resident((1, D_MODEL))]
        args += [final_gain.reshape(1, D_MODEL)]
    resident = _FFN_WEIGHT_BYTES + (tok_dim + MEM_DIM) * D_MODEL * 2
    streamed = 2 * _ROW_TILE_F32 + 2 * TILE_ROWS * (tok_dim + MEM_DIM) * 2
    return pl.pallas_call(
        functools.partial(_outffn_kernel, split_tok=split_tok, final=final, tile0=tile0),
        out_shape=jax.ShapeDtypeStruct((n_tiles * TILE_ROWS, D_MODEL), F32),
        grid=(n_tiles,),
        in_specs=in_specs,
        out_specs=_row_spec(D_MODEL),
        scratch_shapes=[pltpu.VMEM((TILE_ROWS, FFN_DIM), BF16)],
        compiler_params=_params(_vmem_limit(resident, streamed, _ACT_BYTES, 7 * _ROW_TILE_F32)),
        name="outffn_final" if final else "outffn",
    )(*args)


def kernel(x_prompt, x_sample, state_conv, cache_swa_k, cache_swa_v, cache_mem_k, cache_mem_v, mem_prompt, ffn1_norm, ffn1_wg, ffn1_wu, ffn1_wd, mix_norm, w_in_a, conv_w, w_out_a, kv_norm, w_kv, w_in_b, attn_sinks, rel_bias, w_out_b, mem_norm, w_mem_kv, ffn2_norm, ffn2_wg, ffn2_wu, ffn2_wd, final_norm):
    bf = lambda w: w.astype(BF16)
    ffn1 = (bf(ffn1_wg), bf(ffn1_wu), bf(ffn1_wd))
    ffn2 = (bf(ffn2_wg), bf(ffn2_wu), bf(ffn2_wd))
    w_in_a_bf, w_out_a_bf, w_in_b_bf, w_out_b_bf = bf(w_in_a), bf(w_out_a), bf(w_in_b), bf(w_out_b)
    wk = w_kv[:, :KV_DIM]
    wv = w_kv[:, KV_DIM:]
    wv2 = jnp.broadcast_to(wv.reshape(D_MODEL, N_KV_HEADS, 1, HEAD_DIM),
                           (D_MODEL, N_KV_HEADS, 2, HEAD_DIM)).reshape(D_MODEL, 2 * KV_DIM)

    mem_k, mem_v, mem_k_bf, mem_v_ext = _memkv(mem_prompt, mem_norm, bf(w_mem_kv))
    bias_p, bias_s = _bias_tables(rel_bias, attn_sinks)
    swa_k_cache = cache_swa_k.reshape(DEC_BATCH, WINDOW, KV_DIM)
    swa_v_cache = cache_swa_v.reshape(DEC_BATCH, WINDOW, KV_DIM)
    mem_rows_shape = (DEPTH, DEC_BATCH, MEM_TOKENS * MEM_HEADS, MEM_HEAD_DIM)
    cache_k = cache_mem_k.reshape(mem_rows_shape)
    cache_v = cache_mem_v.reshape(mem_rows_shape)

    x = (x_prompt.reshape(ROWS_PROMPT, D_MODEL), x_sample.reshape(ROWS_SAMPLE, D_MODEL))
    tails, sample_us = [], []
    k_rows = v_rows = ktz = vz = k_new = v_new = None
    for l in range(DEPTH):
        last = l == DEPTH - 1
        if l < N_A_LAYERS:
            (x1,) = _ffn(x, ffn1_norm[l], ffn1, l)
            prefix_rows = jnp.pad(state_conv[l], ((0, 0), (0, DEC_SEQ - (CONV_WIDTH - 1)), (0, 0)))
            y_tok, qm, tail, us = _inproj_conv(x1, mix_norm[l], w_in_a_bf, conv_w,
                                               prefix_rows.reshape(ROWS_SAMPLE, CONV_DIM), l)
            tails.append(tail)
            sample_us.append(us)
            (ymem_p,) = _attn_prompt(qm, mem_k_bf, mem_v_ext, l)
            (ymem_s,) = _attn_sample(qm, cache_k, cache_v, l)
            wo, wo_layer = w_out_a_bf, l
        else:
            j = l - N_A_LAYERS
            if j == 0:
                k_rows, v_rows, ktz, vz = _kvproj(x, kv_norm, bf(wk), bf(wk.T), bf(wv), bf(wv2))
                k_new = k_rows[ROWS_PROMPT:]
                v_new = v_rows[ROWS_PROMPT:]
            x1, qa, qm = _ffn(x, ffn1_norm[l], ffn1, l, mix_norm[l], w_in_b_bf, j)
            ymem_p, ytok_p = _attn_prompt(qm, mem_k_bf, mem_v_ext, l, (qa, ktz, vz, bias_p, j))
            ymem_s, ytok_s = _attn_sample(qm, cache_k, cache_v, l,
                                          (qa, swa_k_cache, swa_v_cache, k_new, v_new, bias_s, j))
            y_tok = (ytok_p, ytok_s)
            wo, wo_layer = w_out_b_bf, j
        if not last:
            x = _outffn(x1, y_tok, ymem_p, ymem_s, wo, wo_layer, ffn2_norm[l], ffn2, l)
        else:
            y_prompt, y_sample = [
                _outffn(x1, y_tok, ymem_p, ymem_s, wo, wo_layer, ffn2_norm[l], ffn2, l,
                        final_gain=final_norm, tiles=t)
                for t in ((0, PROMPT_TILES), (PROMPT_TILES, SAMPLE_TILES))]

    keep = CONV_WIDTH - 1
    last_tiles = np.arange(BATCH) * TILES_PER_SEQ + TILES_PER_SEQ - 1
    conv_state_prompt = jnp.stack([t[last_tiles, V7X_SUBLANES - keep:, :] for t in tails])
    conv_state_sample = jnp.stack([u.reshape(DEC_BATCH, DEC_SEQ, CONV_DIM)[:, DEC_SEQ - keep:, :] for u in sample_us])
    k_tail = jnp.stack([k_rows[(b + 1) * SEQ - WINDOW:(b + 1) * SEQ] for b in range(BATCH)])
    v_tail = jnp.stack([v_rows[(b + 1) * SEQ - WINDOW:(b + 1) * SEQ] for b in range(BATCH)])
    swa_k_prompt = k_tail.reshape(BATCH, WINDOW, N_KV_HEADS, HEAD_DIM)
    swa_v_prompt = v_tail.reshape(BATCH, WINDOW, N_KV_HEADS, HEAD_DIM)
    swa_k_sample = jnp.concatenate(
        [cache_swa_k[:, DEC_SEQ:], k_new.reshape(DEC_BATCH, DEC_SEQ, N_KV_HEADS, HEAD_DIM)], axis=1)
    swa_v_sample = jnp.concatenate(
        [cache_swa_v[:, DEC_SEQ:], v_new.reshape(DEC_BATCH, DEC_SEQ, N_KV_HEADS, HEAD_DIM)], axis=1)
    mem_shape = (DEPTH, BATCH, MEM_TOKENS, MEM_HEADS, MEM_HEAD_DIM)
    return (y_prompt.reshape(BATCH, SEQ, D_MODEL), y_sample.reshape(DEC_BATCH, DEC_SEQ, D_MODEL),
            conv_state_prompt, conv_state_sample,
            swa_k_prompt, swa_v_prompt, swa_k_sample, swa_v_sample,
            mem_k.reshape(mem_shape), mem_v.reshape(mem_shape))
```

```python
import functools
import math

import numpy as np
import jax
import jax.numpy as jnp
from jax import lax
from jax.experimental import pallas as pl
from jax.experimental.pallas import tpu as pltpu

D_MODEL = 1024
BATCH = 2
SEQ = 8192
DEPTH = 4
DEC_BATCH = 128
DEC_SEQ = 8
N_A_LAYERS = DEPTH // 2
FFN_DIM = 2816
CONV_DIM = D_MODEL
CONV_WIDTH = 3
N_HEADS = 16
N_KV_HEADS = 4
HEAD_DIM = 64
GROUP = N_HEADS // N_KV_HEADS
ATTN_DIM = N_HEADS * HEAD_DIM
KV_DIM = N_KV_HEADS * HEAD_DIM
WINDOW = 128
REL_BUCKETS = 32
REL_MAX_DIST = 128
MEM_TOKENS = 256
MEM_HEADS = 4
MEM_HEAD_DIM = 128
MEM_DIM = MEM_HEADS * MEM_HEAD_DIM
RMS_EPS = 1e-5

F32 = jnp.float32
BF16 = jnp.bfloat16
NEG_INF = float("-inf")

V7X_LANES = 128
V7X_SUBLANES = 8
V7X_MXU_DIM = 256
V7X_VMEM_BYTES = 64 * 1024 * 1024

ROWS_PROMPT = BATCH * SEQ
ROWS_SAMPLE = DEC_BATCH * DEC_SEQ
ROWS = ROWS_PROMPT + ROWS_SAMPLE
TILE_ROWS = 512
PROMPT_TILES = ROWS_PROMPT // TILE_ROWS
SAMPLE_TILES = ROWS_SAMPLE // TILE_ROWS
ROW_TILES = PROMPT_TILES + SAMPLE_TILES
TILES_PER_SEQ = SEQ // TILE_ROWS
FFN_CHUNK = V7X_MXU_DIM
CONV_CHUNK = V7X_MXU_DIM
BLOCKS_PER_TILE = TILE_ROWS // WINDOW
SAMPLE_BATCH_BLOCK = 8
SAMPLE_BLOCK_ROWS = SAMPLE_BATCH_BLOCK * DEC_SEQ
SAMPLE_KEYS = WINDOW + DEC_SEQ
SAMPLE_KEYS_PADDED = 2 * WINDOW
HALF_LANES = V7X_LANES // 2
LOG2E = math.log2(math.e)
MEM_Q_SCALE = MEM_HEAD_DIM ** -0.5 * LOG2E
ATTN_Q_SCALE = HEAD_DIM ** -0.5 * LOG2E

assert HEAD_DIM == HALF_LANES and MEM_HEAD_DIM == V7X_LANES
assert ROWS_PROMPT % TILE_ROWS == 0 and ROWS_SAMPLE % TILE_ROWS == 0 and SEQ % TILE_ROWS == 0
assert FFN_DIM % FFN_CHUNK == 0 and TILE_ROWS % WINDOW == 0


def _vmem_limit(resident_bytes, streamed_bytes, scratch_bytes, temp_bytes):
    need = resident_bytes + 2 * streamed_bytes + scratch_bytes + temp_bytes
    assert need < V7X_VMEM_BYTES, need
    return int(need)


def _params(vmem_bytes, n_axes=1):
    return pltpu.CompilerParams(
        dimension_semantics=("arbitrary",) * n_axes, vmem_limit_bytes=vmem_bytes)


def _resident(shape):
    zeros = (0,) * len(shape)
    return pl.BlockSpec(shape, lambda *_: zeros, pipeline_mode=pl.Buffered(1))


def _resident_layer(shape, layer):
    idx = (layer,) + (0,) * len(shape)
    return pl.BlockSpec((None,) + tuple(shape), lambda *_: idx, pipeline_mode=pl.Buffered(1))


def _row_spec(width, tile0=0):
    return pl.BlockSpec((TILE_ROWS, width), lambda i: (i + tile0, 0))


def _group_specs(width, tile0=0):
    return [pl.BlockSpec((TILE_ROWS, width), lambda i: (jnp.minimum(i + tile0, PROMPT_TILES - 1), 0)),
            pl.BlockSpec((TILE_ROWS, width), lambda i: (jnp.maximum(i + tile0 - PROMPT_TILES, 0), 0))]


def _pick_group(prompt_ref, sample_ref, tile0=0):
    return jnp.where(pl.program_id(0) + tile0 < PROMPT_TILES, prompt_ref[...], sample_ref[...])


def _dot(a, b):
    return jnp.dot(a, b, preferred_element_type=F32)


def _dot_nt(a, b):
    return lax.dot_general(a, b, (((1,), (1,)), ((), ())), preferred_element_type=F32)


def _rms(x, g):
    return x * lax.rsqrt(jnp.mean(x * x, axis=-1, keepdims=True) + RMS_EPS) * g


FFN_CHUNKS = FFN_DIM // FFN_CHUNK
FFN_STAGE_SLOTS = 2


class _FfnWeights:
    def __init__(self, layer, hbm_refs, scratch_refs):
        self.layer = layer
        self.wg_hbm, self.wu_hbm, self.wd_hbm = hbm_refs
        self.wg, self.wu, self.wd, self.stage_in, self.stage_out, self.sems = scratch_refs

    @staticmethod
    def in_specs():
        return [pl.BlockSpec(memory_space=pl.ANY)] * 3

    @staticmethod
    def scratch_shapes():
        return [pltpu.VMEM((D_MODEL, FFN_DIM), BF16), pltpu.VMEM((D_MODEL, FFN_DIM), BF16),
                pltpu.VMEM((FFN_DIM, D_MODEL), BF16),
                pltpu.VMEM((2, FFN_STAGE_SLOTS, D_MODEL, FFN_CHUNK), F32),
                pltpu.VMEM((FFN_STAGE_SLOTS, FFN_CHUNK, D_MODEL), F32),
                pltpu.SemaphoreType.DMA((3, FFN_STAGE_SLOTS))]

    SCRATCH_BYTES = 3 * D_MODEL * FFN_DIM * 2 + 3 * FFN_STAGE_SLOTS * D_MODEL * FFN_CHUNK * 4

    def _copy(self, stream, c):
        slot = c % FFN_STAGE_SLOTS
        cols = pl.ds(c * FFN_CHUNK, FFN_CHUNK)
        if stream == 0:
            src, dst = self.wg_hbm.at[self.layer, :, cols], self.stage_in.at[0, slot]
        elif stream == 1:
            src, dst = self.wu_hbm.at[self.layer, :, cols], self.stage_in.at[1, slot]
        else:
            src, dst = self.wd_hbm.at[self.layer, cols, :], self.stage_out.at[slot]
        return pltpu.make_async_copy(src, dst, self.sems.at[stream, slot])

    def prime(self):
        for stream in range(3):
            for c in range(FFN_STAGE_SLOTS):
                self._copy(stream, c).start()

    def fetch(self, c):
        slot = c % FFN_STAGE_SLOTS
        sl = slice(c * FFN_CHUNK, (c + 1) * FFN_CHUNK)
        for stream in range(3):
            self._copy(stream, c).wait()
            if stream == 0:
                self.wg[:, sl] = self.stage_in[0, slot].astype(BF16)
            elif stream == 1:
                self.wu[:, sl] = self.stage_in[1, slot].astype(BF16)
            else:
                self.wd[sl, :] = self.stage_out[slot].astype(BF16)
            if c + FFN_STAGE_SLOTS < FFN_CHUNKS:
                self._copy(stream, c + FFN_STAGE_SLOTS).start()


def _ffn_half_step(x, g_ref, w, act_ref, load_weights):
    h = _rms(x, g_ref[...]).astype(BF16)
    for c in range(FFN_CHUNKS):
        if load_weights:
            w.fetch(c)
        sl = slice(c * FFN_CHUNK, (c + 1) * FFN_CHUNK)
        gate = _dot(h, w.wg[:, sl])
        up = _dot(h, w.wu[:, sl])
        act_ref[:, sl] = (gate / (1.0 + jnp.exp(-gate)) * up).astype(BF16)
    return x + 0.5 * _dot(act_ref[...], w.wd[...])


def _first_step_loads(w, body):
    @pl.when(pl.program_id(0) == 0)
    def _():
        w.prime()
        body(True)

    @pl.when(pl.program_id(0) > 0)
    def _():
        body(False)


_ROW_TILE_F32 = TILE_ROWS * D_MODEL * 4
_ACT_BYTES = TILE_ROWS * FFN_DIM * 2


def _t5_bucket_np(dist):
    n = np.maximum(dist, 0)
    exact = REL_BUCKETS // 2
    nf = np.maximum(n, 1).astype(np.float32)
    large = exact + (np.log(nf / np.float32(exact)) / np.float32(math.log(REL_MAX_DIST / exact))
                     * np.float32(REL_BUCKETS - exact)).astype(np.int32)
    large = np.minimum(large, REL_BUCKETS - 1)
    return np.where(n < exact, n, large).astype(np.int32)


def _bucket_tables():
    q = np.arange(WINDOW)[:, None]
    k = np.arange(2 * WINDOW)[None, :]
    dist = WINDOW + q - k
    prompt = np.where((dist >= 0) & (dist < WINDOW), _t5_bucket_np(dist), -1)
    t = (np.arange(N_HEADS * DEC_SEQ) % DEC_SEQ)[:, None]
    k = np.arange(SAMPLE_KEYS_PADDED)[None, :]
    dist = WINDOW + t - k
    ok = (dist >= 0) & (dist < WINDOW) & (k < SAMPLE_KEYS)
    sample = np.where(ok, _t5_bucket_np(dist), -1)
    return prompt.astype(np.int32), sample.astype(np.int32)


def _bias_kernel(rel_ref, sink_ref, bp_ref, bs_ref, op_ref, os_ref):
    layer = pl.program_id(0)
    j = pl.program_id(1)
    sink = sink_ref[layer, j] * LOG2E

    def build(bucket):
        acc = jnp.zeros(bucket.shape, F32)
        for b in range(REL_BUCKETS):
            acc = jnp.where(bucket == b, rel_ref[b, j], acc)
        return jnp.where(bucket < 0, NEG_INF, acc * LOG2E)

    col_p = lax.broadcasted_iota(jnp.int32, (WINDOW, 2 * WINDOW), 1)
    table = build(bp_ref[...])
    op_ref[0, 0, 0] = jnp.where(col_p == 0, sink, table)
    op_ref[0, 1, 0] = jnp.where(col_p == 0, sink, jnp.where(col_p < WINDOW, NEG_INF, table))
    col_s = lax.broadcasted_iota(jnp.int32, (DEC_SEQ, SAMPLE_KEYS_PADDED), 1)
    os_ref[0] = jnp.where(col_s == 0, sink, build(bs_ref[...]))


def _bias_tables(rel_bias, attn_sinks):
    bp, bs = _bucket_tables()
    n_layers = attn_sinks.shape[0]
    return pl.pallas_call(
        _bias_kernel,
        out_shape=(jax.ShapeDtypeStruct((n_layers, 2, N_HEADS, WINDOW, 2 * WINDOW), F32),
                   jax.ShapeDtypeStruct((n_layers, N_HEADS * DEC_SEQ, SAMPLE_KEYS_PADDED), F32)),
        grid=(n_layers, N_HEADS),
        in_specs=[pl.BlockSpec(memory_space=pltpu.SMEM), pl.BlockSpec(memory_space=pltpu.SMEM),
                  pl.BlockSpec((WINDOW, 2 * WINDOW), lambda l, j: (0, 0)),
                  pl.BlockSpec((DEC_SEQ, SAMPLE_KEYS_PADDED), lambda l, j: (j, 0))],
        out_specs=(pl.BlockSpec((1, 2, 1, WINDOW, 2 * WINDOW), lambda l, j: (l, 0, j, 0, 0)),
                   pl.BlockSpec((1, DEC_SEQ, SAMPLE_KEYS_PADDED), lambda l, j: (l, j, 0))),
        compiler_params=_params(16 * 1024 * 1024, 2),
        name="bias_tables",
    )(rel_bias, attn_sinks, jnp.asarray(bp), jnp.asarray(bs))


def _memkv_kernel(m_ref, g_ref, w_ref, k_ref, v_ref, kb_ref, ve_ref):
    hn = _rms(m_ref[...], g_ref[...]).astype(BF16)
    kv = _dot(hn, w_ref[...])
    k = kv[:, :MEM_DIM]
    v = kv[:, MEM_DIM:]
    k_ref[...] = k
    v_ref[...] = v
    kb_ref[...] = k.astype(BF16)
    ones = jnp.ones((MEM_TOKENS, MEM_HEAD_DIM), BF16)
    for h in range(MEM_HEADS):
        vh = v[:, h * MEM_HEAD_DIM:(h + 1) * MEM_HEAD_DIM].astype(BF16)
        ve_ref[h] = jnp.concatenate([vh, ones], axis=1)


def _memkv(mem_prompt, mem_norm, w_mem_kv_bf):
    shp = (DEPTH, BATCH, MEM_TOKENS, MEM_DIM)
    blk = pl.BlockSpec((None, None, MEM_TOKENS, MEM_DIM), lambda l, b: (l, b, 0, 0))
    return pl.pallas_call(
        _memkv_kernel,
        out_shape=(jax.ShapeDtypeStruct(shp, F32), jax.ShapeDtypeStruct(shp, F32),
                   jax.ShapeDtypeStruct(shp, BF16),
                   jax.ShapeDtypeStruct((DEPTH, BATCH, MEM_HEADS, MEM_TOKENS, 2 * MEM_HEAD_DIM), BF16)),
        grid=(DEPTH, BATCH),
        in_specs=[pl.BlockSpec((None, MEM_TOKENS, D_MODEL), lambda l, b: (b, 0, 0)),
                  pl.BlockSpec((None, 1, D_MODEL), lambda l, b: (l, 0, 0)),
                  pl.BlockSpec((None, D_MODEL, 2 * MEM_DIM), lambda l, b: (l, 0, 0))],
        out_specs=(blk, blk, blk,
                   pl.BlockSpec((None, None, MEM_HEADS, MEM_TOKENS, 2 * MEM_HEAD_DIM),
                                lambda l, b: (l, b, 0, 0, 0))),
        compiler_params=_params(32 * 1024 * 1024, 2),
        name="memkv",
    )(mem_prompt, mem_norm.reshape(DEPTH, 1, D_MODEL), w_mem_kv_bf)


def _ffn_kernel(*refs, n_x, attn_proj, layer):
    refs = list(refs)
    x_refs = [refs.pop(0) for _ in range(n_x)]
    g_ref = refs.pop(0)
    w_hbm = [refs.pop(0) for _ in range(3)]
    if attn_proj:
        gm_ref, win_ref, x1_ref, qa_ref, qm_ref, act_ref = refs[:6]
    else:
        x1_ref, act_ref = refs[:2]
    w = _FfnWeights(layer, w_hbm, refs[-6:])

    def body(load_weights):
        x = _pick_group(*x_refs) if n_x == 2 else x_refs[0][...]
        x1 = _ffn_half_step(x, g_ref, w, act_ref, load_weights)
        x1_ref[...] = x1
        if attn_proj:
            hm = _rms(x1, gm_ref[...]).astype(BF16)
            qa_ref[...] = (_dot(hm, win_ref[:, :ATTN_DIM]) * ATTN_Q_SCALE).astype(BF16)
            qm_ref[...] = (_dot(hm, win_ref[:, ATTN_DIM:]) * MEM_Q_SCALE).astype(BF16)

    _first_step_loads(w, body)


def _ffn(x, g, ffn_w, layer, gm=None, win=None, win_layer=None):
    attn_proj = win is not None
    xs = list(x) if isinstance(x, tuple) else [x]
    in_specs = ((_group_specs(D_MODEL) if len(xs) == 2 else [_row_spec(D_MODEL)])
                + [_resident((1, D_MODEL))] + _FfnWeights.in_specs())
    args = xs + [g.reshape(1, D_MODEL)] + list(ffn_w)
    out_shape = [jax.ShapeDtypeStruct((ROWS, D_MODEL), F32)]
    out_specs = [_row_spec(D_MODEL)]
    resident = 0
    streamed = (1 + len(xs)) * _ROW_TILE_F32
    if attn_proj:
        in_specs += [_resident((1, D_MODEL)), _resident_layer((D_MODEL, ATTN_DIM + MEM_DIM), win_layer)]
        args += [gm.reshape(1, D_MODEL), win]
        out_shape += [jax.ShapeDtypeStruct((ROWS, ATTN_DIM), BF16), jax.ShapeDtypeStruct((ROWS, MEM_DIM), BF16)]
        out_specs += [_row_spec(ATTN_DIM), _row_spec(MEM_DIM)]
        resident += D_MODEL * (ATTN_DIM + MEM_DIM) * 2
        streamed += TILE_ROWS * (ATTN_DIM + MEM_DIM) * 2
    return pl.pallas_call(
        functools.partial(_ffn_kernel, n_x=len(xs), attn_proj=attn_proj, layer=layer),
        out_shape=tuple(out_shape),
        grid=(ROW_TILES,),
        in_specs=in_specs,
        out_specs=tuple(out_specs),
        scratch_shapes=[pltpu.VMEM((TILE_ROWS, FFN_DIM), BF16)] + _FfnWeights.scratch_shapes(),
        compiler_params=_params(_vmem_limit(
            resident, streamed, _ACT_BYTES + _FfnWeights.SCRATCH_BYTES, 6 * _ROW_TILE_F32)),
        name="ffn_attn_proj" if attn_proj else "ffn",
    )(*args)


def _inproj_conv_kernel(x_ref, gm_ref, win_ref, cw_ref, pre_ref,
                        ytok_ref, qm_ref, tail_ref, us_ref, shift_ref):
    i = pl.program_id(0)
    hm = _rms(x_ref[...], gm_ref[...]).astype(BF16)
    qm_ref[...] = (_dot(hm, win_ref[:, 3 * CONV_DIM:]) * MEM_Q_SCALE).astype(BF16)

    def chunk(cc, prompt):
        sl = slice(cc * CONV_CHUNK, (cc + 1) * CONV_CHUNK)
        c_gate = _dot(hm, win_ref[:, CONV_DIM + cc * CONV_CHUNK:CONV_DIM + (cc + 1) * CONV_CHUNK])
        x_in = _dot(hm, win_ref[:, 2 * CONV_DIM + cc * CONV_CHUNK:2 * CONV_DIM + (cc + 1) * CONV_CHUNK])
        u = c_gate * x_in
        if prompt:
            shift_ref[V7X_SUBLANES:, sl] = u
            u1 = shift_ref[V7X_SUBLANES - 1:V7X_SUBLANES - 1 + TILE_ROWS, sl]
            u2 = shift_ref[V7X_SUBLANES - 2:V7X_SUBLANES - 2 + TILE_ROWS, sl]
            last = u[TILE_ROWS - V7X_SUBLANES:, :]
            shift_ref[:V7X_SUBLANES, sl] = last
            tail_ref[0, :, sl] = last
        else:
            t = lax.broadcasted_iota(jnp.int32, (TILE_ROWS, CONV_CHUNK), 0) % DEC_SEQ
            p2 = pre_ref[:, sl]
            p1 = pltpu.roll(p2, TILE_ROWS - 1, axis=0)
            u1 = jnp.where(t == 0, p1, pltpu.roll(u, 1, axis=0))
            u2 = jnp.where(t < 2, p2, pltpu.roll(u, 2, axis=0))
            us_ref[:, sl] = u
            tail_ref[0, :, sl] = jnp.zeros((V7X_SUBLANES, CONV_CHUNK), F32)
        w = cw_ref[:, sl]
        conv = w[0:1] * u2 + w[1:2] * u1 + w[2:3] * u
        b_gate = _dot(hm, win_ref[:, sl])
        ytok_ref[:, sl] = (b_gate * conv).astype(BF16)

    @pl.when(i < PROMPT_TILES)
    def _():
        @pl.when(i % TILES_PER_SEQ == 0)
        def _():
            shift_ref[:V7X_SUBLANES, :] = jnp.zeros((V7X_SUBLANES, CONV_DIM), F32)
        for cc in range(CONV_DIM // CONV_CHUNK):
            chunk(cc, True)

    @pl.when(i >= PROMPT_TILES)
    def _():
        for cc in range(CONV_DIM // CONV_CHUNK):
            chunk(cc, False)


def _inproj_conv(x1, gm, win, conv_w, prefix_rows, layer):
    sample_idx = lambda i: (jnp.maximum(i - PROMPT_TILES, 0), 0)
    win_bytes = D_MODEL * (3 * CONV_DIM + MEM_DIM) * 2
    return pl.pallas_call(
        _inproj_conv_kernel,
        out_shape=(jax.ShapeDtypeStruct((ROWS, CONV_DIM), BF16),
                   jax.ShapeDtypeStruct((ROWS, MEM_DIM), BF16),
                   jax.ShapeDtypeStruct((ROW_TILES, V7X_SUBLANES, CONV_DIM), F32),
                   jax.ShapeDtypeStruct((ROWS_SAMPLE, CONV_DIM), F32)),
        grid=(ROW_TILES,),
        in_specs=[_row_spec(D_MODEL), _resident((1, D_MODEL)),
                  _resident_layer((D_MODEL, 3 * CONV_DIM + MEM_DIM), layer),
                  _resident_layer((CONV_WIDTH, CONV_DIM), layer),
                  pl.BlockSpec((TILE_ROWS, CONV_DIM), sample_idx)],
        out_specs=(_row_spec(CONV_DIM), _row_spec(MEM_DIM),
                   pl.BlockSpec((1, V7X_SUBLANES, CONV_DIM), lambda i: (i, 0, 0)),
                   pl.BlockSpec((TILE_ROWS, CONV_DIM), sample_idx)),
        scratch_shapes=[pltpu.VMEM((TILE_ROWS + V7X_SUBLANES, CONV_DIM), F32)],
        compiler_params=_params(_vmem_limit(
            win_bytes, 3 * _ROW_TILE_F32 + TILE_ROWS * (CONV_DIM + MEM_DIM) * 2,
            _ROW_TILE_F32 + V7X_SUBLANES * CONV_DIM * 4, 6 * _ROW_TILE_F32)),
        name="inproj_conv",
    )(x1, gm.reshape(1, D_MODEL), win, conv_w, prefix_rows)


def _kvproj_kernel(x_ref, g_ref, wk_ref, wkt_ref, wv_ref, wv2_ref, k_ref, v_ref, ktz_ref, vz_ref):
    hk = _rms(x_ref[...], g_ref[...]).astype(BF16)
    k_ref[...] = _dot(hk, wk_ref[...])
    v_ref[...] = _dot(hk, wv_ref[...])
    kt = _dot_nt(wkt_ref[...], hk)
    v2 = _dot(hk, wv2_ref[...])
    lo = lax.broadcasted_iota(jnp.int32, (TILE_ROWS, V7X_LANES), 1) < HALF_LANES
    ones = (jnp.where(lo, 1.0, 0.0).astype(BF16), jnp.where(lo, 0.0, 1.0).astype(BF16))
    zero_k = jnp.zeros((HEAD_DIM, TILE_ROWS), BF16)
    for h in range(N_KV_HEADS):
        kth = kt[h * HEAD_DIM:(h + 1) * HEAD_DIM, :].astype(BF16)
        ktz_ref[h, 0] = jnp.concatenate([kth, zero_k], axis=0)
        ktz_ref[h, 1] = jnp.concatenate([zero_k, kth], axis=0)
        v2h = v2[:, h * V7X_LANES:(h + 1) * V7X_LANES]
        for e in range(2):
            vals = jnp.where(lo if e == 0 else jnp.logical_not(lo), v2h, 0.0).astype(BF16)
            c0 = (h * 2 + e) * 2 * V7X_LANES
            vz_ref[:, c0:c0 + 2 * V7X_LANES] = jnp.concatenate([vals, ones[e]], axis=1)


VZ_WIDTH = N_KV_HEADS * 2 * 2 * V7X_LANES


def _kvproj(x, g, wk, wkt, wv, wv2):
    return pl.pallas_call(
        _kvproj_kernel,
        out_shape=(jax.ShapeDtypeStruct((ROWS, KV_DIM), F32),
                   jax.ShapeDtypeStruct((ROWS, KV_DIM), F32),
                   jax.ShapeDtypeStruct((N_KV_HEADS, 2, V7X_LANES, ROWS), BF16),
                   jax.ShapeDtypeStruct((ROWS, VZ_WIDTH), BF16)),
        grid=(ROW_TILES,),
        in_specs=[_row_spec(D_MODEL), _resident((1, D_MODEL)), _resident((D_MODEL, KV_DIM)),
                  _resident((KV_DIM, D_MODEL)), _resident((D_MODEL, KV_DIM)), _resident((D_MODEL, 2 * KV_DIM))],
        out_specs=(_row_spec(KV_DIM), _row_spec(KV_DIM),
                   pl.BlockSpec((N_KV_HEADS, 2, V7X_LANES, TILE_ROWS), lambda i: (0, 0, 0, i)),
                   _row_spec(VZ_WIDTH)),
        compiler_params=_params(_vmem_limit(
            5 * D_MODEL * KV_DIM * 2, _ROW_TILE_F32 + TILE_ROWS * (2 * KV_DIM * 4 + 2 * KV_DIM * 2 + VZ_WIDTH * 2),
            0, 6 * _ROW_TILE_F32)),
        name="kvproj",
    )(x, g.reshape(1, D_MODEL), wk, wkt, wv, wv2)


def _attn_prompt_kernel(*refs, swa):
    if swa:
        (qm_ref, mk_ref, mve_ref, qa_ref, ktp_ref, ktc_ref, vzp_ref, vzc_ref, bias_ref,
         ymem_ref, ytok_ref) = refs
    else:
        qm_ref, mk_ref, mve_ref, ymem_ref = refs

    for h in range(MEM_HEADS):
        sl = slice(h * MEM_HEAD_DIM, (h + 1) * MEM_HEAD_DIM)
        s = _dot_nt(qm_ref[:, sl], mk_ref[:, sl])
        p = jnp.exp2(s - jnp.max(s, axis=-1, keepdims=True)).astype(BF16)
        oe = _dot(p, mve_ref[h])
        ymem_ref[:, sl] = (oe[:, :MEM_HEAD_DIM] / oe[:, MEM_HEAD_DIM:]).astype(BF16)
    if not swa:
        return

    no_prev = ((pl.program_id(0) % TILES_PER_SEQ) == 0).astype(jnp.int32)
    key0_col = lax.broadcasted_iota(jnp.int32, (V7X_LANES, WINDOW), 1) == 0
    key0_row = lax.broadcasted_iota(jnp.int32, (WINDOW, V7X_LANES), 0) == 0
    zero_kt = jnp.zeros((V7X_LANES, WINDOW), BF16)
    zero_v = jnp.zeros((WINDOW, V7X_LANES), BF16)

    for n in range(BLOCKS_PER_TILE):
        rows = slice(n * WINDOW, (n + 1) * WINDOW)
        for h in range(N_KV_HEADS):
            k_sel, v_sel = [], []
            for e in range(2):
                k_prev = ktp_ref[h, e] if n == 0 else ktc_ref[h, e, :, (n - 1) * WINDOW:n * WINDOW]
                k_prev = jnp.where(key0_col, zero_kt, k_prev)
                k_sel.append(jnp.concatenate([k_prev, ktc_ref[h, e, :, rows]], axis=1))
                c0 = (h * 2 + e) * 2 * V7X_LANES
                cols = slice(c0, c0 + 2 * V7X_LANES)
                v_prev = vzp_ref[:, cols] if n == 0 else vzc_ref[(n - 1) * WINDOW:n * WINDOW, cols]
                v_prev = jnp.concatenate(
                    [jnp.where(key0_row, zero_v, v_prev[:, :V7X_LANES]), v_prev[:, V7X_LANES:]], axis=1)
                v_sel.append(jnp.concatenate([v_prev, vzc_ref[rows, cols]], axis=0))
            for pr in range(GROUP // 2):
                c0 = (h * (GROUP // 2) + pr) * V7X_LANES
                qp = qa_ref[rows, c0:c0 + V7X_LANES]
                acc = None
                for e in range(2):
                    j = h * GROUP + pr * 2 + e
                    bias = bias_ref[no_prev, j] if n == 0 else bias_ref[0, j]
                    s = _dot(qp, k_sel[e]) + bias
                    p = jnp.exp2(s - jnp.max(s, axis=-1, keepdims=True)).astype(BF16)
                    part = _dot(p, v_sel[e])
                    acc = part if acc is None else acc + part
                ytok_ref[rows, c0:c0 + V7X_LANES] = (acc[:, :V7X_LANES] / acc[:, V7X_LANES:]).astype(BF16)


def _attn_prompt(qm, mk_bf, mv_ext, layer, swa_args=None):
    swa = swa_args is not None
    batch_of = lambda i: i // TILES_PER_SEQ
    in_specs = [_row_spec(MEM_DIM),
                pl.BlockSpec((None, None, MEM_TOKENS, MEM_DIM), lambda i: (layer, batch_of(i), 0, 0)),
                pl.BlockSpec((None, None, MEM_HEADS, MEM_TOKENS, 2 * MEM_HEAD_DIM),
                             lambda i: (layer, batch_of(i), 0, 0, 0))]
    args = [qm, mk_bf, mv_ext]
    out_shape = [jax.ShapeDtypeStruct((ROWS_PROMPT, MEM_DIM), BF16)]
    out_specs = [_row_spec(MEM_DIM)]
    streamed = TILE_ROWS * MEM_DIM * 4 + MEM_TOKENS * MEM_DIM * 2 * 3
    resident = 0
    if swa:
        qa, ktz, vz, bias_p, swa_layer = swa_args
        prev_blk = lambda i: jnp.maximum(i * BLOCKS_PER_TILE - 1, 0)
        in_specs += [_row_spec(ATTN_DIM),
                     pl.BlockSpec((N_KV_HEADS, 2, V7X_LANES, WINDOW), lambda i: (0, 0, 0, prev_blk(i))),
                     pl.BlockSpec((N_KV_HEADS, 2, V7X_LANES, TILE_ROWS), lambda i: (0, 0, 0, i)),
                     pl.BlockSpec((WINDOW, VZ_WIDTH), lambda i: (prev_blk(i), 0)),
                     pl.BlockSpec((TILE_ROWS, VZ_WIDTH), lambda i: (i, 0)),
                     _resident_layer((2, N_HEADS, WINDOW, 2 * WINDOW), swa_layer)]
        args += [qa, ktz, ktz, vz, vz, bias_p]
        out_shape += [jax.ShapeDtypeStruct((ROWS_PROMPT, ATTN_DIM), BF16)]
        out_specs += [_row_spec(ATTN_DIM)]
        resident = 2 * N_HEADS * WINDOW * 2 * WINDOW * 4
        streamed += 2 * TILE_ROWS * ATTN_DIM * 2 + (TILE_ROWS + WINDOW) * (2 * KV_DIM + VZ_WIDTH) * 2
    return pl.pallas_call(
        functools.partial(_attn_prompt_kernel, swa=swa),
        out_shape=tuple(out_shape),
        grid=(PROMPT_TILES,),
        in_specs=in_specs,
        out_specs=tuple(out_specs),
        compiler_params=_params(_vmem_limit(resident, streamed, 0, 8 * _ROW_TILE_F32)),
        name="attn_prompt_swa" if swa else "attn_prompt_mem",
    )(*args)


def _attn_sample_kernel(*refs, swa):
    if swa:
        (qm_ref, mk_ref, mv_ref, qa_ref, ck_ref, cv_ref, kn_ref, vn_ref, bias_ref,
         ymem_ref, ytok_ref) = refs
    else:
        qm_ref, mk_ref, mv_ref, ymem_ref = refs

    qm_all = qm_ref[...].astype(F32)
    mem_rows = MEM_HEADS * DEC_SEQ
    own_head = (lax.broadcasted_iota(jnp.int32, (mem_rows, MEM_TOKENS * MEM_HEADS), 1) % MEM_HEADS
                == lax.broadcasted_iota(jnp.int32, (mem_rows, MEM_TOKENS * MEM_HEADS), 0) // DEC_SEQ)
    if swa:
        qa_all = qa_ref[...].astype(F32)
        kn_all = kn_ref[...]
        vn_all = vn_ref[...]
        bias = bias_ref[...]
        key0 = lax.broadcasted_iota(jnp.int32, (WINDOW, KV_DIM), 0) == 0
        pad = jnp.zeros((SAMPLE_KEYS_PADDED - SAMPLE_KEYS, KV_DIM), F32)
        lo = lax.broadcasted_iota(jnp.int32, (DEC_SEQ, V7X_LANES), 1) < HALF_LANES
        hi = jnp.logical_not(lo)
        zero_slab = jnp.zeros((DEC_SEQ, V7X_LANES), F32)

    ymem_rows, ytok_rows = [], []
    for b in range(SAMPLE_BATCH_BLOCK):
        rows = slice(b * DEC_SEQ, (b + 1) * DEC_SEQ)
        qm = qm_all[rows]
        q_heads = jnp.concatenate(
            [qm[:, h * MEM_HEAD_DIM:(h + 1) * MEM_HEAD_DIM] for h in range(MEM_HEADS)], axis=0)
        s = jnp.where(own_head, _dot_nt(q_heads, mk_ref[b]), NEG_INF)
        p = jnp.exp2(s - jnp.max(s, axis=-1, keepdims=True))
        o = _dot(p, mv_ref[b]) / jnp.sum(p, axis=-1, keepdims=True)
        ymem_rows.append(jnp.concatenate(
            [o[h * DEC_SEQ:(h + 1) * DEC_SEQ] for h in range(MEM_HEADS)], axis=1))
        if not swa:
            continue
        qb = qa_all[rows]
        groups = []
        for j in range(N_HEADS):
            kvh = j // GROUP
            slab = qb[:, (j // 2) * V7X_LANES:(j // 2 + 1) * V7X_LANES]
            if j % 2 != kvh % 2:
                slab = pltpu.roll(slab, HALF_LANES, axis=1)
            slab = jnp.where(lo if kvh % 2 == 0 else hi, slab, 0.0)
            groups.append(jnp.concatenate([slab, zero_slab] if kvh // 2 == 0 else [zero_slab, slab], axis=1))
        q_bd = jnp.concatenate(groups, axis=0)
        k_cat = jnp.concatenate([jnp.where(key0, 0.0, ck_ref[b]), kn_all[rows], pad], axis=0)
        v_cat = jnp.concatenate([jnp.where(key0, 0.0, cv_ref[b]), vn_all[rows], pad], axis=0)
        s = _dot_nt(q_bd, k_cat) + bias
        p = jnp.exp2(s - jnp.max(s, axis=-1, keepdims=True))
        o_full = _dot(p, v_cat) / jnp.sum(p, axis=-1, keepdims=True)
        pairs = []
        for pair in range(N_HEADS // 2):
            acc = None
            for e in range(2):
                j = pair * 2 + e
                kvh = j // GROUP
                slab = o_full[j * DEC_SEQ:(j + 1) * DEC_SEQ, (kvh // 2) * V7X_LANES:(kvh // 2 + 1) * V7X_LANES]
                if e != kvh % 2:
                    slab = pltpu.roll(slab, HALF_LANES, axis=1)
                slab = jnp.where(lo if e == 0 else hi, slab, 0.0)
                acc = slab if acc is None else acc + slab
            pairs.append(acc)
        ytok_rows.append(jnp.concatenate(pairs, axis=1))

    ymem_ref[...] = jnp.concatenate(ymem_rows, axis=0).astype(BF16)
    if swa:
        ytok_ref[...] = jnp.concatenate(ytok_rows, axis=0).astype(BF16)


def _attn_sample(qm, cache_k, cache_v, layer, swa_args=None):
    swa = swa_args is not None
    row0 = ROWS_PROMPT // SAMPLE_BLOCK_ROWS
    blk_rows = lambda width: pl.BlockSpec((SAMPLE_BLOCK_ROWS, width), lambda i: (row0 + i, 0))
    out_rows = lambda width: pl.BlockSpec((SAMPLE_BLOCK_ROWS, width), lambda i: (i, 0))
    cache_spec = pl.BlockSpec((None, SAMPLE_BATCH_BLOCK, MEM_TOKENS * MEM_HEADS, MEM_HEAD_DIM),
                              lambda i: (layer, i, 0, 0))
    in_specs = [blk_rows(MEM_DIM), cache_spec, cache_spec]
    args = [qm, cache_k, cache_v]
    out_shape = [jax.ShapeDtypeStruct((ROWS_SAMPLE, MEM_DIM), BF16)]
    out_specs = [out_rows(MEM_DIM)]
    streamed = 2 * SAMPLE_BATCH_BLOCK * MEM_TOKENS * MEM_DIM * 4 + SAMPLE_BLOCK_ROWS * MEM_DIM * 4
    if swa:
        qa, swa_k, swa_v, k_new, v_new, bias_s, swa_layer = swa_args
        swa_spec = pl.BlockSpec((SAMPLE_BATCH_BLOCK, WINDOW, KV_DIM), lambda i: (i, 0, 0))
        in_specs += [blk_rows(ATTN_DIM), swa_spec, swa_spec, out_rows(KV_DIM), out_rows(KV_DIM),
                     _resident_layer((N_HEADS * DEC_SEQ, SAMPLE_KEYS_PADDED), swa_layer)]
        args += [qa, swa_k, swa_v, k_new, v_new, bias_s]
        out_shape += [jax.ShapeDtypeStruct((ROWS_SAMPLE, ATTN_DIM), BF16)]
        out_specs += [out_rows(ATTN_DIM)]
        streamed += 2 * SAMPLE_BATCH_BLOCK * WINDOW * KV_DIM * 4 + SAMPLE_BLOCK_ROWS * (ATTN_DIM + KV_DIM) * 4
    return pl.pallas_call(
        functools.partial(_attn_sample_kernel, swa=swa),
        out_shape=tuple(out_shape),
        grid=(DEC_BATCH // SAMPLE_BATCH_BLOCK,),
        in_specs=in_specs,
        out_specs=tuple(out_specs),
        compiler_params=_params(_vmem_limit(1 << 20, streamed, 0, 6 * _ROW_TILE_F32)),
        name="attn_sample_swa" if swa else "attn_sample_mem",
    )(*args)


def _outffn_kernel(*refs, split_tok, final, tile0, layer):
    refs = list(refs)
    x1_ref = refs.pop(0)
    tok_refs = [refs.pop(0) for _ in range(2 if split_tok else 1)]
    ymp_ref, yms_ref, wo_ref, g_ref = [refs.pop(0) for _ in range(4)]
    w_hbm = [refs.pop(0) for _ in range(3)]
    gf_ref = refs.pop(0) if final else None
    out_ref, act_ref = refs[:2]
    w = _FfnWeights(layer, w_hbm, refs[-6:])
    tok_dim = wo_ref.shape[0] - MEM_DIM

    def body(load_weights):
        y_tok = _pick_group(*tok_refs, tile0=tile0) if split_tok else tok_refs[0][...]
        y_mem = _pick_group(ymp_ref, yms_ref, tile0=tile0)
        x2 = x1_ref[...] + _dot(y_tok, wo_ref[:tok_dim, :]) + _dot(y_mem, wo_ref[tok_dim:, :])
        x3 = _ffn_half_step(x2, g_ref, w, act_ref, load_weights)
        out_ref[...] = _rms(x3, gf_ref[...]) if final else x3

    _first_step_loads(w, body)


def _outffn(x1, y_tok, ymem_p, ymem_s, wo, wo_layer, g, ffn_w, layer, final_gain=None, tiles=(0, ROW_TILES)):
    split_tok = isinstance(y_tok, tuple)
    final = final_gain is not None
    tile0, n_tiles = tiles
    tok_dim = wo.shape[1] - MEM_DIM
    in_specs = [_row_spec(D_MODEL, tile0)]
    args = [x1]
    if split_tok:
        in_specs += _group_specs(tok_dim, tile0)
        args += list(y_tok)
    else:
        in_specs += [_row_spec(tok_dim, tile0)]
        args += [y_tok]
    in_specs += (_group_specs(MEM_DIM, tile0) + [_resident_layer((tok_dim + MEM_DIM, D_MODEL), wo_layer)]
                 + [_resident((1, D_MODEL))] + _FfnWeights.in_specs())
    args += [ymem_p, ymem_s, wo, g.reshape(1, D_MODEL)] + list(ffn_w)
    if final:
        in_specs += [_resident((1, D_MODEL))]
        args += [final_gain.reshape(1, D_MODEL)]
    resident = (tok_dim + MEM_DIM) * D_MODEL * 2
    streamed = 2 * _ROW_TILE_F32 + 2 * TILE_ROWS * (tok_dim + MEM_DIM) * 2
    return pl.pallas_call(
        functools.partial(_outffn_kernel, split_tok=split_tok, final=final, tile0=tile0, layer=layer),
        out_shape=jax.ShapeDtypeStruct((n_tiles * TILE_ROWS, D_MODEL), F32),
        grid=(n_tiles,),
        in_specs=in_specs,
        out_specs=_row_spec(D_MODEL),
        scratch_shapes=[pltpu.VMEM((TILE_ROWS, FFN_DIM), BF16)] + _FfnWeights.scratch_shapes(),
        compiler_params=_params(_vmem_limit(
            resident, streamed, _ACT_BYTES + _FfnWeights.SCRATCH_BYTES, 7 * _ROW_TILE_F32)),
        name="outffn_final" if final else "outffn",
    )(*args)


def kernel(x_prompt, x_sample, state_conv, cache_swa_k, cache_swa_v, cache_mem_k, cache_mem_v, mem_prompt, ffn1_norm, ffn1_wg, ffn1_wu, ffn1_wd, mix_norm, w_in_a, conv_w, w_out_a, kv_norm, w_kv, w_in_b, attn_sinks, rel_bias, w_out_b, mem_norm, w_mem_kv, ffn2_norm, ffn2_wg, ffn2_wu, ffn2_wd, final_norm):
    bf = lambda w: w.astype(BF16)
    ffn1 = (ffn1_wg, ffn1_wu, ffn1_wd)
    ffn2 = (ffn2_wg, ffn2_wu, ffn2_wd)
    w_in_a_bf, w_out_a_bf, w_in_b_bf, w_out_b_bf = bf(w_in_a), bf(w_out_a), bf(w_in_b), bf(w_out_b)
    wk = w_kv[:, :KV_DIM]
    wv = w_kv[:, KV_DIM:]
    wv2 = jnp.broadcast_to(wv.reshape(D_MODEL, N_KV_HEADS, 1, HEAD_DIM),
                           (D_MODEL, N_KV_HEADS, 2, HEAD_DIM)).reshape(D_MODEL, 2 * KV_DIM)

    mem_k, mem_v, mem_k_bf, mem_v_ext = _memkv(mem_prompt, mem_norm, bf(w_mem_kv))
    bias_p, bias_s = _bias_tables(rel_bias, attn_sinks)
    swa_k_cache = cache_swa_k.reshape(DEC_BATCH, WINDOW, KV_DIM)
    swa_v_cache = cache_swa_v.reshape(DEC_BATCH, WINDOW, KV_DIM)
    mem_rows_shape = (DEPTH, DEC_BATCH, MEM_TOKENS * MEM_HEADS, MEM_HEAD_DIM)
    cache_k = cache_mem_k.reshape(mem_rows_shape)
    cache_v = cache_mem_v.reshape(mem_rows_shape)

    x = (x_prompt.reshape(ROWS_PROMPT, D_MODEL), x_sample.reshape(ROWS_SAMPLE, D_MODEL))
    tails, sample_us = [], []
    k_rows = v_rows = ktz = vz = k_new = v_new = None
    for l in range(DEPTH):
        last = l == DEPTH - 1
        if l < N_A_LAYERS:
            (x1,) = _ffn(x, ffn1_norm[l], ffn1, l)
            prefix_rows = jnp.pad(state_conv[l], ((0, 0), (0, DEC_SEQ - (CONV_WIDTH - 1)), (0, 0)))
            y_tok, qm, tail, us = _inproj_conv(x1, mix_norm[l], w_in_a_bf, conv_w,
                                               prefix_rows.reshape(ROWS_SAMPLE, CONV_DIM), l)
            tails.append(tail)
            sample_us.append(us)
            (ymem_p,) = _attn_prompt(qm, mem_k_bf, mem_v_ext, l)
            (ymem_s,) = _attn_sample(qm, cache_k, cache_v, l)
            wo, wo_layer = w_out_a_bf, l
        else:
            j = l - N_A_LAYERS
            if j == 0:
                k_rows, v_rows, ktz, vz = _kvproj(x, kv_norm, bf(wk), bf(wk.T), bf(wv), bf(wv2))
                k_new = k_rows[ROWS_PROMPT:]
                v_new = v_rows[ROWS_PROMPT:]
            x1, qa, qm = _ffn(x, ffn1_norm[l], ffn1, l, mix_norm[l], w_in_b_bf, j)
            ymem_p, ytok_p = _attn_prompt(qm, mem_k_bf, mem_v_ext, l, (qa, ktz, vz, bias_p, j))
            ymem_s, ytok_s = _attn_sample(qm, cache_k, cache_v, l,
                                          (qa, swa_k_cache, swa_v_cache, k_new, v_new, bias_s, j))
            y_tok = (ytok_p, ytok_s)
            wo, wo_layer = w_out_b_bf, j
        if not last:
            x = _outffn(x1, y_tok, ymem_p, ymem_s, wo, wo_layer, ffn2_norm[l], ffn2, l)
        else:
            y_prompt, y_sample = [
                _outffn(x1, y_tok, ymem_p, ymem_s, wo, wo_layer, ffn2_norm[l], ffn2, l,
                        final_gain=final_norm, tiles=t)
                for t in ((0, PROMPT_TILES), (PROMPT_TILES, SAMPLE_TILES))]

    keep = CONV_WIDTH - 1
    last_tiles = np.arange(BATCH) * TILES_PER_SEQ + TILES_PER_SEQ - 1
    conv_state_prompt = jnp.stack([t[last_tiles, V7X_SUBLANES - keep:, :] for t in tails])
    conv_state_sample = jnp.stack([u.reshape(DEC_BATCH, DEC_SEQ, CONV_DIM)[:, DEC_SEQ - keep:, :] for u in sample_us])
    k_tail = jnp.stack([k_rows[(b + 1) * SEQ - WINDOW:(b + 1) * SEQ] for b in range(BATCH)])
    v_tail = jnp.stack([v_rows[(b + 1) * SEQ - WINDOW:(b + 1) * SEQ] for b in range(BATCH)])
    swa_k_prompt = k_tail.reshape(BATCH, WINDOW, N_KV_HEADS, HEAD_DIM)
    swa_v_prompt = v_tail.reshape(BATCH, WINDOW, N_KV_HEADS, HEAD_DIM)
    swa_k_sample = jnp.concatenate(
        [cache_swa_k[:, DEC_SEQ:], k_new.reshape(DEC_BATCH, DEC_SEQ, N_KV_HEADS, HEAD_DIM)], axis=1)
    swa_v_sample = jnp.concatenate(
        [cache_swa_v[:, DEC_SEQ:], v_new.reshape(DEC_BATCH, DEC_SEQ, N_KV_HEADS, HEAD_DIM)], axis=1)
    mem_shape = (DEPTH, BATCH, MEM_TOKENS, MEM_HEADS, MEM_HEAD_DIM)
    return (y_prompt.reshape(BATCH, SEQ, D_MODEL), y_sample.reshape(DEC_BATCH, DEC_SEQ, D_MODEL),
            conv_state_prompt, conv_state_sample,
            swa_k_prompt, swa_v_prompt, swa_k_sample, swa_v_sample,
            mem_k.reshape(mem_shape), mem_v.reshape(mem_shape))
```

```python
import functools
import math

import numpy as np
import jax
import jax.numpy as jnp
from jax import lax
from jax.experimental import pallas as pl
from jax.experimental.pallas import tpu as pltpu

D_MODEL = 1024
BATCH = 2
SEQ = 8192
DEPTH = 4
DEC_BATCH = 128
DEC_SEQ = 8
N_A_LAYERS = DEPTH // 2
FFN_DIM = 2816
CONV_DIM = D_MODEL
CONV_WIDTH = 3
N_HEADS = 16
N_KV_HEADS = 4
HEAD_DIM = 64
GROUP = N_HEADS // N_KV_HEADS
ATTN_DIM = N_HEADS * HEAD_DIM
KV_DIM = N_KV_HEADS * HEAD_DIM
WINDOW = 128
REL_BUCKETS = 32
REL_MAX_DIST = 128
MEM_TOKENS = 256
MEM_HEADS = 4
MEM_HEAD_DIM = 128
MEM_DIM = MEM_HEADS * MEM_HEAD_DIM
RMS_EPS = 1e-5

F32 = jnp.float32
BF16 = jnp.bfloat16
NEG_INF = float("-inf")

V7X_LANES = 128
V7X_SUBLANES = 8
V7X_MXU_DIM = 256
V7X_VMEM_BYTES = 64 * 1024 * 1024

ROWS_PROMPT = BATCH * SEQ
ROWS_SAMPLE = DEC_BATCH * DEC_SEQ
ROWS = ROWS_PROMPT + ROWS_SAMPLE
TILE_ROWS = 512
PROMPT_TILES = ROWS_PROMPT // TILE_ROWS
SAMPLE_TILES = ROWS_SAMPLE // TILE_ROWS
ROW_TILES = PROMPT_TILES + SAMPLE_TILES
TILES_PER_SEQ = SEQ // TILE_ROWS
FFN_CHUNK = V7X_MXU_DIM
CONV_CHUNK = V7X_MXU_DIM
BLOCKS_PER_TILE = TILE_ROWS // WINDOW
SAMPLE_BATCH_BLOCK = 8
SAMPLE_BLOCK_ROWS = SAMPLE_BATCH_BLOCK * DEC_SEQ
SAMPLE_KEYS = WINDOW + DEC_SEQ
SAMPLE_KEYS_PADDED = 2 * WINDOW
HALF_LANES = V7X_LANES // 2
LOG2E = math.log2(math.e)
MEM_Q_SCALE = MEM_HEAD_DIM ** -0.5 * LOG2E
ATTN_Q_SCALE = HEAD_DIM ** -0.5 * LOG2E

assert HEAD_DIM == HALF_LANES and MEM_HEAD_DIM == V7X_LANES
assert ROWS_PROMPT % TILE_ROWS == 0 and ROWS_SAMPLE % TILE_ROWS == 0 and SEQ % TILE_ROWS == 0
assert FFN_DIM % FFN_CHUNK == 0 and TILE_ROWS % WINDOW == 0


def _vmem_limit(resident_bytes, streamed_bytes, scratch_bytes, temp_bytes):
    need = resident_bytes + 2 * streamed_bytes + scratch_bytes + temp_bytes
    assert need < V7X_VMEM_BYTES, need
    return int(need)


def _params(vmem_bytes, n_axes=1):
    return pltpu.CompilerParams(
        dimension_semantics=("arbitrary",) * n_axes, vmem_limit_bytes=vmem_bytes)


def _resident(shape):
    zeros = (0,) * len(shape)
    return pl.BlockSpec(shape, lambda *_: zeros, pipeline_mode=pl.Buffered(1))


def _resident_layer(shape, layer):
    idx = (layer,) + (0,) * len(shape)
    return pl.BlockSpec((None,) + tuple(shape), lambda *_: idx, pipeline_mode=pl.Buffered(1))


def _row_spec(width, tile0=0):
    return pl.BlockSpec((TILE_ROWS, width), lambda i: (i + tile0, 0))


def _group_specs(width, tile0=0):
    return [pl.BlockSpec((TILE_ROWS, width), lambda i: (jnp.minimum(i + tile0, PROMPT_TILES - 1), 0)),
            pl.BlockSpec((TILE_ROWS, width), lambda i: (jnp.maximum(i + tile0 - PROMPT_TILES, 0), 0))]


def _pick_group(prompt_ref, sample_ref, tile0=0):
    return jnp.where(pl.program_id(0) + tile0 < PROMPT_TILES, prompt_ref[...], sample_ref[...])


def _dot(a, b):
    return jnp.dot(a, b, preferred_element_type=F32)


def _dot_nt(a, b):
    return lax.dot_general(a, b, (((1,), (1,)), ((), ())), preferred_element_type=F32)


def _rms(x, g):
    return x * lax.rsqrt(jnp.mean(x * x, axis=-1, keepdims=True) + RMS_EPS) * g


FFN_CHUNKS = FFN_DIM // FFN_CHUNK
FFN_STAGE_SLOTS = 2


class _FfnWeights:
    def __init__(self, layer, hbm_refs, scratch_refs):
        self.layer = layer
        self.wg_hbm, self.wu_hbm, self.wd_hbm = hbm_refs
        self.wg, self.wu, self.wd, self.stage_in, self.stage_out, self.sems = scratch_refs

    @staticmethod
    def in_specs():
        return [pl.BlockSpec(memory_space=pl.ANY)] * 3

    @staticmethod
    def scratch_shapes():
        return [pltpu.VMEM((D_MODEL, FFN_DIM), BF16), pltpu.VMEM((D_MODEL, FFN_DIM), BF16),
                pltpu.VMEM((FFN_DIM, D_MODEL), BF16),
                pltpu.VMEM((2, FFN_STAGE_SLOTS, D_MODEL, FFN_CHUNK), F32),
                pltpu.VMEM((FFN_STAGE_SLOTS, FFN_CHUNK, D_MODEL), F32),
                pltpu.SemaphoreType.DMA((3, FFN_STAGE_SLOTS))]

    SCRATCH_BYTES = 3 * D_MODEL * FFN_DIM * 2 + 3 * FFN_STAGE_SLOTS * D_MODEL * FFN_CHUNK * 4

    def _copy(self, stream, c):
        slot = c % FFN_STAGE_SLOTS
        cols = pl.ds(c * FFN_CHUNK, FFN_CHUNK)
        if stream == 0:
            src, dst = self.wg_hbm.at[self.layer, :, cols], self.stage_in.at[0, slot]
        elif stream == 1:
            src, dst = self.wu_hbm.at[self.layer, :, cols], self.stage_in.at[1, slot]
        else:
            src, dst = self.wd_hbm.at[self.layer, cols, :], self.stage_out.at[slot]
        return pltpu.make_async_copy(src, dst, self.sems.at[stream, slot])

    def prime(self):
        for stream in range(3):
            for c in range(FFN_STAGE_SLOTS):
                self._copy(stream, c).start()

    def fetch(self, c):
        slot = c % FFN_STAGE_SLOTS
        sl = slice(c * FFN_CHUNK, (c + 1) * FFN_CHUNK)
        for stream in range(3):
            self._copy(stream, c).wait()
            if stream == 0:
                self.wg[:, sl] = self.stage_in[0, slot].astype(BF16)
            elif stream == 1:
                self.wu[:, sl] = self.stage_in[1, slot].astype(BF16)
            else:
                self.wd[sl, :] = self.stage_out[slot].astype(BF16)
            if c + FFN_STAGE_SLOTS < FFN_CHUNKS:
                self._copy(stream, c + FFN_STAGE_SLOTS).start()


def _ffn_half_step(x, g_ref, w, act_ref, load_weights):
    inv_rms = lax.rsqrt(jnp.mean(x * x, axis=-1, keepdims=True) + RMS_EPS)
    inv_rms_chunk = jnp.broadcast_to(inv_rms, (x.shape[0], FFN_CHUNK))
    h = (x * g_ref[...]).astype(BF16)
    for c in range(FFN_CHUNKS):
        if load_weights:
            w.fetch(c)
        sl = slice(c * FFN_CHUNK, (c + 1) * FFN_CHUNK)
        gate = _dot(h, w.wg[:, sl]) * inv_rms_chunk
        up = _dot(h, w.wu[:, sl])
        act_ref[:, sl] = (gate / (1.0 + jnp.exp(-gate)) * up).astype(BF16)
    return x + (0.5 * inv_rms) * _dot(act_ref[...], w.wd[...])


def _first_step_loads(w, body):
    @pl.when(pl.program_id(0) == 0)
    def _():
        w.prime()
        body(True)

    @pl.when(pl.program_id(0) > 0)
    def _():
        body(False)


_ROW_TILE_F32 = TILE_ROWS * D_MODEL * 4
_ACT_BYTES = TILE_ROWS * FFN_DIM * 2


def _t5_bucket_np(dist):
    n = np.maximum(dist, 0)
    exact = REL_BUCKETS // 2
    nf = np.maximum(n, 1).astype(np.float32)
    large = exact + (np.log(nf / np.float32(exact)) / np.float32(math.log(REL_MAX_DIST / exact))
                     * np.float32(REL_BUCKETS - exact)).astype(np.int32)
    large = np.minimum(large, REL_BUCKETS - 1)
    return np.where(n < exact, n, large).astype(np.int32)


def _bucket_tables():
    q = np.arange(WINDOW)[:, None]
    k = np.arange(2 * WINDOW)[None, :]
    dist = WINDOW + q - k
    prompt = np.where((dist >= 0) & (dist < WINDOW), _t5_bucket_np(dist), -1)
    t = (np.arange(N_HEADS * DEC_SEQ) % DEC_SEQ)[:, None]
    k = np.arange(SAMPLE_KEYS_PADDED)[None, :]
    dist = WINDOW + t - k
    ok = (dist >= 0) & (dist < WINDOW) & (k < SAMPLE_KEYS)
    sample = np.where(ok, _t5_bucket_np(dist), -1)
    return prompt.astype(np.int32), sample.astype(np.int32)


def _bias_kernel(rel_ref, sink_ref, bp_ref, bs_ref, op_ref, os_ref):
    layer = pl.program_id(0)
    j = pl.program_id(1)
    sink = sink_ref[layer, j] * LOG2E

    def build(bucket):
        acc = jnp.zeros(bucket.shape, F32)
        for b in range(REL_BUCKETS):
            acc = jnp.where(bucket == b, rel_ref[b, j], acc)
        return jnp.where(bucket < 0, NEG_INF, acc * LOG2E)

    col_p = lax.broadcasted_iota(jnp.int32, (WINDOW, 2 * WINDOW), 1)
    table = build(bp_ref[...])
    op_ref[0, 0, 0] = jnp.where(col_p == 0, sink, table)
    op_ref[0, 1, 0] = jnp.where(col_p == 0, sink, jnp.where(col_p < WINDOW, NEG_INF, table))
    col_s = lax.broadcasted_iota(jnp.int32, (DEC_SEQ, SAMPLE_KEYS_PADDED), 1)
    os_ref[0] = jnp.where(col_s == 0, sink, build(bs_ref[...]))


def _bias_tables(rel_bias, attn_sinks):
    bp, bs = _bucket_tables()
    n_layers = attn_sinks.shape[0]
    return pl.pallas_call(
        _bias_kernel,
        out_shape=(jax.ShapeDtypeStruct((n_layers, 2, N_HEADS, WINDOW, 2 * WINDOW), F32),
                   jax.ShapeDtypeStruct((n_layers, N_HEADS * DEC_SEQ, SAMPLE_KEYS_PADDED), F32)),
        grid=(n_layers, N_HEADS),
        in_specs=[pl.BlockSpec(memory_space=pltpu.SMEM), pl.BlockSpec(memory_space=pltpu.SMEM),
                  pl.BlockSpec((WINDOW, 2 * WINDOW), lambda l, j: (0, 0)),
                  pl.BlockSpec((DEC_SEQ, SAMPLE_KEYS_PADDED), lambda l, j: (j, 0))],
        out_specs=(pl.BlockSpec((1, 2, 1, WINDOW, 2 * WINDOW), lambda l, j: (l, 0, j, 0, 0)),
                   pl.BlockSpec((1, DEC_SEQ, SAMPLE_KEYS_PADDED), lambda l, j: (l, j, 0))),
        compiler_params=_params(16 * 1024 * 1024, 2),
        name="bias_tables",
    )(rel_bias, attn_sinks, jnp.asarray(bp), jnp.asarray(bs))


def _memkv_kernel(m_ref, g_ref, w_ref, k_ref, v_ref, kb_ref, ve_ref):
    hn = _rms(m_ref[...], g_ref[...]).astype(BF16)
    kv = _dot(hn, w_ref[...])
    k = kv[:, :MEM_DIM]
    v = kv[:, MEM_DIM:]
    k_ref[...] = k
    v_ref[...] = v
    kb_ref[...] = k.astype(BF16)
    ones = jnp.ones((MEM_TOKENS, MEM_HEAD_DIM), BF16)
    for h in range(MEM_HEADS):
        vh = v[:, h * MEM_HEAD_DIM:(h + 1) * MEM_HEAD_DIM].astype(BF16)
        ve_ref[h] = jnp.concatenate([vh, ones], axis=1)


def _memkv(mem_prompt, mem_norm, w_mem_kv_bf):
    shp = (DEPTH, BATCH, MEM_TOKENS, MEM_DIM)
    blk = pl.BlockSpec((None, None, MEM_TOKENS, MEM_DIM), lambda l, b: (l, b, 0, 0))
    return pl.pallas_call(
        _memkv_kernel,
        out_shape=(jax.ShapeDtypeStruct(shp, F32), jax.ShapeDtypeStruct(shp, F32),
                   jax.ShapeDtypeStruct(shp, BF16),
                   jax.ShapeDtypeStruct((DEPTH, BATCH, MEM_HEADS, MEM_TOKENS, 2 * MEM_HEAD_DIM), BF16)),
        grid=(DEPTH, BATCH),
        in_specs=[pl.BlockSpec((None, MEM_TOKENS, D_MODEL), lambda l, b: (b, 0, 0)),
                  pl.BlockSpec((None, 1, D_MODEL), lambda l, b: (l, 0, 0)),
                  pl.BlockSpec((None, D_MODEL, 2 * MEM_DIM), lambda l, b: (l, 0, 0))],
        out_specs=(blk, blk, blk,
                   pl.BlockSpec((None, None, MEM_HEADS, MEM_TOKENS, 2 * MEM_HEAD_DIM),
                                lambda l, b: (l, b, 0, 0, 0))),
        compiler_params=_params(32 * 1024 * 1024, 2),
        name="memkv",
    )(mem_prompt, mem_norm.reshape(DEPTH, 1, D_MODEL), w_mem_kv_bf)


def _ffn_kernel(*refs, n_x, attn_proj, layer):
    refs = list(refs)
    x_refs = [refs.pop(0) for _ in range(n_x)]
    g_ref = refs.pop(0)
    w_hbm = [refs.pop(0) for _ in range(3)]
    if attn_proj:
        gm_ref, win_ref, x1_ref, qa_ref, qm_ref, act_ref = refs[:6]
    else:
        x1_ref, act_ref = refs[:2]
    w = _FfnWeights(layer, w_hbm, refs[-6:])

    def body(load_weights):
        x = _pick_group(*x_refs) if n_x == 2 else x_refs[0][...]
        x1 = _ffn_half_step(x, g_ref, w, act_ref, load_weights)
        x1_ref[...] = x1
        if attn_proj:
            hm = _rms(x1, gm_ref[...]).astype(BF16)
            qa_ref[...] = (_dot(hm, win_ref[:, :ATTN_DIM]) * ATTN_Q_SCALE).astype(BF16)
            qm_ref[...] = (_dot(hm, win_ref[:, ATTN_DIM:]) * MEM_Q_SCALE).astype(BF16)

    _first_step_loads(w, body)


def _ffn(x, g, ffn_w, layer, gm=None, win=None, win_layer=None):
    attn_proj = win is not None
    xs = list(x) if isinstance(x, tuple) else [x]
    in_specs = ((_group_specs(D_MODEL) if len(xs) == 2 else [_row_spec(D_MODEL)])
                + [_resident((1, D_MODEL))] + _FfnWeights.in_specs())
    args = xs + [g.reshape(1, D_MODEL)] + list(ffn_w)
    out_shape = [jax.ShapeDtypeStruct((ROWS, D_MODEL), F32)]
    out_specs = [_row_spec(D_MODEL)]
    resident = 0
    streamed = (1 + len(xs)) * _ROW_TILE_F32
    if attn_proj:
        in_specs += [_resident((1, D_MODEL)), _resident_layer((D_MODEL, ATTN_DIM + MEM_DIM), win_layer)]
        args += [gm.reshape(1, D_MODEL), win]
        out_shape += [jax.ShapeDtypeStruct((ROWS, ATTN_DIM), BF16), jax.ShapeDtypeStruct((ROWS, MEM_DIM), BF16)]
        out_specs += [_row_spec(ATTN_DIM), _row_spec(MEM_DIM)]
        resident += D_MODEL * (ATTN_DIM + MEM_DIM) * 2
        streamed += TILE_ROWS * (ATTN_DIM + MEM_DIM) * 2
    return pl.pallas_call(
        functools.partial(_ffn_kernel, n_x=len(xs), attn_proj=attn_proj, layer=layer),
        out_shape=tuple(out_shape),
        grid=(ROW_TILES,),
        in_specs=in_specs,
        out_specs=tuple(out_specs),
        scratch_shapes=[pltpu.VMEM((TILE_ROWS, FFN_DIM), BF16)] + _FfnWeights.scratch_shapes(),
        compiler_params=_params(_vmem_limit(
            resident, streamed, _ACT_BYTES + _FfnWeights.SCRATCH_BYTES, 6 * _ROW_TILE_F32)),
        name="ffn_attn_proj" if attn_proj else "ffn",
    )(*args)


def _inproj_conv_kernel(x_ref, gm_ref, win_ref, cw_ref, pre_ref,
                        ytok_ref, qm_ref, tail_ref, us_ref, shift_ref):
    i = pl.program_id(0)
    hm = _rms(x_ref[...], gm_ref[...]).astype(BF16)
    qm_ref[...] = (_dot(hm, win_ref[:, 3 * CONV_DIM:]) * MEM_Q_SCALE).astype(BF16)

    def chunk(cc, prompt):
        sl = slice(cc * CONV_CHUNK, (cc + 1) * CONV_CHUNK)
        c_gate = _dot(hm, win_ref[:, CONV_DIM + cc * CONV_CHUNK:CONV_DIM + (cc + 1) * CONV_CHUNK])
        x_in = _dot(hm, win_ref[:, 2 * CONV_DIM + cc * CONV_CHUNK:2 * CONV_DIM + (cc + 1) * CONV_CHUNK])
        u = c_gate * x_in
        if prompt:
            shift_ref[V7X_SUBLANES:, sl] = u
            u1 = shift_ref[V7X_SUBLANES - 1:V7X_SUBLANES - 1 + TILE_ROWS, sl]
            u2 = shift_ref[V7X_SUBLANES - 2:V7X_SUBLANES - 2 + TILE_ROWS, sl]
            last = u[TILE_ROWS - V7X_SUBLANES:, :]
            shift_ref[:V7X_SUBLANES, sl] = last
            tail_ref[0, :, sl] = last
        else:
            t = lax.broadcasted_iota(jnp.int32, (TILE_ROWS, CONV_CHUNK), 0) % DEC_SEQ
            p2 = pre_ref[:, sl]
            p1 = pltpu.roll(p2, TILE_ROWS - 1, axis=0)
            u1 = jnp.where(t == 0, p1, pltpu.roll(u, 1, axis=0))
            u2 = jnp.where(t < 2, p2, pltpu.roll(u, 2, axis=0))
            us_ref[:, sl] = u
            tail_ref[0, :, sl] = jnp.zeros((V7X_SUBLANES, CONV_CHUNK), F32)
        w = cw_ref[:, sl]
        conv = w[0:1] * u2 + w[1:2] * u1 + w[2:3] * u
        b_gate = _dot(hm, win_ref[:, sl])
        ytok_ref[:, sl] = (b_gate * conv).astype(BF16)

    @pl.when(i < PROMPT_TILES)
    def _():
        @pl.when(i % TILES_PER_SEQ == 0)
        def _():
            shift_ref[:V7X_SUBLANES, :] = jnp.zeros((V7X_SUBLANES, CONV_DIM), F32)
        for cc in range(CONV_DIM // CONV_CHUNK):
            chunk(cc, True)

    @pl.when(i >= PROMPT_TILES)
    def _():
        for cc in range(CONV_DIM // CONV_CHUNK):
            chunk(cc, False)


def _inproj_conv(x1, gm, win, conv_w, prefix_rows, layer):
    sample_idx = lambda i: (jnp.maximum(i - PROMPT_TILES, 0), 0)
    win_bytes = D_MODEL * (3 * CONV_DIM + MEM_DIM) * 2
    return pl.pallas_call(
        _inproj_conv_kernel,
        out_shape=(jax.ShapeDtypeStruct((ROWS, CONV_DIM), BF16),
                   jax.ShapeDtypeStruct((ROWS, MEM_DIM), BF16),
                   jax.ShapeDtypeStruct((ROW_TILES, V7X_SUBLANES, CONV_DIM), F32),
                   jax.ShapeDtypeStruct((ROWS_SAMPLE, CONV_DIM), F32)),
        grid=(ROW_TILES,),
        in_specs=[_row_spec(D_MODEL), _resident((1, D_MODEL)),
                  _resident_layer((D_MODEL, 3 * CONV_DIM + MEM_DIM), layer),
                  _resident_layer((CONV_WIDTH, CONV_DIM), layer),
                  pl.BlockSpec((TILE_ROWS, CONV_DIM), sample_idx)],
        out_specs=(_row_spec(CONV_DIM), _row_spec(MEM_DIM),
                   pl.BlockSpec((1, V7X_SUBLANES, CONV_DIM), lambda i: (i, 0, 0)),
                   pl.BlockSpec((TILE_ROWS, CONV_DIM), sample_idx)),
        scratch_shapes=[pltpu.VMEM((TILE_ROWS + V7X_SUBLANES, CONV_DIM), F32)],
        compiler_params=_params(_vmem_limit(
            win_bytes, 3 * _ROW_TILE_F32 + TILE_ROWS * (CONV_DIM + MEM_DIM) * 2,
            _ROW_TILE_F32 + V7X_SUBLANES * CONV_DIM * 4, 6 * _ROW_TILE_F32)),
        name="inproj_conv",
    )(x1, gm.reshape(1, D_MODEL), win, conv_w, prefix_rows)


VZ_WIDTH = N_KV_HEADS * 2 * 2 * V7X_LANES


def _emit_shared_kv(x, g_ref, wk_ref, wkt_ref, wv_ref, wv2_ref, k_ref, v_ref, ktz_ref, vz_ref):
    hk = _rms(x, g_ref[...]).astype(BF16)
    k_ref[...] = _dot(hk, wk_ref[...])
    v_ref[...] = _dot(hk, wv_ref[...])
    kt = _dot_nt(wkt_ref[...], hk)
    v2 = _dot(hk, wv2_ref[...])
    lo = lax.broadcasted_iota(jnp.int32, (TILE_ROWS, V7X_LANES), 1) < HALF_LANES
    ones = (jnp.where(lo, 1.0, 0.0).astype(BF16), jnp.where(lo, 0.0, 1.0).astype(BF16))
    zero_k = jnp.zeros((HEAD_DIM, TILE_ROWS), BF16)
    for h in range(N_KV_HEADS):
        kth = kt[h * HEAD_DIM:(h + 1) * HEAD_DIM, :].astype(BF16)
        ktz_ref[h, 0] = jnp.concatenate([kth, zero_k], axis=0)
        ktz_ref[h, 1] = jnp.concatenate([zero_k, kth], axis=0)
        v2h = v2[:, h * V7X_LANES:(h + 1) * V7X_LANES]
        for e in range(2):
            vals = jnp.where(lo if e == 0 else jnp.logical_not(lo), v2h, 0.0).astype(BF16)
            c0 = (h * 2 + e) * 2 * V7X_LANES
            vz_ref[:, c0:c0 + 2 * V7X_LANES] = jnp.concatenate([vals, ones[e]], axis=1)


def _shared_kv_specs():
    in_specs = [_resident((1, D_MODEL)), _resident((D_MODEL, KV_DIM)), _resident((KV_DIM, D_MODEL)),
                _resident((D_MODEL, KV_DIM)), _resident((D_MODEL, 2 * KV_DIM))]
    out_shape = [jax.ShapeDtypeStruct((ROWS, KV_DIM), F32), jax.ShapeDtypeStruct((ROWS, KV_DIM), F32),
                 jax.ShapeDtypeStruct((N_KV_HEADS, 2, V7X_LANES, ROWS), BF16),
                 jax.ShapeDtypeStruct((ROWS, VZ_WIDTH), BF16)]
    out_specs = [_row_spec(KV_DIM), _row_spec(KV_DIM),
                 pl.BlockSpec((N_KV_HEADS, 2, V7X_LANES, TILE_ROWS), lambda i: (0, 0, 0, i)),
                 _row_spec(VZ_WIDTH)]
    resident = 5 * D_MODEL * KV_DIM * 2
    streamed = TILE_ROWS * (2 * KV_DIM * 4 + 2 * KV_DIM * 2 + VZ_WIDTH * 2)
    return in_specs, out_shape, out_specs, resident, streamed


def _attn_prompt_kernel(*refs, swa):
    if swa:
        (qm_ref, mk_ref, mve_ref, qa_ref, ktp_ref, ktc_ref, vzp_ref, vzc_ref, bias_ref,
         ymem_ref, ytok_ref) = refs
    else:
        qm_ref, mk_ref, mve_ref, ymem_ref = refs

    for h in range(MEM_HEADS):
        sl = slice(h * MEM_HEAD_DIM, (h + 1) * MEM_HEAD_DIM)
        s = _dot_nt(qm_ref[:, sl], mk_ref[:, sl])
        p = jnp.exp2(s - jnp.max(s, axis=-1, keepdims=True)).astype(BF16)
        oe = _dot(p, mve_ref[h])
        ymem_ref[:, sl] = (oe[:, :MEM_HEAD_DIM] / oe[:, MEM_HEAD_DIM:]).astype(BF16)
    if not swa:
        return

    no_prev = ((pl.program_id(0) % TILES_PER_SEQ) == 0).astype(jnp.int32)
    key0_col = lax.broadcasted_iota(jnp.int32, (V7X_LANES, WINDOW), 1) == 0
    key0_row = lax.broadcasted_iota(jnp.int32, (WINDOW, V7X_LANES), 0) == 0
    zero_kt = jnp.zeros((V7X_LANES, WINDOW), BF16)
    zero_v = jnp.zeros((WINDOW, V7X_LANES), BF16)

    for n in range(BLOCKS_PER_TILE):
        rows = slice(n * WINDOW, (n + 1) * WINDOW)
        for h in range(N_KV_HEADS):
            k_sel, v_sel = [], []
            for e in range(2):
                k_prev = ktp_ref[h, e] if n == 0 else ktc_ref[h, e, :, (n - 1) * WINDOW:n * WINDOW]
                k_prev = jnp.where(key0_col, zero_kt, k_prev)
                k_sel.append(jnp.concatenate([k_prev, ktc_ref[h, e, :, rows]], axis=1))
                c0 = (h * 2 + e) * 2 * V7X_LANES
                cols = slice(c0, c0 + 2 * V7X_LANES)
                v_prev = vzp_ref[:, cols] if n == 0 else vzc_ref[(n - 1) * WINDOW:n * WINDOW, cols]
                v_prev = jnp.concatenate(
                    [jnp.where(key0_row, zero_v, v_prev[:, :V7X_LANES]), v_prev[:, V7X_LANES:]], axis=1)
                v_sel.append(jnp.concatenate([v_prev, vzc_ref[rows, cols]], axis=0))
            for pr in range(GROUP // 2):
                c0 = (h * (GROUP // 2) + pr) * V7X_LANES
                qp = qa_ref[rows, c0:c0 + V7X_LANES]
                acc = None
                for e in range(2):
                    j = h * GROUP + pr * 2 + e
                    bias = bias_ref[no_prev, j] if n == 0 else bias_ref[0, j]
                    s = _dot(qp, k_sel[e]) + bias
                    p = jnp.exp2(s - jnp.max(s, axis=-1, keepdims=True)).astype(BF16)
                    part = _dot(p, v_sel[e])
                    acc = part if acc is None else acc + part
                ytok_ref[rows, c0:c0 + V7X_LANES] = (acc[:, :V7X_LANES] / acc[:, V7X_LANES:]).astype(BF16)


def _attn_prompt(qm, mk_bf, mv_ext, layer, swa_args=None):
    swa = swa_args is not None
    batch_of = lambda i: i // TILES_PER_SEQ
    in_specs = [_row_spec(MEM_DIM),
                pl.BlockSpec((None, None, MEM_TOKENS, MEM_DIM), lambda i: (layer, batch_of(i), 0, 0)),
                pl.BlockSpec((None, None, MEM_HEADS, MEM_TOKENS, 2 * MEM_HEAD_DIM),
                             lambda i: (layer, batch_of(i), 0, 0, 0))]
    args = [qm, mk_bf, mv_ext]
    out_shape = [jax.ShapeDtypeStruct((ROWS_PROMPT, MEM_DIM), BF16)]
    out_specs = [_row_spec(MEM_DIM)]
    streamed = TILE_ROWS * MEM_DIM * 4 + MEM_TOKENS * MEM_DIM * 2 * 3
    resident = 0
    if swa:
        qa, ktz, vz, bias_p, swa_layer = swa_args
        prev_blk = lambda i: jnp.maximum(i * BLOCKS_PER_TILE - 1, 0)
        in_specs += [_row_spec(ATTN_DIM),
                     pl.BlockSpec((N_KV_HEADS, 2, V7X_LANES, WINDOW), lambda i: (0, 0, 0, prev_blk(i))),
                     pl.BlockSpec((N_KV_HEADS, 2, V7X_LANES, TILE_ROWS), lambda i: (0, 0, 0, i)),
                     pl.BlockSpec((WINDOW, VZ_WIDTH), lambda i: (prev_blk(i), 0)),
                     pl.BlockSpec((TILE_ROWS, VZ_WIDTH), lambda i: (i, 0)),
                     _resident_layer((2, N_HEADS, WINDOW, 2 * WINDOW), swa_layer)]
        args += [qa, ktz, ktz, vz, vz, bias_p]
        out_shape += [jax.ShapeDtypeStruct((ROWS_PROMPT, ATTN_DIM), BF16)]
        out_specs += [_row_spec(ATTN_DIM)]
        resident = 2 * N_HEADS * WINDOW * 2 * WINDOW * 4
        streamed += 2 * TILE_ROWS * ATTN_DIM * 2 + (TILE_ROWS + WINDOW) * (2 * KV_DIM + VZ_WIDTH) * 2
    return pl.pallas_call(
        functools.partial(_attn_prompt_kernel, swa=swa),
        out_shape=tuple(out_shape),
        grid=(PROMPT_TILES,),
        in_specs=in_specs,
        out_specs=tuple(out_specs),
        compiler_params=_params(_vmem_limit(resident, streamed, 0, 8 * _ROW_TILE_F32)),
        name="attn_prompt_swa" if swa else "attn_prompt_mem",
    )(*args)


def _attn_sample_kernel(*refs, swa):
    if swa:
        (qm_ref, mk_ref, mv_ref, qa_ref, ck_ref, cv_ref, kn_ref, vn_ref, bias_ref,
         ymem_ref, ytok_ref) = refs
    else:
        qm_ref, mk_ref, mv_ref, ymem_ref = refs

    qm_all = qm_ref[...].astype(F32)
    mem_rows = MEM_HEADS * DEC_SEQ
    own_head = (lax.broadcasted_iota(jnp.int32, (mem_rows, MEM_TOKENS * MEM_HEADS), 1) % MEM_HEADS
                == lax.broadcasted_iota(jnp.int32, (mem_rows, MEM_TOKENS * MEM_HEADS), 0) // DEC_SEQ)
    if swa:
        qa_all = qa_ref[...].astype(F32)
        kn_all = kn_ref[...]
        vn_all = vn_ref[...]
        bias = bias_ref[...]
        key0 = lax.broadcasted_iota(jnp.int32, (WINDOW, KV_DIM), 0) == 0
        pad = jnp.zeros((SAMPLE_KEYS_PADDED - SAMPLE_KEYS, KV_DIM), F32)
        lo = lax.broadcasted_iota(jnp.int32, (DEC_SEQ, V7X_LANES), 1) < HALF_LANES
        hi = jnp.logical_not(lo)
        zero_slab = jnp.zeros((DEC_SEQ, V7X_LANES), F32)

    ymem_rows, ytok_rows = [], []
    for b in range(SAMPLE_BATCH_BLOCK):
        rows = slice(b * DEC_SEQ, (b + 1) * DEC_SEQ)
        qm = qm_all[rows]
        q_heads = jnp.concatenate(
            [qm[:, h * MEM_HEAD_DIM:(h + 1) * MEM_HEAD_DIM] for h in range(MEM_HEADS)], axis=0)
        s = jnp.where(own_head, _dot_nt(q_heads, mk_ref[b]), NEG_INF)
        p = jnp.exp2(s - jnp.max(s, axis=-1, keepdims=True))
        o = _dot(p, mv_ref[b]) / jnp.sum(p, axis=-1, keepdims=True)
        ymem_rows.append(jnp.concatenate(
            [o[h * DEC_SEQ:(h + 1) * DEC_SEQ] for h in range(MEM_HEADS)], axis=1))
        if not swa:
            continue
        qb = qa_all[rows]
        groups = []
        for j in range(N_HEADS):
            kvh = j // GROUP
            slab = qb[:, (j // 2) * V7X_LANES:(j // 2 + 1) * V7X_LANES]
            if j % 2 != kvh % 2:
                slab = pltpu.roll(slab, HALF_LANES, axis=1)
            slab = jnp.where(lo if kvh % 2 == 0 else hi, slab, 0.0)
            groups.append(jnp.concatenate([slab, zero_slab] if kvh // 2 == 0 else [zero_slab, slab], axis=1))
        q_bd = jnp.concatenate(groups, axis=0)
        k_cat = jnp.concatenate([jnp.where(key0, 0.0, ck_ref[b]), kn_all[rows], pad], axis=0)
        v_cat = jnp.concatenate([jnp.where(key0, 0.0, cv_ref[b]), vn_all[rows], pad], axis=0)
        s = _dot_nt(q_bd, k_cat) + bias
        p = jnp.exp2(s - jnp.max(s, axis=-1, keepdims=True))
        o_full = _dot(p, v_cat) / jnp.sum(p, axis=-1, keepdims=True)
        pairs = []
        for pair in range(N_HEADS // 2):
            acc = None
            for e in range(2):
                j = pair * 2 + e
                kvh = j // GROUP
                slab = o_full[j * DEC_SEQ:(j + 1) * DEC_SEQ, (kvh // 2) * V7X_LANES:(kvh // 2 + 1) * V7X_LANES]
                if e != kvh % 2:
                    slab = pltpu.roll(slab, HALF_LANES, axis=1)
                slab = jnp.where(lo if e == 0 else hi, slab, 0.0)
                acc = slab if acc is None else acc + slab
            pairs.append(acc)
        ytok_rows.append(jnp.concatenate(pairs, axis=1))

    ymem_ref[...] = jnp.concatenate(ymem_rows, axis=0).astype(BF16)
    if swa:
        ytok_ref[...] = jnp.concatenate(ytok_rows, axis=0).astype(BF16)


def _attn_sample(qm, cache_k, cache_v, layer, swa_args=None):
    swa = swa_args is not None
    row0 = ROWS_PROMPT // SAMPLE_BLOCK_ROWS
    blk_rows = lambda width: pl.BlockSpec((SAMPLE_BLOCK_ROWS, width), lambda i: (row0 + i, 0))
    out_rows = lambda width: pl.BlockSpec((SAMPLE_BLOCK_ROWS, width), lambda i: (i, 0))
    cache_spec = pl.BlockSpec((None, SAMPLE_BATCH_BLOCK, MEM_TOKENS * MEM_HEADS, MEM_HEAD_DIM),
                              lambda i: (layer, i, 0, 0))
    in_specs = [blk_rows(MEM_DIM), cache_spec, cache_spec]
    args = [qm, cache_k, cache_v]
    out_shape = [jax.ShapeDtypeStruct((ROWS_SAMPLE, MEM_DIM), BF16)]
    out_specs = [out_rows(MEM_DIM)]
    streamed = 2 * SAMPLE_BATCH_BLOCK * MEM_TOKENS * MEM_DIM * 4 + SAMPLE_BLOCK_ROWS * MEM_DIM * 4
    if swa:
        qa, swa_k, swa_v, k_new, v_new, bias_s, swa_layer = swa_args
        swa_spec = pl.BlockSpec((SAMPLE_BATCH_BLOCK, WINDOW, KV_DIM), lambda i: (i, 0, 0))
        in_specs += [blk_rows(ATTN_DIM), swa_spec, swa_spec, out_rows(KV_DIM), out_rows(KV_DIM),
                     _resident_layer((N_HEADS * DEC_SEQ, SAMPLE_KEYS_PADDED), swa_layer)]
        args += [qa, swa_k, swa_v, k_new, v_new, bias_s]
        out_shape += [jax.ShapeDtypeStruct((ROWS_SAMPLE, ATTN_DIM), BF16)]
        out_specs += [out_rows(ATTN_DIM)]
        streamed += 2 * SAMPLE_BATCH_BLOCK * WINDOW * KV_DIM * 4 + SAMPLE_BLOCK_ROWS * (ATTN_DIM + KV_DIM) * 4
    return pl.pallas_call(
        functools.partial(_attn_sample_kernel, swa=swa),
        out_shape=tuple(out_shape),
        grid=(DEC_BATCH // SAMPLE_BATCH_BLOCK,),
        in_specs=in_specs,
        out_specs=tuple(out_specs),
        compiler_params=_params(_vmem_limit(1 << 20, streamed, 0, 6 * _ROW_TILE_F32)),
        name="attn_sample_swa" if swa else "attn_sample_mem",
    )(*args)


def _outffn_kernel(*refs, split_tok, final, shared_kv, tile0, layer):
    refs = list(refs)
    x1_ref = refs.pop(0)
    tok_refs = [refs.pop(0) for _ in range(2 if split_tok else 1)]
    ymp_ref, yms_ref, wo_ref, g_ref = [refs.pop(0) for _ in range(4)]
    w_hbm = [refs.pop(0) for _ in range(3)]
    gf_ref = refs.pop(0) if final else None
    kv_in = [refs.pop(0) for _ in range(5)] if shared_kv else None
    out_ref = refs.pop(0)
    kv_out = [refs.pop(0) for _ in range(4)] if shared_kv else None
    act_ref = refs.pop(0)
    w = _FfnWeights(layer, w_hbm, refs)
    tok_dim = wo_ref.shape[0] - MEM_DIM

    def body(load_weights):
        y_tok = _pick_group(*tok_refs, tile0=tile0) if split_tok else tok_refs[0][...]
        y_mem = _pick_group(ymp_ref, yms_ref, tile0=tile0)
        x2 = x1_ref[...] + _dot(y_tok, wo_ref[:tok_dim, :]) + _dot(y_mem, wo_ref[tok_dim:, :])
        x3 = _ffn_half_step(x2, g_ref, w, act_ref, load_weights)
        out_ref[...] = _rms(x3, gf_ref[...]) if final else x3
        if shared_kv:
            _emit_shared_kv(x3, *kv_in, *kv_out)

    _first_step_loads(w, body)


def _outffn(x1, y_tok, ymem_p, ymem_s, wo, wo_layer, g, ffn_w, layer, final_gain=None, tiles=(0, ROW_TILES),
            shared_kv_w=None):
    split_tok = isinstance(y_tok, tuple)
    final = final_gain is not None
    shared_kv = shared_kv_w is not None
    tile0, n_tiles = tiles
    tok_dim = wo.shape[1] - MEM_DIM
    in_specs = [_row_spec(D_MODEL, tile0)]
    args = [x1]
    if split_tok:
        in_specs += _group_specs(tok_dim, tile0)
        args += list(y_tok)
    else:
        in_specs += [_row_spec(tok_dim, tile0)]
        args += [y_tok]
    in_specs += (_group_specs(MEM_DIM, tile0) + [_resident_layer((tok_dim + MEM_DIM, D_MODEL), wo_layer)]
                 + [_resident((1, D_MODEL))] + _FfnWeights.in_specs())
    args += [ymem_p, ymem_s, wo, g.reshape(1, D_MODEL)] + list(ffn_w)
    if final:
        in_specs += [_resident((1, D_MODEL))]
        args += [final_gain.reshape(1, D_MODEL)]
    resident = (tok_dim + MEM_DIM) * D_MODEL * 2
    streamed = 2 * _ROW_TILE_F32 + 2 * TILE_ROWS * (tok_dim + MEM_DIM) * 2
    out_shape = [jax.ShapeDtypeStruct((n_tiles * TILE_ROWS, D_MODEL), F32)]
    out_specs = [_row_spec(D_MODEL)]
    if shared_kv:
        assert tiles == (0, ROW_TILES)
        kv_in_specs, kv_shape, kv_specs, kv_resident, kv_streamed = _shared_kv_specs()
        in_specs += kv_in_specs
        args += [shared_kv_w[0].reshape(1, D_MODEL)] + list(shared_kv_w[1:])
        out_shape += kv_shape
        out_specs += kv_specs
        resident += kv_resident
        streamed += kv_streamed
    return pl.pallas_call(
        functools.partial(_outffn_kernel, split_tok=split_tok, final=final, shared_kv=shared_kv,
                          tile0=tile0, layer=layer),
        out_shape=tuple(out_shape),
        grid=(n_tiles,),
        in_specs=in_specs,
        out_specs=tuple(out_specs),
        scratch_shapes=[pltpu.VMEM((TILE_ROWS, FFN_DIM), BF16)] + _FfnWeights.scratch_shapes(),
        compiler_params=_params(_vmem_limit(
            resident, streamed, _ACT_BYTES + _FfnWeights.SCRATCH_BYTES, 3 * _ROW_TILE_F32)),
        name="outffn_final" if final else "outffn",
    )(*args)


def kernel(x_prompt, x_sample, state_conv, cache_swa_k, cache_swa_v, cache_mem_k, cache_mem_v, mem_prompt, ffn1_norm, ffn1_wg, ffn1_wu, ffn1_wd, mix_norm, w_in_a, conv_w, w_out_a, kv_norm, w_kv, w_in_b, attn_sinks, rel_bias, w_out_b, mem_norm, w_mem_kv, ffn2_norm, ffn2_wg, ffn2_wu, ffn2_wd, final_norm):
    bf = lambda w: w.astype(BF16)
    ffn1 = (ffn1_wg, ffn1_wu, ffn1_wd)
    ffn2 = (ffn2_wg, ffn2_wu, ffn2_wd)
    w_in_a_bf, w_out_a_bf, w_in_b_bf, w_out_b_bf = bf(w_in_a), bf(w_out_a), bf(w_in_b), bf(w_out_b)
    wk = w_kv[:, :KV_DIM]
    wv = w_kv[:, KV_DIM:]
    wv2 = jnp.broadcast_to(wv.reshape(D_MODEL, N_KV_HEADS, 1, HEAD_DIM),
                           (D_MODEL, N_KV_HEADS, 2, HEAD_DIM)).reshape(D_MODEL, 2 * KV_DIM)

    mem_k, mem_v, mem_k_bf, mem_v_ext = _memkv(mem_prompt, mem_norm, bf(w_mem_kv))
    bias_p, bias_s = _bias_tables(rel_bias, attn_sinks)
    swa_k_cache = cache_swa_k.reshape(DEC_BATCH, WINDOW, KV_DIM)
    swa_v_cache = cache_swa_v.reshape(DEC_BATCH, WINDOW, KV_DIM)
    mem_rows_shape = (DEPTH, DEC_BATCH, MEM_TOKENS * MEM_HEADS, MEM_HEAD_DIM)
    cache_k = cache_mem_k.reshape(mem_rows_shape)
    cache_v = cache_mem_v.reshape(mem_rows_shape)

    x = (x_prompt.reshape(ROWS_PROMPT, D_MODEL), x_sample.reshape(ROWS_SAMPLE, D_MODEL))
    tails, sample_us = [], []
    k_rows = v_rows = ktz = vz = k_new = v_new = None
    for l in range(DEPTH):
        last = l == DEPTH - 1
        if l < N_A_LAYERS:
            (x1,) = _ffn(x, ffn1_norm[l], ffn1, l)
            prefix_rows = jnp.pad(state_conv[l], ((0, 0), (0, DEC_SEQ - (CONV_WIDTH - 1)), (0, 0)))
            y_tok, qm, tail, us = _inproj_conv(x1, mix_norm[l], w_in_a_bf, conv_w,
                                               prefix_rows.reshape(ROWS_SAMPLE, CONV_DIM), l)
            tails.append(tail)
            sample_us.append(us)
            (ymem_p,) = _attn_prompt(qm, mem_k_bf, mem_v_ext, l)
            (ymem_s,) = _attn_sample(qm, cache_k, cache_v, l)
            wo, wo_layer = w_out_a_bf, l
        else:
            j = l - N_A_LAYERS
            x1, qa, qm = _ffn(x, ffn1_norm[l], ffn1, l, mix_norm[l], w_in_b_bf, j)
            ymem_p, ytok_p = _attn_prompt(qm, mem_k_bf, mem_v_ext, l, (qa, ktz, vz, bias_p, j))
            ymem_s, ytok_s = _attn_sample(qm, cache_k, cache_v, l,
                                          (qa, swa_k_cache, swa_v_cache, k_new, v_new, bias_s, j))
            y_tok = (ytok_p, ytok_s)
            wo, wo_layer = w_out_b_bf, j
        if l == N_A_LAYERS - 1:
            x, k_rows, v_rows, ktz, vz = _outffn(
                x1, y_tok, ymem_p, ymem_s, wo, wo_layer, ffn2_norm[l], ffn2, l,
                shared_kv_w=(kv_norm, bf(wk), bf(wk.T), bf(wv), bf(wv2)))
            k_new = k_rows[ROWS_PROMPT:]
            v_new = v_rows[ROWS_PROMPT:]
        elif not last:
            (x,) = _outffn(x1, y_tok, ymem_p, ymem_s, wo, wo_layer, ffn2_norm[l], ffn2, l)
        else:
            y_prompt, y_sample = [
                _outffn(x1, y_tok, ymem_p, ymem_s, wo, wo_layer, ffn2_norm[l], ffn2, l,
                        final_gain=final_norm, tiles=t)[0]
                for t in ((0, PROMPT_TILES), (PROMPT_TILES, SAMPLE_TILES))]

    keep = CONV_WIDTH - 1
    last_tiles = np.arange(BATCH) * TILES_PER_SEQ + TILES_PER_SEQ - 1
    conv_state_prompt = jnp.stack([t[last_tiles, V7X_SUBLANES - keep:, :] for t in tails])
    conv_state_sample = jnp.stack([u.reshape(DEC_BATCH, DEC_SEQ, CONV_DIM)[:, DEC_SEQ - keep:, :] for u in sample_us])
    k_tail = jnp.stack([k_rows[(b + 1) * SEQ - WINDOW:(b + 1) * SEQ] for b in range(BATCH)])
    v_tail = jnp.stack([v_rows[(b + 1) * SEQ - WINDOW:(b + 1) * SEQ] for b in range(BATCH)])
    swa_k_prompt = k_tail.reshape(BATCH, WINDOW, N_KV_HEADS, HEAD_DIM)
    swa_v_prompt = v_tail.reshape(BATCH, WINDOW, N_KV_HEADS, HEAD_DIM)
    swa_k_sample = jnp.concatenate(
        [cache_swa_k[:, DEC_SEQ:], k_new.reshape(DEC_BATCH, DEC_SEQ, N_KV_HEADS, HEAD_DIM)], axis=1)
    swa_v_sample = jnp.concatenate(
        [cache_swa_v[:, DEC_SEQ:], v_new.reshape(DEC_BATCH, DEC_SEQ, N_KV_HEADS, HEAD_DIM)], axis=1)
    mem_shape = (DEPTH, BATCH, MEM_TOKENS, MEM_HEADS, MEM_HEAD_DIM)
    return (y_prompt.reshape(BATCH, SEQ, D_MODEL), y_sample.reshape(DEC_BATCH, DEC_SEQ, D_MODEL),
            conv_state_prompt, conv_state_sample,
            swa_k_prompt, swa_v_prompt, swa_k_sample, swa_v_sample,
            mem_k.reshape(mem_shape), mem_v.reshape(mem_shape))
```

```python
import functools
import math

import numpy as np
import jax
import jax.numpy as jnp
from jax import lax
from jax.experimental import pallas as pl
from jax.experimental.pallas import tpu as pltpu

D_MODEL = 1024
BATCH = 2
SEQ = 8192
DEPTH = 4
DEC_BATCH = 128
DEC_SEQ = 8
N_A_LAYERS = DEPTH // 2
FFN_DIM = 2816
CONV_DIM = D_MODEL
CONV_WIDTH = 3
N_HEADS = 16
N_KV_HEADS = 4
HEAD_DIM = 64
GROUP = N_HEADS // N_KV_HEADS
ATTN_DIM = N_HEADS * HEAD_DIM
KV_DIM = N_KV_HEADS * HEAD_DIM
WINDOW = 128
REL_BUCKETS = 32
REL_MAX_DIST = 128
MEM_TOKENS = 256
MEM_HEADS = 4
MEM_HEAD_DIM = 128
MEM_DIM = MEM_HEADS * MEM_HEAD_DIM
RMS_EPS = 1e-5

F32 = jnp.float32
BF16 = jnp.bfloat16
NEG_INF = float("-inf")

V7X_LANES = 128
V7X_SUBLANES = 8
V7X_MXU_DIM = 256
V7X_VMEM_BYTES = 64 * 1024 * 1024

ROWS_PROMPT = BATCH * SEQ
ROWS_SAMPLE = DEC_BATCH * DEC_SEQ
ROWS = ROWS_PROMPT + ROWS_SAMPLE
TILE_ROWS = 512
PROMPT_TILES = ROWS_PROMPT // TILE_ROWS
SAMPLE_TILES = ROWS_SAMPLE // TILE_ROWS
ROW_TILES = PROMPT_TILES + SAMPLE_TILES
TILES_PER_SEQ = SEQ // TILE_ROWS
FFN_CHUNK = V7X_MXU_DIM
CONV_CHUNK = V7X_MXU_DIM
BLOCKS_PER_TILE = TILE_ROWS // WINDOW
SAMPLE_BATCH_BLOCK = 8
SAMPLE_BLOCK_ROWS = SAMPLE_BATCH_BLOCK * DEC_SEQ
SAMPLE_KEYS = WINDOW + DEC_SEQ
SAMPLE_KEYS_PADDED = 2 * WINDOW
HALF_LANES = V7X_LANES // 2
LOG2E = math.log2(math.e)
MEM_Q_SCALE = MEM_HEAD_DIM ** -0.5 * LOG2E
ATTN_Q_SCALE = HEAD_DIM ** -0.5 * LOG2E

assert HEAD_DIM == HALF_LANES and MEM_HEAD_DIM == V7X_LANES
assert ROWS_PROMPT % TILE_ROWS == 0 and ROWS_SAMPLE % TILE_ROWS == 0 and SEQ % TILE_ROWS == 0
assert FFN_DIM % FFN_CHUNK == 0 and TILE_ROWS % WINDOW == 0


def _vmem_limit(resident_bytes, streamed_bytes, scratch_bytes, temp_bytes):
    need = resident_bytes + 2 * streamed_bytes + scratch_bytes + temp_bytes
    assert need < V7X_VMEM_BYTES, need
    return int(need)


def _params(vmem_bytes, n_axes=1):
    return pltpu.CompilerParams(
        dimension_semantics=("arbitrary",) * n_axes, vmem_limit_bytes=vmem_bytes)


def _resident(shape):
    zeros = (0,) * len(shape)
    return pl.BlockSpec(shape, lambda *_: zeros, pipeline_mode=pl.Buffered(1))


def _resident_layer(shape, layer):
    idx = (layer,) + (0,) * len(shape)
    return pl.BlockSpec((None,) + tuple(shape), lambda *_: idx, pipeline_mode=pl.Buffered(1))


def _row_spec(width):
    return pl.BlockSpec((TILE_ROWS, width), lambda i: (i, 0))


def _group_specs(width):
    return [pl.BlockSpec((TILE_ROWS, width), lambda i: (jnp.minimum(i, PROMPT_TILES - 1), 0)),
            pl.BlockSpec((TILE_ROWS, width), lambda i: (jnp.maximum(i - PROMPT_TILES, 0), 0))]


def _pick_group(prompt_ref, sample_ref):
    return jnp.where(pl.program_id(0) < PROMPT_TILES, prompt_ref[...], sample_ref[...])


def _dot(a, b):
    return jnp.dot(a, b, preferred_element_type=F32)


def _dot_nt(a, b):
    return lax.dot_general(a, b, (((1,), (1,)), ((), ())), preferred_element_type=F32)


def _rms(x, g):
    return x * lax.rsqrt(jnp.mean(x * x, axis=-1, keepdims=True) + RMS_EPS) * g


FFN_CHUNKS = FFN_DIM // FFN_CHUNK
FFN_STAGE_SLOTS = 2


class _FfnWeights:
    def __init__(self, layer, hbm_refs, scratch_refs):
        self.layer = layer
        self.wg_hbm, self.wu_hbm, self.wd_hbm = hbm_refs
        self.wg, self.wu, self.wd, self.stage_in, self.stage_out, self.sems = scratch_refs

    @staticmethod
    def in_specs():
        return [pl.BlockSpec(memory_space=pl.ANY)] * 3

    @staticmethod
    def scratch_shapes():
        return [pltpu.VMEM((D_MODEL, FFN_DIM), BF16), pltpu.VMEM((D_MODEL, FFN_DIM), BF16),
                pltpu.VMEM((FFN_DIM, D_MODEL), BF16),
                pltpu.VMEM((2, FFN_STAGE_SLOTS, D_MODEL, FFN_CHUNK), F32),
                pltpu.VMEM((FFN_STAGE_SLOTS, FFN_CHUNK, D_MODEL), F32),
                pltpu.SemaphoreType.DMA((3, FFN_STAGE_SLOTS))]

    SCRATCH_BYTES = 3 * D_MODEL * FFN_DIM * 2 + 3 * FFN_STAGE_SLOTS * D_MODEL * FFN_CHUNK * 4

    def _copy(self, stream, c):
        slot = c % FFN_STAGE_SLOTS
        cols = pl.ds(c * FFN_CHUNK, FFN_CHUNK)
        if stream == 0:
            src, dst = self.wg_hbm.at[self.layer, :, cols], self.stage_in.at[0, slot]
        elif stream == 1:
            src, dst = self.wu_hbm.at[self.layer, :, cols], self.stage_in.at[1, slot]
        else:
            src, dst = self.wd_hbm.at[self.layer, cols, :], self.stage_out.at[slot]
        return pltpu.make_async_copy(src, dst, self.sems.at[stream, slot])

    def prime(self):
        for stream in range(3):
            for c in range(FFN_STAGE_SLOTS):
                self._copy(stream, c).start()

    def fetch(self, c):
        slot = c % FFN_STAGE_SLOTS
        sl = slice(c * FFN_CHUNK, (c + 1) * FFN_CHUNK)
        for stream in range(3):
            self._copy(stream, c).wait()
            if stream == 0:
                self.wg[:, sl] = self.stage_in[0, slot].astype(BF16)
            elif stream == 1:
                self.wu[:, sl] = self.stage_in[1, slot].astype(BF16)
            else:
                self.wd[sl, :] = self.stage_out[slot].astype(BF16)
            if c + FFN_STAGE_SLOTS < FFN_CHUNKS:
                self._copy(stream, c + FFN_STAGE_SLOTS).start()


def _ffn_half_step(x, g_ref, w, act_ref, load_weights):
    inv_rms = lax.rsqrt(jnp.mean(x * x, axis=-1, keepdims=True) + RMS_EPS)
    inv_rms_chunk = jnp.broadcast_to(inv_rms, (x.shape[0], FFN_CHUNK))
    h = (x * g_ref[...]).astype(BF16)
    for c in range(FFN_CHUNKS):
        if load_weights:
            w.fetch(c)
        sl = slice(c * FFN_CHUNK, (c + 1) * FFN_CHUNK)
        gate = _dot(h, w.wg[:, sl]) * inv_rms_chunk
        up = _dot(h, w.wu[:, sl])
        act_ref[:, sl] = (gate / (1.0 + jnp.exp(-gate)) * up).astype(BF16)
    return x + (0.5 * inv_rms) * _dot(act_ref[...], w.wd[...])


def _first_step_loads(w, body):
    @pl.when(pl.program_id(0) == 0)
    def _():
        w.prime()
        body(True)

    @pl.when(pl.program_id(0) > 0)
    def _():
        body(False)


_ROW_TILE_F32 = TILE_ROWS * D_MODEL * 4
_ACT_BYTES = TILE_ROWS * FFN_DIM * 2


def _t5_bucket_np(dist):
    n = np.maximum(dist, 0)
    exact = REL_BUCKETS // 2
    nf = np.maximum(n, 1).astype(np.float32)
    large = exact + (np.log(nf / np.float32(exact)) / np.float32(math.log(REL_MAX_DIST / exact))
                     * np.float32(REL_BUCKETS - exact)).astype(np.int32)
    large = np.minimum(large, REL_BUCKETS - 1)
    return np.where(n < exact, n, large).astype(np.int32)


def _bucket_tables():
    q = np.arange(WINDOW)[:, None]
    k = np.arange(2 * WINDOW)[None, :]
    dist = WINDOW + q - k
    prompt = np.where((dist >= 0) & (dist < WINDOW), _t5_bucket_np(dist), -1)
    t = (np.arange(N_HEADS * DEC_SEQ) % DEC_SEQ)[:, None]
    k = np.arange(SAMPLE_KEYS_PADDED)[None, :]
    dist = WINDOW + t - k
    ok = (dist >= 0) & (dist < WINDOW) & (k < SAMPLE_KEYS)
    sample = np.where(ok, _t5_bucket_np(dist), -1)
    return prompt.astype(np.int32), sample.astype(np.int32)


def _bias_kernel(rel_ref, sink_ref, bp_ref, bs_ref, op_ref, os_ref):
    layer = pl.program_id(0)
    j = pl.program_id(1)
    sink = sink_ref[layer, j] * LOG2E

    def build(bucket):
        acc = jnp.zeros(bucket.shape, F32)
        for b in range(REL_BUCKETS):
            acc = jnp.where(bucket == b, rel_ref[b, j], acc)
        return jnp.where(bucket < 0, NEG_INF, acc * LOG2E)

    col_p = lax.broadcasted_iota(jnp.int32, (WINDOW, 2 * WINDOW), 1)
    table = build(bp_ref[...])
    op_ref[0, 0, 0] = jnp.where(col_p == 0, sink, table)
    op_ref[0, 1, 0] = jnp.where(col_p == 0, sink, jnp.where(col_p < WINDOW, NEG_INF, table))
    col_s = lax.broadcasted_iota(jnp.int32, (DEC_SEQ, SAMPLE_KEYS_PADDED), 1)
    os_ref[0] = jnp.where(col_s == 0, sink, build(bs_ref[...]))


def _bias_tables(rel_bias, attn_sinks):
    bp, bs = _bucket_tables()
    n_layers = attn_sinks.shape[0]
    return pl.pallas_call(
        _bias_kernel,
        out_shape=(jax.ShapeDtypeStruct((n_layers, 2, N_HEADS, WINDOW, 2 * WINDOW), F32),
                   jax.ShapeDtypeStruct((n_layers, N_HEADS * DEC_SEQ, SAMPLE_KEYS_PADDED), F32)),
        grid=(n_layers, N_HEADS),
        in_specs=[pl.BlockSpec(memory_space=pltpu.SMEM), pl.BlockSpec(memory_space=pltpu.SMEM),
                  pl.BlockSpec((WINDOW, 2 * WINDOW), lambda l, j: (0, 0)),
                  pl.BlockSpec((DEC_SEQ, SAMPLE_KEYS_PADDED), lambda l, j: (j, 0))],
        out_specs=(pl.BlockSpec((1, 2, 1, WINDOW, 2 * WINDOW), lambda l, j: (l, 0, j, 0, 0)),
                   pl.BlockSpec((1, DEC_SEQ, SAMPLE_KEYS_PADDED), lambda l, j: (l, j, 0))),
        compiler_params=_params(16 * 1024 * 1024, 2),
        name="bias_tables",
    )(rel_bias, attn_sinks, jnp.asarray(bp), jnp.asarray(bs))


def _memkv_kernel(m_ref, g_ref, w_ref, k_ref, v_ref, kb_ref, ve_ref):
    hn = _rms(m_ref[...], g_ref[...]).astype(BF16)
    kv = _dot(hn, w_ref[...])
    k = kv[:, :MEM_DIM]
    v = kv[:, MEM_DIM:]
    k_ref[...] = k
    v_ref[...] = v
    kb_ref[...] = k.astype(BF16)
    ones = jnp.ones((MEM_TOKENS, MEM_HEAD_DIM), BF16)
    for h in range(MEM_HEADS):
        vh = v[:, h * MEM_HEAD_DIM:(h + 1) * MEM_HEAD_DIM].astype(BF16)
        ve_ref[h] = jnp.concatenate([vh, ones], axis=1)


def _memkv(mem_prompt, mem_norm, w_mem_kv_bf):
    shp = (DEPTH, BATCH, MEM_TOKENS, MEM_DIM)
    blk = pl.BlockSpec((None, None, MEM_TOKENS, MEM_DIM), lambda l, b: (l, b, 0, 0))
    return pl.pallas_call(
        _memkv_kernel,
        out_shape=(jax.ShapeDtypeStruct(shp, F32), jax.ShapeDtypeStruct(shp, F32),
                   jax.ShapeDtypeStruct(shp, BF16),
                   jax.ShapeDtypeStruct((DEPTH, BATCH, MEM_HEADS, MEM_TOKENS, 2 * MEM_HEAD_DIM), BF16)),
        grid=(DEPTH, BATCH),
        in_specs=[pl.BlockSpec((None, MEM_TOKENS, D_MODEL), lambda l, b: (b, 0, 0)),
                  pl.BlockSpec((None, 1, D_MODEL), lambda l, b: (l, 0, 0)),
                  pl.BlockSpec((None, D_MODEL, 2 * MEM_DIM), lambda l, b: (l, 0, 0))],
        out_specs=(blk, blk, blk,
                   pl.BlockSpec((None, None, MEM_HEADS, MEM_TOKENS, 2 * MEM_HEAD_DIM),
                                lambda l, b: (l, b, 0, 0, 0))),
        compiler_params=_params(32 * 1024 * 1024, 2),
        name="memkv",
    )(mem_prompt, mem_norm.reshape(DEPTH, 1, D_MODEL), w_mem_kv_bf)


def _ffn_kernel(*refs, n_x, attn_proj, shared_kv, layer):
    refs = list(refs)
    x_refs = [refs.pop(0) for _ in range(n_x)]
    g_ref = refs.pop(0)
    w_hbm = [refs.pop(0) for _ in range(3)]
    if attn_proj:
        gm_ref, win_ref = refs.pop(0), refs.pop(0)
    kv_in = [refs.pop(0) for _ in range(5)] if shared_kv else None
    x1_ref = refs.pop(0)
    if attn_proj:
        qa_ref, qm_ref = refs.pop(0), refs.pop(0)
    kv_out = [refs.pop(0) for _ in range(4)] if shared_kv else None
    act_ref = refs.pop(0)
    w = _FfnWeights(layer, w_hbm, refs)

    def body(load_weights):
        x = _pick_group(*x_refs) if n_x == 2 else x_refs[0][...]
        if shared_kv:
            _emit_shared_kv(x, *kv_in, *kv_out)
        x1 = _ffn_half_step(x, g_ref, w, act_ref, load_weights)
        x1_ref[...] = x1
        if attn_proj:
            hm = _rms(x1, gm_ref[...]).astype(BF16)
            qa_ref[...] = (_dot(hm, win_ref[:, :ATTN_DIM]) * ATTN_Q_SCALE).astype(BF16)
            qm_ref[...] = (_dot(hm, win_ref[:, ATTN_DIM:]) * MEM_Q_SCALE).astype(BF16)

    _first_step_loads(w, body)


def _ffn(x, g, ffn_w, layer, gm=None, win=None, win_layer=None, shared_kv_w=None):
    attn_proj = win is not None
    shared_kv = shared_kv_w is not None
    xs = list(x) if isinstance(x, tuple) else [x]
    in_specs = ((_group_specs(D_MODEL) if len(xs) == 2 else [_row_spec(D_MODEL)])
                + [_resident((1, D_MODEL))] + _FfnWeights.in_specs())
    args = xs + [g.reshape(1, D_MODEL)] + list(ffn_w)
    out_shape = [jax.ShapeDtypeStruct((ROWS, D_MODEL), F32)]
    out_specs = [_row_spec(D_MODEL)]
    resident = 0
    streamed = (1 + len(xs)) * _ROW_TILE_F32
    if attn_proj:
        in_specs += [_resident((1, D_MODEL)), _resident_layer((D_MODEL, ATTN_DIM + MEM_DIM), win_layer)]
        args += [gm.reshape(1, D_MODEL), win]
        out_shape += [jax.ShapeDtypeStruct((ROWS, ATTN_DIM), BF16), jax.ShapeDtypeStruct((ROWS, MEM_DIM), BF16)]
        out_specs += [_row_spec(ATTN_DIM), _row_spec(MEM_DIM)]
        resident += D_MODEL * (ATTN_DIM + MEM_DIM) * 2
        streamed += TILE_ROWS * (ATTN_DIM + MEM_DIM) * 2
    if shared_kv:
        kv_in_specs, kv_shape, kv_specs, kv_resident, kv_streamed = _shared_kv_specs()
        in_specs += kv_in_specs
        args += [shared_kv_w[0].reshape(1, D_MODEL)] + list(shared_kv_w[1:])
        out_shape += kv_shape
        out_specs += kv_specs
        resident += kv_resident
        streamed += kv_streamed
    return pl.pallas_call(
        functools.partial(_ffn_kernel, n_x=len(xs), attn_proj=attn_proj, shared_kv=shared_kv, layer=layer),
        out_shape=tuple(out_shape),
        grid=(ROW_TILES,),
        in_specs=in_specs,
        out_specs=tuple(out_specs),
        scratch_shapes=[pltpu.VMEM((TILE_ROWS, FFN_DIM), BF16)] + _FfnWeights.scratch_shapes(),
        compiler_params=_params(_vmem_limit(
            resident, streamed, _ACT_BYTES + _FfnWeights.SCRATCH_BYTES, 3 * _ROW_TILE_F32)),
        name="ffn_attn_proj" if attn_proj else "ffn",
    )(*args)


def _inproj_conv_kernel(x_ref, gm_ref, win_ref, cw_ref, pre_ref,
                        ytok_ref, qm_ref, tail_ref, us_ref, shift_ref):
    i = pl.program_id(0)
    hm = _rms(x_ref[...], gm_ref[...]).astype(BF16)
    qm_ref[...] = (_dot(hm, win_ref[:, 3 * CONV_DIM:]) * MEM_Q_SCALE).astype(BF16)

    def chunk(cc, prompt):
        sl = slice(cc * CONV_CHUNK, (cc + 1) * CONV_CHUNK)
        c_gate = _dot(hm, win_ref[:, CONV_DIM + cc * CONV_CHUNK:CONV_DIM + (cc + 1) * CONV_CHUNK])
        x_in = _dot(hm, win_ref[:, 2 * CONV_DIM + cc * CONV_CHUNK:2 * CONV_DIM + (cc + 1) * CONV_CHUNK])
        u = c_gate * x_in
        if prompt:
            shift_ref[V7X_SUBLANES:, sl] = u
            u1 = shift_ref[V7X_SUBLANES - 1:V7X_SUBLANES - 1 + TILE_ROWS, sl]
            u2 = shift_ref[V7X_SUBLANES - 2:V7X_SUBLANES - 2 + TILE_ROWS, sl]
            last = u[TILE_ROWS - V7X_SUBLANES:, :]
            shift_ref[:V7X_SUBLANES, sl] = last
            tail_ref[0, :, sl] = last
        else:
            t = lax.broadcasted_iota(jnp.int32, (TILE_ROWS, CONV_CHUNK), 0) % DEC_SEQ
            p2 = pre_ref[:, sl]
            p1 = pltpu.roll(p2, TILE_ROWS - 1, axis=0)
            u1 = jnp.where(t == 0, p1, pltpu.roll(u, 1, axis=0))
            u2 = jnp.where(t < 2, p2, pltpu.roll(u, 2, axis=0))
            us_ref[:, sl] = u
            tail_ref[0, :, sl] = jnp.zeros((V7X_SUBLANES, CONV_CHUNK), F32)
        w = cw_ref[:, sl]
        conv = w[0:1] * u2 + w[1:2] * u1 + w[2:3] * u
        b_gate = _dot(hm, win_ref[:, sl])
        ytok_ref[:, sl] = (b_gate * conv).astype(BF16)

    @pl.when(i < PROMPT_TILES)
    def _():
        @pl.when(i % TILES_PER_SEQ == 0)
        def _():
            shift_ref[:V7X_SUBLANES, :] = jnp.zeros((V7X_SUBLANES, CONV_DIM), F32)
        for cc in range(CONV_DIM // CONV_CHUNK):
            chunk(cc, True)

    @pl.when(i >= PROMPT_TILES)
    def _():
        for cc in range(CONV_DIM // CONV_CHUNK):
            chunk(cc, False)


def _inproj_conv(x1, gm, win, conv_w, prefix_rows, layer):
    sample_idx = lambda i: (jnp.maximum(i - PROMPT_TILES, 0), 0)
    win_bytes = D_MODEL * (3 * CONV_DIM + MEM_DIM) * 2
    return pl.pallas_call(
        _inproj_conv_kernel,
        out_shape=(jax.ShapeDtypeStruct((ROWS, CONV_DIM), BF16),
                   jax.ShapeDtypeStruct((ROWS, MEM_DIM), BF16),
                   jax.ShapeDtypeStruct((ROW_TILES, V7X_SUBLANES, CONV_DIM), F32),
                   jax.ShapeDtypeStruct((ROWS_SAMPLE, CONV_DIM), F32)),
        grid=(ROW_TILES,),
        in_specs=[_row_spec(D_MODEL), _resident((1, D_MODEL)),
                  _resident_layer((D_MODEL, 3 * CONV_DIM + MEM_DIM), layer),
                  _resident_layer((CONV_WIDTH, CONV_DIM), layer),
                  pl.BlockSpec((TILE_ROWS, CONV_DIM), sample_idx)],
        out_specs=(_row_spec(CONV_DIM), _row_spec(MEM_DIM),
                   pl.BlockSpec((1, V7X_SUBLANES, CONV_DIM), lambda i: (i, 0, 0)),
                   pl.BlockSpec((TILE_ROWS, CONV_DIM), sample_idx)),
        scratch_shapes=[pltpu.VMEM((TILE_ROWS + V7X_SUBLANES, CONV_DIM), F32)],
        compiler_params=_params(_vmem_limit(
            win_bytes, 3 * _ROW_TILE_F32 + TILE_ROWS * (CONV_DIM + MEM_DIM) * 2,
            _ROW_TILE_F32 + V7X_SUBLANES * CONV_DIM * 4, 6 * _ROW_TILE_F32)),
        name="inproj_conv",
    )(x1, gm.reshape(1, D_MODEL), win, conv_w, prefix_rows)


VZ_WIDTH = N_KV_HEADS * 2 * 2 * V7X_LANES


def _emit_shared_kv(x, g_ref, wk_ref, wkt_ref, wv_ref, wv2_ref, k_ref, v_ref, ktz_ref, vz_ref):
    hk = _rms(x, g_ref[...]).astype(BF16)
    k_ref[...] = _dot(hk, wk_ref[...])
    v_ref[...] = _dot(hk, wv_ref[...])
    kt = _dot_nt(wkt_ref[...], hk)
    v2 = _dot(hk, wv2_ref[...])
    lo = lax.broadcasted_iota(jnp.int32, (TILE_ROWS, V7X_LANES), 1) < HALF_LANES
    ones = (jnp.where(lo, 1.0, 0.0).astype(BF16), jnp.where(lo, 0.0, 1.0).astype(BF16))
    zero_k = jnp.zeros((HEAD_DIM, TILE_ROWS), BF16)
    for h in range(N_KV_HEADS):
        kth = kt[h * HEAD_DIM:(h + 1) * HEAD_DIM, :].astype(BF16)
        ktz_ref[h, 0] = jnp.concatenate([kth, zero_k], axis=0)
        ktz_ref[h, 1] = jnp.concatenate([zero_k, kth], axis=0)
        v2h = v2[:, h * V7X_LANES:(h + 1) * V7X_LANES]
        for e in range(2):
            vals = jnp.where(lo if e == 0 else jnp.logical_not(lo), v2h, 0.0).astype(BF16)
            c0 = (h * 2 + e) * 2 * V7X_LANES
            vz_ref[:, c0:c0 + 2 * V7X_LANES] = jnp.concatenate([vals, ones[e]], axis=1)


def _shared_kv_specs():
    in_specs = [_resident((1, D_MODEL)), _resident((D_MODEL, KV_DIM)), _resident((KV_DIM, D_MODEL)),
                _resident((D_MODEL, KV_DIM)), _resident((D_MODEL, 2 * KV_DIM))]
    out_shape = [jax.ShapeDtypeStruct((ROWS, KV_DIM), F32), jax.ShapeDtypeStruct((ROWS, KV_DIM), F32),
                 jax.ShapeDtypeStruct((N_KV_HEADS, 2, V7X_LANES, ROWS), BF16),
                 jax.ShapeDtypeStruct((ROWS, VZ_WIDTH), BF16)]
    out_specs = [_row_spec(KV_DIM), _row_spec(KV_DIM),
                 pl.BlockSpec((N_KV_HEADS, 2, V7X_LANES, TILE_ROWS), lambda i: (0, 0, 0, i)),
                 _row_spec(VZ_WIDTH)]
    resident = 5 * D_MODEL * KV_DIM * 2
    streamed = TILE_ROWS * (2 * KV_DIM * 4 + 2 * KV_DIM * 2 + VZ_WIDTH * 2)
    return in_specs, out_shape, out_specs, resident, streamed


def _attn_prompt_kernel(*refs, swa):
    if swa:
        (qm_ref, mk_ref, mve_ref, qa_ref, ktp_ref, ktc_ref, vzp_ref, vzc_ref, bias_ref,
         ymem_ref, ytok_ref) = refs
    else:
        qm_ref, mk_ref, mve_ref, ymem_ref = refs

    for h in range(MEM_HEADS):
        sl = slice(h * MEM_HEAD_DIM, (h + 1) * MEM_HEAD_DIM)
        s = _dot_nt(qm_ref[:, sl], mk_ref[:, sl])
        p = jnp.exp2(s - jnp.max(s, axis=-1, keepdims=True)).astype(BF16)
        oe = _dot(p, mve_ref[h])
        ymem_ref[:, sl] = (oe[:, :MEM_HEAD_DIM] / oe[:, MEM_HEAD_DIM:]).astype(BF16)
    if not swa:
        return

    no_prev = ((pl.program_id(0) % TILES_PER_SEQ) == 0).astype(jnp.int32)
    key0_col = lax.broadcasted_iota(jnp.int32, (V7X_LANES, WINDOW), 1) == 0
    key0_row = lax.broadcasted_iota(jnp.int32, (WINDOW, V7X_LANES), 0) == 0
    zero_kt = jnp.zeros((V7X_LANES, WINDOW), BF16)
    zero_v = jnp.zeros((WINDOW, V7X_LANES), BF16)

    for n in range(BLOCKS_PER_TILE):
        rows = slice(n * WINDOW, (n + 1) * WINDOW)
        for h in range(N_KV_HEADS):
            k_sel, v_sel = [], []
            for e in range(2):
                k_prev = ktp_ref[h, e] if n == 0 else ktc_ref[h, e, :, (n - 1) * WINDOW:n * WINDOW]
                k_prev = jnp.where(key0_col, zero_kt, k_prev)
                k_sel.append(jnp.concatenate([k_prev, ktc_ref[h, e, :, rows]], axis=1))
                c0 = (h * 2 + e) * 2 * V7X_LANES
                cols = slice(c0, c0 + 2 * V7X_LANES)
                v_prev = vzp_ref[:, cols] if n == 0 else vzc_ref[(n - 1) * WINDOW:n * WINDOW, cols]
                v_prev = jnp.concatenate(
                    [jnp.where(key0_row, zero_v, v_prev[:, :V7X_LANES]), v_prev[:, V7X_LANES:]], axis=1)
                v_sel.append(jnp.concatenate([v_prev, vzc_ref[rows, cols]], axis=0))
            for pr in range(GROUP // 2):
                c0 = (h * (GROUP // 2) + pr) * V7X_LANES
                qp = qa_ref[rows, c0:c0 + V7X_LANES]
                acc = None
                for e in range(2):
                    j = h * GROUP + pr * 2 + e
                    bias = bias_ref[no_prev, j] if n == 0 else bias_ref[0, j]
                    s = _dot(qp, k_sel[e]) + bias
                    p = jnp.exp2(s - jnp.max(s, axis=-1, keepdims=True)).astype(BF16)
                    part = _dot(p, v_sel[e])
                    acc = part if acc is None else acc + part
                ytok_ref[rows, c0:c0 + V7X_LANES] = (acc[:, :V7X_LANES] / acc[:, V7X_LANES:]).astype(BF16)


def _attn_prompt(qm, mk_bf, mv_ext, layer, swa_args=None):
    swa = swa_args is not None
    batch_of = lambda i: i // TILES_PER_SEQ
    in_specs = [_row_spec(MEM_DIM),
                pl.BlockSpec((None, None, MEM_TOKENS, MEM_DIM), lambda i: (layer, batch_of(i), 0, 0)),
                pl.BlockSpec((None, None, MEM_HEADS, MEM_TOKENS, 2 * MEM_HEAD_DIM),
                             lambda i: (layer, batch_of(i), 0, 0, 0))]
    args = [qm, mk_bf, mv_ext]
    out_shape = [jax.ShapeDtypeStruct((ROWS_PROMPT, MEM_DIM), BF16)]
    out_specs = [_row_spec(MEM_DIM)]
    streamed = TILE_ROWS * MEM_DIM * 4 + MEM_TOKENS * MEM_DIM * 2 * 3
    resident = 0
    if swa:
        qa, ktz, vz, bias_p, swa_layer = swa_args
        prev_blk = lambda i: jnp.maximum(i * BLOCKS_PER_TILE - 1, 0)
        in_specs += [_row_spec(ATTN_DIM),
                     pl.BlockSpec((N_KV_HEADS, 2, V7X_LANES, WINDOW), lambda i: (0, 0, 0, prev_blk(i))),
                     pl.BlockSpec((N_KV_HEADS, 2, V7X_LANES, TILE_ROWS), lambda i: (0, 0, 0, i)),
                     pl.BlockSpec((WINDOW, VZ_WIDTH), lambda i: (prev_blk(i), 0)),
                     pl.BlockSpec((TILE_ROWS, VZ_WIDTH), lambda i: (i, 0)),
                     _resident_layer((2, N_HEADS, WINDOW, 2 * WINDOW), swa_layer)]
        args += [qa, ktz, ktz, vz, vz, bias_p]
        out_shape += [jax.ShapeDtypeStruct((ROWS_PROMPT, ATTN_DIM), BF16)]
        out_specs += [_row_spec(ATTN_DIM)]
        resident = 2 * N_HEADS * WINDOW * 2 * WINDOW * 4
        streamed += 2 * TILE_ROWS * ATTN_DIM * 2 + (TILE_ROWS + WINDOW) * (2 * KV_DIM + VZ_WIDTH) * 2
    return pl.pallas_call(
        functools.partial(_attn_prompt_kernel, swa=swa),
        out_shape=tuple(out_shape),
        grid=(PROMPT_TILES,),
        in_specs=in_specs,
        out_specs=tuple(out_specs),
        compiler_params=_params(_vmem_limit(resident, streamed, 0, 8 * _ROW_TILE_F32)),
        name="attn_prompt_swa" if swa else "attn_prompt_mem",
    )(*args)


def _attn_sample_kernel(*refs, swa):
    if swa:
        (qm_ref, mk_ref, mv_ref, qa_ref, ck_ref, cv_ref, kn_ref, vn_ref, bias_ref,
         ymem_ref, ytok_ref) = refs
    else:
        qm_ref, mk_ref, mv_ref, ymem_ref = refs

    qm_all = qm_ref[...].astype(F32)
    mem_rows = MEM_HEADS * DEC_SEQ
    own_head = (lax.broadcasted_iota(jnp.int32, (mem_rows, MEM_TOKENS * MEM_HEADS), 1) % MEM_HEADS
                == lax.broadcasted_iota(jnp.int32, (mem_rows, MEM_TOKENS * MEM_HEADS), 0) // DEC_SEQ)
    if swa:
        qa_all = qa_ref[...].astype(F32)
        kn_all = kn_ref[...]
        vn_all = vn_ref[...]
        bias = bias_ref[...]
        key0 = lax.broadcasted_iota(jnp.int32, (KV_DIM, WINDOW), 1) == 0
        pad = jnp.zeros((WINDOW - DEC_SEQ, KV_DIM), F32)
        lo = lax.broadcasted_iota(jnp.int32, (DEC_SEQ, V7X_LANES), 1) < HALF_LANES
        hi = jnp.logical_not(lo)
        zero_slab = jnp.zeros((DEC_SEQ, V7X_LANES), F32)

    ymem_rows, ytok_rows = [], []
    for b in range(SAMPLE_BATCH_BLOCK):
        rows = slice(b * DEC_SEQ, (b + 1) * DEC_SEQ)
        qm = qm_all[rows]
        q_heads = jnp.concatenate(
            [qm[:, h * MEM_HEAD_DIM:(h + 1) * MEM_HEAD_DIM] for h in range(MEM_HEADS)], axis=0)
        s = jnp.where(own_head, _dot_nt(q_heads, mk_ref[b]), NEG_INF)
        p = jnp.exp2(s - jnp.max(s, axis=-1, keepdims=True))
        o = _dot(p, mv_ref[b]) / jnp.sum(p, axis=-1, keepdims=True)
        ymem_rows.append(jnp.concatenate(
            [o[h * DEC_SEQ:(h + 1) * DEC_SEQ] for h in range(MEM_HEADS)], axis=1))
        if not swa:
            continue
        qb = qa_all[rows]
        groups = []
        for j in range(N_HEADS):
            kvh = j // GROUP
            slab = qb[:, (j // 2) * V7X_LANES:(j // 2 + 1) * V7X_LANES]
            if j % 2 != kvh % 2:
                slab = pltpu.roll(slab, HALF_LANES, axis=1)
            slab = jnp.where(lo if kvh % 2 == 0 else hi, slab, 0.0)
            groups.append(jnp.concatenate([slab, zero_slab] if kvh // 2 == 0 else [zero_slab, slab], axis=1))
        q_bd = jnp.concatenate(groups, axis=0)
        kt_cache = jnp.where(key0, 0.0, ck_ref[b])
        vt_cache = jnp.where(key0, 0.0, cv_ref[b])
        k_new = jnp.concatenate([kn_all[rows], pad], axis=0)
        v_new = jnp.concatenate([vn_all[rows], pad], axis=0)
        s = jnp.concatenate([_dot(q_bd, kt_cache), _dot_nt(q_bd, k_new)], axis=1) + bias
        p = jnp.exp2(s - jnp.max(s, axis=-1, keepdims=True))
        o_full = ((_dot_nt(p[:, :WINDOW], vt_cache) + _dot(p[:, WINDOW:], v_new))
                  / jnp.sum(p, axis=-1, keepdims=True))
        pairs = []
        for pair in range(N_HEADS // 2):
            acc = None
            for e in range(2):
                j = pair * 2 + e
                kvh = j // GROUP
                slab = o_full[j * DEC_SEQ:(j + 1) * DEC_SEQ, (kvh // 2) * V7X_LANES:(kvh // 2 + 1) * V7X_LANES]
                if e != kvh % 2:
                    slab = pltpu.roll(slab, HALF_LANES, axis=1)
                slab = jnp.where(lo if e == 0 else hi, slab, 0.0)
                acc = slab if acc is None else acc + slab
            pairs.append(acc)
        ytok_rows.append(jnp.concatenate(pairs, axis=1))

    ymem_ref[...] = jnp.concatenate(ymem_rows, axis=0).astype(BF16)
    if swa:
        ytok_ref[...] = jnp.concatenate(ytok_rows, axis=0).astype(BF16)


def _attn_sample(qm, cache_k, cache_v, layer, swa_args=None):
    swa = swa_args is not None
    row0 = ROWS_PROMPT // SAMPLE_BLOCK_ROWS
    blk_rows = lambda width: pl.BlockSpec((SAMPLE_BLOCK_ROWS, width), lambda i: (row0 + i, 0))
    out_rows = lambda width: pl.BlockSpec((SAMPLE_BLOCK_ROWS, width), lambda i: (i, 0))
    cache_spec = pl.BlockSpec((None, SAMPLE_BATCH_BLOCK, MEM_TOKENS * MEM_HEADS, MEM_HEAD_DIM),
                              lambda i: (layer, i, 0, 0))
    in_specs = [blk_rows(MEM_DIM), cache_spec, cache_spec]
    args = [qm, cache_k, cache_v]
    out_shape = [jax.ShapeDtypeStruct((ROWS_SAMPLE, MEM_DIM), BF16)]
    out_specs = [out_rows(MEM_DIM)]
    streamed = 2 * SAMPLE_BATCH_BLOCK * MEM_TOKENS * MEM_DIM * 4 + SAMPLE_BLOCK_ROWS * MEM_DIM * 4
    if swa:
        qa, swa_k, swa_v, k_new, v_new, bias_s, swa_layer = swa_args
        swa_spec = pl.BlockSpec((SAMPLE_BATCH_BLOCK, KV_DIM, WINDOW), lambda i: (i, 0, 0))
        in_specs += [blk_rows(ATTN_DIM), swa_spec, swa_spec, out_rows(KV_DIM), out_rows(KV_DIM),
                     _resident_layer((N_HEADS * DEC_SEQ, SAMPLE_KEYS_PADDED), swa_layer)]
        args += [qa, swa_k, swa_v, k_new, v_new, bias_s]
        out_shape += [jax.ShapeDtypeStruct((ROWS_SAMPLE, ATTN_DIM), BF16)]
        out_specs += [out_rows(ATTN_DIM)]
        streamed += 2 * SAMPLE_BATCH_BLOCK * WINDOW * KV_DIM * 4 + SAMPLE_BLOCK_ROWS * (ATTN_DIM + KV_DIM) * 4
    return pl.pallas_call(
        functools.partial(_attn_sample_kernel, swa=swa),
        out_shape=tuple(out_shape),
        grid=(DEC_BATCH // SAMPLE_BATCH_BLOCK,),
        in_specs=in_specs,
        out_specs=tuple(out_specs),
        compiler_params=_params(_vmem_limit(1 << 20, streamed, 0, 6 * _ROW_TILE_F32)),
        name="attn_sample_swa" if swa else "attn_sample_mem",
    )(*args)


def _outffn_kernel(*refs, split_tok, final, layer):
    refs = list(refs)
    x1_ref = refs.pop(0)
    tok_refs = [refs.pop(0) for _ in range(2 if split_tok else 1)]
    ymp_ref, yms_ref, wo_ref, g_ref = [refs.pop(0) for _ in range(4)]
    w_hbm = [refs.pop(0) for _ in range(3)]
    gf_ref = refs.pop(0) if final else None
    out_refs = [refs.pop(0) for _ in range(2 if final else 1)]
    act_ref = refs.pop(0)
    w = _FfnWeights(layer, w_hbm, refs)
    tok_dim = wo_ref.shape[0] - MEM_DIM

    def body(load_weights):
        y_tok = _pick_group(*tok_refs) if split_tok else tok_refs[0][...]
        y_mem = _pick_group(ymp_ref, yms_ref)
        x2 = x1_ref[...] + _dot(y_tok, wo_ref[:tok_dim, :]) + _dot(y_mem, wo_ref[tok_dim:, :])
        x3 = _ffn_half_step(x2, g_ref, w, act_ref, load_weights)
        if not final:
            out_refs[0][...] = x3
            return
        y = _rms(x3, gf_ref[...])

        @pl.when(pl.program_id(0) < PROMPT_TILES)
        def _():
            out_refs[0][...] = y

        @pl.when(pl.program_id(0) >= PROMPT_TILES)
        def _():
            out_refs[1][...] = y

    _first_step_loads(w, body)


def _outffn(x1, y_tok, ymem_p, ymem_s, wo, wo_layer, g, ffn_w, layer, final_gain=None):
    split_tok = isinstance(y_tok, tuple)
    final = final_gain is not None
    tok_dim = wo.shape[1] - MEM_DIM
    in_specs = [_row_spec(D_MODEL)]
    args = [x1]
    if split_tok:
        in_specs += _group_specs(tok_dim)
        args += list(y_tok)
    else:
        in_specs += [_row_spec(tok_dim)]
        args += [y_tok]
    in_specs += (_group_specs(MEM_DIM) + [_resident_layer((tok_dim + MEM_DIM, D_MODEL), wo_layer)]
                 + [_resident((1, D_MODEL))] + _FfnWeights.in_specs())
    args += [ymem_p, ymem_s, wo, g.reshape(1, D_MODEL)] + list(ffn_w)
    if final:
        in_specs += [_resident((1, D_MODEL))]
        args += [final_gain.reshape(1, D_MODEL)]
    resident = (tok_dim + MEM_DIM) * D_MODEL * 2
    streamed = 2 * _ROW_TILE_F32 + 2 * TILE_ROWS * (tok_dim + MEM_DIM) * 2
    if final:
        out_shape = [jax.ShapeDtypeStruct((ROWS_PROMPT, D_MODEL), F32),
                     jax.ShapeDtypeStruct((ROWS_SAMPLE, D_MODEL), F32)]
        out_specs = _group_specs(D_MODEL)
        streamed += _ROW_TILE_F32
    else:
        out_shape = [jax.ShapeDtypeStruct((ROWS, D_MODEL), F32)]
        out_specs = [_row_spec(D_MODEL)]
    return pl.pallas_call(
        functools.partial(_outffn_kernel, split_tok=split_tok, final=final, layer=layer),
        out_shape=tuple(out_shape),
        grid=(ROW_TILES,),
        in_specs=in_specs,
        out_specs=tuple(out_specs),
        scratch_shapes=[pltpu.VMEM((TILE_ROWS, FFN_DIM), BF16)] + _FfnWeights.scratch_shapes(),
        compiler_params=_params(_vmem_limit(
            resident, streamed, _ACT_BYTES + _FfnWeights.SCRATCH_BYTES, 3 * _ROW_TILE_F32)),
        name="outffn_final" if final else "outffn",
    )(*args)


def kernel(x_prompt, x_sample, state_conv, cache_swa_k, cache_swa_v, cache_mem_k, cache_mem_v, mem_prompt, ffn1_norm, ffn1_wg, ffn1_wu, ffn1_wd, mix_norm, w_in_a, conv_w, w_out_a, kv_norm, w_kv, w_in_b, attn_sinks, rel_bias, w_out_b, mem_norm, w_mem_kv, ffn2_norm, ffn2_wg, ffn2_wu, ffn2_wd, final_norm):
    bf = lambda w: w.astype(BF16)
    ffn1 = (ffn1_wg, ffn1_wu, ffn1_wd)
    ffn2 = (ffn2_wg, ffn2_wu, ffn2_wd)
    w_in_a_bf, w_out_a_bf, w_in_b_bf, w_out_b_bf = bf(w_in_a), bf(w_out_a), bf(w_in_b), bf(w_out_b)
    wk = w_kv[:, :KV_DIM]
    wv = w_kv[:, KV_DIM:]
    wv2 = jnp.broadcast_to(wv.reshape(D_MODEL, N_KV_HEADS, 1, HEAD_DIM),
                           (D_MODEL, N_KV_HEADS, 2, HEAD_DIM)).reshape(D_MODEL, 2 * KV_DIM)

    mem_k, mem_v, mem_k_bf, mem_v_ext = _memkv(mem_prompt, mem_norm, bf(w_mem_kv))
    bias_p, bias_s = _bias_tables(rel_bias, attn_sinks)
    swa_k_cache = cache_swa_k.transpose(0, 2, 3, 1).reshape(DEC_BATCH, KV_DIM, WINDOW)
    swa_v_cache = cache_swa_v.transpose(0, 2, 3, 1).reshape(DEC_BATCH, KV_DIM, WINDOW)
    mem_rows_shape = (DEPTH, DEC_BATCH, MEM_TOKENS * MEM_HEADS, MEM_HEAD_DIM)
    cache_k = cache_mem_k.reshape(mem_rows_shape)
    cache_v = cache_mem_v.reshape(mem_rows_shape)

    x = (x_prompt.reshape(ROWS_PROMPT, D_MODEL), x_sample.reshape(ROWS_SAMPLE, D_MODEL))
    tails, sample_us = [], []
    k_rows = v_rows = ktz = vz = k_new = v_new = None
    for l in range(DEPTH):
        last = l == DEPTH - 1
        if l < N_A_LAYERS:
            (x1,) = _ffn(x, ffn1_norm[l], ffn1, l)
            prefix_rows = jnp.pad(state_conv[l], ((0, 0), (0, DEC_SEQ - (CONV_WIDTH - 1)), (0, 0)))
            y_tok, qm, tail, us = _inproj_conv(x1, mix_norm[l], w_in_a_bf, conv_w,
                                               prefix_rows.reshape(ROWS_SAMPLE, CONV_DIM), l)
            tails.append(tail)
            sample_us.append(us)
            (ymem_p,) = _attn_prompt(qm, mem_k_bf, mem_v_ext, l)
            (ymem_s,) = _attn_sample(qm, cache_k, cache_v, l)
            wo, wo_layer = w_out_a_bf, l
        else:
            j = l - N_A_LAYERS
            if j == 0:
                x1, qa, qm, k_rows, v_rows, ktz, vz = _ffn(
                    x, ffn1_norm[l], ffn1, l, mix_norm[l], w_in_b_bf, j,
                    shared_kv_w=(kv_norm, bf(wk), bf(wk.T), bf(wv), bf(wv2)))
                k_new = k_rows[ROWS_PROMPT:]
                v_new = v_rows[ROWS_PROMPT:]
            else:
                x1, qa, qm = _ffn(x, ffn1_norm[l], ffn1, l, mix_norm[l], w_in_b_bf, j)
            ymem_p, ytok_p = _attn_prompt(qm, mem_k_bf, mem_v_ext, l, (qa, ktz, vz, bias_p, j))
            ymem_s, ytok_s = _attn_sample(qm, cache_k, cache_v, l,
                                          (qa, swa_k_cache, swa_v_cache, k_new, v_new, bias_s, j))
            y_tok = (ytok_p, ytok_s)
            wo, wo_layer = w_out_b_bf, j
        if not last:
            (x,) = _outffn(x1, y_tok, ymem_p, ymem_s, wo, wo_layer, ffn2_norm[l], ffn2, l)
        else:
            y_prompt, y_sample = _outffn(x1, y_tok, ymem_p, ymem_s, wo, wo_layer, ffn2_norm[l], ffn2, l,
                                         final_gain=final_norm)

    keep = CONV_WIDTH - 1
    last_tiles = np.arange(BATCH) * TILES_PER_SEQ + TILES_PER_SEQ - 1
    conv_state_prompt = jnp.stack([t[last_tiles, V7X_SUBLANES - keep:, :] for t in tails])
    conv_state_sample = jnp.stack([u.reshape(DEC_BATCH, DEC_SEQ, CONV_DIM)[:, DEC_SEQ - keep:, :] for u in sample_us])
    k_tail = jnp.stack([k_rows[(b + 1) * SEQ - WINDOW:(b + 1) * SEQ] for b in range(BATCH)])
    v_tail = jnp.stack([v_rows[(b + 1) * SEQ - WINDOW:(b + 1) * SEQ] for b in range(BATCH)])
    swa_k_prompt = k_tail.reshape(BATCH, WINDOW, N_KV_HEADS, HEAD_DIM)
    swa_v_prompt = v_tail.reshape(BATCH, WINDOW, N_KV_HEADS, HEAD_DIM)
    swa_k_sample = jnp.concatenate(
        [cache_swa_k[:, DEC_SEQ:], k_new.reshape(DEC_BATCH, DEC_SEQ, N_KV_HEADS, HEAD_DIM)], axis=1)
    swa_v_sample = jnp.concatenate(
        [cache_swa_v[:, DEC_SEQ:], v_new.reshape(DEC_BATCH, DEC_SEQ, N_KV_HEADS, HEAD_DIM)], axis=1)
    mem_shape = (DEPTH, BATCH, MEM_TOKENS, MEM_HEADS, MEM_HEAD_DIM)
    return (y_prompt.reshape(BATCH, SEQ, D_MODEL), y_sample.reshape(DEC_BATCH, DEC_SEQ, D_MODEL),
            conv_state_prompt, conv_state_sample,
            swa_k_prompt, swa_v_prompt, swa_k_sample, swa_v_sample,
            mem_k.reshape(mem_shape), mem_v.reshape(mem_shape))
```

```python
import functools
import math

import numpy as np
import jax
import jax.numpy as jnp
from jax import lax
from jax.experimental import pallas as pl
from jax.experimental.pallas import tpu as pltpu

D_MODEL = 1024
BATCH = 2
SEQ = 8192
DEPTH = 4
DEC_BATCH = 128
DEC_SEQ = 8
N_A_LAYERS = DEPTH // 2
FFN_DIM = 2816
CONV_DIM = D_MODEL
CONV_WIDTH = 3
N_HEADS = 16
N_KV_HEADS = 4
HEAD_DIM = 64
GROUP = N_HEADS // N_KV_HEADS
ATTN_DIM = N_HEADS * HEAD_DIM
KV_DIM = N_KV_HEADS * HEAD_DIM
WINDOW = 128
REL_BUCKETS = 32
REL_MAX_DIST = 128
MEM_TOKENS = 256
MEM_HEADS = 4
MEM_HEAD_DIM = 128
MEM_DIM = MEM_HEADS * MEM_HEAD_DIM
RMS_EPS = 1e-5

F32 = jnp.float32
BF16 = jnp.bfloat16
NEG_INF = float("-inf")

V7X_LANES = 128
V7X_SUBLANES = 8
V7X_MXU_DIM = 256
V7X_VMEM_BYTES = 64 * 1024 * 1024

ROWS_PROMPT = BATCH * SEQ
ROWS_SAMPLE = DEC_BATCH * DEC_SEQ
ROWS = ROWS_PROMPT + ROWS_SAMPLE
TILE_ROWS = 512
PROMPT_TILES = ROWS_PROMPT // TILE_ROWS
SAMPLE_TILES = ROWS_SAMPLE // TILE_ROWS
ROW_TILES = PROMPT_TILES + SAMPLE_TILES
TILES_PER_SEQ = SEQ // TILE_ROWS
FFN_CHUNK = V7X_MXU_DIM
CONV_CHUNK = V7X_MXU_DIM
BLOCKS_PER_TILE = TILE_ROWS // WINDOW
SAMPLE_BATCH_BLOCK = 8
SAMPLE_BLOCK_ROWS = SAMPLE_BATCH_BLOCK * DEC_SEQ
SAMPLE_KEYS = WINDOW + DEC_SEQ
SAMPLE_KEYS_PADDED = 2 * WINDOW
HALF_LANES = V7X_LANES // 2
LOG2E = math.log2(math.e)
MEM_Q_SCALE = MEM_HEAD_DIM ** -0.5 * LOG2E
ATTN_Q_SCALE = HEAD_DIM ** -0.5 * LOG2E

assert HEAD_DIM == HALF_LANES and MEM_HEAD_DIM == V7X_LANES
assert ROWS_PROMPT % TILE_ROWS == 0 and ROWS_SAMPLE % TILE_ROWS == 0 and SEQ % TILE_ROWS == 0
assert FFN_DIM % FFN_CHUNK == 0 and TILE_ROWS % WINDOW == 0


def _vmem_limit(resident_bytes, streamed_bytes, scratch_bytes, temp_bytes):
    need = resident_bytes + 2 * streamed_bytes + scratch_bytes + temp_bytes
    assert need < V7X_VMEM_BYTES, need
    return int(need)


def _params(vmem_bytes, n_axes=1):
    return pltpu.CompilerParams(
        dimension_semantics=("arbitrary",) * n_axes, vmem_limit_bytes=vmem_bytes)


def _resident(shape):
    zeros = (0,) * len(shape)
    return pl.BlockSpec(shape, lambda *_: zeros, pipeline_mode=pl.Buffered(1))


def _resident_layer(shape, layer):
    idx = (layer,) + (0,) * len(shape)
    return pl.BlockSpec((None,) + tuple(shape), lambda *_: idx, pipeline_mode=pl.Buffered(1))


def _row_spec(width):
    return pl.BlockSpec((TILE_ROWS, width), lambda i: (i, 0))


def _group_specs(width):
    return [pl.BlockSpec((TILE_ROWS, width), lambda i: (jnp.minimum(i, PROMPT_TILES - 1), 0)),
            pl.BlockSpec((TILE_ROWS, width), lambda i: (jnp.maximum(i - PROMPT_TILES, 0), 0))]


def _pick_group(prompt_ref, sample_ref):
    return jnp.where(pl.program_id(0) < PROMPT_TILES, prompt_ref[...], sample_ref[...])


def _dot(a, b):
    return jnp.dot(a, b, preferred_element_type=F32)


def _dot_nt(a, b):
    return lax.dot_general(a, b, (((1,), (1,)), ((), ())), preferred_element_type=F32)


def _rms(x, g):
    return x * lax.rsqrt(jnp.mean(x * x, axis=-1, keepdims=True) + RMS_EPS) * g


FFN_CHUNKS = FFN_DIM // FFN_CHUNK
FFN_STAGE_SLOTS = 2


class _FfnWeights:
    def __init__(self, layer, hbm_refs, scratch_refs):
        self.layer = layer
        self.wg_hbm, self.wu_hbm, self.wd_hbm = hbm_refs
        self.wg, self.wu, self.wd, self.stage_in, self.stage_out, self.sems = scratch_refs

    @staticmethod
    def in_specs():
        return [pl.BlockSpec(memory_space=pl.ANY)] * 3

    @staticmethod
    def scratch_shapes():
        return [pltpu.VMEM((D_MODEL, FFN_DIM), BF16), pltpu.VMEM((D_MODEL, FFN_DIM), BF16),
                pltpu.VMEM((FFN_DIM, D_MODEL), BF16),
                pltpu.VMEM((2, FFN_STAGE_SLOTS, D_MODEL, FFN_CHUNK), F32),
                pltpu.VMEM((FFN_STAGE_SLOTS, FFN_CHUNK, D_MODEL), F32),
                pltpu.SemaphoreType.DMA((3, FFN_STAGE_SLOTS))]

    SCRATCH_BYTES = 3 * D_MODEL * FFN_DIM * 2 + 3 * FFN_STAGE_SLOTS * D_MODEL * FFN_CHUNK * 4

    def _copy(self, stream, c):
        slot = c % FFN_STAGE_SLOTS
        cols = pl.ds(c * FFN_CHUNK, FFN_CHUNK)
        if stream == 0:
            src, dst = self.wg_hbm.at[self.layer, :, cols], self.stage_in.at[0, slot]
        elif stream == 1:
            src, dst = self.wu_hbm.at[self.layer, :, cols], self.stage_in.at[1, slot]
        else:
            src, dst = self.wd_hbm.at[self.layer, cols, :], self.stage_out.at[slot]
        return pltpu.make_async_copy(src, dst, self.sems.at[stream, slot])

    def prime(self):
        for stream in range(3):
            for c in range(FFN_STAGE_SLOTS):
                self._copy(stream, c).start()

    def fetch(self, c):
        slot = c % FFN_STAGE_SLOTS
        sl = slice(c * FFN_CHUNK, (c + 1) * FFN_CHUNK)
        for stream in range(3):
            self._copy(stream, c).wait()
            if stream == 0:
                self.wg[:, sl] = self.stage_in[0, slot].astype(BF16)
            elif stream == 1:
                self.wu[:, sl] = self.stage_in[1, slot].astype(BF16)
            else:
                self.wd[sl, :] = self.stage_out[slot].astype(BF16)
            if c + FFN_STAGE_SLOTS < FFN_CHUNKS:
                self._copy(stream, c + FFN_STAGE_SLOTS).start()


def _ffn_half_step(x, g_ref, w, act_ref, load_weights):
    inv_rms = lax.rsqrt(jnp.mean(x * x, axis=-1, keepdims=True) + RMS_EPS)
    inv_rms_chunk = jnp.broadcast_to(inv_rms, (x.shape[0], FFN_CHUNK))
    h = (x * g_ref[...]).astype(BF16)
    for c in range(FFN_CHUNKS):
        if load_weights:
            w.fetch(c)
        sl = slice(c * FFN_CHUNK, (c + 1) * FFN_CHUNK)
        gate = _dot(h, w.wg[:, sl]) * inv_rms_chunk
        up = _dot(h, w.wu[:, sl])
        act_ref[:, sl] = (gate / (1.0 + jnp.exp(-gate)) * up).astype(BF16)
    return x + (0.5 * inv_rms) * _dot(act_ref[...], w.wd[...])


def _first_step_loads(w, body):
    @pl.when(pl.program_id(0) == 0)
    def _():
        w.prime()
        body(True)

    @pl.when(pl.program_id(0) > 0)
    def _():
        body(False)


_ROW_TILE_F32 = TILE_ROWS * D_MODEL * 4
_ACT_BYTES = TILE_ROWS * FFN_DIM * 2


def _t5_bucket_np(dist):
    n = np.maximum(dist, 0)
    exact = REL_BUCKETS // 2
    nf = np.maximum(n, 1).astype(np.float32)
    large = exact + (np.log(nf / np.float32(exact)) / np.float32(math.log(REL_MAX_DIST / exact))
                     * np.float32(REL_BUCKETS - exact)).astype(np.int32)
    large = np.minimum(large, REL_BUCKETS - 1)
    return np.where(n < exact, n, large).astype(np.int32)


def _bucket_tables():
    q = np.arange(WINDOW)[:, None]
    k = np.arange(2 * WINDOW)[None, :]
    dist = WINDOW + q - k
    prompt = np.where((dist >= 0) & (dist < WINDOW), _t5_bucket_np(dist), -1)
    t = (np.arange(N_HEADS * DEC_SEQ) % DEC_SEQ)[:, None]
    k = np.arange(SAMPLE_KEYS_PADDED)[None, :]
    dist = WINDOW + t - k
    ok = (dist >= 0) & (dist < WINDOW) & (k < SAMPLE_KEYS)
    sample = np.where(ok, _t5_bucket_np(dist), -1)
    return prompt.astype(np.int32), sample.astype(np.int32)


def _bias_kernel(rel_ref, sink_ref, bp_ref, bs_ref, op_ref, os_ref):
    layer = pl.program_id(0)
    j = pl.program_id(1)
    sink = sink_ref[layer, j] * LOG2E

    def build(bucket):
        acc = jnp.zeros(bucket.shape, F32)
        for b in range(REL_BUCKETS):
            acc = jnp.where(bucket == b, rel_ref[b, j], acc)
        return jnp.where(bucket < 0, NEG_INF, acc * LOG2E)

    col_p = lax.broadcasted_iota(jnp.int32, (WINDOW, 2 * WINDOW), 1)
    table = build(bp_ref[...])
    op_ref[0, 0, 0] = jnp.where(col_p == 0, sink, table)
    op_ref[0, 1, 0] = jnp.where(col_p == 0, sink, jnp.where(col_p < WINDOW, NEG_INF, table))
    col_s = lax.broadcasted_iota(jnp.int32, (DEC_SEQ, SAMPLE_KEYS_PADDED), 1)
    os_ref[0] = jnp.where(col_s == 0, sink, build(bs_ref[...]))


def _bias_tables(rel_bias, attn_sinks):
    bp, bs = _bucket_tables()
    n_layers = attn_sinks.shape[0]
    return pl.pallas_call(
        _bias_kernel,
        out_shape=(jax.ShapeDtypeStruct((n_layers, 2, N_HEADS, WINDOW, 2 * WINDOW), F32),
                   jax.ShapeDtypeStruct((n_layers, N_HEADS * DEC_SEQ, SAMPLE_KEYS_PADDED), F32)),
        grid=(n_layers, N_HEADS),
        in_specs=[pl.BlockSpec(memory_space=pltpu.SMEM), pl.BlockSpec(memory_space=pltpu.SMEM),
                  pl.BlockSpec((WINDOW, 2 * WINDOW), lambda l, j: (0, 0)),
                  pl.BlockSpec((DEC_SEQ, SAMPLE_KEYS_PADDED), lambda l, j: (j, 0))],
        out_specs=(pl.BlockSpec((1, 2, 1, WINDOW, 2 * WINDOW), lambda l, j: (l, 0, j, 0, 0)),
                   pl.BlockSpec((1, DEC_SEQ, SAMPLE_KEYS_PADDED), lambda l, j: (l, j, 0))),
        compiler_params=_params(16 * 1024 * 1024, 2),
        name="bias_tables",
    )(rel_bias, attn_sinks, jnp.asarray(bp), jnp.asarray(bs))


def _memkv_kernel(m_ref, g_ref, w_ref, k_ref, v_ref, kb_ref, ve_ref):
    hn = _rms(m_ref[...], g_ref[...]).astype(BF16)
    kv = _dot(hn, w_ref[...].astype(BF16))
    k = kv[:, :MEM_DIM]
    v = kv[:, MEM_DIM:]
    k_ref[...] = k
    v_ref[...] = v
    kb_ref[...] = k.astype(BF16)
    ones = jnp.ones((MEM_TOKENS, MEM_HEAD_DIM), BF16)
    for h in range(MEM_HEADS):
        vh = v[:, h * MEM_HEAD_DIM:(h + 1) * MEM_HEAD_DIM].astype(BF16)
        ve_ref[h] = jnp.concatenate([vh, ones], axis=1)


def _memkv(mem_prompt, mem_norm, w_mem_kv):
    shp = (DEPTH, BATCH, MEM_TOKENS, MEM_DIM)
    blk = pl.BlockSpec((None, None, MEM_TOKENS, MEM_DIM), lambda l, b: (l, b, 0, 0))
    return pl.pallas_call(
        _memkv_kernel,
        out_shape=(jax.ShapeDtypeStruct(shp, F32), jax.ShapeDtypeStruct(shp, F32),
                   jax.ShapeDtypeStruct(shp, BF16),
                   jax.ShapeDtypeStruct((DEPTH, BATCH, MEM_HEADS, MEM_TOKENS, 2 * MEM_HEAD_DIM), BF16)),
        grid=(DEPTH, BATCH),
        in_specs=[pl.BlockSpec((None, MEM_TOKENS, D_MODEL), lambda l, b: (b, 0, 0)),
                  pl.BlockSpec((None, 1, D_MODEL), lambda l, b: (l, 0, 0)),
                  pl.BlockSpec((None, D_MODEL, 2 * MEM_DIM), lambda l, b: (l, 0, 0))],
        out_specs=(blk, blk, blk,
                   pl.BlockSpec((None, None, MEM_HEADS, MEM_TOKENS, 2 * MEM_HEAD_DIM),
                                lambda l, b: (l, b, 0, 0, 0))),
        compiler_params=_params(32 * 1024 * 1024, 2),
        name="memkv",
    )(mem_prompt, mem_norm.reshape(DEPTH, 1, D_MODEL), w_mem_kv)


def _ffn_kernel(*refs, n_x, attn_proj, shared_kv, layer):
    refs = list(refs)
    x_refs = [refs.pop(0) for _ in range(n_x)]
    g_ref = refs.pop(0)
    w_hbm = [refs.pop(0) for _ in range(3)]
    if attn_proj:
        gm_ref, win_ref = refs.pop(0), refs.pop(0)
    kv_in = [refs.pop(0) for _ in range(4)] if shared_kv else None
    x1_ref = refs.pop(0)
    if attn_proj:
        qa_ref, qm_ref = refs.pop(0), refs.pop(0)
    kv_out = [refs.pop(0) for _ in range(4)] if shared_kv else None
    act_ref = refs.pop(0)
    w = _FfnWeights(layer, w_hbm, refs)

    def body(load_weights):
        x = _pick_group(*x_refs) if n_x == 2 else x_refs[0][...]
        if shared_kv:
            _emit_shared_kv(x, *kv_in, *kv_out)
        x1 = _ffn_half_step(x, g_ref, w, act_ref, load_weights)
        x1_ref[...] = x1
        if attn_proj:
            hm = _rms(x1, gm_ref[...]).astype(BF16)
            qa_ref[...] = (_dot(hm, win_ref[:, :ATTN_DIM].astype(BF16)) * ATTN_Q_SCALE).astype(BF16)
            qm_ref[...] = (_dot(hm, win_ref[:, ATTN_DIM:].astype(BF16)) * MEM_Q_SCALE).astype(BF16)

    _first_step_loads(w, body)


def _ffn(x, g, ffn_w, layer, gm=None, win=None, win_layer=None, shared_kv_w=None):
    attn_proj = win is not None
    shared_kv = shared_kv_w is not None
    xs = list(x) if isinstance(x, tuple) else [x]
    in_specs = ((_group_specs(D_MODEL) if len(xs) == 2 else [_row_spec(D_MODEL)])
                + [_resident((1, D_MODEL))] + _FfnWeights.in_specs())
    args = xs + [g.reshape(1, D_MODEL)] + list(ffn_w)
    out_shape = [jax.ShapeDtypeStruct((ROWS, D_MODEL), F32)]
    out_specs = [_row_spec(D_MODEL)]
    resident = 0
    streamed = (1 + len(xs)) * _ROW_TILE_F32
    if attn_proj:
        in_specs += [_resident((1, D_MODEL)), _resident_layer((D_MODEL, ATTN_DIM + MEM_DIM), win_layer)]
        args += [gm.reshape(1, D_MODEL), win]
        out_shape += [jax.ShapeDtypeStruct((ROWS, ATTN_DIM), BF16), jax.ShapeDtypeStruct((ROWS, MEM_DIM), BF16)]
        out_specs += [_row_spec(ATTN_DIM), _row_spec(MEM_DIM)]
        resident += D_MODEL * (ATTN_DIM + MEM_DIM) * 4
        streamed += TILE_ROWS * (ATTN_DIM + MEM_DIM) * 2
    if shared_kv:
        kv_in_specs, kv_shape, kv_specs, kv_resident, kv_streamed = _shared_kv_specs()
        in_specs += kv_in_specs
        args += [shared_kv_w[0].reshape(1, D_MODEL)] + list(shared_kv_w[1:])
        out_shape += kv_shape
        out_specs += kv_specs
        resident += kv_resident
        streamed += kv_streamed
    return pl.pallas_call(
        functools.partial(_ffn_kernel, n_x=len(xs), attn_proj=attn_proj, shared_kv=shared_kv, layer=layer),
        out_shape=tuple(out_shape),
        grid=(ROW_TILES,),
        in_specs=in_specs,
        out_specs=tuple(out_specs),
        scratch_shapes=[pltpu.VMEM((TILE_ROWS, FFN_DIM), BF16)] + _FfnWeights.scratch_shapes(),
        compiler_params=_params(_vmem_limit(
            resident, streamed, _ACT_BYTES + _FfnWeights.SCRATCH_BYTES, 3 * _ROW_TILE_F32)),
        name="ffn_attn_proj" if attn_proj else "ffn",
    )(*args)


def _inproj_conv_kernel(x_ref, gm_ref, win_ref, cw_ref, pre_ref,
                        ytok_ref, qm_ref, tail_ref, us_ref, shift_ref):
    i = pl.program_id(0)
    hm = _rms(x_ref[...], gm_ref[...]).astype(BF16)
    qm_ref[...] = (_dot(hm, win_ref[:, 3 * CONV_DIM:].astype(BF16)) * MEM_Q_SCALE).astype(BF16)

    def chunk(cc, prompt):
        sl = slice(cc * CONV_CHUNK, (cc + 1) * CONV_CHUNK)
        c_gate = _dot(hm, win_ref[:, CONV_DIM + cc * CONV_CHUNK:CONV_DIM + (cc + 1) * CONV_CHUNK].astype(BF16))
        x_in = _dot(hm, win_ref[:, 2 * CONV_DIM + cc * CONV_CHUNK:2 * CONV_DIM + (cc + 1) * CONV_CHUNK].astype(BF16))
        u = c_gate * x_in
        if prompt:
            shift_ref[V7X_SUBLANES:, sl] = u
            u1 = shift_ref[V7X_SUBLANES - 1:V7X_SUBLANES - 1 + TILE_ROWS, sl]
            u2 = shift_ref[V7X_SUBLANES - 2:V7X_SUBLANES - 2 + TILE_ROWS, sl]
            last = u[TILE_ROWS - V7X_SUBLANES:, :]
            shift_ref[:V7X_SUBLANES, sl] = last
            tail_ref[0, :, sl] = last
        else:
            t = lax.broadcasted_iota(jnp.int32, (TILE_ROWS, CONV_CHUNK), 0) % DEC_SEQ
            p2 = pre_ref[:, sl]
            p1 = pltpu.roll(p2, TILE_ROWS - 1, axis=0)
            u1 = jnp.where(t == 0, p1, pltpu.roll(u, 1, axis=0))
            u2 = jnp.where(t < 2, p2, pltpu.roll(u, 2, axis=0))
            us_ref[:, sl] = u
            tail_ref[0, :, sl] = jnp.zeros((V7X_SUBLANES, CONV_CHUNK), F32)
        w = cw_ref[:, sl]
        conv = w[0:1] * u2 + w[1:2] * u1 + w[2:3] * u
        b_gate = _dot(hm, win_ref[:, sl].astype(BF16))
        ytok_ref[:, sl] = (b_gate * conv).astype(BF16)

    @pl.when(i < PROMPT_TILES)
    def _():
        @pl.when(i % TILES_PER_SEQ == 0)
        def _():
            shift_ref[:V7X_SUBLANES, :] = jnp.zeros((V7X_SUBLANES, CONV_DIM), F32)
        for cc in range(CONV_DIM // CONV_CHUNK):
            chunk(cc, True)

    @pl.when(i >= PROMPT_TILES)
    def _():
        for cc in range(CONV_DIM // CONV_CHUNK):
            chunk(cc, False)


def _inproj_conv(x1, gm, win, conv_w, prefix_rows, layer):
    sample_idx = lambda i: (jnp.maximum(i - PROMPT_TILES, 0), 0)
    win_bytes = D_MODEL * (3 * CONV_DIM + MEM_DIM) * 4
    return pl.pallas_call(
        _inproj_conv_kernel,
        out_shape=(jax.ShapeDtypeStruct((ROWS, CONV_DIM), BF16),
                   jax.ShapeDtypeStruct((ROWS, MEM_DIM), BF16),
                   jax.ShapeDtypeStruct((ROW_TILES, V7X_SUBLANES, CONV_DIM), F32),
                   jax.ShapeDtypeStruct((ROWS_SAMPLE, CONV_DIM), F32)),
        grid=(ROW_TILES,),
        in_specs=[_row_spec(D_MODEL), _resident((1, D_MODEL)),
                  _resident_layer((D_MODEL, 3 * CONV_DIM + MEM_DIM), layer),
                  _resident_layer((CONV_WIDTH, CONV_DIM), layer),
                  pl.BlockSpec((TILE_ROWS, CONV_DIM), sample_idx)],
        out_specs=(_row_spec(CONV_DIM), _row_spec(MEM_DIM),
                   pl.BlockSpec((1, V7X_SUBLANES, CONV_DIM), lambda i: (i, 0, 0)),
                   pl.BlockSpec((TILE_ROWS, CONV_DIM), sample_idx)),
        scratch_shapes=[pltpu.VMEM((TILE_ROWS + V7X_SUBLANES, CONV_DIM), F32)],
        compiler_params=_params(_vmem_limit(
            win_bytes, 3 * _ROW_TILE_F32 + TILE_ROWS * (CONV_DIM + MEM_DIM) * 2,
            _ROW_TILE_F32 + V7X_SUBLANES * CONV_DIM * 4, 6 * _ROW_TILE_F32)),
        name="inproj_conv",
    )(x1, gm.reshape(1, D_MODEL), win, conv_w, prefix_rows)


VZ_WIDTH = N_KV_HEADS * 2 * 2 * V7X_LANES


def _emit_shared_kv(x, g_ref, wk_ref, wkt_ref, wv_ref, k_ref, v_ref, ktz_ref, vz_ref):
    hk = _rms(x, g_ref[...]).astype(BF16)
    k_ref[...] = _dot(hk, wk_ref[...].astype(BF16))
    v = _dot(hk, wv_ref[...].astype(BF16))
    v_ref[...] = v
    kt = _dot_nt(wkt_ref[...].astype(BF16), hk)
    lo = lax.broadcasted_iota(jnp.int32, (TILE_ROWS, V7X_LANES), 1) < HALF_LANES
    hi = jnp.logical_not(lo)
    ones = (jnp.where(lo, 1.0, 0.0).astype(BF16), jnp.where(lo, 0.0, 1.0).astype(BF16))
    zero_k = jnp.zeros((HEAD_DIM, TILE_ROWS), BF16)
    for pair in range(N_KV_HEADS // 2):
        v_pair = v[:, pair * V7X_LANES:(pair + 1) * V7X_LANES]
        v_swap = pltpu.roll(v_pair, HALF_LANES, axis=1)
        for odd in range(2):
            h = 2 * pair + odd
            kth = kt[h * HEAD_DIM:(h + 1) * HEAD_DIM, :].astype(BF16)
            ktz_ref[h, 0] = jnp.concatenate([kth, zero_k], axis=0)
            ktz_ref[h, 1] = jnp.concatenate([zero_k, kth], axis=0)
            halves = (jnp.where(lo, v_swap if odd else v_pair, 0.0), jnp.where(hi, v_pair if odd else v_swap, 0.0))
            for e in range(2):
                c0 = (h * 2 + e) * 2 * V7X_LANES
                vz_ref[:, c0:c0 + 2 * V7X_LANES] = jnp.concatenate(
                    [halves[e].astype(BF16), ones[e]], axis=1)


def _shared_kv_specs():
    in_specs = [_resident((1, D_MODEL)), _resident((D_MODEL, KV_DIM)), _resident((KV_DIM, D_MODEL)),
                _resident((D_MODEL, KV_DIM))]
    out_shape = [jax.ShapeDtypeStruct((ROWS, KV_DIM), F32), jax.ShapeDtypeStruct((ROWS, KV_DIM), F32),
                 jax.ShapeDtypeStruct((N_KV_HEADS, 2, V7X_LANES, ROWS), BF16),
                 jax.ShapeDtypeStruct((ROWS, VZ_WIDTH), BF16)]
    out_specs = [_row_spec(KV_DIM), _row_spec(KV_DIM),
                 pl.BlockSpec((N_KV_HEADS, 2, V7X_LANES, TILE_ROWS), lambda i: (0, 0, 0, i)),
                 _row_spec(VZ_WIDTH)]
    resident = 3 * D_MODEL * KV_DIM * 4
    streamed = TILE_ROWS * (2 * KV_DIM * 4 + 2 * KV_DIM * 2 + VZ_WIDTH * 2)
    return in_specs, out_shape, out_specs, resident, streamed


def _attn_prompt_kernel(*refs, swa):
    if swa:
        (qm_ref, mk_ref, mve_ref, qa_ref, ktp_ref, ktc_ref, vzp_ref, vzc_ref, bias_ref,
         ymem_ref, ytok_ref) = refs
    else:
        qm_ref, mk_ref, mve_ref, ymem_ref = refs

    for h in range(MEM_HEADS):
        sl = slice(h * MEM_HEAD_DIM, (h + 1) * MEM_HEAD_DIM)
        s = _dot_nt(qm_ref[:, sl], mk_ref[:, sl])
        p = jnp.exp2(s - jnp.max(s, axis=-1, keepdims=True)).astype(BF16)
        oe = _dot(p, mve_ref[h])
        ymem_ref[:, sl] = (oe[:, :MEM_HEAD_DIM] / oe[:, MEM_HEAD_DIM:]).astype(BF16)
    if not swa:
        return

    no_prev = ((pl.program_id(0) % TILES_PER_SEQ) == 0).astype(jnp.int32)
    key0_col = lax.broadcasted_iota(jnp.int32, (V7X_LANES, WINDOW), 1) == 0
    key0_row = lax.broadcasted_iota(jnp.int32, (WINDOW, V7X_LANES), 0) == 0
    zero_kt = jnp.zeros((V7X_LANES, WINDOW), BF16)
    zero_v = jnp.zeros((WINDOW, V7X_LANES), BF16)

    for n in range(BLOCKS_PER_TILE):
        rows = slice(n * WINDOW, (n + 1) * WINDOW)
        for h in range(N_KV_HEADS):
            k_sel, v_sel = [], []
            for e in range(2):
                k_prev = ktp_ref[h, e] if n == 0 else ktc_ref[h, e, :, (n - 1) * WINDOW:n * WINDOW]
                k_prev = jnp.where(key0_col, zero_kt, k_prev)
                k_sel.append(jnp.concatenate([k_prev, ktc_ref[h, e, :, rows]], axis=1))
                c0 = (h * 2 + e) * 2 * V7X_LANES
                cols = slice(c0, c0 + 2 * V7X_LANES)
                v_prev = vzp_ref[:, cols] if n == 0 else vzc_ref[(n - 1) * WINDOW:n * WINDOW, cols]
                v_prev = jnp.concatenate(
                    [jnp.where(key0_row, zero_v, v_prev[:, :V7X_LANES]), v_prev[:, V7X_LANES:]], axis=1)
                v_sel.append(jnp.concatenate([v_prev, vzc_ref[rows, cols]], axis=0))
            for pr in range(GROUP // 2):
                c0 = (h * (GROUP // 2) + pr) * V7X_LANES
                qp = qa_ref[rows, c0:c0 + V7X_LANES]
                acc = None
                for e in range(2):
                    j = h * GROUP + pr * 2 + e
                    bias = bias_ref[no_prev, j] if n == 0 else bias_ref[0, j]
                    s = _dot(qp, k_sel[e]) + bias
                    p = jnp.exp2(s - jnp.max(s, axis=-1, keepdims=True)).astype(BF16)
                    part = _dot(p, v_sel[e])
                    acc = part if acc is None else acc + part
                ytok_ref[rows, c0:c0 + V7X_LANES] = (acc[:, :V7X_LANES] / acc[:, V7X_LANES:]).astype(BF16)


def _attn_prompt(qm, mk_bf, mv_ext, layer, swa_args=None):
    swa = swa_args is not None
    batch_of = lambda i: i // TILES_PER_SEQ
    in_specs = [_row_spec(MEM_DIM),
                pl.BlockSpec((None, None, MEM_TOKENS, MEM_DIM), lambda i: (layer, batch_of(i), 0, 0)),
                pl.BlockSpec((None, None, MEM_HEADS, MEM_TOKENS, 2 * MEM_HEAD_DIM),
                             lambda i: (layer, batch_of(i), 0, 0, 0))]
    args = [qm, mk_bf, mv_ext]
    out_shape = [jax.ShapeDtypeStruct((ROWS_PROMPT, MEM_DIM), BF16)]
    out_specs = [_row_spec(MEM_DIM)]
    streamed = TILE_ROWS * MEM_DIM * 4 + MEM_TOKENS * MEM_DIM * 2 * 3
    resident = 0
    if swa:
        qa, ktz, vz, bias_p, swa_layer = swa_args
        prev_blk = lambda i: jnp.maximum(i * BLOCKS_PER_TILE - 1, 0)
        in_specs += [_row_spec(ATTN_DIM),
                     pl.BlockSpec((N_KV_HEADS, 2, V7X_LANES, WINDOW), lambda i: (0, 0, 0, prev_blk(i))),
                     pl.BlockSpec((N_KV_HEADS, 2, V7X_LANES, TILE_ROWS), lambda i: (0, 0, 0, i)),
                     pl.BlockSpec((WINDOW, VZ_WIDTH), lambda i: (prev_blk(i), 0)),
                     pl.BlockSpec((TILE_ROWS, VZ_WIDTH), lambda i: (i, 0)),
                     _resident_layer((2, N_HEADS, WINDOW, 2 * WINDOW), swa_layer)]
        args += [qa, ktz, ktz, vz, vz, bias_p]
        out_shape += [jax.ShapeDtypeStruct((ROWS_PROMPT, ATTN_DIM), BF16)]
        out_specs += [_row_spec(ATTN_DIM)]
        resident = 2 * N_HEADS * WINDOW * 2 * WINDOW * 4
        streamed += 2 * TILE_ROWS * ATTN_DIM * 2 + (TILE_ROWS + WINDOW) * (2 * KV_DIM + VZ_WIDTH) * 2
    return pl.pallas_call(
        functools.partial(_attn_prompt_kernel, swa=swa),
        out_shape=tuple(out_shape),
        grid=(PROMPT_TILES,),
        in_specs=in_specs,
        out_specs=tuple(out_specs),
        compiler_params=_params(_vmem_limit(resident, streamed, 0, 8 * _ROW_TILE_F32)),
        name="attn_prompt_swa" if swa else "attn_prompt_mem",
    )(*args)


def _attn_sample_kernel(*refs, swa, emit_cache):
    if swa:
        (qm_ref, mk_ref, mv_ref, qa_ref, ck_ref, cv_ref, kn_ref, vn_ref, bias_ref,
         ymem_ref, ytok_ref) = refs[:11]
        if emit_cache:
            kc_out_ref, vc_out_ref = refs[11:]
            keep_cached = lax.broadcasted_iota(jnp.int32, (KV_DIM, WINDOW), 1) < WINDOW - DEC_SEQ
    else:
        qm_ref, mk_ref, mv_ref, ymem_ref = refs

    qm_all = qm_ref[...].astype(F32)
    mem_rows = MEM_HEADS * DEC_SEQ
    own_head = (lax.broadcasted_iota(jnp.int32, (mem_rows, MEM_TOKENS * MEM_HEADS), 1) % MEM_HEADS
                == lax.broadcasted_iota(jnp.int32, (mem_rows, MEM_TOKENS * MEM_HEADS), 0) // DEC_SEQ)
    if swa:
        qa_all = qa_ref[...].astype(F32)
        kn_all = kn_ref[...]
        vn_all = vn_ref[...]
        bias = bias_ref[...]
        key0 = lax.broadcasted_iota(jnp.int32, (KV_DIM, WINDOW), 1) == 0
        pad = jnp.zeros((WINDOW - DEC_SEQ, KV_DIM), F32)
        lo = lax.broadcasted_iota(jnp.int32, (DEC_SEQ, V7X_LANES), 1) < HALF_LANES
        hi = jnp.logical_not(lo)
        zero_slab = jnp.zeros((DEC_SEQ, V7X_LANES), F32)

    ymem_rows, ytok_rows = [], []
    for b in range(SAMPLE_BATCH_BLOCK):
        rows = slice(b * DEC_SEQ, (b + 1) * DEC_SEQ)
        qm = qm_all[rows]
        q_heads = jnp.concatenate(
            [qm[:, h * MEM_HEAD_DIM:(h + 1) * MEM_HEAD_DIM] for h in range(MEM_HEADS)], axis=0)
        s = jnp.where(own_head, _dot_nt(q_heads, mk_ref[b]), NEG_INF)
        p = jnp.exp2(s - jnp.max(s, axis=-1, keepdims=True))
        o = _dot(p, mv_ref[b]) / jnp.sum(p, axis=-1, keepdims=True)
        ymem_rows.append(jnp.concatenate(
            [o[h * DEC_SEQ:(h + 1) * DEC_SEQ] for h in range(MEM_HEADS)], axis=1))
        if not swa:
            continue
        qb = qa_all[rows]
        groups = []
        for j in range(N_HEADS):
            kvh = j // GROUP
            slab = qb[:, (j // 2) * V7X_LANES:(j // 2 + 1) * V7X_LANES]
            if j % 2 != kvh % 2:
                slab = pltpu.roll(slab, HALF_LANES, axis=1)
            slab = jnp.where(lo if kvh % 2 == 0 else hi, slab, 0.0)
            groups.append(jnp.concatenate([slab, zero_slab] if kvh // 2 == 0 else [zero_slab, slab], axis=1))
        q_bd = jnp.concatenate(groups, axis=0)
        kt_cache = jnp.where(key0, 0.0, ck_ref[b])
        vt_cache = jnp.where(key0, 0.0, cv_ref[b])
        k_new = jnp.concatenate([kn_all[rows], pad], axis=0)
        v_new = jnp.concatenate([vn_all[rows], pad], axis=0)
        if emit_cache:
            shift = WINDOW - DEC_SEQ
            kc_out_ref[b] = jnp.where(keep_cached, pltpu.roll(ck_ref[b], shift, axis=1),
                                      pltpu.roll(k_new.T, shift, axis=1))
            vc_out_ref[b] = jnp.where(keep_cached, pltpu.roll(cv_ref[b], shift, axis=1),
                                      pltpu.roll(v_new.T, shift, axis=1))
        s = jnp.concatenate([_dot(q_bd, kt_cache), _dot_nt(q_bd, k_new)], axis=1) + bias
        p = jnp.exp2(s - jnp.max(s, axis=-1, keepdims=True))
        o_full = ((_dot_nt(p[:, :WINDOW], vt_cache) + _dot(p[:, WINDOW:], v_new))
                  / jnp.sum(p, axis=-1, keepdims=True))
        pairs = []
        for pair in range(N_HEADS // 2):
            acc = None
            for e in range(2):
                j = pair * 2 + e
                kvh = j // GROUP
                slab = o_full[j * DEC_SEQ:(j + 1) * DEC_SEQ, (kvh // 2) * V7X_LANES:(kvh // 2 + 1) * V7X_LANES]
                if e != kvh % 2:
                    slab = pltpu.roll(slab, HALF_LANES, axis=1)
                slab = jnp.where(lo if e == 0 else hi, slab, 0.0)
                acc = slab if acc is None else acc + slab
            pairs.append(acc)
        ytok_rows.append(jnp.concatenate(pairs, axis=1))

    ymem_ref[...] = jnp.concatenate(ymem_rows, axis=0).astype(BF16)
    if swa:
        ytok_ref[...] = jnp.concatenate(ytok_rows, axis=0).astype(BF16)


def _attn_sample(qm, cache_k, cache_v, layer, swa_args=None, emit_cache=False):
    swa = swa_args is not None
    assert swa or not emit_cache
    row0 = ROWS_PROMPT // SAMPLE_BLOCK_ROWS
    blk_rows = lambda width: pl.BlockSpec((SAMPLE_BLOCK_ROWS, width), lambda i: (row0 + i, 0))
    out_rows = lambda width: pl.BlockSpec((SAMPLE_BLOCK_ROWS, width), lambda i: (i, 0))
    cache_spec = pl.BlockSpec((None, SAMPLE_BATCH_BLOCK, MEM_TOKENS * MEM_HEADS, MEM_HEAD_DIM),
                              lambda i: (layer, i, 0, 0))
    in_specs = [blk_rows(MEM_DIM), cache_spec, cache_spec]
    args = [qm, cache_k, cache_v]
    out_shape = [jax.ShapeDtypeStruct((ROWS_SAMPLE, MEM_DIM), BF16)]
    out_specs = [out_rows(MEM_DIM)]
    streamed = 2 * SAMPLE_BATCH_BLOCK * MEM_TOKENS * MEM_DIM * 4 + SAMPLE_BLOCK_ROWS * MEM_DIM * 4
    if swa:
        qa, swa_k, swa_v, k_new, v_new, bias_s, swa_layer = swa_args
        swa_spec = pl.BlockSpec((SAMPLE_BATCH_BLOCK, KV_DIM, WINDOW), lambda i: (i, 0, 0))
        in_specs += [blk_rows(ATTN_DIM), swa_spec, swa_spec, out_rows(KV_DIM), out_rows(KV_DIM),
                     _resident_layer((N_HEADS * DEC_SEQ, SAMPLE_KEYS_PADDED), swa_layer)]
        args += [qa, swa_k, swa_v, k_new, v_new, bias_s]
        out_shape += [jax.ShapeDtypeStruct((ROWS_SAMPLE, ATTN_DIM), BF16)]
        out_specs += [out_rows(ATTN_DIM)]
        streamed += 2 * SAMPLE_BATCH_BLOCK * WINDOW * KV_DIM * 4 + SAMPLE_BLOCK_ROWS * (ATTN_DIM + KV_DIM) * 4
        if emit_cache:
            out_shape += [jax.ShapeDtypeStruct((DEC_BATCH, KV_DIM, WINDOW), F32)] * 2
            out_specs += [swa_spec, swa_spec]
            streamed += 2 * SAMPLE_BATCH_BLOCK * WINDOW * KV_DIM * 4
    return pl.pallas_call(
        functools.partial(_attn_sample_kernel, swa=swa, emit_cache=emit_cache),
        out_shape=tuple(out_shape),
        grid=(DEC_BATCH // SAMPLE_BATCH_BLOCK,),
        in_specs=in_specs,
        out_specs=tuple(out_specs),
        compiler_params=_params(_vmem_limit(1 << 20, streamed, 0, 6 * _ROW_TILE_F32)),
        name="attn_sample_swa" if swa else "attn_sample_mem",
    )(*args)


def _outffn_kernel(*refs, split_tok, final, layer):
    refs = list(refs)
    x1_ref = refs.pop(0)
    tok_refs = [refs.pop(0) for _ in range(2 if split_tok else 1)]
    ymp_ref, yms_ref, wo_ref, g_ref = [refs.pop(0) for _ in range(4)]
    w_hbm = [refs.pop(0) for _ in range(3)]
    gf_ref = refs.pop(0) if final else None
    out_refs = [refs.pop(0) for _ in range(2 if final else 1)]
    act_ref = refs.pop(0)
    w = _FfnWeights(layer, w_hbm, refs)
    tok_dim = wo_ref.shape[0] - MEM_DIM

    def body(load_weights):
        y_tok = _pick_group(*tok_refs) if split_tok else tok_refs[0][...]
        y_mem = _pick_group(ymp_ref, yms_ref)
        x2 = (x1_ref[...] + _dot(y_tok, wo_ref[:tok_dim, :].astype(BF16))
              + _dot(y_mem, wo_ref[tok_dim:, :].astype(BF16)))
        x3 = _ffn_half_step(x2, g_ref, w, act_ref, load_weights)
        if not final:
            out_refs[0][...] = x3
            return
        y = _rms(x3, gf_ref[...])

        @pl.when(pl.program_id(0) < PROMPT_TILES)
        def _():
            out_refs[0][...] = y

        @pl.when(pl.program_id(0) >= PROMPT_TILES)
        def _():
            out_refs[1][...] = y

    _first_step_loads(w, body)


def _outffn(x1, y_tok, ymem_p, ymem_s, wo, wo_layer, g, ffn_w, layer, final_gain=None):
    split_tok = isinstance(y_tok, tuple)
    final = final_gain is not None
    tok_dim = wo.shape[1] - MEM_DIM
    in_specs = [_row_spec(D_MODEL)]
    args = [x1]
    if split_tok:
        in_specs += _group_specs(tok_dim)
        args += list(y_tok)
    else:
        in_specs += [_row_spec(tok_dim)]
        args += [y_tok]
    in_specs += (_group_specs(MEM_DIM) + [_resident_layer((tok_dim + MEM_DIM, D_MODEL), wo_layer)]
                 + [_resident((1, D_MODEL))] + _FfnWeights.in_specs())
    args += [ymem_p, ymem_s, wo, g.reshape(1, D_MODEL)] + list(ffn_w)
    if final:
        in_specs += [_resident((1, D_MODEL))]
        args += [final_gain.reshape(1, D_MODEL)]
    resident = (tok_dim + MEM_DIM) * D_MODEL * 4
    streamed = 2 * _ROW_TILE_F32 + 2 * TILE_ROWS * (tok_dim + MEM_DIM) * 2
    if final:
        out_shape = [jax.ShapeDtypeStruct((ROWS_PROMPT, D_MODEL), F32),
                     jax.ShapeDtypeStruct((ROWS_SAMPLE, D_MODEL), F32)]
        out_specs = _group_specs(D_MODEL)
        streamed += _ROW_TILE_F32
    else:
        out_shape = [jax.ShapeDtypeStruct((ROWS, D_MODEL), F32)]
        out_specs = [_row_spec(D_MODEL)]
    return pl.pallas_call(
        functools.partial(_outffn_kernel, split_tok=split_tok, final=final, layer=layer),
        out_shape=tuple(out_shape),
        grid=(ROW_TILES,),
        in_specs=in_specs,
        out_specs=tuple(out_specs),
        scratch_shapes=[pltpu.VMEM((TILE_ROWS, FFN_DIM), BF16)] + _FfnWeights.scratch_shapes(),
        compiler_params=_params(_vmem_limit(
            resident, streamed, _ACT_BYTES + _FfnWeights.SCRATCH_BYTES, 3 * _ROW_TILE_F32)),
        name="outffn_final" if final else "outffn",
    )(*args)


def kernel(x_prompt, x_sample, state_conv, cache_swa_k, cache_swa_v, cache_mem_k, cache_mem_v, mem_prompt, ffn1_norm, ffn1_wg, ffn1_wu, ffn1_wd, mix_norm, w_in_a, conv_w, w_out_a, kv_norm, w_kv, w_in_b, attn_sinks, rel_bias, w_out_b, mem_norm, w_mem_kv, ffn2_norm, ffn2_wg, ffn2_wu, ffn2_wd, final_norm):
    ffn1 = (ffn1_wg, ffn1_wu, ffn1_wd)
    ffn2 = (ffn2_wg, ffn2_wu, ffn2_wd)
    wk = w_kv[:, :KV_DIM]
    wv = w_kv[:, KV_DIM:]

    mem_k, mem_v, mem_k_bf, mem_v_ext = _memkv(mem_prompt, mem_norm, w_mem_kv)
    bias_p, bias_s = _bias_tables(rel_bias, attn_sinks)
    swa_k_cache = cache_swa_k.transpose(0, 2, 3, 1).reshape(DEC_BATCH, KV_DIM, WINDOW)
    swa_v_cache = cache_swa_v.transpose(0, 2, 3, 1).reshape(DEC_BATCH, KV_DIM, WINDOW)
    mem_rows_shape = (DEPTH, DEC_BATCH, MEM_TOKENS * MEM_HEADS, MEM_HEAD_DIM)
    cache_k = cache_mem_k.reshape(mem_rows_shape)
    cache_v = cache_mem_v.reshape(mem_rows_shape)

    x = (x_prompt.reshape(ROWS_PROMPT, D_MODEL), x_sample.reshape(ROWS_SAMPLE, D_MODEL))
    tails, sample_us = [], []
    k_rows = v_rows = ktz = vz = k_new = v_new = None
    for l in range(DEPTH):
        last = l == DEPTH - 1
        if l < N_A_LAYERS:
            (x1,) = _ffn(x, ffn1_norm[l], ffn1, l)
            prefix_rows = jnp.pad(state_conv[l], ((0, 0), (0, DEC_SEQ - (CONV_WIDTH - 1)), (0, 0)))
            y_tok, qm, tail, us = _inproj_conv(x1, mix_norm[l], w_in_a, conv_w,
                                               prefix_rows.reshape(ROWS_SAMPLE, CONV_DIM), l)
            tails.append(tail)
            sample_us.append(us)
            (ymem_p,) = _attn_prompt(qm, mem_k_bf, mem_v_ext, l)
            (ymem_s,) = _attn_sample(qm, cache_k, cache_v, l)
            wo, wo_layer = w_out_a, l
        else:
            j = l - N_A_LAYERS
            if j == 0:
                x1, qa, qm, k_rows, v_rows, ktz, vz = _ffn(
                    x, ffn1_norm[l], ffn1, l, mix_norm[l], w_in_b, j,
                    shared_kv_w=(kv_norm, wk, wk.T, wv))
                k_new = k_rows[ROWS_PROMPT:]
                v_new = v_rows[ROWS_PROMPT:]
            else:
                x1, qa, qm = _ffn(x, ffn1_norm[l], ffn1, l, mix_norm[l], w_in_b, j)
            ymem_p, ytok_p = _attn_prompt(qm, mem_k_bf, mem_v_ext, l, (qa, ktz, vz, bias_p, j))
            sample_out = _attn_sample(qm, cache_k, cache_v, l,
                                      (qa, swa_k_cache, swa_v_cache, k_new, v_new, bias_s, j), emit_cache=j == 0)
            ymem_s, ytok_s = sample_out[:2]
            if j == 0:
                new_k_cache, new_v_cache = sample_out[2:]
            y_tok = (ytok_p, ytok_s)
            wo, wo_layer = w_out_b, j
        if not last:
            (x,) = _outffn(x1, y_tok, ymem_p, ymem_s, wo, wo_layer, ffn2_norm[l], ffn2, l)
        else:
            y_prompt, y_sample = _outffn(x1, y_tok, ymem_p, ymem_s, wo, wo_layer, ffn2_norm[l], ffn2, l,
                                         final_gain=final_norm)

    keep = CONV_WIDTH - 1
    last_tiles = np.arange(BATCH) * TILES_PER_SEQ + TILES_PER_SEQ - 1
    conv_state_prompt = jnp.stack([t[last_tiles, V7X_SUBLANES - keep:, :] for t in tails])
    conv_state_sample = jnp.stack([u.reshape(DEC_BATCH, DEC_SEQ, CONV_DIM)[:, DEC_SEQ - keep:, :] for u in sample_us])
    k_tail = jnp.stack([k_rows[(b + 1) * SEQ - WINDOW:(b + 1) * SEQ] for b in range(BATCH)])
    v_tail = jnp.stack([v_rows[(b + 1) * SEQ - WINDOW:(b + 1) * SEQ] for b in range(BATCH)])
    swa_k_prompt = k_tail.reshape(BATCH, WINDOW, N_KV_HEADS, HEAD_DIM)
    swa_v_prompt = v_tail.reshape(BATCH, WINDOW, N_KV_HEADS, HEAD_DIM)
    to_cache_layout = lambda c: c.reshape(DEC_BATCH, N_KV_HEADS, HEAD_DIM, WINDOW).transpose(0, 3, 1, 2)
    swa_k_sample = to_cache_layout(new_k_cache)
    swa_v_sample = to_cache_layout(new_v_cache)
    mem_shape = (DEPTH, BATCH, MEM_TOKENS, MEM_HEADS, MEM_HEAD_DIM)
    return (y_prompt.reshape(BATCH, SEQ, D_MODEL), y_sample.reshape(DEC_BATCH, DEC_SEQ, D_MODEL),
            conv_state_prompt, conv_state_sample,
            swa_k_prompt, swa_v_prompt, swa_k_sample, swa_v_sample,
            mem_k.reshape(mem_shape), mem_v.reshape(mem_shape))
```

```python
import functools
import math

import numpy as np
import jax
import jax.numpy as jnp
from jax import lax
from jax.experimental import pallas as pl
from jax.experimental.pallas import tpu as pltpu

D_MODEL = 1024
BATCH = 2
SEQ = 8192
DEPTH = 4
DEC_BATCH = 128
DEC_SEQ = 8
N_A_LAYERS = DEPTH // 2
FFN_DIM = 2816
CONV_DIM = D_MODEL
CONV_WIDTH = 3
N_HEADS = 16
N_KV_HEADS = 4
HEAD_DIM = 64
GROUP = N_HEADS // N_KV_HEADS
ATTN_DIM = N_HEADS * HEAD_DIM
KV_DIM = N_KV_HEADS * HEAD_DIM
WINDOW = 128
REL_BUCKETS = 32
REL_MAX_DIST = 128
MEM_TOKENS = 256
MEM_HEADS = 4
MEM_HEAD_DIM = 128
MEM_DIM = MEM_HEADS * MEM_HEAD_DIM
RMS_EPS = 1e-5

F32 = jnp.float32
BF16 = jnp.bfloat16
NEG_INF = float("-inf")

V7X_LANES = 128
V7X_SUBLANES = 8
V7X_MXU_DIM = 256
V7X_VMEM_BYTES = 64 * 1024 * 1024

ROWS_PROMPT = BATCH * SEQ
ROWS_SAMPLE = DEC_BATCH * DEC_SEQ
ROWS = ROWS_PROMPT + ROWS_SAMPLE
TILE_ROWS = 512
PROMPT_TILES = ROWS_PROMPT // TILE_ROWS
SAMPLE_TILES = ROWS_SAMPLE // TILE_ROWS
ROW_TILES = PROMPT_TILES + SAMPLE_TILES
TILES_PER_SEQ = SEQ // TILE_ROWS
FFN_CHUNK = V7X_MXU_DIM
CONV_CHUNK = V7X_MXU_DIM
BLOCKS_PER_TILE = TILE_ROWS // WINDOW
SAMPLE_BATCH_BLOCK = 8
SAMPLE_BLOCK_ROWS = SAMPLE_BATCH_BLOCK * DEC_SEQ
SAMPLE_KEYS = WINDOW + DEC_SEQ
SAMPLE_KEYS_PADDED = 2 * WINDOW
HALF_LANES = V7X_LANES // 2
LOG2E = math.log2(math.e)
MEM_Q_SCALE = MEM_HEAD_DIM ** -0.5 * LOG2E
ATTN_Q_SCALE = HEAD_DIM ** -0.5 * LOG2E

assert HEAD_DIM == HALF_LANES and MEM_HEAD_DIM == V7X_LANES
assert ROWS_PROMPT % TILE_ROWS == 0 and ROWS_SAMPLE % TILE_ROWS == 0 and SEQ % TILE_ROWS == 0
assert FFN_DIM % FFN_CHUNK == 0 and TILE_ROWS % WINDOW == 0


MIN_SCOPED_VMEM_BYTES = 48 * 1024 * 1024


def _vmem_limit(resident_bytes, streamed_bytes, scratch_bytes, temp_bytes):
    need = resident_bytes + 2 * streamed_bytes + scratch_bytes + temp_bytes
    assert need < V7X_VMEM_BYTES, need
    return max(int(need), MIN_SCOPED_VMEM_BYTES)


def _params(vmem_bytes, n_axes=1):
    return pltpu.CompilerParams(
        dimension_semantics=("arbitrary",) * n_axes, vmem_limit_bytes=vmem_bytes)


def _resident(shape):
    zeros = (0,) * len(shape)
    return pl.BlockSpec(shape, lambda *_: zeros, pipeline_mode=pl.Buffered(1))


def _resident_layer(shape, layer):
    idx = (layer,) + (0,) * len(shape)
    return pl.BlockSpec((None,) + tuple(shape), lambda *_: idx, pipeline_mode=pl.Buffered(1))


def _row_spec(width):
    return pl.BlockSpec((TILE_ROWS, width), lambda i: (i, 0))


def _group_specs(width):
    return [pl.BlockSpec((TILE_ROWS, width), lambda i: (jnp.minimum(i, PROMPT_TILES - 1), 0)),
            pl.BlockSpec((TILE_ROWS, width), lambda i: (jnp.maximum(i - PROMPT_TILES, 0), 0))]


def _pick_group(prompt_ref, sample_ref):
    return jnp.where(pl.program_id(0) < PROMPT_TILES, prompt_ref[...], sample_ref[...])


def _dot(a, b):
    return jnp.dot(a, b, preferred_element_type=F32)


def _dot_nt(a, b):
    return lax.dot_general(a, b, (((1,), (1,)), ((), ())), preferred_element_type=F32)


def _rms(x, g):
    return x * lax.rsqrt(jnp.mean(x * x, axis=-1, keepdims=True) + RMS_EPS) * g


FFN_CHUNKS = FFN_DIM // FFN_CHUNK
FFN_STAGE_SLOTS = 2


class _FfnWeights:
    def __init__(self, layer, hbm_refs, scratch_refs):
        self.layer = layer
        self.wg_hbm, self.wu_hbm, self.wd_hbm = hbm_refs
        self.wg, self.wu, self.wd, self.stage_in, self.stage_out, self.sems = scratch_refs

    @staticmethod
    def in_specs():
        return [pl.BlockSpec(memory_space=pl.ANY)] * 3

    @staticmethod
    def scratch_shapes():
        return [pltpu.VMEM((D_MODEL, FFN_DIM), BF16), pltpu.VMEM((D_MODEL, FFN_DIM), BF16),
                pltpu.VMEM((FFN_DIM, D_MODEL), BF16),
                pltpu.VMEM((2, FFN_STAGE_SLOTS, D_MODEL, FFN_CHUNK), F32),
                pltpu.VMEM((FFN_STAGE_SLOTS, FFN_CHUNK, D_MODEL), F32),
                pltpu.SemaphoreType.DMA((3, FFN_STAGE_SLOTS))]

    SCRATCH_BYTES = 3 * D_MODEL * FFN_DIM * 2 + 3 * FFN_STAGE_SLOTS * D_MODEL * FFN_CHUNK * 4

    def _copy(self, stream, c):
        slot = c % FFN_STAGE_SLOTS
        cols = pl.ds(c * FFN_CHUNK, FFN_CHUNK)
        if stream == 0:
            src, dst = self.wg_hbm.at[self.layer, :, cols], self.stage_in.at[0, slot]
        elif stream == 1:
            src, dst = self.wu_hbm.at[self.layer, :, cols], self.stage_in.at[1, slot]
        else:
            src, dst = self.wd_hbm.at[self.layer, cols, :], self.stage_out.at[slot]
        return pltpu.make_async_copy(src, dst, self.sems.at[stream, slot])

    def prime(self):
        for stream in range(3):
            for c in range(FFN_STAGE_SLOTS):
                self._copy(stream, c).start()

    def fetch(self, c):
        slot = c % FFN_STAGE_SLOTS
        sl = slice(c * FFN_CHUNK, (c + 1) * FFN_CHUNK)
        for stream in range(3):
            self._copy(stream, c).wait()
            if stream == 0:
                self.wg[:, sl] = self.stage_in[0, slot].astype(BF16)
            elif stream == 1:
                self.wu[:, sl] = self.stage_in[1, slot].astype(BF16)
            else:
                self.wd[sl, :] = self.stage_out[slot].astype(BF16)
            if c + FFN_STAGE_SLOTS < FFN_CHUNKS:
                self._copy(stream, c + FFN_STAGE_SLOTS).start()


def _ffn_half_step(x, g_ref, w, act_ref, load_weights):
    inv_rms = lax.rsqrt(jnp.mean(x * x, axis=-1, keepdims=True) + RMS_EPS)
    inv_rms_chunk = jnp.broadcast_to(inv_rms, (x.shape[0], FFN_CHUNK))
    h = (x * g_ref[...]).astype(BF16)
    for c in range(FFN_CHUNKS):
        if load_weights:
            w.fetch(c)
        sl = slice(c * FFN_CHUNK, (c + 1) * FFN_CHUNK)
        gate = _dot(h, w.wg[:, sl]) * inv_rms_chunk
        up = _dot(h, w.wu[:, sl])
        act_ref[:, sl] = (gate / (1.0 + jnp.exp(-gate)) * up).astype(BF16)
    return x + (0.5 * inv_rms) * _dot(act_ref[...], w.wd[...])


def _first_step_loads(w, body):
    @pl.when(pl.program_id(0) == 0)
    def _():
        w.prime()
        body(True)

    @pl.when(pl.program_id(0) > 0)
    def _():
        body(False)


_ROW_TILE_F32 = TILE_ROWS * D_MODEL * 4
_ACT_BYTES = TILE_ROWS * FFN_DIM * 2


def _t5_bucket_np(dist):
    n = np.maximum(dist, 0)
    exact = REL_BUCKETS // 2
    nf = np.maximum(n, 1).astype(np.float32)
    large = exact + (np.log(nf / np.float32(exact)) / np.float32(math.log(REL_MAX_DIST / exact))
                     * np.float32(REL_BUCKETS - exact)).astype(np.int32)
    large = np.minimum(large, REL_BUCKETS - 1)
    return np.where(n < exact, n, large).astype(np.int32)


def _bucket_tables():
    q = np.arange(WINDOW)[:, None]
    k = np.arange(2 * WINDOW)[None, :]
    dist = WINDOW + q - k
    prompt = np.where((dist >= 0) & (dist < WINDOW), _t5_bucket_np(dist), -1)
    t = (np.arange(N_HEADS * DEC_SEQ) % DEC_SEQ)[:, None]
    k = np.arange(SAMPLE_KEYS_PADDED)[None, :]
    dist = WINDOW + t - k
    ok = (dist >= 0) & (dist < WINDOW) & (k < SAMPLE_KEYS)
    sample = np.where(ok, _t5_bucket_np(dist), -1)
    return prompt.astype(np.int32), sample.astype(np.int32)


def _bias_kernel(rel_ref, sink_ref, bp_ref, bs_ref, op_ref, os_ref):
    layer = pl.program_id(0)
    j = pl.program_id(1)
    sink = sink_ref[layer, j] * LOG2E

    def build(bucket):
        acc = jnp.zeros(bucket.shape, F32)
        for b in range(REL_BUCKETS):
            acc = jnp.where(bucket == b, rel_ref[b, j], acc)
        return jnp.where(bucket < 0, NEG_INF, acc * LOG2E)

    col_p = lax.broadcasted_iota(jnp.int32, (WINDOW, 2 * WINDOW), 1)
    table = build(bp_ref[...])
    op_ref[0, 0, 0] = jnp.where(col_p == 0, sink, table)
    op_ref[0, 1, 0] = jnp.where(col_p == 0, sink, jnp.where(col_p < WINDOW, NEG_INF, table))
    col_s = lax.broadcasted_iota(jnp.int32, (DEC_SEQ, SAMPLE_KEYS_PADDED), 1)
    os_ref[0] = jnp.where(col_s == 0, sink, build(bs_ref[...]))


def _bias_tables(rel_bias, attn_sinks):
    bp, bs = _bucket_tables()
    n_layers = attn_sinks.shape[0]
    return pl.pallas_call(
        _bias_kernel,
        out_shape=(jax.ShapeDtypeStruct((n_layers, 2, N_HEADS, WINDOW, 2 * WINDOW), F32),
                   jax.ShapeDtypeStruct((n_layers, N_HEADS * DEC_SEQ, SAMPLE_KEYS_PADDED), F32)),
        grid=(n_layers, N_HEADS),
        in_specs=[pl.BlockSpec(memory_space=pltpu.SMEM), pl.BlockSpec(memory_space=pltpu.SMEM),
                  pl.BlockSpec((WINDOW, 2 * WINDOW), lambda l, j: (0, 0)),
                  pl.BlockSpec((DEC_SEQ, SAMPLE_KEYS_PADDED), lambda l, j: (j, 0))],
        out_specs=(pl.BlockSpec((1, 2, 1, WINDOW, 2 * WINDOW), lambda l, j: (l, 0, j, 0, 0)),
                   pl.BlockSpec((1, DEC_SEQ, SAMPLE_KEYS_PADDED), lambda l, j: (l, j, 0))),
        compiler_params=_params(MIN_SCOPED_VMEM_BYTES, 2),
        name="bias_tables",
    )(rel_bias, attn_sinks, jnp.asarray(bp), jnp.asarray(bs))


def _memkv_kernel(m_ref, g_ref, w_ref, k_ref, v_ref, kb_ref, ve_ref):
    hn = _rms(m_ref[...], g_ref[...]).astype(BF16)
    kv = _dot(hn, w_ref[...].astype(BF16))
    k = kv[:, :MEM_DIM]
    v = kv[:, MEM_DIM:]
    k_ref[...] = k
    v_ref[...] = v
    kb_ref[...] = k.astype(BF16)
    ones = jnp.ones((MEM_TOKENS, MEM_HEAD_DIM), BF16)
    for h in range(MEM_HEADS):
        vh = v[:, h * MEM_HEAD_DIM:(h + 1) * MEM_HEAD_DIM].astype(BF16)
        ve_ref[h] = jnp.concatenate([vh, ones], axis=1)


def _memkv(mem_prompt, mem_norm, w_mem_kv):
    shp = (DEPTH, BATCH, MEM_TOKENS, MEM_DIM)
    blk = pl.BlockSpec((None, None, MEM_TOKENS, MEM_DIM), lambda l, b: (l, b, 0, 0))
    return pl.pallas_call(
        _memkv_kernel,
        out_shape=(jax.ShapeDtypeStruct(shp, F32), jax.ShapeDtypeStruct(shp, F32),
                   jax.ShapeDtypeStruct(shp, BF16),
                   jax.ShapeDtypeStruct((DEPTH, BATCH, MEM_HEADS, MEM_TOKENS, 2 * MEM_HEAD_DIM), BF16)),
        grid=(DEPTH, BATCH),
        in_specs=[pl.BlockSpec((None, MEM_TOKENS, D_MODEL), lambda l, b: (b, 0, 0)),
                  pl.BlockSpec((None, 1, D_MODEL), lambda l, b: (l, 0, 0)),
                  pl.BlockSpec((None, D_MODEL, 2 * MEM_DIM), lambda l, b: (l, 0, 0))],
        out_specs=(blk, blk, blk,
                   pl.BlockSpec((None, None, MEM_HEADS, MEM_TOKENS, 2 * MEM_HEAD_DIM),
                                lambda l, b: (l, b, 0, 0, 0))),
        compiler_params=_params(MIN_SCOPED_VMEM_BYTES, 2),
        name="memkv",
    )(mem_prompt, mem_norm.reshape(DEPTH, 1, D_MODEL), w_mem_kv)


def _ffn_kernel(*refs, n_x, attn_proj, shared_kv, layer):
    refs = list(refs)
    x_refs = [refs.pop(0) for _ in range(n_x)]
    g_ref = refs.pop(0)
    w_hbm = [refs.pop(0) for _ in range(3)]
    if attn_proj:
        gm_ref, win_ref = refs.pop(0), refs.pop(0)
    kv_in = [refs.pop(0) for _ in range(4)] if shared_kv else None
    x1_ref = refs.pop(0)
    if attn_proj:
        qa_ref, qm_ref = refs.pop(0), refs.pop(0)
    kv_out = [refs.pop(0) for _ in range(4)] if shared_kv else None
    act_ref = refs.pop(0)
    w = _FfnWeights(layer, w_hbm, refs)

    def body(load_weights):
        x = _pick_group(*x_refs) if n_x == 2 else x_refs[0][...]
        if shared_kv:
            _emit_shared_kv(x, *kv_in, *kv_out)
        x1 = _ffn_half_step(x, g_ref, w, act_ref, load_weights)
        x1_ref[...] = x1
        if attn_proj:
            hm = _rms(x1, gm_ref[...]).astype(BF16)
            qa_ref[...] = (_dot(hm, win_ref[:, :ATTN_DIM].astype(BF16)) * ATTN_Q_SCALE).astype(BF16)
            qm_ref[...] = (_dot(hm, win_ref[:, ATTN_DIM:].astype(BF16)) * MEM_Q_SCALE).astype(BF16)

    _first_step_loads(w, body)


def _ffn(x, g, ffn_w, layer, gm=None, win=None, win_layer=None, shared_kv_w=None):
    attn_proj = win is not None
    shared_kv = shared_kv_w is not None
    xs = list(x) if isinstance(x, tuple) else [x]
    in_specs = ((_group_specs(D_MODEL) if len(xs) == 2 else [_row_spec(D_MODEL)])
                + [_resident((1, D_MODEL))] + _FfnWeights.in_specs())
    args = xs + [g.reshape(1, D_MODEL)] + list(ffn_w)
    out_shape = [jax.ShapeDtypeStruct((ROWS, D_MODEL), F32)]
    out_specs = [_row_spec(D_MODEL)]
    resident = 0
    streamed = (1 + len(xs)) * _ROW_TILE_F32
    if attn_proj:
        in_specs += [_resident((1, D_MODEL)), _resident_layer((D_MODEL, ATTN_DIM + MEM_DIM), win_layer)]
        args += [gm.reshape(1, D_MODEL), win]
        out_shape += [jax.ShapeDtypeStruct((ROWS, ATTN_DIM), BF16), jax.ShapeDtypeStruct((ROWS, MEM_DIM), BF16)]
        out_specs += [_row_spec(ATTN_DIM), _row_spec(MEM_DIM)]
        resident += D_MODEL * (ATTN_DIM + MEM_DIM) * 4
        streamed += TILE_ROWS * (ATTN_DIM + MEM_DIM) * 2
    if shared_kv:
        kv_in_specs, kv_shape, kv_specs, kv_resident, kv_streamed = _shared_kv_specs()
        in_specs += kv_in_specs
        args += [shared_kv_w[0].reshape(1, D_MODEL)] + list(shared_kv_w[1:])
        out_shape += kv_shape
        out_specs += kv_specs
        resident += kv_resident
        streamed += kv_streamed
    return pl.pallas_call(
        functools.partial(_ffn_kernel, n_x=len(xs), attn_proj=attn_proj, shared_kv=shared_kv, layer=layer),
        out_shape=tuple(out_shape),
        grid=(ROW_TILES,),
        in_specs=in_specs,
        out_specs=tuple(out_specs),
        scratch_shapes=[pltpu.VMEM((TILE_ROWS, FFN_DIM), BF16)] + _FfnWeights.scratch_shapes(),
        compiler_params=_params(_vmem_limit(
            resident, streamed, _ACT_BYTES + _FfnWeights.SCRATCH_BYTES, 3 * _ROW_TILE_F32)),
        name="ffn_attn_proj" if attn_proj else "ffn",
    )(*args)


def _inproj_conv_kernel(x_ref, gm_ref, win_ref, cw_ref, pre_ref,
                        ytok_ref, qm_ref, tail_ref, us_ref, shift_ref):
    i = pl.program_id(0)
    x = x_ref[...]
    inv_rms = lax.rsqrt(jnp.mean(x * x, axis=-1, keepdims=True) + RMS_EPS)
    inv_rms_chunk = jnp.broadcast_to(inv_rms, (TILE_ROWS, CONV_CHUNK))
    inv_ms_chunk = inv_rms_chunk * inv_rms_chunk
    hm = (x * gm_ref[...]).astype(BF16)
    qm_ref[...] = (_dot(hm, win_ref[:, 3 * CONV_DIM:].astype(BF16)) * (inv_rms * MEM_Q_SCALE)).astype(BF16)

    def chunk(cc, prompt):
        sl = slice(cc * CONV_CHUNK, (cc + 1) * CONV_CHUNK)
        c_gate = _dot(hm, win_ref[:, CONV_DIM + cc * CONV_CHUNK:CONV_DIM + (cc + 1) * CONV_CHUNK].astype(BF16))
        x_in = _dot(hm, win_ref[:, 2 * CONV_DIM + cc * CONV_CHUNK:2 * CONV_DIM + (cc + 1) * CONV_CHUNK].astype(BF16))
        u = c_gate * x_in * inv_ms_chunk
        if prompt:
            shift_ref[V7X_SUBLANES:, sl] = u
            u1 = shift_ref[V7X_SUBLANES - 1:V7X_SUBLANES - 1 + TILE_ROWS, sl]
            u2 = shift_ref[V7X_SUBLANES - 2:V7X_SUBLANES - 2 + TILE_ROWS, sl]
            last = u[TILE_ROWS - V7X_SUBLANES:, :]
            shift_ref[:V7X_SUBLANES, sl] = last
            tail_ref[0, :, sl] = last
        else:
            t = lax.broadcasted_iota(jnp.int32, (TILE_ROWS, CONV_CHUNK), 0) % DEC_SEQ
            p2 = pre_ref[:, sl]
            p1 = pltpu.roll(p2, TILE_ROWS - 1, axis=0)
            u1 = jnp.where(t == 0, p1, pltpu.roll(u, 1, axis=0))
            u2 = jnp.where(t < 2, p2, pltpu.roll(u, 2, axis=0))
            us_ref[:, sl] = u
            tail_ref[0, :, sl] = jnp.zeros((V7X_SUBLANES, CONV_CHUNK), F32)
        w = cw_ref[:, sl]
        conv = w[0:1] * u2 + w[1:2] * u1 + w[2:3] * u
        b_gate = _dot(hm, win_ref[:, sl].astype(BF16))
        ytok_ref[:, sl] = (b_gate * inv_rms_chunk * conv).astype(BF16)

    @pl.when(i < PROMPT_TILES)
    def _():
        @pl.when(i % TILES_PER_SEQ == 0)
        def _():
            shift_ref[:V7X_SUBLANES, :] = jnp.zeros((V7X_SUBLANES, CONV_DIM), F32)
        for cc in range(CONV_DIM // CONV_CHUNK):
            chunk(cc, True)

    @pl.when(i >= PROMPT_TILES)
    def _():
        for cc in range(CONV_DIM // CONV_CHUNK):
            chunk(cc, False)


def _inproj_conv(x1, gm, win, conv_w, prefix_rows, layer):
    sample_idx = lambda i: (jnp.maximum(i - PROMPT_TILES, 0), 0)
    win_bytes = D_MODEL * (3 * CONV_DIM + MEM_DIM) * 4
    return pl.pallas_call(
        _inproj_conv_kernel,
        out_shape=(jax.ShapeDtypeStruct((ROWS, CONV_DIM), BF16),
                   jax.ShapeDtypeStruct((ROWS, MEM_DIM), BF16),
                   jax.ShapeDtypeStruct((ROW_TILES, V7X_SUBLANES, CONV_DIM), F32),
                   jax.ShapeDtypeStruct((ROWS_SAMPLE, CONV_DIM), F32)),
        grid=(ROW_TILES,),
        in_specs=[_row_spec(D_MODEL), _resident((1, D_MODEL)),
                  _resident_layer((D_MODEL, 3 * CONV_DIM + MEM_DIM), layer),
                  _resident_layer((CONV_WIDTH, CONV_DIM), layer),
                  pl.BlockSpec((TILE_ROWS, CONV_DIM), sample_idx)],
        out_specs=(_row_spec(CONV_DIM), _row_spec(MEM_DIM),
                   pl.BlockSpec((1, V7X_SUBLANES, CONV_DIM), lambda i: (i, 0, 0)),
                   pl.BlockSpec((TILE_ROWS, CONV_DIM), sample_idx)),
        scratch_shapes=[pltpu.VMEM((TILE_ROWS + V7X_SUBLANES, CONV_DIM), F32)],
        compiler_params=_params(_vmem_limit(
            win_bytes, 3 * _ROW_TILE_F32 + TILE_ROWS * (CONV_DIM + MEM_DIM) * 2,
            _ROW_TILE_F32 + V7X_SUBLANES * CONV_DIM * 4, 6 * _ROW_TILE_F32)),
        name="inproj_conv",
    )(x1, gm.reshape(1, D_MODEL), win, conv_w, prefix_rows)


VZ_WIDTH = N_KV_HEADS * 2 * 2 * V7X_LANES


def _emit_shared_kv(x, g_ref, wk_ref, wkt_ref, wv_ref, k_ref, v_ref, ktz_ref, vz_ref):
    hk = _rms(x, g_ref[...]).astype(BF16)
    k_ref[...] = _dot(hk, wk_ref[...].astype(BF16))
    v = _dot(hk, wv_ref[...].astype(BF16))
    v_ref[...] = v
    kt = _dot_nt(wkt_ref[...].astype(BF16), hk)
    lo = lax.broadcasted_iota(jnp.int32, (TILE_ROWS, V7X_LANES), 1) < HALF_LANES
    hi = jnp.logical_not(lo)
    ones = (jnp.where(lo, 1.0, 0.0).astype(BF16), jnp.where(lo, 0.0, 1.0).astype(BF16))
    zero_k = jnp.zeros((HEAD_DIM, TILE_ROWS), BF16)
    for pair in range(N_KV_HEADS // 2):
        v_pair = v[:, pair * V7X_LANES:(pair + 1) * V7X_LANES]
        v_swap = pltpu.roll(v_pair, HALF_LANES, axis=1)
        for odd in range(2):
            h = 2 * pair + odd
            kth = kt[h * HEAD_DIM:(h + 1) * HEAD_DIM, :].astype(BF16)
            ktz_ref[h, 0] = jnp.concatenate([kth, zero_k], axis=0)
            ktz_ref[h, 1] = jnp.concatenate([zero_k, kth], axis=0)
            halves = (jnp.where(lo, v_swap if odd else v_pair, 0.0), jnp.where(hi, v_pair if odd else v_swap, 0.0))
            for e in range(2):
                c0 = (h * 2 + e) * 2 * V7X_LANES
                vz_ref[:, c0:c0 + 2 * V7X_LANES] = jnp.concatenate(
                    [halves[e].astype(BF16), ones[e]], axis=1)


def _shared_kv_specs():
    in_specs = [_resident((1, D_MODEL)), _resident((D_MODEL, KV_DIM)), _resident((KV_DIM, D_MODEL)),
                _resident((D_MODEL, KV_DIM))]
    out_shape = [jax.ShapeDtypeStruct((ROWS, KV_DIM), F32), jax.ShapeDtypeStruct((ROWS, KV_DIM), F32),
                 jax.ShapeDtypeStruct((N_KV_HEADS, 2, V7X_LANES, ROWS), BF16),
                 jax.ShapeDtypeStruct((ROWS, VZ_WIDTH), BF16)]
    out_specs = [_row_spec(KV_DIM), _row_spec(KV_DIM),
                 pl.BlockSpec((N_KV_HEADS, 2, V7X_LANES, TILE_ROWS), lambda i: (0, 0, 0, i)),
                 _row_spec(VZ_WIDTH)]
    resident = 3 * D_MODEL * KV_DIM * 4
    streamed = TILE_ROWS * (2 * KV_DIM * 4 + 2 * KV_DIM * 2 + VZ_WIDTH * 2)
    return in_specs, out_shape, out_specs, resident, streamed


def _attn_prompt_kernel(*refs, swa):
    if swa:
        (qm_ref, mk_ref, mve_ref, qa_ref, ktp_ref, ktc_ref, vzp_ref, vzc_ref, bias_ref,
         ymem_ref, ytok_ref) = refs
    else:
        qm_ref, mk_ref, mve_ref, ymem_ref = refs

    for half in range(2):
        rows = slice(half * TILE_ROWS // 2, (half + 1) * TILE_ROWS // 2)
        for h in range(MEM_HEADS):
            sl = slice(h * MEM_HEAD_DIM, (h + 1) * MEM_HEAD_DIM)
            s = _dot_nt(qm_ref[rows, sl], mk_ref[:, sl])
            p = jnp.exp2(s - jnp.max(s, axis=-1, keepdims=True)).astype(BF16)
            oe = _dot(p, mve_ref[h])
            ymem_ref[rows, sl] = (oe[:, :MEM_HEAD_DIM] / oe[:, MEM_HEAD_DIM:]).astype(BF16)
    if not swa:
        return

    no_prev = ((pl.program_id(0) % TILES_PER_SEQ) == 0).astype(jnp.int32)
    key0_col = lax.broadcasted_iota(jnp.int32, (V7X_LANES, WINDOW), 1) == 0
    key0_row = lax.broadcasted_iota(jnp.int32, (WINDOW, V7X_LANES), 0) == 0
    zero_kt = jnp.zeros((V7X_LANES, WINDOW), BF16)
    zero_v = jnp.zeros((WINDOW, V7X_LANES), BF16)

    for n in range(BLOCKS_PER_TILE):
        rows = slice(n * WINDOW, (n + 1) * WINDOW)
        for h in range(N_KV_HEADS):
            k_sel, v_sel = [], []
            for e in range(2):
                k_prev = ktp_ref[h, e] if n == 0 else ktc_ref[h, e, :, (n - 1) * WINDOW:n * WINDOW]
                k_prev = jnp.where(key0_col, zero_kt, k_prev)
                k_sel.append(jnp.concatenate([k_prev, ktc_ref[h, e, :, rows]], axis=1))
                c0 = (h * 2 + e) * 2 * V7X_LANES
                cols = slice(c0, c0 + 2 * V7X_LANES)
                v_prev = vzp_ref[:, cols] if n == 0 else vzc_ref[(n - 1) * WINDOW:n * WINDOW, cols]
                v_prev = jnp.concatenate(
                    [jnp.where(key0_row, zero_v, v_prev[:, :V7X_LANES]), v_prev[:, V7X_LANES:]], axis=1)
                v_sel.append(jnp.concatenate([v_prev, vzc_ref[rows, cols]], axis=0))
            for pr in range(GROUP // 2):
                c0 = (h * (GROUP // 2) + pr) * V7X_LANES
                qp = qa_ref[rows, c0:c0 + V7X_LANES]
                acc = None
                for e in range(2):
                    j = h * GROUP + pr * 2 + e
                    bias = bias_ref[no_prev, j] if n == 0 else bias_ref[0, j]
                    s = _dot(qp, k_sel[e]) + bias
                    p = jnp.exp2(s - jnp.max(s, axis=-1, keepdims=True)).astype(BF16)
                    part = _dot(p, v_sel[e])
                    acc = part if acc is None else acc + part
                ytok_ref[rows, c0:c0 + V7X_LANES] = (acc[:, :V7X_LANES] / acc[:, V7X_LANES:]).astype(BF16)


def _attn_prompt(qm, mk_bf, mv_ext, layer, swa_args=None):
    swa = swa_args is not None
    batch_of = lambda i: i // TILES_PER_SEQ
    in_specs = [_row_spec(MEM_DIM),
                pl.BlockSpec((None, None, MEM_TOKENS, MEM_DIM), lambda i: (layer, batch_of(i), 0, 0)),
                pl.BlockSpec((None, None, MEM_HEADS, MEM_TOKENS, 2 * MEM_HEAD_DIM),
                             lambda i: (layer, batch_of(i), 0, 0, 0))]
    args = [qm, mk_bf, mv_ext]
    out_shape = [jax.ShapeDtypeStruct((ROWS_PROMPT, MEM_DIM), BF16)]
    out_specs = [_row_spec(MEM_DIM)]
    streamed = TILE_ROWS * MEM_DIM * 4 + MEM_TOKENS * MEM_DIM * 2 * 3
    resident = 0
    if swa:
        qa, ktz, vz, bias_p, swa_layer = swa_args
        prev_blk = lambda i: jnp.maximum(i * BLOCKS_PER_TILE - 1, 0)
        in_specs += [_row_spec(ATTN_DIM),
                     pl.BlockSpec((N_KV_HEADS, 2, V7X_LANES, WINDOW), lambda i: (0, 0, 0, prev_blk(i))),
                     pl.BlockSpec((N_KV_HEADS, 2, V7X_LANES, TILE_ROWS), lambda i: (0, 0, 0, i)),
                     pl.BlockSpec((WINDOW, VZ_WIDTH), lambda i: (prev_blk(i), 0)),
                     pl.BlockSpec((TILE_ROWS, VZ_WIDTH), lambda i: (i, 0)),
                     _resident_layer((2, N_HEADS, WINDOW, 2 * WINDOW), swa_layer)]
        args += [qa, ktz, ktz, vz, vz, bias_p]
        out_shape += [jax.ShapeDtypeStruct((ROWS_PROMPT, ATTN_DIM), BF16)]
        out_specs += [_row_spec(ATTN_DIM)]
        resident = 2 * N_HEADS * WINDOW * 2 * WINDOW * 4
        streamed += 2 * TILE_ROWS * ATTN_DIM * 2 + (TILE_ROWS + WINDOW) * (2 * KV_DIM + VZ_WIDTH) * 2
    return pl.pallas_call(
        functools.partial(_attn_prompt_kernel, swa=swa),
        out_shape=tuple(out_shape),
        grid=(PROMPT_TILES,),
        in_specs=in_specs,
        out_specs=tuple(out_specs),
        compiler_params=_params(_vmem_limit(resident, streamed, 0, 8 * _ROW_TILE_F32)),
        name="attn_prompt_swa" if swa else "attn_prompt_mem",
    )(*args)


def _attn_sample_kernel(*refs, swa, emit_cache):
    if swa:
        (qm_ref, mk_ref, mv_ref, qa_ref, ck_ref, cv_ref, kn_ref, vn_ref, bias_ref,
         ymem_ref, ytok_ref) = refs[:11]
        if emit_cache:
            kc_out_ref, vc_out_ref = refs[11:]
            keep_cached = lax.broadcasted_iota(jnp.int32, (KV_DIM, WINDOW), 1) < WINDOW - DEC_SEQ
    else:
        qm_ref, mk_ref, mv_ref, ymem_ref = refs

    qm_all = qm_ref[...].astype(F32)
    mem_rows = MEM_HEADS * DEC_SEQ
    own_head = (lax.broadcasted_iota(jnp.int32, (mem_rows, MEM_TOKENS * MEM_HEADS), 1) % MEM_HEADS
                == lax.broadcasted_iota(jnp.int32, (mem_rows, MEM_TOKENS * MEM_HEADS), 0) // DEC_SEQ)
    if swa:
        qa_all = qa_ref[...].astype(F32)
        kn_all = kn_ref[...]
        vn_all = vn_ref[...]
        bias = bias_ref[...]
        key0 = lax.broadcasted_iota(jnp.int32, (KV_DIM, WINDOW), 1) == 0
        pad = jnp.zeros((WINDOW - DEC_SEQ, KV_DIM), F32)
        lo = lax.broadcasted_iota(jnp.int32, (DEC_SEQ, V7X_LANES), 1) < HALF_LANES
        hi = jnp.logical_not(lo)
        zero_slab = jnp.zeros((DEC_SEQ, V7X_LANES), F32)

    ymem_rows, ytok_rows = [], []
    for b in range(SAMPLE_BATCH_BLOCK):
        rows = slice(b * DEC_SEQ, (b + 1) * DEC_SEQ)
        qm = qm_all[rows]
        q_heads = jnp.concatenate(
            [qm[:, h * MEM_HEAD_DIM:(h + 1) * MEM_HEAD_DIM] for h in range(MEM_HEADS)], axis=0)
        s = jnp.where(own_head, _dot_nt(q_heads, mk_ref[b]), NEG_INF)
        p = jnp.exp2(s - jnp.max(s, axis=-1, keepdims=True))
        o = _dot(p, mv_ref[b]) / jnp.sum(p, axis=-1, keepdims=True)
        ymem_rows.append(jnp.concatenate(
            [o[h * DEC_SEQ:(h + 1) * DEC_SEQ] for h in range(MEM_HEADS)], axis=1))
        if not swa:
            continue
        qb = qa_all[rows]
        groups = []
        for j in range(N_HEADS):
            kvh = j // GROUP
            slab = qb[:, (j // 2) * V7X_LANES:(j // 2 + 1) * V7X_LANES]
            if j % 2 != kvh % 2:
                slab = pltpu.roll(slab, HALF_LANES, axis=1)
            slab = jnp.where(lo if kvh % 2 == 0 else hi, slab, 0.0)
            groups.append(jnp.concatenate([slab, zero_slab] if kvh // 2 == 0 else [zero_slab, slab], axis=1))
        q_bd = jnp.concatenate(groups, axis=0)
        kt_cache = jnp.where(key0, 0.0, ck_ref[b])
        vt_cache = jnp.where(key0, 0.0, cv_ref[b])
        k_new = jnp.concatenate([kn_all[rows], pad], axis=0)
        v_new = jnp.concatenate([vn_all[rows], pad], axis=0)
        if emit_cache:
            shift = WINDOW - DEC_SEQ
            kc_out_ref[b] = jnp.where(keep_cached, pltpu.roll(ck_ref[b], shift, axis=1),
                                      pltpu.roll(k_new.T, shift, axis=1))
            vc_out_ref[b] = jnp.where(keep_cached, pltpu.roll(cv_ref[b], shift, axis=1),
                                      pltpu.roll(v_new.T, shift, axis=1))
        s = jnp.concatenate([_dot(q_bd, kt_cache), _dot_nt(q_bd, k_new)], axis=1) + bias
        p = jnp.exp2(s - jnp.max(s, axis=-1, keepdims=True))
        o_full = ((_dot_nt(p[:, :WINDOW], vt_cache) + _dot(p[:, WINDOW:], v_new))
                  / jnp.sum(p, axis=-1, keepdims=True))
        pairs = []
        for pair in range(N_HEADS // 2):
            acc = None
            for e in range(2):
                j = pair * 2 + e
                kvh = j // GROUP
                slab = o_full[j * DEC_SEQ:(j + 1) * DEC_SEQ, (kvh // 2) * V7X_LANES:(kvh // 2 + 1) * V7X_LANES]
                if e != kvh % 2:
                    slab = pltpu.roll(slab, HALF_LANES, axis=1)
                slab = jnp.where(lo if e == 0 else hi, slab, 0.0)
                acc = slab if acc is None else acc + slab
            pairs.append(acc)
        ytok_rows.append(jnp.concatenate(pairs, axis=1))

    ymem_ref[...] = jnp.concatenate(ymem_rows, axis=0).astype(BF16)
    if swa:
        ytok_ref[...] = jnp.concatenate(ytok_rows, axis=0).astype(BF16)


def _attn_sample(qm, cache_k, cache_v, layer, swa_args=None, emit_cache=False):
    swa = swa_args is not None
    assert swa or not emit_cache
    row0 = ROWS_PROMPT // SAMPLE_BLOCK_ROWS
    blk_rows = lambda width: pl.BlockSpec((SAMPLE_BLOCK_ROWS, width), lambda i: (row0 + i, 0))
    out_rows = lambda width: pl.BlockSpec((SAMPLE_BLOCK_ROWS, width), lambda i: (i, 0))
    cache_spec = pl.BlockSpec((None, SAMPLE_BATCH_BLOCK, MEM_TOKENS * MEM_HEADS, MEM_HEAD_DIM),
                              lambda i: (layer, i, 0, 0))
    in_specs = [blk_rows(MEM_DIM), cache_spec, cache_spec]
    args = [qm, cache_k, cache_v]
    out_shape = [jax.ShapeDtypeStruct((ROWS_SAMPLE, MEM_DIM), BF16)]
    out_specs = [out_rows(MEM_DIM)]
    streamed = 2 * SAMPLE_BATCH_BLOCK * MEM_TOKENS * MEM_DIM * 4 + SAMPLE_BLOCK_ROWS * MEM_DIM * 4
    if swa:
        qa, swa_k, swa_v, k_new, v_new, bias_s, swa_layer = swa_args
        swa_spec = pl.BlockSpec((SAMPLE_BATCH_BLOCK, KV_DIM, WINDOW), lambda i: (i, 0, 0))
        in_specs += [blk_rows(ATTN_DIM), swa_spec, swa_spec, out_rows(KV_DIM), out_rows(KV_DIM),
                     _resident_layer((N_HEADS * DEC_SEQ, SAMPLE_KEYS_PADDED), swa_layer)]
        args += [qa, swa_k, swa_v, k_new, v_new, bias_s]
        out_shape += [jax.ShapeDtypeStruct((ROWS_SAMPLE, ATTN_DIM), BF16)]
        out_specs += [out_rows(ATTN_DIM)]
        streamed += 2 * SAMPLE_BATCH_BLOCK * WINDOW * KV_DIM * 4 + SAMPLE_BLOCK_ROWS * (ATTN_DIM + KV_DIM) * 4
        if emit_cache:
            out_shape += [jax.ShapeDtypeStruct((DEC_BATCH, KV_DIM, WINDOW), F32)] * 2
            out_specs += [swa_spec, swa_spec]
            streamed += 2 * SAMPLE_BATCH_BLOCK * WINDOW * KV_DIM * 4
    return pl.pallas_call(
        functools.partial(_attn_sample_kernel, swa=swa, emit_cache=emit_cache),
        out_shape=tuple(out_shape),
        grid=(DEC_BATCH // SAMPLE_BATCH_BLOCK,),
        in_specs=in_specs,
        out_specs=tuple(out_specs),
        compiler_params=_params(_vmem_limit(1 << 20, streamed, 0, 6 * _ROW_TILE_F32)),
        name="attn_sample_swa" if swa else "attn_sample_mem",
    )(*args)


def _outffn_kernel(*refs, split_tok, final, layer):
    refs = list(refs)
    x1_ref = refs.pop(0)
    tok_refs = [refs.pop(0) for _ in range(2 if split_tok else 1)]
    ymp_ref, yms_ref, wo_ref, g_ref = [refs.pop(0) for _ in range(4)]
    w_hbm = [refs.pop(0) for _ in range(3)]
    gf_ref = refs.pop(0) if final else None
    out_refs = [refs.pop(0) for _ in range(2 if final else 1)]
    act_ref = refs.pop(0)
    w = _FfnWeights(layer, w_hbm, refs)
    tok_dim = wo_ref.shape[0] - MEM_DIM

    def body(load_weights):
        y_tok = _pick_group(*tok_refs) if split_tok else tok_refs[0][...]
        y_mem = _pick_group(ymp_ref, yms_ref)
        x2 = (x1_ref[...] + _dot(y_tok, wo_ref[:tok_dim, :].astype(BF16))
              + _dot(y_mem, wo_ref[tok_dim:, :].astype(BF16)))
        x3 = _ffn_half_step(x2, g_ref, w, act_ref, load_weights)
        if not final:
            out_refs[0][...] = x3
            return
        y = _rms(x3, gf_ref[...])

        @pl.when(pl.program_id(0) < PROMPT_TILES)
        def _():
            out_refs[0][...] = y

        @pl.when(pl.program_id(0) >= PROMPT_TILES)
        def _():
            out_refs[1][...] = y

    _first_step_loads(w, body)


def _outffn(x1, y_tok, ymem_p, ymem_s, wo, wo_layer, g, ffn_w, layer, final_gain=None):
    split_tok = isinstance(y_tok, tuple)
    final = final_gain is not None
    tok_dim = wo.shape[1] - MEM_DIM
    in_specs = [_row_spec(D_MODEL)]
    args = [x1]
    if split_tok:
        in_specs += _group_specs(tok_dim)
        args += list(y_tok)
    else:
        in_specs += [_row_spec(tok_dim)]
        args += [y_tok]
    in_specs += (_group_specs(MEM_DIM) + [_resident_layer((tok_dim + MEM_DIM, D_MODEL), wo_layer)]
                 + [_resident((1, D_MODEL))] + _FfnWeights.in_specs())
    args += [ymem_p, ymem_s, wo, g.reshape(1, D_MODEL)] + list(ffn_w)
    if final:
        in_specs += [_resident((1, D_MODEL))]
        args += [final_gain.reshape(1, D_MODEL)]
    resident = (tok_dim + MEM_DIM) * D_MODEL * 4
    streamed = 2 * _ROW_TILE_F32 + 2 * TILE_ROWS * (tok_dim + MEM_DIM) * 2
    if final:
        out_shape = [jax.ShapeDtypeStruct((ROWS_PROMPT, D_MODEL), F32),
                     jax.ShapeDtypeStruct((ROWS_SAMPLE, D_MODEL), F32)]
        out_specs = _group_specs(D_MODEL)
        streamed += _ROW_TILE_F32
    else:
        out_shape = [jax.ShapeDtypeStruct((ROWS, D_MODEL), F32)]
        out_specs = [_row_spec(D_MODEL)]
    return pl.pallas_call(
        functools.partial(_outffn_kernel, split_tok=split_tok, final=final, layer=layer),
        out_shape=tuple(out_shape),
        grid=(ROW_TILES,),
        in_specs=in_specs,
        out_specs=tuple(out_specs),
        scratch_shapes=[pltpu.VMEM((TILE_ROWS, FFN_DIM), BF16)] + _FfnWeights.scratch_shapes(),
        compiler_params=_params(_vmem_limit(
            resident, streamed, _ACT_BYTES + _FfnWeights.SCRATCH_BYTES, 3 * _ROW_TILE_F32)),
        name="outffn_final" if final else "outffn",
    )(*args)


def kernel(x_prompt, x_sample, state_conv, cache_swa_k, cache_swa_v, cache_mem_k, cache_mem_v, mem_prompt, ffn1_norm, ffn1_wg, ffn1_wu, ffn1_wd, mix_norm, w_in_a, conv_w, w_out_a, kv_norm, w_kv, w_in_b, attn_sinks, rel_bias, w_out_b, mem_norm, w_mem_kv, ffn2_norm, ffn2_wg, ffn2_wu, ffn2_wd, final_norm):
    ffn1 = (ffn1_wg, ffn1_wu, ffn1_wd)
    ffn2 = (ffn2_wg, ffn2_wu, ffn2_wd)
    wk = w_kv[:, :KV_DIM]
    wv = w_kv[:, KV_DIM:]

    mem_k, mem_v, mem_k_bf, mem_v_ext = _memkv(mem_prompt, mem_norm, w_mem_kv)
    bias_p, bias_s = _bias_tables(rel_bias, attn_sinks)
    swa_k_cache = cache_swa_k.transpose(0, 2, 3, 1).reshape(DEC_BATCH, KV_DIM, WINDOW)
    swa_v_cache = cache_swa_v.transpose(0, 2, 3, 1).reshape(DEC_BATCH, KV_DIM, WINDOW)
    mem_rows_shape = (DEPTH, DEC_BATCH, MEM_TOKENS * MEM_HEADS, MEM_HEAD_DIM)
    cache_k = cache_mem_k.reshape(mem_rows_shape)
    cache_v = cache_mem_v.reshape(mem_rows_shape)

    x = (x_prompt.reshape(ROWS_PROMPT, D_MODEL), x_sample.reshape(ROWS_SAMPLE, D_MODEL))
    tails, sample_us = [], []
    k_rows = v_rows = ktz = vz = k_new = v_new = None
    for l in range(DEPTH):
        last = l == DEPTH - 1
        if l < N_A_LAYERS:
            (x1,) = _ffn(x, ffn1_norm[l], ffn1, l)
            prefix_rows = jnp.pad(state_conv[l], ((0, 0), (0, DEC_SEQ - (CONV_WIDTH - 1)), (0, 0)))
            y_tok, qm, tail, us = _inproj_conv(x1, mix_norm[l], w_in_a, conv_w,
                                               prefix_rows.reshape(ROWS_SAMPLE, CONV_DIM), l)
            tails.append(tail)
            sample_us.append(us)
            (ymem_p,) = _attn_prompt(qm, mem_k_bf, mem_v_ext, l)
            (ymem_s,) = _attn_sample(qm, cache_k, cache_v, l)
            wo, wo_layer = w_out_a, l
        else:
            j = l - N_A_LAYERS
            if j == 0:
                x1, qa, qm, k_rows, v_rows, ktz, vz = _ffn(
                    x, ffn1_norm[l], ffn1, l, mix_norm[l], w_in_b, j,
                    shared_kv_w=(kv_norm, wk, wk.T, wv))
                k_new = k_rows[ROWS_PROMPT:]
                v_new = v_rows[ROWS_PROMPT:]
            else:
                x1, qa, qm = _ffn(x, ffn1_norm[l], ffn1, l, mix_norm[l], w_in_b, j)
            ymem_p, ytok_p = _attn_prompt(qm, mem_k_bf, mem_v_ext, l, (qa, ktz, vz, bias_p, j))
            sample_out = _attn_sample(qm, cache_k, cache_v, l,
                                      (qa, swa_k_cache, swa_v_cache, k_new, v_new, bias_s, j), emit_cache=j == 0)
            ymem_s, ytok_s = sample_out[:2]
            if j == 0:
                new_k_cache, new_v_cache = sample_out[2:]
            y_tok = (ytok_p, ytok_s)
            wo, wo_layer = w_out_b, j
        if not last:
            (x,) = _outffn(x1, y_tok, ymem_p, ymem_s, wo, wo_layer, ffn2_norm[l], ffn2, l)
        else:
            y_prompt, y_sample = _outffn(x1, y_tok, ymem_p, ymem_s, wo, wo_layer, ffn2_norm[l], ffn2, l,
                                         final_gain=final_norm)

    keep = CONV_WIDTH - 1
    last_tiles = np.arange(BATCH) * TILES_PER_SEQ + TILES_PER_SEQ - 1
    conv_state_prompt = jnp.stack([t[last_tiles, V7X_SUBLANES - keep:, :] for t in tails])
    conv_state_sample = jnp.stack([u.reshape(DEC_BATCH, DEC_SEQ, CONV_DIM)[:, DEC_SEQ - keep:, :] for u in sample_us])
    k_tail = jnp.stack([k_rows[(b + 1) * SEQ - WINDOW:(b + 1) * SEQ] for b in range(BATCH)])
    v_tail = jnp.stack([v_rows[(b + 1) * SEQ - WINDOW:(b + 1) * SEQ] for b in range(BATCH)])
    swa_k_prompt = k_tail.reshape(BATCH, WINDOW, N_KV_HEADS, HEAD_DIM)
    swa_v_prompt = v_tail.reshape(BATCH, WINDOW, N_KV_HEADS, HEAD_DIM)
    to_cache_layout = lambda c: c.reshape(DEC_BATCH, N_KV_HEADS, HEAD_DIM, WINDOW).transpose(0, 3, 1, 2)
    swa_k_sample = to_cache_layout(new_k_cache)
    swa_v_sample = to_cache_layout(new_v_cache)
    mem_shape = (DEPTH, BATCH, MEM_TOKENS, MEM_HEADS, MEM_HEAD_DIM)
    return (y_prompt.reshape(BATCH, SEQ, D_MODEL), y_sample.reshape(DEC_BATCH, DEC_SEQ, D_MODEL),
            conv_state_prompt, conv_state_sample,
            swa_k_prompt, swa_v_prompt, swa_k_sample, swa_v_sample,
            mem_k.reshape(mem_shape), mem_v.reshape(mem_shape))
```

```python
import functools
import math

import numpy as np
import jax
import jax.numpy as jnp
from jax import lax
from jax.experimental import pallas as pl
from jax.experimental.pallas import tpu as pltpu

D_MODEL = 1024
BATCH = 2
SEQ = 8192
DEPTH = 4
DEC_BATCH = 128
DEC_SEQ = 8
N_A_LAYERS = DEPTH // 2
FFN_DIM = 2816
CONV_DIM = D_MODEL
CONV_WIDTH = 3
N_HEADS = 16
N_KV_HEADS = 4
HEAD_DIM = 64
GROUP = N_HEADS // N_KV_HEADS
ATTN_DIM = N_HEADS * HEAD_DIM
KV_DIM = N_KV_HEADS * HEAD_DIM
WINDOW = 128
REL_BUCKETS = 32
REL_MAX_DIST = 128
MEM_TOKENS = 256
MEM_HEADS = 4
MEM_HEAD_DIM = 128
MEM_DIM = MEM_HEADS * MEM_HEAD_DIM
RMS_EPS = 1e-5

F32 = jnp.float32
BF16 = jnp.bfloat16
NEG_INF = float("-inf")

V7X_LANES = 128
V7X_SUBLANES = 8
V7X_MXU_DIM = 256
V7X_VMEM_BYTES = 64 * 1024 * 1024

ROWS_PROMPT = BATCH * SEQ
ROWS_SAMPLE = DEC_BATCH * DEC_SEQ
ROWS = ROWS_PROMPT + ROWS_SAMPLE
TILE_ROWS = 512
PROMPT_TILES = ROWS_PROMPT // TILE_ROWS
SAMPLE_TILES = ROWS_SAMPLE // TILE_ROWS
ROW_TILES = PROMPT_TILES + SAMPLE_TILES
TILES_PER_SEQ = SEQ // TILE_ROWS
FFN_CHUNK = V7X_MXU_DIM
CONV_CHUNK = V7X_MXU_DIM
BLOCKS_PER_TILE = TILE_ROWS // WINDOW
SAMPLE_BATCH_BLOCK = 8
SAMPLE_BLOCK_ROWS = SAMPLE_BATCH_BLOCK * DEC_SEQ
SAMPLE_KEYS = WINDOW + DEC_SEQ
SAMPLE_KEYS_PADDED = 2 * WINDOW
HALF_LANES = V7X_LANES // 2
LOG2E = math.log2(math.e)
MEM_Q_SCALE = MEM_HEAD_DIM ** -0.5 * LOG2E
ATTN_Q_SCALE = HEAD_DIM ** -0.5 * LOG2E

assert HEAD_DIM == HALF_LANES and MEM_HEAD_DIM == V7X_LANES
assert ROWS_PROMPT % TILE_ROWS == 0 and ROWS_SAMPLE % TILE_ROWS == 0 and SEQ % TILE_ROWS == 0
assert FFN_DIM % FFN_CHUNK == 0 and TILE_ROWS % WINDOW == 0


MIN_SCOPED_VMEM_BYTES = 48 * 1024 * 1024


def _vmem_limit(resident_bytes, streamed_bytes, scratch_bytes, temp_bytes):
    need = resident_bytes + 2 * streamed_bytes + scratch_bytes + temp_bytes
    assert need < V7X_VMEM_BYTES, need
    return max(int(need), MIN_SCOPED_VMEM_BYTES)


def _params(vmem_bytes, n_axes=1):
    return pltpu.CompilerParams(
        dimension_semantics=("arbitrary",) * n_axes, vmem_limit_bytes=vmem_bytes)


def _resident(shape):
    zeros = (0,) * len(shape)
    return pl.BlockSpec(shape, lambda *_: zeros, pipeline_mode=pl.Buffered(1))


def _resident_layer(shape, layer):
    idx = (layer,) + (0,) * len(shape)
    return pl.BlockSpec((None,) + tuple(shape), lambda *_: idx, pipeline_mode=pl.Buffered(1))


def _row_spec(width):
    return pl.BlockSpec((TILE_ROWS, width), lambda i: (i, 0))


def _group_specs(width):
    return [pl.BlockSpec((TILE_ROWS, width), lambda i: (jnp.minimum(i, PROMPT_TILES - 1), 0)),
            pl.BlockSpec((TILE_ROWS, width), lambda i: (jnp.maximum(i - PROMPT_TILES, 0), 0))]


def _pick_group(prompt_ref, sample_ref):
    return jnp.where(pl.program_id(0) < PROMPT_TILES, prompt_ref[...], sample_ref[...])


def _dot(a, b):
    return jnp.dot(a, b, preferred_element_type=F32)


def _dot_nt(a, b):
    return lax.dot_general(a, b, (((1,), (1,)), ((), ())), preferred_element_type=F32)


def _rms(x, g):
    return x * lax.rsqrt(jnp.mean(x * x, axis=-1, keepdims=True) + RMS_EPS) * g


FFN_CHUNKS = FFN_DIM // FFN_CHUNK
FFN_STAGE_SLOTS = 2


class _FfnWeights:
    def __init__(self, layer, hbm_refs, scratch_refs):
        self.layer = layer
        self.wg_hbm, self.wu_hbm, self.wd_hbm = hbm_refs
        self.wg, self.wu, self.wd, self.stage_in, self.stage_out, self.sems = scratch_refs

    @staticmethod
    def in_specs():
        return [pl.BlockSpec(memory_space=pl.ANY)] * 3

    @staticmethod
    def scratch_shapes():
        return [pltpu.VMEM((D_MODEL, FFN_DIM), BF16), pltpu.VMEM((D_MODEL, FFN_DIM), BF16),
                pltpu.VMEM((FFN_DIM, D_MODEL), BF16),
                pltpu.VMEM((2, FFN_STAGE_SLOTS, D_MODEL, FFN_CHUNK), F32),
                pltpu.VMEM((FFN_STAGE_SLOTS, FFN_CHUNK, D_MODEL), F32),
                pltpu.SemaphoreType.DMA((3, FFN_STAGE_SLOTS))]

    SCRATCH_BYTES = 3 * D_MODEL * FFN_DIM * 2 + 3 * FFN_STAGE_SLOTS * D_MODEL * FFN_CHUNK * 4

    def _copy(self, stream, c):
        slot = c % FFN_STAGE_SLOTS
        cols = pl.ds(c * FFN_CHUNK, FFN_CHUNK)
        if stream == 0:
            src, dst = self.wg_hbm.at[self.layer, :, cols], self.stage_in.at[0, slot]
        elif stream == 1:
            src, dst = self.wu_hbm.at[self.layer, :, cols], self.stage_in.at[1, slot]
        else:
            src, dst = self.wd_hbm.at[self.layer, cols, :], self.stage_out.at[slot]
        return pltpu.make_async_copy(src, dst, self.sems.at[stream, slot])

    def prime(self):
        for stream in range(3):
            for c in range(FFN_STAGE_SLOTS):
                self._copy(stream, c).start()

    def fetch(self, c):
        slot = c % FFN_STAGE_SLOTS
        sl = slice(c * FFN_CHUNK, (c + 1) * FFN_CHUNK)
        for stream in range(3):
            self._copy(stream, c).wait()
            if stream == 0:
                self.wg[:, sl] = self.stage_in[0, slot].astype(BF16)
            elif stream == 1:
                self.wu[:, sl] = self.stage_in[1, slot].astype(BF16)
            else:
                self.wd[sl, :] = self.stage_out[slot].astype(BF16)
            if c + FFN_STAGE_SLOTS < FFN_CHUNKS:
                self._copy(stream, c + FFN_STAGE_SLOTS).start()


def _ffn_half_step(x, g_ref, w, act_ref, load_weights):
    inv_rms = lax.rsqrt(jnp.mean(x * x, axis=-1, keepdims=True) + RMS_EPS)
    inv_rms_chunk = jnp.broadcast_to(inv_rms, (x.shape[0], FFN_CHUNK))
    h = (x * g_ref[...]).astype(BF16)
    for c in range(FFN_CHUNKS):
        if load_weights:
            w.fetch(c)
        sl = slice(c * FFN_CHUNK, (c + 1) * FFN_CHUNK)
        gate = _dot(h, w.wg[:, sl]) * inv_rms_chunk
        up = _dot(h, w.wu[:, sl])
        act_ref[:, sl] = (gate / (1.0 + jnp.exp(-gate)) * up).astype(BF16)
    return x + (0.5 * inv_rms) * _dot(act_ref[...], w.wd[...])


def _first_step_loads(w, body):
    @pl.when(pl.program_id(0) == 0)
    def _():
        w.prime()
        body(True)

    @pl.when(pl.program_id(0) > 0)
    def _():
        body(False)


_ROW_TILE_F32 = TILE_ROWS * D_MODEL * 4
_ACT_BYTES = TILE_ROWS * FFN_DIM * 2


def _t5_bucket_np(dist):
    n = np.maximum(dist, 0)
    exact = REL_BUCKETS // 2
    nf = np.maximum(n, 1).astype(np.float32)
    large = exact + (np.log(nf / np.float32(exact)) / np.float32(math.log(REL_MAX_DIST / exact))
                     * np.float32(REL_BUCKETS - exact)).astype(np.int32)
    large = np.minimum(large, REL_BUCKETS - 1)
    return np.where(n < exact, n, large).astype(np.int32)


def _bucket_tables():
    q = np.arange(WINDOW)[:, None]
    k = np.arange(2 * WINDOW)[None, :]
    dist = WINDOW + q - k
    prompt = np.where((dist >= 0) & (dist < WINDOW), _t5_bucket_np(dist), -1)
    t = (np.arange(N_HEADS * DEC_SEQ) % DEC_SEQ)[:, None]
    k = np.arange(SAMPLE_KEYS_PADDED)[None, :]
    dist = WINDOW + t - k
    ok = (dist >= 0) & (dist < WINDOW) & (k < SAMPLE_KEYS)
    sample = np.where(ok, _t5_bucket_np(dist), -1)
    return prompt.astype(np.int32), sample.astype(np.int32)


def _bias_kernel(rel_ref, sink_ref, bp_ref, bs_ref, op_ref, os_ref):
    layer = pl.program_id(0)
    j = pl.program_id(1)
    sink = sink_ref[layer, j] * LOG2E

    def build(bucket):
        acc = jnp.zeros(bucket.shape, F32)
        for b in range(REL_BUCKETS):
            acc = jnp.where(bucket == b, rel_ref[b, j], acc)
        return jnp.where(bucket < 0, NEG_INF, acc * LOG2E)

    col_p = lax.broadcasted_iota(jnp.int32, (WINDOW, 2 * WINDOW), 1)
    table = build(bp_ref[...])
    op_ref[0, 0, 0] = jnp.where(col_p == 0, sink, table)
    op_ref[0, 1, 0] = jnp.where(col_p == 0, sink, jnp.where(col_p < WINDOW, NEG_INF, table))
    col_s = lax.broadcasted_iota(jnp.int32, (DEC_SEQ, SAMPLE_KEYS_PADDED), 1)
    os_ref[0] = jnp.where(col_s == 0, sink, build(bs_ref[...]))


def _bias_tables(rel_bias, attn_sinks):
    bp, bs = _bucket_tables()
    n_layers = attn_sinks.shape[0]
    return pl.pallas_call(
        _bias_kernel,
        out_shape=(jax.ShapeDtypeStruct((n_layers, 2, N_HEADS, WINDOW, 2 * WINDOW), F32),
                   jax.ShapeDtypeStruct((n_layers, N_HEADS * DEC_SEQ, SAMPLE_KEYS_PADDED), F32)),
        grid=(n_layers, N_HEADS),
        in_specs=[pl.BlockSpec(memory_space=pltpu.SMEM), pl.BlockSpec(memory_space=pltpu.SMEM),
                  pl.BlockSpec((WINDOW, 2 * WINDOW), lambda l, j: (0, 0)),
                  pl.BlockSpec((DEC_SEQ, SAMPLE_KEYS_PADDED), lambda l, j: (j, 0))],
        out_specs=(pl.BlockSpec((1, 2, 1, WINDOW, 2 * WINDOW), lambda l, j: (l, 0, j, 0, 0)),
                   pl.BlockSpec((1, DEC_SEQ, SAMPLE_KEYS_PADDED), lambda l, j: (l, j, 0))),
        compiler_params=_params(MIN_SCOPED_VMEM_BYTES, 2),
        name="bias_tables",
    )(rel_bias, attn_sinks, jnp.asarray(bp), jnp.asarray(bs))


def _memkv_kernel(m_ref, g_ref, w_ref, k_ref, v_ref, kb_ref, ve_ref):
    hn = _rms(m_ref[...], g_ref[...]).astype(BF16)
    kv = _dot(hn, w_ref[...].astype(BF16))
    k = kv[:, :MEM_DIM]
    v = kv[:, MEM_DIM:]
    k_ref[...] = k
    v_ref[...] = v
    kb_ref[...] = k.astype(BF16)
    ones = jnp.ones((MEM_TOKENS, MEM_HEAD_DIM), BF16)
    for h in range(MEM_HEADS):
        vh = v[:, h * MEM_HEAD_DIM:(h + 1) * MEM_HEAD_DIM].astype(BF16)
        ve_ref[h] = jnp.concatenate([vh, ones], axis=1)


def _memkv(mem_prompt, mem_norm, w_mem_kv):
    shp = (DEPTH, BATCH, MEM_TOKENS, MEM_DIM)
    blk = pl.BlockSpec((None, None, MEM_TOKENS, MEM_DIM), lambda l, b: (l, b, 0, 0))
    return pl.pallas_call(
        _memkv_kernel,
        out_shape=(jax.ShapeDtypeStruct(shp, F32), jax.ShapeDtypeStruct(shp, F32),
                   jax.ShapeDtypeStruct(shp, BF16),
                   jax.ShapeDtypeStruct((DEPTH, BATCH, MEM_HEADS, MEM_TOKENS, 2 * MEM_HEAD_DIM), BF16)),
        grid=(DEPTH, BATCH),
        in_specs=[pl.BlockSpec((None, MEM_TOKENS, D_MODEL), lambda l, b: (b, 0, 0)),
                  pl.BlockSpec((None, 1, D_MODEL), lambda l, b: (l, 0, 0)),
                  pl.BlockSpec((None, D_MODEL, 2 * MEM_DIM), lambda l, b: (l, 0, 0))],
        out_specs=(blk, blk, blk,
                   pl.BlockSpec((None, None, MEM_HEADS, MEM_TOKENS, 2 * MEM_HEAD_DIM),
                                lambda l, b: (l, b, 0, 0, 0))),
        compiler_params=_params(MIN_SCOPED_VMEM_BYTES, 2),
        name="memkv",
    )(mem_prompt, mem_norm.reshape(DEPTH, 1, D_MODEL), w_mem_kv)


def _ffn_kernel(*refs, n_x, attn_proj, shared_kv, layer):
    refs = list(refs)
    x_refs = [refs.pop(0) for _ in range(n_x)]
    g_ref = refs.pop(0)
    w_hbm = [refs.pop(0) for _ in range(3)]
    if attn_proj:
        gm_ref, win_ref = refs.pop(0), refs.pop(0)
    kv_in = [refs.pop(0) for _ in range(4)] if shared_kv else None
    x1_ref = refs.pop(0)
    if attn_proj:
        qa_ref, qm_ref = refs.pop(0), refs.pop(0)
    kv_out = [refs.pop(0) for _ in range(4)] if shared_kv else None
    act_ref = refs.pop(0)
    w = _FfnWeights(layer, w_hbm, refs)

    def body(load_weights):
        x = _pick_group(*x_refs) if n_x == 2 else x_refs[0][...]
        if shared_kv:
            _emit_shared_kv(x, *kv_in, *kv_out)
        x1 = _ffn_half_step(x, g_ref, w, act_ref, load_weights)
        x1_ref[...] = x1
        if attn_proj:
            hm = _rms(x1, gm_ref[...]).astype(BF16)
            qa_ref[...] = (_dot(hm, win_ref[:, :ATTN_DIM].astype(BF16)) * ATTN_Q_SCALE).astype(BF16)
            qm_ref[...] = (_dot(hm, win_ref[:, ATTN_DIM:].astype(BF16)) * MEM_Q_SCALE).astype(BF16)

    _first_step_loads(w, body)


def _ffn(x, g, ffn_w, layer, gm=None, win=None, win_layer=None, shared_kv_w=None):
    attn_proj = win is not None
    shared_kv = shared_kv_w is not None
    xs = list(x) if isinstance(x, tuple) else [x]
    in_specs = ((_group_specs(D_MODEL) if len(xs) == 2 else [_row_spec(D_MODEL)])
                + [_resident((1, D_MODEL))] + _FfnWeights.in_specs())
    args = xs + [g.reshape(1, D_MODEL)] + list(ffn_w)
    out_shape = [jax.ShapeDtypeStruct((ROWS, D_MODEL), F32)]
    out_specs = [_row_spec(D_MODEL)]
    resident = 0
    streamed = (1 + len(xs)) * _ROW_TILE_F32
    if attn_proj:
        in_specs += [_resident((1, D_MODEL)), _resident_layer((D_MODEL, ATTN_DIM + MEM_DIM), win_layer)]
        args += [gm.reshape(1, D_MODEL), win]
        out_shape += [jax.ShapeDtypeStruct((ROWS, ATTN_DIM), BF16), jax.ShapeDtypeStruct((ROWS, MEM_DIM), BF16)]
        out_specs += [_row_spec(ATTN_DIM), _row_spec(MEM_DIM)]
        resident += D_MODEL * (ATTN_DIM + MEM_DIM) * 4
        streamed += TILE_ROWS * (ATTN_DIM + MEM_DIM) * 2
    if shared_kv:
        kv_in_specs, kv_shape, kv_specs, kv_resident, kv_streamed = _shared_kv_specs()
        in_specs += kv_in_specs
        args += [shared_kv_w[0].reshape(1, D_MODEL)] + list(shared_kv_w[1:])
        out_shape += kv_shape
        out_specs += kv_specs
        resident += kv_resident
        streamed += kv_streamed
    return pl.pallas_call(
        functools.partial(_ffn_kernel, n_x=len(xs), attn_proj=attn_proj, shared_kv=shared_kv, layer=layer),
        out_shape=tuple(out_shape),
        grid=(ROW_TILES,),
        in_specs=in_specs,
        out_specs=tuple(out_specs),
        scratch_shapes=[pltpu.VMEM((TILE_ROWS, FFN_DIM), BF16)] + _FfnWeights.scratch_shapes(),
        compiler_params=_params(_vmem_limit(
            resident, streamed, _ACT_BYTES + _FfnWeights.SCRATCH_BYTES, 3 * _ROW_TILE_F32)),
        name="ffn_attn_proj" if attn_proj else "ffn",
    )(*args)


def _inproj_conv_kernel(x_ref, gm_ref, win_ref, cw_ref, pre_ref,
                        ytok_ref, qm_ref, tail_ref, us_ref, shift_ref):
    i = pl.program_id(0)
    hm = _rms(x_ref[...], gm_ref[...]).astype(BF16)
    qm_ref[...] = (_dot(hm, win_ref[:, 3 * CONV_DIM:].astype(BF16)) * MEM_Q_SCALE).astype(BF16)

    def chunk(cc, prompt):
        sl = slice(cc * CONV_CHUNK, (cc + 1) * CONV_CHUNK)
        c_gate = _dot(hm, win_ref[:, CONV_DIM + cc * CONV_CHUNK:CONV_DIM + (cc + 1) * CONV_CHUNK].astype(BF16))
        x_in = _dot(hm, win_ref[:, 2 * CONV_DIM + cc * CONV_CHUNK:2 * CONV_DIM + (cc + 1) * CONV_CHUNK].astype(BF16))
        u = c_gate * x_in
        if prompt:
            shift_ref[V7X_SUBLANES:, sl] = u
            u1 = shift_ref[V7X_SUBLANES - 1:V7X_SUBLANES - 1 + TILE_ROWS, sl]
            u2 = shift_ref[V7X_SUBLANES - 2:V7X_SUBLANES - 2 + TILE_ROWS, sl]
            last = u[TILE_ROWS - V7X_SUBLANES:, :]
            shift_ref[:V7X_SUBLANES, sl] = last
            tail_ref[0, :, sl] = last
        else:
            t = lax.broadcasted_iota(jnp.int32, (TILE_ROWS, CONV_CHUNK), 0) % DEC_SEQ
            p2 = pre_ref[:, sl]
            p1 = pltpu.roll(p2, TILE_ROWS - 1, axis=0)
            u1 = jnp.where(t == 0, p1, pltpu.roll(u, 1, axis=0))
            u2 = jnp.where(t < 2, p2, pltpu.roll(u, 2, axis=0))
            us_ref[:, sl] = u
            tail_ref[0, :, sl] = jnp.zeros((V7X_SUBLANES, CONV_CHUNK), F32)
        w = cw_ref[:, sl]
        conv = w[0:1] * u2 + w[1:2] * u1 + w[2:3] * u
        b_gate = _dot(hm, win_ref[:, sl].astype(BF16))
        ytok_ref[:, sl] = (b_gate * conv).astype(BF16)

    @pl.when(i < PROMPT_TILES)
    def _():
        @pl.when(i % TILES_PER_SEQ == 0)
        def _():
            shift_ref[:V7X_SUBLANES, :] = jnp.zeros((V7X_SUBLANES, CONV_DIM), F32)
        for cc in range(CONV_DIM // CONV_CHUNK):
            chunk(cc, True)

    @pl.when(i >= PROMPT_TILES)
    def _():
        for cc in range(CONV_DIM // CONV_CHUNK):
            chunk(cc, False)


def _inproj_conv(x1, gm, win, conv_w, prefix_rows, layer):
    sample_idx = lambda i: (jnp.maximum(i - PROMPT_TILES, 0), 0)
    win_bytes = D_MODEL * (3 * CONV_DIM + MEM_DIM) * 4
    return pl.pallas_call(
        _inproj_conv_kernel,
        out_shape=(jax.ShapeDtypeStruct((ROWS, CONV_DIM), BF16),
                   jax.ShapeDtypeStruct((ROWS, MEM_DIM), BF16),
                   jax.ShapeDtypeStruct((ROW_TILES, V7X_SUBLANES, CONV_DIM), F32),
                   jax.ShapeDtypeStruct((ROWS_SAMPLE, CONV_DIM), F32)),
        grid=(ROW_TILES,),
        in_specs=[_row_spec(D_MODEL), _resident((1, D_MODEL)),
                  _resident_layer((D_MODEL, 3 * CONV_DIM + MEM_DIM), layer),
                  _resident_layer((CONV_WIDTH, CONV_DIM), layer),
                  pl.BlockSpec((TILE_ROWS, CONV_DIM), sample_idx)],
        out_specs=(_row_spec(CONV_DIM), _row_spec(MEM_DIM),
                   pl.BlockSpec((1, V7X_SUBLANES, CONV_DIM), lambda i: (i, 0, 0)),
                   pl.BlockSpec((TILE_ROWS, CONV_DIM), sample_idx)),
        scratch_shapes=[pltpu.VMEM((TILE_ROWS + V7X_SUBLANES, CONV_DIM), F32)],
        compiler_params=_params(_vmem_limit(
            win_bytes, 3 * _ROW_TILE_F32 + TILE_ROWS * (CONV_DIM + MEM_DIM) * 2,
            _ROW_TILE_F32 + V7X_SUBLANES * CONV_DIM * 4, 6 * _ROW_TILE_F32)),
        name="inproj_conv",
    )(x1, gm.reshape(1, D_MODEL), win, conv_w, prefix_rows)


VZ_WIDTH = N_KV_HEADS * 2 * 2 * V7X_LANES


def _emit_shared_kv(x, g_ref, wk_ref, wkt_ref, wv_ref, k_ref, v_ref, ktz_ref, vz_ref):
    hk = _rms(x, g_ref[...]).astype(BF16)
    k_ref[...] = _dot(hk, wk_ref[...].astype(BF16))
    v = _dot(hk, wv_ref[...].astype(BF16))
    v_ref[...] = v
    kt = _dot_nt(wkt_ref[...].astype(BF16), hk)
    lo = lax.broadcasted_iota(jnp.int32, (TILE_ROWS, V7X_LANES), 1) < HALF_LANES
    hi = jnp.logical_not(lo)
    ones = (jnp.where(lo, 1.0, 0.0).astype(BF16), jnp.where(lo, 0.0, 1.0).astype(BF16))
    zero_k = jnp.zeros((HEAD_DIM, TILE_ROWS), BF16)
    for pair in range(N_KV_HEADS // 2):
        v_pair = v[:, pair * V7X_LANES:(pair + 1) * V7X_LANES]
        v_swap = pltpu.roll(v_pair, HALF_LANES, axis=1)
        for odd in range(2):
            h = 2 * pair + odd
            kth = kt[h * HEAD_DIM:(h + 1) * HEAD_DIM, :].astype(BF16)
            ktz_ref[h, 0] = jnp.concatenate([kth, zero_k], axis=0)
            ktz_ref[h, 1] = jnp.concatenate([zero_k, kth], axis=0)
            halves = (jnp.where(lo, v_swap if odd else v_pair, 0.0), jnp.where(hi, v_pair if odd else v_swap, 0.0))
            for e in range(2):
                c0 = (h * 2 + e) * 2 * V7X_LANES
                vz_ref[:, c0:c0 + 2 * V7X_LANES] = jnp.concatenate(
                    [halves[e].astype(BF16), ones[e]], axis=1)


def _shared_kv_specs():
    in_specs = [_resident((1, D_MODEL)), _resident((D_MODEL, KV_DIM)), _resident((KV_DIM, D_MODEL)),
                _resident((D_MODEL, KV_DIM))]
    out_shape = [jax.ShapeDtypeStruct((ROWS, KV_DIM), F32), jax.ShapeDtypeStruct((ROWS, KV_DIM), F32),
                 jax.ShapeDtypeStruct((N_KV_HEADS, 2, V7X_LANES, ROWS), BF16),
                 jax.ShapeDtypeStruct((ROWS, VZ_WIDTH), BF16)]
    out_specs = [_row_spec(KV_DIM), _row_spec(KV_DIM),
                 pl.BlockSpec((N_KV_HEADS, 2, V7X_LANES, TILE_ROWS), lambda i: (0, 0, 0, i)),
                 _row_spec(VZ_WIDTH)]
    resident = 3 * D_MODEL * KV_DIM * 4
    streamed = TILE_ROWS * (2 * KV_DIM * 4 + 2 * KV_DIM * 2 + VZ_WIDTH * 2)
    return in_specs, out_shape, out_specs, resident, streamed


def _attn_prompt_kernel(*refs, swa):
    if swa:
        (qm_ref, mk_ref, mve_ref, qa_ref, ktp_ref, ktc_ref, vzp_ref, vzc_ref, bias_ref,
         ymem_ref, ytok_ref) = refs
    else:
        qm_ref, mk_ref, mve_ref, ymem_ref = refs

    head_cols = [slice(h * MEM_HEAD_DIM, (h + 1) * MEM_HEAD_DIM) for h in range(MEM_HEADS)]
    scores = [_dot_nt(qm_ref[:, sl], mk_ref[:, sl]) for sl in head_cols]
    probs = [jnp.exp2(s - jnp.max(s, axis=-1, keepdims=True)).astype(BF16) for s in scores]
    for h, sl in enumerate(head_cols):
        oe = _dot(probs[h], mve_ref[h])
        ymem_ref[:, sl] = (oe[:, :MEM_HEAD_DIM] / oe[:, MEM_HEAD_DIM:]).astype(BF16)
    if not swa:
        return

    no_prev = ((pl.program_id(0) % TILES_PER_SEQ) == 0).astype(jnp.int32)
    key0_col = lax.broadcasted_iota(jnp.int32, (V7X_LANES, WINDOW), 1) == 0
    key0_row = lax.broadcasted_iota(jnp.int32, (WINDOW, V7X_LANES), 0) == 0
    zero_kt = jnp.zeros((V7X_LANES, WINDOW), BF16)
    zero_v = jnp.zeros((WINDOW, V7X_LANES), BF16)

    def rows_of(n):
        return slice(n * WINDOW, (n + 1) * WINDOW)

    @functools.lru_cache(maxsize=None)
    def key_window(n, h, e):
        k_prev = ktp_ref[h, e] if n == 0 else ktc_ref[h, e, :, rows_of(n - 1)]
        k_prev = jnp.where(key0_col, zero_kt, k_prev)
        return jnp.concatenate([k_prev, ktc_ref[h, e, :, rows_of(n)]], axis=1)

    @functools.lru_cache(maxsize=None)
    def value_window(n, h, e):
        c0 = (h * 2 + e) * 2 * V7X_LANES
        cols = slice(c0, c0 + 2 * V7X_LANES)
        v_prev = vzp_ref[:, cols] if n == 0 else vzc_ref[rows_of(n - 1), cols]
        v_prev = jnp.concatenate(
            [jnp.where(key0_row, zero_v, v_prev[:, :V7X_LANES]), v_prev[:, V7X_LANES:]], axis=1)
        return jnp.concatenate([v_prev, vzc_ref[rows_of(n), cols]], axis=0)

    def slab_col(h, pr):
        return (h * (GROUP // 2) + pr) * V7X_LANES

    def head_scores(n, h, pr, e):
        j = h * GROUP + pr * 2 + e
        bias = bias_ref[no_prev, j] if n == 0 else bias_ref[0, j]
        c0 = slab_col(h, pr)
        return _dot(qa_ref[rows_of(n), c0:c0 + V7X_LANES], key_window(n, h, e)) + bias

    heads = [(n, h, pr, e) for n in range(BLOCKS_PER_TILE) for h in range(N_KV_HEADS)
             for pr in range(GROUP // 2) for e in range(2)]
    s_next = head_scores(*heads[0])
    acc = None
    for idx, (n, h, pr, e) in enumerate(heads):
        s = s_next
        if idx + 1 < len(heads):
            s_next = head_scores(*heads[idx + 1])
        p = jnp.exp2(s - jnp.max(s, axis=-1, keepdims=True)).astype(BF16)
        part = _dot(p, value_window(n, h, e))
        if e == 0:
            acc = part
        else:
            acc = acc + part
            c0 = slab_col(h, pr)
            ytok_ref[rows_of(n), c0:c0 + V7X_LANES] = (acc[:, :V7X_LANES] / acc[:, V7X_LANES:]).astype(BF16)


def _attn_prompt(qm, mk_bf, mv_ext, layer, swa_args=None):
    swa = swa_args is not None
    batch_of = lambda i: i // TILES_PER_SEQ
    in_specs = [_row_spec(MEM_DIM),
                pl.BlockSpec((None, None, MEM_TOKENS, MEM_DIM), lambda i: (layer, batch_of(i), 0, 0)),
                pl.BlockSpec((None, None, MEM_HEADS, MEM_TOKENS, 2 * MEM_HEAD_DIM),
                             lambda i: (layer, batch_of(i), 0, 0, 0))]
    args = [qm, mk_bf, mv_ext]
    out_shape = [jax.ShapeDtypeStruct((ROWS_PROMPT, MEM_DIM), BF16)]
    out_specs = [_row_spec(MEM_DIM)]
    streamed = TILE_ROWS * MEM_DIM * 4 + MEM_TOKENS * MEM_DIM * 2 * 3
    resident = 0
    if swa:
        qa, ktz, vz, bias_p, swa_layer = swa_args
        prev_blk = lambda i: jnp.maximum(i * BLOCKS_PER_TILE - 1, 0)
        in_specs += [_row_spec(ATTN_DIM),
                     pl.BlockSpec((N_KV_HEADS, 2, V7X_LANES, WINDOW), lambda i: (0, 0, 0, prev_blk(i))),
                     pl.BlockSpec((N_KV_HEADS, 2, V7X_LANES, TILE_ROWS), lambda i: (0, 0, 0, i)),
                     pl.BlockSpec((WINDOW, VZ_WIDTH), lambda i: (prev_blk(i), 0)),
                     pl.BlockSpec((TILE_ROWS, VZ_WIDTH), lambda i: (i, 0)),
                     _resident_layer((2, N_HEADS, WINDOW, 2 * WINDOW), swa_layer)]
        args += [qa, ktz, ktz, vz, vz, bias_p]
        out_shape += [jax.ShapeDtypeStruct((ROWS_PROMPT, ATTN_DIM), BF16)]
        out_specs += [_row_spec(ATTN_DIM)]
        resident = 2 * N_HEADS * WINDOW * 2 * WINDOW * 4
        streamed += 2 * TILE_ROWS * ATTN_DIM * 2 + (TILE_ROWS + WINDOW) * (2 * KV_DIM + VZ_WIDTH) * 2
    return pl.pallas_call(
        functools.partial(_attn_prompt_kernel, swa=swa),
        out_shape=tuple(out_shape),
        grid=(PROMPT_TILES,),
        in_specs=in_specs,
        out_specs=tuple(out_specs),
        compiler_params=_params(_vmem_limit(resident, streamed, 0, 8 * _ROW_TILE_F32)),
        name="attn_prompt_swa" if swa else "attn_prompt_mem",
    )(*args)


def _attn_sample_kernel(*refs, swa, emit_cache):
    if swa:
        (qm_ref, mk_ref, mv_ref, qa_ref, ck_ref, cv_ref, kn_ref, vn_ref, bias_ref,
         ymem_ref, ytok_ref) = refs[:11]
        if emit_cache:
            kc_out_ref, vc_out_ref = refs[11:]
            keep_cached = lax.broadcasted_iota(jnp.int32, (KV_DIM, WINDOW), 1) < WINDOW - DEC_SEQ
    else:
        qm_ref, mk_ref, mv_ref, ymem_ref = refs

    qm_all = qm_ref[...].astype(F32)
    mem_rows = MEM_HEADS * DEC_SEQ
    own_head = (lax.broadcasted_iota(jnp.int32, (mem_rows, MEM_TOKENS * MEM_HEADS), 1) % MEM_HEADS
                == lax.broadcasted_iota(jnp.int32, (mem_rows, MEM_TOKENS * MEM_HEADS), 0) // DEC_SEQ)
    if swa:
        qa_all = qa_ref[...].astype(F32)
        kn_all = kn_ref[...]
        vn_all = vn_ref[...]
        bias = bias_ref[...]
        key0 = lax.broadcasted_iota(jnp.int32, (KV_DIM, WINDOW), 1) == 0
        pad = jnp.zeros((WINDOW - DEC_SEQ, KV_DIM), F32)
        lo = lax.broadcasted_iota(jnp.int32, (DEC_SEQ, V7X_LANES), 1) < HALF_LANES
        hi = jnp.logical_not(lo)
        zero_slab = jnp.zeros((DEC_SEQ, V7X_LANES), F32)

    batch = range(SAMPLE_BATCH_BLOCK)
    rows_of = [slice(b * DEC_SEQ, (b + 1) * DEC_SEQ) for b in batch]
    q_heads = [jnp.concatenate([qm_all[rows_of[b], h * MEM_HEAD_DIM:(h + 1) * MEM_HEAD_DIM]
                                for h in range(MEM_HEADS)], axis=0) for b in batch]
    scores = [jnp.where(own_head, _dot_nt(q_heads[b], mk_ref[b]), NEG_INF) for b in batch]
    probs = [jnp.exp2(s - jnp.max(s, axis=-1, keepdims=True)) for s in scores]
    outs = [_dot(probs[b], mv_ref[b]) / jnp.sum(probs[b], axis=-1, keepdims=True) for b in batch]
    ymem_rows = [jnp.concatenate([o[h * DEC_SEQ:(h + 1) * DEC_SEQ] for h in range(MEM_HEADS)], axis=1)
                 for o in outs]

    if swa:
        def block_diag_queries(qb):
            groups = []
            for j in range(N_HEADS):
                kvh = j // GROUP
                slab = qb[:, (j // 2) * V7X_LANES:(j // 2 + 1) * V7X_LANES]
                if j % 2 != kvh % 2:
                    slab = pltpu.roll(slab, HALF_LANES, axis=1)
                slab = jnp.where(lo if kvh % 2 == 0 else hi, slab, 0.0)
                groups.append(jnp.concatenate([slab, zero_slab] if kvh // 2 == 0 else [zero_slab, slab], axis=1))
            return jnp.concatenate(groups, axis=0)

        def gather_heads(o_full):
            pairs = []
            for pair in range(N_HEADS // 2):
                acc = None
                for e in range(2):
                    j = pair * 2 + e
                    kvh = j // GROUP
                    slab = o_full[j * DEC_SEQ:(j + 1) * DEC_SEQ, (kvh // 2) * V7X_LANES:(kvh // 2 + 1) * V7X_LANES]
                    if e != kvh % 2:
                        slab = pltpu.roll(slab, HALF_LANES, axis=1)
                    slab = jnp.where(lo if e == 0 else hi, slab, 0.0)
                    acc = slab if acc is None else acc + slab
                pairs.append(acc)
            return jnp.concatenate(pairs, axis=1)

        q_bd = [block_diag_queries(qa_all[rows_of[b]]) for b in batch]
        kt_cache = [jnp.where(key0, 0.0, ck_ref[b]) for b in batch]
        vt_cache = [jnp.where(key0, 0.0, cv_ref[b]) for b in batch]
        k_new = [jnp.concatenate([kn_all[rows_of[b]], pad], axis=0) for b in batch]
        v_new = [jnp.concatenate([vn_all[rows_of[b]], pad], axis=0) for b in batch]
        if emit_cache:
            shift = WINDOW - DEC_SEQ
            for b in batch:
                kc_out_ref[b] = jnp.where(keep_cached, pltpu.roll(ck_ref[b], shift, axis=1),
                                          pltpu.roll(k_new[b].T, shift, axis=1))
                vc_out_ref[b] = jnp.where(keep_cached, pltpu.roll(cv_ref[b], shift, axis=1),
                                          pltpu.roll(v_new[b].T, shift, axis=1))
        scores = [jnp.concatenate([_dot(q_bd[b], kt_cache[b]), _dot_nt(q_bd[b], k_new[b])], axis=1) + bias
                  for b in batch]
        probs = [jnp.exp2(s - jnp.max(s, axis=-1, keepdims=True)) for s in scores]
        o_full = [(_dot_nt(probs[b][:, :WINDOW], vt_cache[b]) + _dot(probs[b][:, WINDOW:], v_new[b]))
                  / jnp.sum(probs[b], axis=-1, keepdims=True) for b in batch]
        ytok_rows = [gather_heads(o) for o in o_full]

    ymem_ref[...] = jnp.concatenate(ymem_rows, axis=0).astype(BF16)
    if swa:
        ytok_ref[...] = jnp.concatenate(ytok_rows, axis=0).astype(BF16)


def _attn_sample(qm, cache_k, cache_v, layer, swa_args=None, emit_cache=False):
    swa = swa_args is not None
    assert swa or not emit_cache
    row0 = ROWS_PROMPT // SAMPLE_BLOCK_ROWS
    blk_rows = lambda width: pl.BlockSpec((SAMPLE_BLOCK_ROWS, width), lambda i: (row0 + i, 0))
    out_rows = lambda width: pl.BlockSpec((SAMPLE_BLOCK_ROWS, width), lambda i: (i, 0))
    cache_spec = pl.BlockSpec((None, SAMPLE_BATCH_BLOCK, MEM_TOKENS * MEM_HEADS, MEM_HEAD_DIM),
                              lambda i: (layer, i, 0, 0))
    in_specs = [blk_rows(MEM_DIM), cache_spec, cache_spec]
    args = [qm, cache_k, cache_v]
    out_shape = [jax.ShapeDtypeStruct((ROWS_SAMPLE, MEM_DIM), BF16)]
    out_specs = [out_rows(MEM_DIM)]
    streamed = 2 * SAMPLE_BATCH_BLOCK * MEM_TOKENS * MEM_DIM * 4 + SAMPLE_BLOCK_ROWS * MEM_DIM * 4
    if swa:
        qa, swa_k, swa_v, k_new, v_new, bias_s, swa_layer = swa_args
        swa_spec = pl.BlockSpec((SAMPLE_BATCH_BLOCK, KV_DIM, WINDOW), lambda i: (i, 0, 0))
        in_specs += [blk_rows(ATTN_DIM), swa_spec, swa_spec, out_rows(KV_DIM), out_rows(KV_DIM),
                     _resident_layer((N_HEADS * DEC_SEQ, SAMPLE_KEYS_PADDED), swa_layer)]
        args += [qa, swa_k, swa_v, k_new, v_new, bias_s]
        out_shape += [jax.ShapeDtypeStruct((ROWS_SAMPLE, ATTN_DIM), BF16)]
        out_specs += [out_rows(ATTN_DIM)]
        streamed += 2 * SAMPLE_BATCH_BLOCK * WINDOW * KV_DIM * 4 + SAMPLE_BLOCK_ROWS * (ATTN_DIM + KV_DIM) * 4
        if emit_cache:
            out_shape += [jax.ShapeDtypeStruct((DEC_BATCH, KV_DIM, WINDOW), F32)] * 2
            out_specs += [swa_spec, swa_spec]
            streamed += 2 * SAMPLE_BATCH_BLOCK * WINDOW * KV_DIM * 4
    return pl.pallas_call(
        functools.partial(_attn_sample_kernel, swa=swa, emit_cache=emit_cache),
        out_shape=tuple(out_shape),
        grid=(DEC_BATCH // SAMPLE_BATCH_BLOCK,),
        in_specs=in_specs,
        out_specs=tuple(out_specs),
        compiler_params=_params(_vmem_limit(1 << 20, streamed, 0, 6 * _ROW_TILE_F32)),
        name="attn_sample_swa" if swa else "attn_sample_mem",
    )(*args)


def _outffn_kernel(*refs, split_tok, final, layer):
    refs = list(refs)
    x1_ref = refs.pop(0)
    tok_refs = [refs.pop(0) for _ in range(2 if split_tok else 1)]
    ymp_ref, yms_ref, wo_ref, g_ref = [refs.pop(0) for _ in range(4)]
    w_hbm = [refs.pop(0) for _ in range(3)]
    gf_ref = refs.pop(0) if final else None
    out_refs = [refs.pop(0) for _ in range(2 if final else 1)]
    act_ref = refs.pop(0)
    w = _FfnWeights(layer, w_hbm, refs)
    tok_dim = wo_ref.shape[0] - MEM_DIM

    def body(load_weights):
        y_tok = _pick_group(*tok_refs) if split_tok else tok_refs[0][...]
        y_mem = _pick_group(ymp_ref, yms_ref)
        x2 = (x1_ref[...] + _dot(y_tok, wo_ref[:tok_dim, :].astype(BF16))
              + _dot(y_mem, wo_ref[tok_dim:, :].astype(BF16)))
        x3 = _ffn_half_step(x2, g_ref, w, act_ref, load_weights)
        if not final:
            out_refs[0][...] = x3
            return
        y = _rms(x3, gf_ref[...])

        @pl.when(pl.program_id(0) < PROMPT_TILES)
        def _():
            out_refs[0][...] = y

        @pl.when(pl.program_id(0) >= PROMPT_TILES)
        def _():
            out_refs[1][...] = y

    _first_step_loads(w, body)


def _outffn(x1, y_tok, ymem_p, ymem_s, wo, wo_layer, g, ffn_w, layer, final_gain=None):
    split_tok = isinstance(y_tok, tuple)
    final = final_gain is not None
    tok_dim = wo.shape[1] - MEM_DIM
    in_specs = [_row_spec(D_MODEL)]
    args = [x1]
    if split_tok:
        in_specs += _group_specs(tok_dim)
        args += list(y_tok)
    else:
        in_specs += [_row_spec(tok_dim)]
        args += [y_tok]
    in_specs += (_group_specs(MEM_DIM) + [_resident_layer((tok_dim + MEM_DIM, D_MODEL), wo_layer)]
                 + [_resident((1, D_MODEL))] + _FfnWeights.in_specs())
    args += [ymem_p, ymem_s, wo, g.reshape(1, D_MODEL)] + list(ffn_w)
    if final:
        in_specs += [_resident((1, D_MODEL))]
        args += [final_gain.reshape(1, D_MODEL)]
    resident = (tok_dim + MEM_DIM) * D_MODEL * 4
    streamed = 2 * _ROW_TILE_F32 + 2 * TILE_ROWS * (tok_dim + MEM_DIM) * 2
    if final:
        out_shape = [jax.ShapeDtypeStruct((ROWS_PROMPT, D_MODEL), F32),
                     jax.ShapeDtypeStruct((ROWS_SAMPLE, D_MODEL), F32)]
        out_specs = _group_specs(D_MODEL)
        streamed += _ROW_TILE_F32
    else:
        out_shape = [jax.ShapeDtypeStruct((ROWS, D_MODEL), F32)]
        out_specs = [_row_spec(D_MODEL)]
    return pl.pallas_call(
        functools.partial(_outffn_kernel, split_tok=split_tok, final=final, layer=layer),
        out_shape=tuple(out_shape),
        grid=(ROW_TILES,),
        in_specs=in_specs,
        out_specs=tuple(out_specs),
        scratch_shapes=[pltpu.VMEM((TILE_ROWS, FFN_DIM), BF16)] + _FfnWeights.scratch_shapes(),
        compiler_params=_params(_vmem_limit(
            resident, streamed, _ACT_BYTES + _FfnWeights.SCRATCH_BYTES, 3 * _ROW_TILE_F32)),
        name="outffn_final" if final else "outffn",
    )(*args)


def kernel(x_prompt, x_sample, state_conv, cache_swa_k, cache_swa_v, cache_mem_k, cache_mem_v, mem_prompt, ffn1_norm, ffn1_wg, ffn1_wu, ffn1_wd, mix_norm, w_in_a, conv_w, w_out_a, kv_norm, w_kv, w_in_b, attn_sinks, rel_bias, w_out_b, mem_norm, w_mem_kv, ffn2_norm, ffn2_wg, ffn2_wu, ffn2_wd, final_norm):
    ffn1 = (ffn1_wg, ffn1_wu, ffn1_wd)
    ffn2 = (ffn2_wg, ffn2_wu, ffn2_wd)
    wk = w_kv[:, :KV_DIM]
    wv = w_kv[:, KV_DIM:]

    mem_k, mem_v, mem_k_bf, mem_v_ext = _memkv(mem_prompt, mem_norm, w_mem_kv)
    bias_p, bias_s = _bias_tables(rel_bias, attn_sinks)
    swa_k_cache = cache_swa_k.transpose(0, 2, 3, 1).reshape(DEC_BATCH, KV_DIM, WINDOW)
    swa_v_cache = cache_swa_v.transpose(0, 2, 3, 1).reshape(DEC_BATCH, KV_DIM, WINDOW)
    mem_rows_shape = (DEPTH, DEC_BATCH, MEM_TOKENS * MEM_HEADS, MEM_HEAD_DIM)
    cache_k = cache_mem_k.reshape(mem_rows_shape)
    cache_v = cache_mem_v.reshape(mem_rows_shape)

    x = (x_prompt.reshape(ROWS_PROMPT, D_MODEL), x_sample.reshape(ROWS_SAMPLE, D_MODEL))
    tails, sample_us = [], []
    k_rows = v_rows = ktz = vz = k_new = v_new = None
    for l in range(DEPTH):
        last = l == DEPTH - 1
        if l < N_A_LAYERS:
            (x1,) = _ffn(x, ffn1_norm[l], ffn1, l)
            prefix_rows = jnp.pad(state_conv[l], ((0, 0), (0, DEC_SEQ - (CONV_WIDTH - 1)), (0, 0)))
            y_tok, qm, tail, us = _inproj_conv(x1, mix_norm[l], w_in_a, conv_w,
                                               prefix_rows.reshape(ROWS_SAMPLE, CONV_DIM), l)
            tails.append(tail)
            sample_us.append(us)
            (ymem_p,) = _attn_prompt(qm, mem_k_bf, mem_v_ext, l)
            (ymem_s,) = _attn_sample(qm, cache_k, cache_v, l)
            wo, wo_layer = w_out_a, l
        else:
            j = l - N_A_LAYERS
            if j == 0:
                x1, qa, qm, k_rows, v_rows, ktz, vz = _ffn(
                    x, ffn1_norm[l], ffn1, l, mix_norm[l], w_in_b, j,
                    shared_kv_w=(kv_norm, wk, wk.T, wv))
                k_new = k_rows[ROWS_PROMPT:]
                v_new = v_rows[ROWS_PROMPT:]
            else:
                x1, qa, qm = _ffn(x, ffn1_norm[l], ffn1, l, mix_norm[l], w_in_b, j)
            ymem_p, ytok_p = _attn_prompt(qm, mem_k_bf, mem_v_ext, l, (qa, ktz, vz, bias_p, j))
            sample_out = _attn_sample(qm, cache_k, cache_v, l,
                                      (qa, swa_k_cache, swa_v_cache, k_new, v_new, bias_s, j), emit_cache=j == 0)
            ymem_s, ytok_s = sample_out[:2]
            if j == 0:
                new_k_cache, new_v_cache = sample_out[2:]
            y_tok = (ytok_p, ytok_s)
            wo, wo_layer = w_out_b, j
        if not last:
            (x,) = _outffn(x1, y_tok, ymem_p, ymem_s, wo, wo_layer, ffn2_norm[l], ffn2, l)
        else:
            y_prompt, y_sample = _outffn(x1, y_tok, ymem_p, ymem_s, wo, wo_layer, ffn2_norm[l], ffn2, l,
                                         final_gain=final_norm)

    keep = CONV_WIDTH - 1
    last_tiles = np.arange(BATCH) * TILES_PER_SEQ + TILES_PER_SEQ - 1
    conv_state_prompt = jnp.stack([t[last_tiles, V7X_SUBLANES - keep:, :] for t in tails])
    conv_state_sample = jnp.stack([u.reshape(DEC_BATCH, DEC_SEQ, CONV_DIM)[:, DEC_SEQ - keep:, :] for u in sample_us])
    k_tail = jnp.stack([k_rows[(b + 1) * SEQ - WINDOW:(b + 1) * SEQ] for b in range(BATCH)])
    v_tail = jnp.stack([v_rows[(b + 1) * SEQ - WINDOW:(b + 1) * SEQ] for b in range(BATCH)])
    swa_k_prompt = k_tail.reshape(BATCH, WINDOW, N_KV_HEADS, HEAD_DIM)
    swa_v_prompt = v_tail.reshape(BATCH, WINDOW, N_KV_HEADS, HEAD_DIM)
    to_cache_layout = lambda c: c.reshape(DEC_BATCH, N_KV_HEADS, HEAD_DIM, WINDOW).transpose(0, 3, 1, 2)
    swa_k_sample = to_cache_layout(new_k_cache)
    swa_v_sample = to_cache_layout(new_v_cache)
    mem_shape = (DEPTH, BATCH, MEM_TOKENS, MEM_HEADS, MEM_HEAD_DIM)
    return (y_prompt.reshape(BATCH, SEQ, D_MODEL), y_sample.reshape(DEC_BATCH, DEC_SEQ, D_MODEL),
            conv_state_prompt, conv_state_sample,
            swa_k_prompt, swa_v_prompt, swa_k_sample, swa_v_sample,
            mem_k.reshape(mem_shape), mem_v.reshape(mem_shape))
```

```python
import functools
import math

import numpy as np
import jax
import jax.numpy as jnp
from jax import lax
from jax.experimental import pallas as pl
from jax.experimental.pallas import tpu as pltpu

D_MODEL = 1024
BATCH = 2
SEQ = 8192
DEPTH = 4
DEC_BATCH = 128
DEC_SEQ = 8
N_A_LAYERS = DEPTH // 2
FFN_DIM = 2816
CONV_DIM = D_MODEL
CONV_WIDTH = 3
N_HEADS = 16
N_KV_HEADS = 4
HEAD_DIM = 64
GROUP = N_HEADS // N_KV_HEADS
ATTN_DIM = N_HEADS * HEAD_DIM
KV_DIM = N_KV_HEADS * HEAD_DIM
WINDOW = 128
REL_BUCKETS = 32
REL_MAX_DIST = 128
MEM_TOKENS = 256
MEM_HEADS = 4
MEM_HEAD_DIM = 128
MEM_DIM = MEM_HEADS * MEM_HEAD_DIM
RMS_EPS = 1e-5

F32 = jnp.float32
BF16 = jnp.bfloat16
NEG_INF = float("-inf")

V7X_LANES = 128
V7X_SUBLANES = 8
V7X_MXU_DIM = 256
V7X_VMEM_BYTES = 64 * 1024 * 1024

ROWS_PROMPT = BATCH * SEQ
ROWS_SAMPLE = DEC_BATCH * DEC_SEQ
ROWS = ROWS_PROMPT + ROWS_SAMPLE
TILE_ROWS = 512
PROMPT_TILES = ROWS_PROMPT // TILE_ROWS
SAMPLE_TILES = ROWS_SAMPLE // TILE_ROWS
ROW_TILES = PROMPT_TILES + SAMPLE_TILES
TILES_PER_SEQ = SEQ // TILE_ROWS
FFN_CHUNK = V7X_MXU_DIM
CONV_CHUNK = V7X_MXU_DIM
BLOCKS_PER_TILE = TILE_ROWS // WINDOW
SAMPLE_BATCH_BLOCK = 8
SAMPLE_BLOCK_ROWS = SAMPLE_BATCH_BLOCK * DEC_SEQ
SAMPLE_KEYS = WINDOW + DEC_SEQ
SAMPLE_KEYS_PADDED = 2 * WINDOW
HALF_LANES = V7X_LANES // 2
LOG2E = math.log2(math.e)
MEM_Q_SCALE = MEM_HEAD_DIM ** -0.5 * LOG2E
ATTN_Q_SCALE = HEAD_DIM ** -0.5 * LOG2E

assert HEAD_DIM == HALF_LANES and MEM_HEAD_DIM == V7X_LANES
assert ROWS_PROMPT % TILE_ROWS == 0 and ROWS_SAMPLE % TILE_ROWS == 0 and SEQ % TILE_ROWS == 0
assert FFN_DIM % FFN_CHUNK == 0 and TILE_ROWS % WINDOW == 0


MIN_SCOPED_VMEM_BYTES = 48 * 1024 * 1024
SETUP_CALL_VMEM_BYTES = 32 * 1024 * 1024


def _vmem_limit(resident_bytes, streamed_bytes, scratch_bytes, temp_bytes):
    need = resident_bytes + 2 * streamed_bytes + scratch_bytes + temp_bytes
    assert need < V7X_VMEM_BYTES, need
    return max(int(need), MIN_SCOPED_VMEM_BYTES)


def _params(vmem_bytes, n_axes=1):
    return pltpu.CompilerParams(
        dimension_semantics=("arbitrary",) * n_axes, vmem_limit_bytes=vmem_bytes)


def _resident(shape):
    zeros = (0,) * len(shape)
    return pl.BlockSpec(shape, lambda *_: zeros, pipeline_mode=pl.Buffered(1))


def _resident_layer(shape, layer):
    idx = (layer,) + (0,) * len(shape)
    return pl.BlockSpec((None,) + tuple(shape), lambda *_: idx, pipeline_mode=pl.Buffered(1))


def _row_spec(width):
    return pl.BlockSpec((TILE_ROWS, width), lambda i: (i, 0))


def _group_specs(width):
    return [pl.BlockSpec((TILE_ROWS, width), lambda i: (jnp.minimum(i, PROMPT_TILES - 1), 0)),
            pl.BlockSpec((TILE_ROWS, width), lambda i: (jnp.maximum(i - PROMPT_TILES, 0), 0))]


def _pick_group(prompt_ref, sample_ref):
    return jnp.where(pl.program_id(0) < PROMPT_TILES, prompt_ref[...], sample_ref[...])


def _dot(a, b):
    return jnp.dot(a, b, preferred_element_type=F32)


def _dot_nt(a, b):
    return lax.dot_general(a, b, (((1,), (1,)), ((), ())), preferred_element_type=F32)


def _rms(x, g):
    return x * lax.rsqrt(jnp.mean(x * x, axis=-1, keepdims=True) + RMS_EPS) * g


FFN_CHUNKS = FFN_DIM // FFN_CHUNK
FFN_STAGE_SLOTS = 2


class _FfnWeights:
    def __init__(self, layer, hbm_refs, scratch_refs):
        self.layer = layer
        self.wg_hbm, self.wu_hbm, self.wd_hbm = hbm_refs
        self.wg, self.wu, self.wd, self.stage_in, self.stage_out, self.sems = scratch_refs

    @staticmethod
    def in_specs():
        return [pl.BlockSpec(memory_space=pl.ANY)] * 3

    @staticmethod
    def scratch_shapes():
        return [pltpu.VMEM((D_MODEL, FFN_DIM), BF16), pltpu.VMEM((D_MODEL, FFN_DIM), BF16),
                pltpu.VMEM((FFN_DIM, D_MODEL), BF16),
                pltpu.VMEM((2, FFN_STAGE_SLOTS, D_MODEL, FFN_CHUNK), F32),
                pltpu.VMEM((FFN_STAGE_SLOTS, FFN_CHUNK, D_MODEL), F32),
                pltpu.SemaphoreType.DMA((3, FFN_STAGE_SLOTS))]

    SCRATCH_BYTES = 3 * D_MODEL * FFN_DIM * 2 + 3 * FFN_STAGE_SLOTS * D_MODEL * FFN_CHUNK * 4

    def _copy(self, stream, c):
        slot = c % FFN_STAGE_SLOTS
        cols = pl.ds(c * FFN_CHUNK, FFN_CHUNK)
        if stream == 0:
            src, dst = self.wg_hbm.at[self.layer, :, cols], self.stage_in.at[0, slot]
        elif stream == 1:
            src, dst = self.wu_hbm.at[self.layer, :, cols], self.stage_in.at[1, slot]
        else:
            src, dst = self.wd_hbm.at[self.layer, cols, :], self.stage_out.at[slot]
        return pltpu.make_async_copy(src, dst, self.sems.at[stream, slot])

    def prime(self):
        for stream in range(3):
            for c in range(FFN_STAGE_SLOTS):
                self._copy(stream, c).start()

    def fetch(self, c):
        slot = c % FFN_STAGE_SLOTS
        sl = slice(c * FFN_CHUNK, (c + 1) * FFN_CHUNK)
        for stream in range(3):
            self._copy(stream, c).wait()
            if stream == 0:
                self.wg[:, sl] = self.stage_in[0, slot].astype(BF16)
            elif stream == 1:
                self.wu[:, sl] = self.stage_in[1, slot].astype(BF16)
            else:
                self.wd[sl, :] = self.stage_out[slot].astype(BF16)
            if c + FFN_STAGE_SLOTS < FFN_CHUNKS:
                self._copy(stream, c + FFN_STAGE_SLOTS).start()


def _ffn_half_step(x, g_ref, w, act_ref, load_weights):
    inv_rms = lax.rsqrt(jnp.mean(x * x, axis=-1, keepdims=True) + RMS_EPS)
    inv_rms_chunk = jnp.broadcast_to(inv_rms, (x.shape[0], FFN_CHUNK))
    h = (x * g_ref[...]).astype(BF16)
    for c in range(FFN_CHUNKS):
        if load_weights:
            w.fetch(c)
        sl = slice(c * FFN_CHUNK, (c + 1) * FFN_CHUNK)
        gate = _dot(h, w.wg[:, sl]) * inv_rms_chunk
        up = _dot(h, w.wu[:, sl])
        act_ref[:, sl] = (gate / (1.0 + jnp.exp(-gate)) * up).astype(BF16)
    return x + (0.5 * inv_rms) * _dot(act_ref[...], w.wd[...])


def _first_step_loads(w, body):
    @pl.when(pl.program_id(0) == 0)
    def _():
        w.prime()
        body(True)

    @pl.when(pl.program_id(0) > 0)
    def _():
        body(False)


_ROW_TILE_F32 = TILE_ROWS * D_MODEL * 4
_ACT_BYTES = TILE_ROWS * FFN_DIM * 2


def _t5_bucket_np(dist):
    n = np.maximum(dist, 0)
    exact = REL_BUCKETS // 2
    nf = np.maximum(n, 1).astype(np.float32)
    large = exact + (np.log(nf / np.float32(exact)) / np.float32(math.log(REL_MAX_DIST / exact))
                     * np.float32(REL_BUCKETS - exact)).astype(np.int32)
    large = np.minimum(large, REL_BUCKETS - 1)
    return np.where(n < exact, n, large).astype(np.int32)


def _bucket_tables():
    q = np.arange(WINDOW)[:, None]
    k = np.arange(2 * WINDOW)[None, :]
    dist = WINDOW + q - k
    prompt = np.where((dist >= 0) & (dist < WINDOW), _t5_bucket_np(dist), -1)
    t = (np.arange(N_HEADS * DEC_SEQ) % DEC_SEQ)[:, None]
    k = np.arange(SAMPLE_KEYS_PADDED)[None, :]
    dist = WINDOW + t - k
    ok = (dist >= 0) & (dist < WINDOW) & (k < SAMPLE_KEYS)
    sample = np.where(ok, _t5_bucket_np(dist), -1)
    return prompt.astype(np.int32), sample.astype(np.int32)


def _bias_kernel(rel_ref, sink_ref, bp_ref, bs_ref, op_ref, os_ref):
    layer = pl.program_id(0)
    j = pl.program_id(1)
    sink = sink_ref[layer, j] * LOG2E

    def build(bucket):
        acc = jnp.zeros(bucket.shape, F32)
        for b in range(REL_BUCKETS):
            acc = jnp.where(bucket == b, rel_ref[b, j], acc)
        return jnp.where(bucket < 0, NEG_INF, acc * LOG2E)

    col_p = lax.broadcasted_iota(jnp.int32, (WINDOW, 2 * WINDOW), 1)
    table = build(bp_ref[...])
    op_ref[0, 0, 0] = jnp.where(col_p == 0, sink, table)
    op_ref[0, 1, 0] = jnp.where(col_p == 0, sink, jnp.where(col_p < WINDOW, NEG_INF, table))
    col_s = lax.broadcasted_iota(jnp.int32, (DEC_SEQ, SAMPLE_KEYS_PADDED), 1)
    os_ref[0] = jnp.where(col_s == 0, sink, build(bs_ref[...]))


def _bias_tables(rel_bias, attn_sinks):
    bp, bs = _bucket_tables()
    n_layers = attn_sinks.shape[0]
    return pl.pallas_call(
        _bias_kernel,
        out_shape=(jax.ShapeDtypeStruct((n_layers, 2, N_HEADS, WINDOW, 2 * WINDOW), F32),
                   jax.ShapeDtypeStruct((n_layers, N_HEADS * DEC_SEQ, SAMPLE_KEYS_PADDED), F32)),
        grid=(n_layers, N_HEADS),
        in_specs=[pl.BlockSpec(memory_space=pltpu.SMEM), pl.BlockSpec(memory_space=pltpu.SMEM),
                  pl.BlockSpec((WINDOW, 2 * WINDOW), lambda l, j: (0, 0)),
                  pl.BlockSpec((DEC_SEQ, SAMPLE_KEYS_PADDED), lambda l, j: (j, 0))],
        out_specs=(pl.BlockSpec((1, 2, 1, WINDOW, 2 * WINDOW), lambda l, j: (l, 0, j, 0, 0)),
                   pl.BlockSpec((1, DEC_SEQ, SAMPLE_KEYS_PADDED), lambda l, j: (l, j, 0))),
        compiler_params=_params(SETUP_CALL_VMEM_BYTES, 2),
        name="bias_tables",
    )(rel_bias, attn_sinks, jnp.asarray(bp), jnp.asarray(bs))


def _memkv_kernel(m_ref, g_ref, w_ref, k_ref, v_ref, kb_ref, ve_ref):
    hn = _rms(m_ref[...], g_ref[...]).astype(BF16)
    kv = _dot(hn, w_ref[...].astype(BF16))
    k = kv[:, :MEM_DIM]
    v = kv[:, MEM_DIM:]
    k_ref[...] = k
    v_ref[...] = v
    kb_ref[...] = k.astype(BF16)
    ones = jnp.ones((MEM_TOKENS, MEM_HEAD_DIM), BF16)
    for h in range(MEM_HEADS):
        vh = v[:, h * MEM_HEAD_DIM:(h + 1) * MEM_HEAD_DIM].astype(BF16)
        ve_ref[h] = jnp.concatenate([vh, ones], axis=1)


def _memkv(mem_prompt, mem_norm, w_mem_kv):
    shp = (DEPTH, BATCH, MEM_TOKENS, MEM_DIM)
    blk = pl.BlockSpec((None, None, MEM_TOKENS, MEM_DIM), lambda l, b: (l, b, 0, 0))
    return pl.pallas_call(
        _memkv_kernel,
        out_shape=(jax.ShapeDtypeStruct(shp, F32), jax.ShapeDtypeStruct(shp, F32),
                   jax.ShapeDtypeStruct(shp, BF16),
                   jax.ShapeDtypeStruct((DEPTH, BATCH, MEM_HEADS, MEM_TOKENS, 2 * MEM_HEAD_DIM), BF16)),
        grid=(DEPTH, BATCH),
        in_specs=[pl.BlockSpec((None, MEM_TOKENS, D_MODEL), lambda l, b: (b, 0, 0)),
                  pl.BlockSpec((None, 1, D_MODEL), lambda l, b: (l, 0, 0)),
                  pl.BlockSpec((None, D_MODEL, 2 * MEM_DIM), lambda l, b: (l, 0, 0))],
        out_specs=(blk, blk, blk,
                   pl.BlockSpec((None, None, MEM_HEADS, MEM_TOKENS, 2 * MEM_HEAD_DIM),
                                lambda l, b: (l, b, 0, 0, 0))),
        compiler_params=_params(SETUP_CALL_VMEM_BYTES, 2),
        name="memkv",
    )(mem_prompt, mem_norm.reshape(DEPTH, 1, D_MODEL), w_mem_kv)


def _ffn_kernel(*refs, n_x, attn_proj, shared_kv, layer):
    refs = list(refs)
    x_refs = [refs.pop(0) for _ in range(n_x)]
    g_ref = refs.pop(0)
    w_hbm = [refs.pop(0) for _ in range(3)]
    if attn_proj:
        gm_ref, win_ref = refs.pop(0), refs.pop(0)
    kv_in = [refs.pop(0) for _ in range(3)] if shared_kv else None
    x1_ref = refs.pop(0)
    if attn_proj:
        qa_ref, qm_ref = refs.pop(0), refs.pop(0)
    kv_out = [refs.pop(0) for _ in range(4)] if shared_kv else None
    act_ref = refs.pop(0)
    w = _FfnWeights(layer, w_hbm, refs)

    def body(load_weights):
        x = _pick_group(*x_refs) if n_x == 2 else x_refs[0][...]
        if shared_kv:
            _emit_shared_kv(x, *kv_in, *kv_out)
        x1 = _ffn_half_step(x, g_ref, w, act_ref, load_weights)
        x1_ref[...] = x1
        if attn_proj:
            inv_rms = lax.rsqrt(jnp.mean(x1 * x1, axis=-1, keepdims=True) + RMS_EPS)
            hm = (x1 * gm_ref[...]).astype(BF16)
            qa_ref[...] = (_dot(hm, win_ref[:, :ATTN_DIM].astype(BF16)) * (inv_rms * ATTN_Q_SCALE)).astype(BF16)
            qm_ref[...] = (_dot(hm, win_ref[:, ATTN_DIM:].astype(BF16)) * (inv_rms * MEM_Q_SCALE)).astype(BF16)

    _first_step_loads(w, body)


def _ffn(x, g, ffn_w, layer, gm=None, win=None, win_layer=None, shared_kv_w=None):
    attn_proj = win is not None
    shared_kv = shared_kv_w is not None
    xs = list(x) if isinstance(x, tuple) else [x]
    in_specs = ((_group_specs(D_MODEL) if len(xs) == 2 else [_row_spec(D_MODEL)])
                + [_resident((1, D_MODEL))] + _FfnWeights.in_specs())
    args = xs + [g.reshape(1, D_MODEL)] + list(ffn_w)
    out_shape = [jax.ShapeDtypeStruct((ROWS, D_MODEL), F32)]
    out_specs = [_row_spec(D_MODEL)]
    resident = 0
    streamed = (1 + len(xs)) * _ROW_TILE_F32
    if attn_proj:
        in_specs += [_resident((1, D_MODEL)), _resident_layer((D_MODEL, ATTN_DIM + MEM_DIM), win_layer)]
        args += [gm.reshape(1, D_MODEL), win]
        out_shape += [jax.ShapeDtypeStruct((ROWS, ATTN_DIM), BF16), jax.ShapeDtypeStruct((ROWS, MEM_DIM), BF16)]
        out_specs += [_row_spec(ATTN_DIM), _row_spec(MEM_DIM)]
        resident += D_MODEL * (ATTN_DIM + MEM_DIM) * 4
        streamed += TILE_ROWS * (ATTN_DIM + MEM_DIM) * 2
    if shared_kv:
        kv_in_specs, kv_shape, kv_specs, kv_resident, kv_streamed = _shared_kv_specs()
        in_specs += kv_in_specs
        args += [shared_kv_w[0].reshape(1, D_MODEL)] + list(shared_kv_w[1:])
        out_shape += kv_shape
        out_specs += kv_specs
        resident += kv_resident
        streamed += kv_streamed
    return pl.pallas_call(
        functools.partial(_ffn_kernel, n_x=len(xs), attn_proj=attn_proj, shared_kv=shared_kv, layer=layer),
        out_shape=tuple(out_shape),
        grid=(ROW_TILES,),
        in_specs=in_specs,
        out_specs=tuple(out_specs),
        scratch_shapes=[pltpu.VMEM((TILE_ROWS, FFN_DIM), BF16)] + _FfnWeights.scratch_shapes(),
        compiler_params=_params(_vmem_limit(
            resident, streamed, _ACT_BYTES + _FfnWeights.SCRATCH_BYTES, 3 * _ROW_TILE_F32)),
        name="ffn_attn_proj" if attn_proj else "ffn",
    )(*args)


def _inproj_conv_kernel(x_ref, gm_ref, win_ref, cw_ref, pre_ref,
                        ytok_ref, qm_ref, tail_ref, us_ref, shift_ref):
    i = pl.program_id(0)
    hm = _rms(x_ref[...], gm_ref[...]).astype(BF16)
    qm_ref[...] = (_dot(hm, win_ref[:, 3 * CONV_DIM:].astype(BF16)) * MEM_Q_SCALE).astype(BF16)

    def chunk(cc, prompt):
        sl = slice(cc * CONV_CHUNK, (cc + 1) * CONV_CHUNK)
        c_gate = _dot(hm, win_ref[:, CONV_DIM + cc * CONV_CHUNK:CONV_DIM + (cc + 1) * CONV_CHUNK].astype(BF16))
        x_in = _dot(hm, win_ref[:, 2 * CONV_DIM + cc * CONV_CHUNK:2 * CONV_DIM + (cc + 1) * CONV_CHUNK].astype(BF16))
        u = c_gate * x_in
        if prompt:
            shift_ref[V7X_SUBLANES:, sl] = u
            u1 = shift_ref[V7X_SUBLANES - 1:V7X_SUBLANES - 1 + TILE_ROWS, sl]
            u2 = shift_ref[V7X_SUBLANES - 2:V7X_SUBLANES - 2 + TILE_ROWS, sl]
            last = u[TILE_ROWS - V7X_SUBLANES:, :]
            shift_ref[:V7X_SUBLANES, sl] = last
            tail_ref[0, :, sl] = last
        else:
            t = lax.broadcasted_iota(jnp.int32, (TILE_ROWS, CONV_CHUNK), 0) % DEC_SEQ
            p2 = pre_ref[:, sl]
            p1 = pltpu.roll(p2, TILE_ROWS - 1, axis=0)
            u1 = jnp.where(t == 0, p1, pltpu.roll(u, 1, axis=0))
            u2 = jnp.where(t < 2, p2, pltpu.roll(u, 2, axis=0))
            us_ref[:, sl] = u
            tail_ref[0, :, sl] = jnp.zeros((V7X_SUBLANES, CONV_CHUNK), F32)
        w = cw_ref[:, sl]
        conv = w[0:1] * u2 + w[1:2] * u1 + w[2:3] * u
        b_gate = _dot(hm, win_ref[:, sl].astype(BF16))
        ytok_ref[:, sl] = (b_gate * conv).astype(BF16)

    @pl.when(i < PROMPT_TILES)
    def _():
        @pl.when(i % TILES_PER_SEQ == 0)
        def _():
            shift_ref[:V7X_SUBLANES, :] = jnp.zeros((V7X_SUBLANES, CONV_DIM), F32)
        for cc in range(CONV_DIM // CONV_CHUNK):
            chunk(cc, True)

    @pl.when(i >= PROMPT_TILES)
    def _():
        for cc in range(CONV_DIM // CONV_CHUNK):
            chunk(cc, False)


def _inproj_conv(x1, gm, win, conv_w, prefix_rows, layer):
    sample_idx = lambda i: (jnp.maximum(i - PROMPT_TILES, 0), 0)
    win_bytes = D_MODEL * (3 * CONV_DIM + MEM_DIM) * 4
    return pl.pallas_call(
        _inproj_conv_kernel,
        out_shape=(jax.ShapeDtypeStruct((ROWS, CONV_DIM), BF16),
                   jax.ShapeDtypeStruct((ROWS, MEM_DIM), BF16),
                   jax.ShapeDtypeStruct((ROW_TILES, V7X_SUBLANES, CONV_DIM), F32),
                   jax.ShapeDtypeStruct((ROWS_SAMPLE, CONV_DIM), F32)),
        grid=(ROW_TILES,),
        in_specs=[_row_spec(D_MODEL), _resident((1, D_MODEL)),
                  _resident_layer((D_MODEL, 3 * CONV_DIM + MEM_DIM), layer),
                  _resident_layer((CONV_WIDTH, CONV_DIM), layer),
                  pl.BlockSpec((TILE_ROWS, CONV_DIM), sample_idx)],
        out_specs=(_row_spec(CONV_DIM), _row_spec(MEM_DIM),
                   pl.BlockSpec((1, V7X_SUBLANES, CONV_DIM), lambda i: (i, 0, 0)),
                   pl.BlockSpec((TILE_ROWS, CONV_DIM), sample_idx)),
        scratch_shapes=[pltpu.VMEM((TILE_ROWS + V7X_SUBLANES, CONV_DIM), F32)],
        compiler_params=_params(_vmem_limit(
            win_bytes, 3 * _ROW_TILE_F32 + TILE_ROWS * (CONV_DIM + MEM_DIM) * 2,
            _ROW_TILE_F32 + V7X_SUBLANES * CONV_DIM * 4, 6 * _ROW_TILE_F32)),
        name="inproj_conv",
    )(x1, gm.reshape(1, D_MODEL), win, conv_w, prefix_rows)


VZ_WIDTH = N_KV_HEADS * 2 * 2 * V7X_LANES


def _emit_shared_kv(x, g_ref, wk_ref, wv_ref, k_ref, v_ref, ktz_ref, vz_ref):
    hk = _rms(x, g_ref[...]).astype(BF16)
    k = _dot(hk, wk_ref[...].astype(BF16))
    k_ref[...] = k
    v = _dot(hk, wv_ref[...].astype(BF16))
    v_ref[...] = v
    eye = (lax.broadcasted_iota(jnp.int32, (KV_DIM, KV_DIM), 0)
           == lax.broadcasted_iota(jnp.int32, (KV_DIM, KV_DIM), 1)).astype(BF16)
    kt = _dot_nt(eye, k.astype(BF16))
    lo = lax.broadcasted_iota(jnp.int32, (TILE_ROWS, V7X_LANES), 1) < HALF_LANES
    hi = jnp.logical_not(lo)
    ones = (jnp.where(lo, 1.0, 0.0).astype(BF16), jnp.where(lo, 0.0, 1.0).astype(BF16))
    zero_k = jnp.zeros((HEAD_DIM, TILE_ROWS), BF16)
    for pair in range(N_KV_HEADS // 2):
        v_pair = v[:, pair * V7X_LANES:(pair + 1) * V7X_LANES]
        v_swap = pltpu.roll(v_pair, HALF_LANES, axis=1)
        for odd in range(2):
            h = 2 * pair + odd
            kth = kt[h * HEAD_DIM:(h + 1) * HEAD_DIM, :].astype(BF16)
            ktz_ref[h, 0] = jnp.concatenate([kth, zero_k], axis=0)
            ktz_ref[h, 1] = jnp.concatenate([zero_k, kth], axis=0)
            halves = (jnp.where(lo, v_swap if odd else v_pair, 0.0), jnp.where(hi, v_pair if odd else v_swap, 0.0))
            for e in range(2):
                c0 = (h * 2 + e) * 2 * V7X_LANES
                vz_ref[:, c0:c0 + 2 * V7X_LANES] = jnp.concatenate(
                    [halves[e].astype(BF16), ones[e]], axis=1)


def _shared_kv_specs():
    in_specs = [_resident((1, D_MODEL)), _resident((D_MODEL, KV_DIM)), _resident((D_MODEL, KV_DIM))]
    out_shape = [jax.ShapeDtypeStruct((ROWS, KV_DIM), F32), jax.ShapeDtypeStruct((ROWS, KV_DIM), F32),
                 jax.ShapeDtypeStruct((N_KV_HEADS, 2, V7X_LANES, ROWS), BF16),
                 jax.ShapeDtypeStruct((ROWS, VZ_WIDTH), BF16)]
    out_specs = [_row_spec(KV_DIM), _row_spec(KV_DIM),
                 pl.BlockSpec((N_KV_HEADS, 2, V7X_LANES, TILE_ROWS), lambda i: (0, 0, 0, i)),
                 _row_spec(VZ_WIDTH)]
    resident = 2 * D_MODEL * KV_DIM * 4
    streamed = TILE_ROWS * (2 * KV_DIM * 4 + 2 * KV_DIM * 2 + VZ_WIDTH * 2)
    return in_specs, out_shape, out_specs, resident, streamed


def _attn_prompt_kernel(*refs, swa):
    if swa:
        (qm_ref, mk_ref, mve_ref, qa_ref, ktp_ref, ktc_ref, vzp_ref, vzc_ref, bias_ref,
         ymem_ref, ytok_ref) = refs
    else:
        qm_ref, mk_ref, mve_ref, ymem_ref = refs

    head_cols = [slice(h * MEM_HEAD_DIM, (h + 1) * MEM_HEAD_DIM) for h in range(MEM_HEADS)]
    scores = [_dot_nt(qm_ref[:, sl], mk_ref[:, sl]) for sl in head_cols]
    probs = [jnp.exp2(s - jnp.max(s, axis=-1, keepdims=True)).astype(BF16) for s in scores]
    for h, sl in enumerate(head_cols):
        oe = _dot(probs[h], mve_ref[h])
        ymem_ref[:, sl] = (oe[:, :MEM_HEAD_DIM] / oe[:, MEM_HEAD_DIM:]).astype(BF16)
    if not swa:
        return

    no_prev = ((pl.program_id(0) % TILES_PER_SEQ) == 0).astype(jnp.int32)
    key0_col = lax.broadcasted_iota(jnp.int32, (V7X_LANES, WINDOW), 1) == 0
    key0_row = lax.broadcasted_iota(jnp.int32, (WINDOW, V7X_LANES), 0) == 0
    zero_kt = jnp.zeros((V7X_LANES, WINDOW), BF16)
    zero_v = jnp.zeros((WINDOW, V7X_LANES), BF16)

    def rows_of(n):
        return slice(n * WINDOW, (n + 1) * WINDOW)

    @functools.lru_cache(maxsize=None)
    def key_window(n, h, e):
        k_prev = ktp_ref[h, e] if n == 0 else ktc_ref[h, e, :, rows_of(n - 1)]
        k_prev = jnp.where(key0_col, zero_kt, k_prev)
        return jnp.concatenate([k_prev, ktc_ref[h, e, :, rows_of(n)]], axis=1)

    @functools.lru_cache(maxsize=None)
    def value_window(n, h, e):
        c0 = (h * 2 + e) * 2 * V7X_LANES
        cols = slice(c0, c0 + 2 * V7X_LANES)
        v_prev = vzp_ref[:, cols] if n == 0 else vzc_ref[rows_of(n - 1), cols]
        v_prev = jnp.concatenate(
            [jnp.where(key0_row, zero_v, v_prev[:, :V7X_LANES]), v_prev[:, V7X_LANES:]], axis=1)
        return jnp.concatenate([v_prev, vzc_ref[rows_of(n), cols]], axis=0)

    def slab_col(h, pr):
        return (h * (GROUP // 2) + pr) * V7X_LANES

    def head_scores(n, h, pr, e):
        j = h * GROUP + pr * 2 + e
        bias = bias_ref[no_prev, j] if n == 0 else bias_ref[0, j]
        c0 = slab_col(h, pr)
        return _dot(qa_ref[rows_of(n), c0:c0 + V7X_LANES], key_window(n, h, e)) + bias

    heads = [(n, h, pr, e) for n in range(BLOCKS_PER_TILE) for h in range(N_KV_HEADS)
             for pr in range(GROUP // 2) for e in range(2)]
    s_next = head_scores(*heads[0])
    acc = None
    for idx, (n, h, pr, e) in enumerate(heads):
        s = s_next
        if idx + 1 < len(heads):
            s_next = head_scores(*heads[idx + 1])
        p = jnp.exp2(s - jnp.max(s, axis=-1, keepdims=True)).astype(BF16)
        part = _dot(p, value_window(n, h, e))
        if e == 0:
            acc = part
        else:
            acc = acc + part
            c0 = slab_col(h, pr)
            ytok_ref[rows_of(n), c0:c0 + V7X_LANES] = (acc[:, :V7X_LANES] / acc[:, V7X_LANES:]).astype(BF16)


def _attn_prompt(qm, mk_bf, mv_ext, layer, swa_args=None):
    swa = swa_args is not None
    batch_of = lambda i: i // TILES_PER_SEQ
    in_specs = [_row_spec(MEM_DIM),
                pl.BlockSpec((None, None, MEM_TOKENS, MEM_DIM), lambda i: (layer, batch_of(i), 0, 0)),
                pl.BlockSpec((None, None, MEM_HEADS, MEM_TOKENS, 2 * MEM_HEAD_DIM),
                             lambda i: (layer, batch_of(i), 0, 0, 0))]
    args = [qm, mk_bf, mv_ext]
    out_shape = [jax.ShapeDtypeStruct((ROWS_PROMPT, MEM_DIM), BF16)]
    out_specs = [_row_spec(MEM_DIM)]
    streamed = TILE_ROWS * MEM_DIM * 4 + MEM_TOKENS * MEM_DIM * 2 * 3
    resident = 0
    if swa:
        qa, ktz, vz, bias_p, swa_layer = swa_args
        prev_blk = lambda i: jnp.maximum(i * BLOCKS_PER_TILE - 1, 0)
        in_specs += [_row_spec(ATTN_DIM),
                     pl.BlockSpec((N_KV_HEADS, 2, V7X_LANES, WINDOW), lambda i: (0, 0, 0, prev_blk(i))),
                     pl.BlockSpec((N_KV_HEADS, 2, V7X_LANES, TILE_ROWS), lambda i: (0, 0, 0, i)),
                     pl.BlockSpec((WINDOW, VZ_WIDTH), lambda i: (prev_blk(i), 0)),
                     pl.BlockSpec((TILE_ROWS, VZ_WIDTH), lambda i: (i, 0)),
                     _resident_layer((2, N_HEADS, WINDOW, 2 * WINDOW), swa_layer)]
        args += [qa, ktz, ktz, vz, vz, bias_p]
        out_shape += [jax.ShapeDtypeStruct((ROWS_PROMPT, ATTN_DIM), BF16)]
        out_specs += [_row_spec(ATTN_DIM)]
        resident = 2 * N_HEADS * WINDOW * 2 * WINDOW * 4
        streamed += 2 * TILE_ROWS * ATTN_DIM * 2 + (TILE_ROWS + WINDOW) * (2 * KV_DIM + VZ_WIDTH) * 2
    return pl.pallas_call(
        functools.partial(_attn_prompt_kernel, swa=swa),
        out_shape=tuple(out_shape),
        grid=(PROMPT_TILES,),
        in_specs=in_specs,
        out_specs=tuple(out_specs),
        compiler_params=_params(_vmem_limit(resident, streamed, 0, 8 * _ROW_TILE_F32)),
        name="attn_prompt_swa" if swa else "attn_prompt_mem",
    )(*args)


def _attn_sample_kernel(*refs, swa, emit_cache):
    if swa:
        (qm_ref, mk_ref, mv_ref, qa_ref, ck_ref, cv_ref, kn_ref, vn_ref, bias_ref,
         ymem_ref, ytok_ref) = refs[:11]
        if emit_cache:
            kc_out_ref, vc_out_ref = refs[11:]
            keep_cached = lax.broadcasted_iota(jnp.int32, (KV_DIM, WINDOW), 1) < WINDOW - DEC_SEQ
    else:
        qm_ref, mk_ref, mv_ref, ymem_ref = refs

    qm_all = qm_ref[...].astype(F32)
    mem_rows = MEM_HEADS * DEC_SEQ
    own_head = (lax.broadcasted_iota(jnp.int32, (mem_rows, MEM_TOKENS * MEM_HEADS), 1) % MEM_HEADS
                == lax.broadcasted_iota(jnp.int32, (mem_rows, MEM_TOKENS * MEM_HEADS), 0) // DEC_SEQ)
    if swa:
        qa_all = qa_ref[...].astype(F32)
        kn_all = kn_ref[...]
        vn_all = vn_ref[...]
        bias = bias_ref[...]
        key0 = lax.broadcasted_iota(jnp.int32, (KV_DIM, WINDOW), 1) == 0
        pad = jnp.zeros((WINDOW - DEC_SEQ, KV_DIM), F32)
        lo = lax.broadcasted_iota(jnp.int32, (DEC_SEQ, V7X_LANES), 1) < HALF_LANES
        hi = jnp.logical_not(lo)
        zero_slab = jnp.zeros((DEC_SEQ, V7X_LANES), F32)

    batch = range(SAMPLE_BATCH_BLOCK)
    rows_of = [slice(b * DEC_SEQ, (b + 1) * DEC_SEQ) for b in batch]
    q_heads = [jnp.concatenate([qm_all[rows_of[b], h * MEM_HEAD_DIM:(h + 1) * MEM_HEAD_DIM]
                                for h in range(MEM_HEADS)], axis=0) for b in batch]
    scores = [jnp.where(own_head, _dot_nt(q_heads[b], mk_ref[b]), NEG_INF) for b in batch]
    probs = [jnp.exp2(s - jnp.max(s, axis=-1, keepdims=True)) for s in scores]
    outs = [_dot(probs[b], mv_ref[b]) / jnp.sum(probs[b], axis=-1, keepdims=True) for b in batch]
    ymem_rows = [jnp.concatenate([o[h * DEC_SEQ:(h + 1) * DEC_SEQ] for h in range(MEM_HEADS)], axis=1)
                 for o in outs]

    if swa:
        def block_diag_queries(qb):
            groups = []
            for j in range(N_HEADS):
                kvh = j // GROUP
                slab = qb[:, (j // 2) * V7X_LANES:(j // 2 + 1) * V7X_LANES]
                if j % 2 != kvh % 2:
                    slab = pltpu.roll(slab, HALF_LANES, axis=1)
                slab = jnp.where(lo if kvh % 2 == 0 else hi, slab, 0.0)
                groups.append(jnp.concatenate([slab, zero_slab] if kvh // 2 == 0 else [zero_slab, slab], axis=1))
            return jnp.concatenate(groups, axis=0)

        def gather_heads(o_full):
            pairs = []
            for pair in range(N_HEADS // 2):
                acc = None
                for e in range(2):
                    j = pair * 2 + e
                    kvh = j // GROUP
                    slab = o_full[j * DEC_SEQ:(j + 1) * DEC_SEQ, (kvh // 2) * V7X_LANES:(kvh // 2 + 1) * V7X_LANES]
                    if e != kvh % 2:
                        slab = pltpu.roll(slab, HALF_LANES, axis=1)
                    slab = jnp.where(lo if e == 0 else hi, slab, 0.0)
                    acc = slab if acc is None else acc + slab
                pairs.append(acc)
            return jnp.concatenate(pairs, axis=1)

        q_bd = [block_diag_queries(qa_all[rows_of[b]]) for b in batch]
        kt_cache = [jnp.where(key0, 0.0, ck_ref[b]) for b in batch]
        vt_cache = [jnp.where(key0, 0.0, cv_ref[b]) for b in batch]
        k_new = [jnp.concatenate([kn_all[rows_of[b]], pad], axis=0) for b in batch]
        v_new = [jnp.concatenate([vn_all[rows_of[b]], pad], axis=0) for b in batch]
        if emit_cache:
            shift = WINDOW - DEC_SEQ
            for b in batch:
                kc_out_ref[b] = jnp.where(keep_cached, pltpu.roll(ck_ref[b], shift, axis=1),
                                          pltpu.roll(k_new[b].T, shift, axis=1))
                vc_out_ref[b] = jnp.where(keep_cached, pltpu.roll(cv_ref[b], shift, axis=1),
                                          pltpu.roll(v_new[b].T, shift, axis=1))
        scores = [jnp.concatenate([_dot(q_bd[b], kt_cache[b]), _dot_nt(q_bd[b], k_new[b])], axis=1) + bias
                  for b in batch]
        probs = [jnp.exp2(s - jnp.max(s, axis=-1, keepdims=True)) for s in scores]
        o_full = [(_dot_nt(probs[b][:, :WINDOW], vt_cache[b]) + _dot(probs[b][:, WINDOW:], v_new[b]))
                  / jnp.sum(probs[b], axis=-1, keepdims=True) for b in batch]
        ytok_rows = [gather_heads(o) for o in o_full]

    ymem_ref[...] = jnp.concatenate(ymem_rows, axis=0).astype(BF16)
    if swa:
        ytok_ref[...] = jnp.concatenate(ytok_rows, axis=0).astype(BF16)


def _attn_sample(qm, cache_k, cache_v, layer, swa_args=None, emit_cache=False):
    swa = swa_args is not None
    assert swa or not emit_cache
    row0 = ROWS_PROMPT // SAMPLE_BLOCK_ROWS
    blk_rows = lambda width: pl.BlockSpec((SAMPLE_BLOCK_ROWS, width), lambda i: (row0 + i, 0))
    out_rows = lambda width: pl.BlockSpec((SAMPLE_BLOCK_ROWS, width), lambda i: (i, 0))
    cache_spec = pl.BlockSpec((None, SAMPLE_BATCH_BLOCK, MEM_TOKENS * MEM_HEADS, MEM_HEAD_DIM),
                              lambda i: (layer, i, 0, 0))
    in_specs = [blk_rows(MEM_DIM), cache_spec, cache_spec]
    args = [qm, cache_k, cache_v]
    out_shape = [jax.ShapeDtypeStruct((ROWS_SAMPLE, MEM_DIM), BF16)]
    out_specs = [out_rows(MEM_DIM)]
    streamed = 2 * SAMPLE_BATCH_BLOCK * MEM_TOKENS * MEM_DIM * 4 + SAMPLE_BLOCK_ROWS * MEM_DIM * 4
    if swa:
        qa, swa_k, swa_v, k_new, v_new, bias_s, swa_layer = swa_args
        swa_spec = pl.BlockSpec((SAMPLE_BATCH_BLOCK, KV_DIM, WINDOW), lambda i: (i, 0, 0))
        in_specs += [blk_rows(ATTN_DIM), swa_spec, swa_spec, out_rows(KV_DIM), out_rows(KV_DIM),
                     _resident_layer((N_HEADS * DEC_SEQ, SAMPLE_KEYS_PADDED), swa_layer)]
        args += [qa, swa_k, swa_v, k_new, v_new, bias_s]
        out_shape += [jax.ShapeDtypeStruct((ROWS_SAMPLE, ATTN_DIM), BF16)]
        out_specs += [out_rows(ATTN_DIM)]
        streamed += 2 * SAMPLE_BATCH_BLOCK * WINDOW * KV_DIM * 4 + SAMPLE_BLOCK_ROWS * (ATTN_DIM + KV_DIM) * 4
        if emit_cache:
            out_shape += [jax.ShapeDtypeStruct((DEC_BATCH, KV_DIM, WINDOW), F32)] * 2
            out_specs += [swa_spec, swa_spec]
            streamed += 2 * SAMPLE_BATCH_BLOCK * WINDOW * KV_DIM * 4
    return pl.pallas_call(
        functools.partial(_attn_sample_kernel, swa=swa, emit_cache=emit_cache),
        out_shape=tuple(out_shape),
        grid=(DEC_BATCH // SAMPLE_BATCH_BLOCK,),
        in_specs=in_specs,
        out_specs=tuple(out_specs),
        compiler_params=_params(_vmem_limit(1 << 20, streamed, 0, 6 * _ROW_TILE_F32)),
        name="attn_sample_swa" if swa else "attn_sample_mem",
    )(*args)


def _outffn_kernel(*refs, split_tok, final, layer):
    refs = list(refs)
    x1_ref = refs.pop(0)
    tok_refs = [refs.pop(0) for _ in range(2 if split_tok else 1)]
    ymp_ref, yms_ref, wo_ref, g_ref = [refs.pop(0) for _ in range(4)]
    w_hbm = [refs.pop(0) for _ in range(3)]
    gf_ref = refs.pop(0) if final else None
    out_refs = [refs.pop(0) for _ in range(2 if final else 1)]
    act_ref = refs.pop(0)
    w = _FfnWeights(layer, w_hbm, refs)
    tok_dim = wo_ref.shape[0] - MEM_DIM

    def body(load_weights):
        y_tok = _pick_group(*tok_refs) if split_tok else tok_refs[0][...]
        y_mem = _pick_group(ymp_ref, yms_ref)
        x2 = (x1_ref[...] + _dot(y_tok, wo_ref[:tok_dim, :].astype(BF16))
              + _dot(y_mem, wo_ref[tok_dim:, :].astype(BF16)))
        x3 = _ffn_half_step(x2, g_ref, w, act_ref, load_weights)
        if not final:
            out_refs[0][...] = x3
            return
        y = _rms(x3, gf_ref[...])

        @pl.when(pl.program_id(0) < PROMPT_TILES)
        def _():
            out_refs[0][...] = y

        @pl.when(pl.program_id(0) >= PROMPT_TILES)
        def _():
            out_refs[1][...] = y

    _first_step_loads(w, body)


def _outffn(x1, y_tok, ymem_p, ymem_s, wo, wo_layer, g, ffn_w, layer, final_gain=None):
    split_tok = isinstance(y_tok, tuple)
    final = final_gain is not None
    tok_dim = wo.shape[1] - MEM_DIM
    in_specs = [_row_spec(D_MODEL)]
    args = [x1]
    if split_tok:
        in_specs += _group_specs(tok_dim)
        args += list(y_tok)
    else:
        in_specs += [_row_spec(tok_dim)]
        args += [y_tok]
    in_specs += (_group_specs(MEM_DIM) + [_resident_layer((tok_dim + MEM_DIM, D_MODEL), wo_layer)]
                 + [_resident((1, D_MODEL))] + _FfnWeights.in_specs())
    args += [ymem_p, ymem_s, wo, g.reshape(1, D_MODEL)] + list(ffn_w)
    if final:
        in_specs += [_resident((1, D_MODEL))]
        args += [final_gain.reshape(1, D_MODEL)]
    resident = (tok_dim + MEM_DIM) * D_MODEL * 4
    streamed = 2 * _ROW_TILE_F32 + 2 * TILE_ROWS * (tok_dim + MEM_DIM) * 2
    if final:
        out_shape = [jax.ShapeDtypeStruct((ROWS_PROMPT, D_MODEL), F32),
                     jax.ShapeDtypeStruct((ROWS_SAMPLE, D_MODEL), F32)]
        out_specs = _group_specs(D_MODEL)
        streamed += _ROW_TILE_F32
    else:
        out_shape = [jax.ShapeDtypeStruct((ROWS, D_MODEL), F32)]
        out_specs = [_row_spec(D_MODEL)]
    return pl.pallas_call(
        functools.partial(_outffn_kernel, split_tok=split_tok, final=final, layer=layer),
        out_shape=tuple(out_shape),
        grid=(ROW_TILES,),
        in_specs=in_specs,
        out_specs=tuple(out_specs),
        scratch_shapes=[pltpu.VMEM((TILE_ROWS, FFN_DIM), BF16)] + _FfnWeights.scratch_shapes(),
        compiler_params=_params(_vmem_limit(
            resident, streamed, _ACT_BYTES + _FfnWeights.SCRATCH_BYTES, 3 * _ROW_TILE_F32)),
        name="outffn_final" if final else "outffn",
    )(*args)


def kernel(x_prompt, x_sample, state_conv, cache_swa_k, cache_swa_v, cache_mem_k, cache_mem_v, mem_prompt, ffn1_norm, ffn1_wg, ffn1_wu, ffn1_wd, mix_norm, w_in_a, conv_w, w_out_a, kv_norm, w_kv, w_in_b, attn_sinks, rel_bias, w_out_b, mem_norm, w_mem_kv, ffn2_norm, ffn2_wg, ffn2_wu, ffn2_wd, final_norm):
    ffn1 = (ffn1_wg, ffn1_wu, ffn1_wd)
    ffn2 = (ffn2_wg, ffn2_wu, ffn2_wd)
    wk = w_kv[:, :KV_DIM]
    wv = w_kv[:, KV_DIM:]

    mem_k, mem_v, mem_k_bf, mem_v_ext = _memkv(mem_prompt, mem_norm, w_mem_kv)
    bias_p, bias_s = _bias_tables(rel_bias, attn_sinks)
    swa_k_cache = cache_swa_k.transpose(0, 2, 3, 1).reshape(DEC_BATCH, KV_DIM, WINDOW)
    swa_v_cache = cache_swa_v.transpose(0, 2, 3, 1).reshape(DEC_BATCH, KV_DIM, WINDOW)
    mem_rows_shape = (DEPTH, DEC_BATCH, MEM_TOKENS * MEM_HEADS, MEM_HEAD_DIM)
    cache_k = cache_mem_k.reshape(mem_rows_shape)
    cache_v = cache_mem_v.reshape(mem_rows_shape)

    x = (x_prompt.reshape(ROWS_PROMPT, D_MODEL), x_sample.reshape(ROWS_SAMPLE, D_MODEL))
    tails, sample_us = [], []
    k_rows = v_rows = ktz = vz = k_new = v_new = None
    for l in range(DEPTH):
        last = l == DEPTH - 1
        if l < N_A_LAYERS:
            (x1,) = _ffn(x, ffn1_norm[l], ffn1, l)
            prefix_rows = jnp.pad(state_conv[l], ((0, 0), (0, DEC_SEQ - (CONV_WIDTH - 1)), (0, 0)))
            y_tok, qm, tail, us = _inproj_conv(x1, mix_norm[l], w_in_a, conv_w,
                                               prefix_rows.reshape(ROWS_SAMPLE, CONV_DIM), l)
            tails.append(tail)
            sample_us.append(us)
            (ymem_p,) = _attn_prompt(qm, mem_k_bf, mem_v_ext, l)
            (ymem_s,) = _attn_sample(qm, cache_k, cache_v, l)
            wo, wo_layer = w_out_a, l
        else:
            j = l - N_A_LAYERS
            if j == 0:
                x1, qa, qm, k_rows, v_rows, ktz, vz = _ffn(
                    x, ffn1_norm[l], ffn1, l, mix_norm[l], w_in_b, j,
                    shared_kv_w=(kv_norm, wk, wv))
                k_new = k_rows[ROWS_PROMPT:]
                v_new = v_rows[ROWS_PROMPT:]
            else:
                x1, qa, qm = _ffn(x, ffn1_norm[l], ffn1, l, mix_norm[l], w_in_b, j)
            ymem_p, ytok_p = _attn_prompt(qm, mem_k_bf, mem_v_ext, l, (qa, ktz, vz, bias_p, j))
            sample_out = _attn_sample(qm, cache_k, cache_v, l,
                                      (qa, swa_k_cache, swa_v_cache, k_new, v_new, bias_s, j), emit_cache=j == 0)
            ymem_s, ytok_s = sample_out[:2]
            if j == 0:
                new_k_cache, new_v_cache = sample_out[2:]
            y_tok = (ytok_p, ytok_s)
            wo, wo_layer = w_out_b, j
        if not last:
            (x,) = _outffn(x1, y_tok, ymem_p, ymem_s, wo, wo_layer, ffn2_norm[l], ffn2, l)
        else:
            y_prompt, y_sample = _outffn(x1, y_tok, ymem_p, ymem_s, wo, wo_layer, ffn2_norm[l], ffn2, l,
                                         final_gain=final_norm)

    keep = CONV_WIDTH - 1
    last_tiles = np.arange(BATCH) * TILES_PER_SEQ + TILES_PER_SEQ - 1
    conv_state_prompt = jnp.stack([t[last_tiles, V7X_SUBLANES - keep:, :] for t in tails])
    conv_state_sample = jnp.stack([u.reshape(DEC_BATCH, DEC_SEQ, CONV_DIM)[:, DEC_SEQ - keep:, :] for u in sample_us])
    k_tail = jnp.stack([k_rows[(b + 1) * SEQ - WINDOW:(b + 1) * SEQ] for b in range(BATCH)])
    v_tail = jnp.stack([v_rows[(b + 1) * SEQ - WINDOW:(b + 1) * SEQ] for b in range(BATCH)])
    swa_k_prompt = k_tail.reshape(BATCH, WINDOW, N_KV_HEADS, HEAD_DIM)
    swa_v_prompt = v_tail.reshape(BATCH, WINDOW, N_KV_HEADS, HEAD_DIM)
    to_cache_layout = lambda c: c.reshape(DEC_BATCH, N_KV_HEADS, HEAD_DIM, WINDOW).transpose(0, 3, 1, 2)
    swa_k_sample = to_cache_layout(new_k_cache)
    swa_v_sample = to_cache_layout(new_v_cache)
    mem_shape = (DEPTH, BATCH, MEM_TOKENS, MEM_HEADS, MEM_HEAD_DIM)
    return (y_prompt.reshape(BATCH, SEQ, D_MODEL), y_sample.reshape(DEC_BATCH, DEC_SEQ, D_MODEL),
            conv_state_prompt, conv_state_sample,
            swa_k_prompt, swa_v_prompt, swa_k_sample, swa_v_sample,
            mem_k.reshape(mem_shape), mem_v.reshape(mem_shape))
```

```python
import functools
import math

import numpy as np
import jax
import jax.numpy as jnp
from jax import lax
from jax.experimental import pallas as pl
from jax.experimental.pallas import tpu as pltpu

D_MODEL = 1024
BATCH = 2
SEQ = 8192
DEPTH = 4
DEC_BATCH = 128
DEC_SEQ = 8
N_A_LAYERS = DEPTH // 2
FFN_DIM = 2816
CONV_DIM = D_MODEL
CONV_WIDTH = 3
N_HEADS = 16
N_KV_HEADS = 4
HEAD_DIM = 64
GROUP = N_HEADS // N_KV_HEADS
ATTN_DIM = N_HEADS * HEAD_DIM
KV_DIM = N_KV_HEADS * HEAD_DIM
WINDOW = 128
REL_BUCKETS = 32
REL_MAX_DIST = 128
MEM_TOKENS = 256
MEM_HEADS = 4
MEM_HEAD_DIM = 128
MEM_DIM = MEM_HEADS * MEM_HEAD_DIM
RMS_EPS = 1e-5

F32 = jnp.float32
BF16 = jnp.bfloat16
NEG_INF = float("-inf")

V7X_LANES = 128
V7X_SUBLANES = 8
V7X_MXU_DIM = 256
V7X_VMEM_BYTES = 64 * 1024 * 1024

ROWS_PROMPT = BATCH * SEQ
ROWS_SAMPLE = DEC_BATCH * DEC_SEQ
ROWS = ROWS_PROMPT + ROWS_SAMPLE
TILE_ROWS = 512
PROMPT_TILES = ROWS_PROMPT // TILE_ROWS
SAMPLE_TILES = ROWS_SAMPLE // TILE_ROWS
ROW_TILES = PROMPT_TILES + SAMPLE_TILES
TILES_PER_SEQ = SEQ // TILE_ROWS
FFN_CHUNK = V7X_MXU_DIM
CONV_CHUNK = V7X_MXU_DIM
BLOCKS_PER_TILE = TILE_ROWS // WINDOW
SAMPLE_BATCH_BLOCK = 8
SAMPLE_BLOCK_ROWS = SAMPLE_BATCH_BLOCK * DEC_SEQ
SAMPLE_KEYS = WINDOW + DEC_SEQ
SAMPLE_KEYS_PADDED = 2 * WINDOW
HALF_LANES = V7X_LANES // 2
LOG2E = math.log2(math.e)
MEM_Q_SCALE = MEM_HEAD_DIM ** -0.5 * LOG2E
ATTN_Q_SCALE = HEAD_DIM ** -0.5 * LOG2E

assert HEAD_DIM == HALF_LANES and MEM_HEAD_DIM == V7X_LANES
assert ROWS_PROMPT % TILE_ROWS == 0 and ROWS_SAMPLE % TILE_ROWS == 0 and SEQ % TILE_ROWS == 0
assert FFN_DIM % FFN_CHUNK == 0 and TILE_ROWS % WINDOW == 0


MIN_SCOPED_VMEM_BYTES = 48 * 1024 * 1024
SETUP_CALL_VMEM_BYTES = 32 * 1024 * 1024


def _vmem_limit(resident_bytes, streamed_bytes, scratch_bytes, temp_bytes):
    need = resident_bytes + 2 * streamed_bytes + scratch_bytes + temp_bytes
    assert need < V7X_VMEM_BYTES, need
    return max(int(need), MIN_SCOPED_VMEM_BYTES)


def _params(vmem_bytes, n_axes=1):
    return pltpu.CompilerParams(
        dimension_semantics=("arbitrary",) * n_axes, vmem_limit_bytes=vmem_bytes)


def _resident(shape):
    zeros = (0,) * len(shape)
    return pl.BlockSpec(shape, lambda *_: zeros, pipeline_mode=pl.Buffered(1))


def _resident_layer(shape, layer):
    idx = (layer,) + (0,) * len(shape)
    return pl.BlockSpec((None,) + tuple(shape), lambda *_: idx, pipeline_mode=pl.Buffered(1))


def _row_spec(width):
    return pl.BlockSpec((TILE_ROWS, width), lambda i: (i, 0))


def _group_specs(width):
    return [pl.BlockSpec((TILE_ROWS, width), lambda i: (jnp.minimum(i, PROMPT_TILES - 1), 0)),
            pl.BlockSpec((TILE_ROWS, width), lambda i: (jnp.maximum(i - PROMPT_TILES, 0), 0))]


def _pick_group(prompt_ref, sample_ref):
    return jnp.where(pl.program_id(0) < PROMPT_TILES, prompt_ref[...], sample_ref[...])


def _dot(a, b):
    return jnp.dot(a, b, preferred_element_type=F32)


def _dot_nt(a, b):
    return lax.dot_general(a, b, (((1,), (1,)), ((), ())), preferred_element_type=F32)


def _rms(x, g):
    return x * lax.rsqrt(jnp.mean(x * x, axis=-1, keepdims=True) + RMS_EPS) * g


FFN_CHUNKS = FFN_DIM // FFN_CHUNK
FFN_STAGE_SLOTS = 2


class _FfnWeights:
    def __init__(self, layer, hbm_refs, scratch_refs):
        self.layer = layer
        self.wg_hbm, self.wu_hbm, self.wd_hbm = hbm_refs
        self.wg, self.wu, self.wd, self.stage_in, self.stage_out, self.sems = scratch_refs

    @staticmethod
    def in_specs():
        return [pl.BlockSpec(memory_space=pl.ANY)] * 3

    @staticmethod
    def scratch_shapes():
        return [pltpu.VMEM((D_MODEL, FFN_DIM), BF16), pltpu.VMEM((D_MODEL, FFN_DIM), BF16),
                pltpu.VMEM((FFN_DIM, D_MODEL), BF16),
                pltpu.VMEM((2, FFN_STAGE_SLOTS, D_MODEL, FFN_CHUNK), F32),
                pltpu.VMEM((FFN_STAGE_SLOTS, FFN_CHUNK, D_MODEL), F32),
                pltpu.SemaphoreType.DMA((3, FFN_STAGE_SLOTS))]

    SCRATCH_BYTES = 3 * D_MODEL * FFN_DIM * 2 + 3 * FFN_STAGE_SLOTS * D_MODEL * FFN_CHUNK * 4

    def _copy(self, stream, c):
        slot = c % FFN_STAGE_SLOTS
        cols = pl.ds(c * FFN_CHUNK, FFN_CHUNK)
        if stream == 0:
            src, dst = self.wg_hbm.at[self.layer, :, cols], self.stage_in.at[0, slot]
        elif stream == 1:
            src, dst = self.wu_hbm.at[self.layer, :, cols], self.stage_in.at[1, slot]
        else:
            src, dst = self.wd_hbm.at[self.layer, cols, :], self.stage_out.at[slot]
        return pltpu.make_async_copy(src, dst, self.sems.at[stream, slot])

    def prime(self):
        for stream in range(3):
            for c in range(FFN_STAGE_SLOTS):
                self._copy(stream, c).start()

    def fetch(self, c):
        slot = c % FFN_STAGE_SLOTS
        sl = slice(c * FFN_CHUNK, (c + 1) * FFN_CHUNK)
        for stream in range(3):
            self._copy(stream, c).wait()
            if stream == 0:
                self.wg[:, sl] = self.stage_in[0, slot].astype(BF16)
            elif stream == 1:
                self.wu[:, sl] = self.stage_in[1, slot].astype(BF16)
            else:
                self.wd[sl, :] = self.stage_out[slot].astype(BF16)
            if c + FFN_STAGE_SLOTS < FFN_CHUNKS:
                self._copy(stream, c + FFN_STAGE_SLOTS).start()


def _ffn_half_step(x, g_ref, w, act_ref, load_weights):
    inv_rms = lax.rsqrt(jnp.mean(x * x, axis=-1, keepdims=True) + RMS_EPS)
    inv_rms_chunk = jnp.broadcast_to(inv_rms, (x.shape[0], FFN_CHUNK))
    h = (x * g_ref[...]).astype(BF16)
    for c in range(FFN_CHUNKS):
        if load_weights:
            w.fetch(c)
        sl = slice(c * FFN_CHUNK, (c + 1) * FFN_CHUNK)
        gate = _dot(h, w.wg[:, sl]) * inv_rms_chunk
        up = _dot(h, w.wu[:, sl])
        act_ref[:, sl] = (gate / (1.0 + jnp.exp(-gate)) * up).astype(BF16)
    return x + (0.5 * inv_rms) * _dot(act_ref[...], w.wd[...])


def _first_step_loads(w, body):
    @pl.when(pl.program_id(0) == 0)
    def _():
        w.prime()
        body(True)

    @pl.when(pl.program_id(0) > 0)
    def _():
        body(False)


_ROW_TILE_F32 = TILE_ROWS * D_MODEL * 4
_ACT_BYTES = TILE_ROWS * FFN_DIM * 2


def _t5_bucket_np(dist):
    n = np.maximum(dist, 0)
    exact = REL_BUCKETS // 2
    nf = np.maximum(n, 1).astype(np.float32)
    large = exact + (np.log(nf / np.float32(exact)) / np.float32(math.log(REL_MAX_DIST / exact))
                     * np.float32(REL_BUCKETS - exact)).astype(np.int32)
    large = np.minimum(large, REL_BUCKETS - 1)
    return np.where(n < exact, n, large).astype(np.int32)


def _bucket_tables():
    q = np.arange(WINDOW)[:, None]
    k = np.arange(2 * WINDOW)[None, :]
    dist = WINDOW + q - k
    prompt = np.where((dist >= 0) & (dist < WINDOW), _t5_bucket_np(dist), -1)
    t = (np.arange(N_HEADS * DEC_SEQ) % DEC_SEQ)[:, None]
    k = np.arange(SAMPLE_KEYS_PADDED)[None, :]
    dist = WINDOW + t - k
    ok = (dist >= 0) & (dist < WINDOW) & (k < SAMPLE_KEYS)
    sample = np.where(ok, _t5_bucket_np(dist), -1)
    return prompt.astype(np.int32), sample.astype(np.int32)


def _bias_kernel(rel_ref, sink_ref, bp_ref, bs_ref, op_ref, os_ref):
    layer = pl.program_id(0)
    j = pl.program_id(1)
    sink = sink_ref[layer, j] * LOG2E

    def build(bucket):
        acc = jnp.zeros(bucket.shape, F32)
        for b in range(REL_BUCKETS):
            acc = jnp.where(bucket == b, rel_ref[b, j], acc)
        return jnp.where(bucket < 0, NEG_INF, acc * LOG2E)

    col_p = lax.broadcasted_iota(jnp.int32, (WINDOW, 2 * WINDOW), 1)
    table = build(bp_ref[...])
    op_ref[0, 0, 0] = jnp.where(col_p == 0, sink, table)
    op_ref[0, 1, 0] = jnp.where(col_p == 0, sink, jnp.where(col_p < WINDOW, NEG_INF, table))
    col_s = lax.broadcasted_iota(jnp.int32, (DEC_SEQ, SAMPLE_KEYS_PADDED), 1)
    os_ref[0] = jnp.where(col_s == 0, sink, build(bs_ref[...]))


def _bias_tables(rel_bias, attn_sinks):
    bp, bs = _bucket_tables()
    n_layers = attn_sinks.shape[0]
    return pl.pallas_call(
        _bias_kernel,
        out_shape=(jax.ShapeDtypeStruct((n_layers, 2, N_HEADS, WINDOW, 2 * WINDOW), F32),
                   jax.ShapeDtypeStruct((n_layers, N_HEADS * DEC_SEQ, SAMPLE_KEYS_PADDED), F32)),
        grid=(n_layers, N_HEADS),
        in_specs=[pl.BlockSpec(memory_space=pltpu.SMEM), pl.BlockSpec(memory_space=pltpu.SMEM),
                  pl.BlockSpec((WINDOW, 2 * WINDOW), lambda l, j: (0, 0)),
                  pl.BlockSpec((DEC_SEQ, SAMPLE_KEYS_PADDED), lambda l, j: (j, 0))],
        out_specs=(pl.BlockSpec((1, 2, 1, WINDOW, 2 * WINDOW), lambda l, j: (l, 0, j, 0, 0)),
                   pl.BlockSpec((1, DEC_SEQ, SAMPLE_KEYS_PADDED), lambda l, j: (l, j, 0))),
        compiler_params=_params(SETUP_CALL_VMEM_BYTES, 2),
        name="bias_tables",
    )(rel_bias, attn_sinks, jnp.asarray(bp), jnp.asarray(bs))


def _memkv_kernel(m_ref, g_ref, w_ref, k_ref, v_ref, kb_ref, ve_ref):
    hn = _rms(m_ref[...], g_ref[...]).astype(BF16)
    kv = _dot(hn, w_ref[...].astype(BF16))
    k = kv[:, :MEM_DIM]
    v = kv[:, MEM_DIM:]
    k_ref[...] = k
    v_ref[...] = v
    kb_ref[...] = k.astype(BF16)
    ones = jnp.ones((MEM_TOKENS, MEM_HEAD_DIM), BF16)
    for h in range(MEM_HEADS):
        vh = v[:, h * MEM_HEAD_DIM:(h + 1) * MEM_HEAD_DIM].astype(BF16)
        ve_ref[h] = jnp.concatenate([vh, ones], axis=1)


def _memkv(mem_prompt, mem_norm, w_mem_kv):
    shp = (DEPTH, BATCH, MEM_TOKENS, MEM_DIM)
    blk = pl.BlockSpec((None, None, MEM_TOKENS, MEM_DIM), lambda l, b: (l, b, 0, 0))
    return pl.pallas_call(
        _memkv_kernel,
        out_shape=(jax.ShapeDtypeStruct(shp, F32), jax.ShapeDtypeStruct(shp, F32),
                   jax.ShapeDtypeStruct(shp, BF16),
                   jax.ShapeDtypeStruct((DEPTH, BATCH, MEM_HEADS, MEM_TOKENS, 2 * MEM_HEAD_DIM), BF16)),
        grid=(DEPTH, BATCH),
        in_specs=[pl.BlockSpec((None, MEM_TOKENS, D_MODEL), lambda l, b: (b, 0, 0)),
                  pl.BlockSpec((None, 1, D_MODEL), lambda l, b: (l, 0, 0)),
                  pl.BlockSpec((None, D_MODEL, 2 * MEM_DIM), lambda l, b: (l, 0, 0))],
        out_specs=(blk, blk, blk,
                   pl.BlockSpec((None, None, MEM_HEADS, MEM_TOKENS, 2 * MEM_HEAD_DIM),
                                lambda l, b: (l, b, 0, 0, 0))),
        compiler_params=_params(SETUP_CALL_VMEM_BYTES, 2),
        name="memkv",
    )(mem_prompt, mem_norm.reshape(DEPTH, 1, D_MODEL), w_mem_kv)


def _ffn_kernel(*refs, n_x, attn_proj, shared_kv, layer):
    refs = list(refs)
    x_refs = [refs.pop(0) for _ in range(n_x)]
    g_ref = refs.pop(0)
    w_hbm = [refs.pop(0) for _ in range(3)]
    if attn_proj:
        gm_ref, win_ref = refs.pop(0), refs.pop(0)
    kv_in = [refs.pop(0) for _ in range(3)] if shared_kv else None
    x1_ref = refs.pop(0)
    if attn_proj:
        qa_ref, qm_ref = refs.pop(0), refs.pop(0)
    kv_out = [refs.pop(0) for _ in range(4)] if shared_kv else None
    act_ref = refs.pop(0)
    w = _FfnWeights(layer, w_hbm, refs)

    def body(load_weights):
        x = _pick_group(*x_refs) if n_x == 2 else x_refs[0][...]
        if shared_kv:
            _emit_shared_kv(x, *kv_in, *kv_out)
        x1 = _ffn_half_step(x, g_ref, w, act_ref, load_weights)
        x1_ref[...] = x1
        if attn_proj:
            inv_rms = lax.rsqrt(jnp.mean(x1 * x1, axis=-1, keepdims=True) + RMS_EPS)
            hm = (x1 * gm_ref[...]).astype(BF16)
            qa_ref[...] = (_dot(hm, win_ref[:, :ATTN_DIM].astype(BF16)) * (inv_rms * ATTN_Q_SCALE)).astype(BF16)
            qm_ref[...] = (_dot(hm, win_ref[:, ATTN_DIM:].astype(BF16)) * (inv_rms * MEM_Q_SCALE)).astype(BF16)

    _first_step_loads(w, body)


def _ffn(x, g, ffn_w, layer, gm=None, win=None, win_layer=None, shared_kv_w=None):
    attn_proj = win is not None
    shared_kv = shared_kv_w is not None
    xs = list(x) if isinstance(x, tuple) else [x]
    in_specs = ((_group_specs(D_MODEL) if len(xs) == 2 else [_row_spec(D_MODEL)])
                + [_resident((1, D_MODEL))] + _FfnWeights.in_specs())
    args = xs + [g.reshape(1, D_MODEL)] + list(ffn_w)
    out_shape = [jax.ShapeDtypeStruct((ROWS, D_MODEL), F32)]
    out_specs = [_row_spec(D_MODEL)]
    resident = 0
    streamed = (1 + len(xs)) * _ROW_TILE_F32
    if attn_proj:
        in_specs += [_resident((1, D_MODEL)), _resident_layer((D_MODEL, ATTN_DIM + MEM_DIM), win_layer)]
        args += [gm.reshape(1, D_MODEL), win]
        out_shape += [jax.ShapeDtypeStruct((ROWS, ATTN_DIM), BF16), jax.ShapeDtypeStruct((ROWS, MEM_DIM), BF16)]
        out_specs += [_row_spec(ATTN_DIM), _row_spec(MEM_DIM)]
        resident += D_MODEL * (ATTN_DIM + MEM_DIM) * 4
        streamed += TILE_ROWS * (ATTN_DIM + MEM_DIM) * 2
    if shared_kv:
        kv_in_specs, kv_shape, kv_specs, kv_resident, kv_streamed = _shared_kv_specs()
        in_specs += kv_in_specs
        args += [shared_kv_w[0].reshape(1, D_MODEL)] + list(shared_kv_w[1:])
        out_shape += kv_shape
        out_specs += kv_specs
        resident += kv_resident
        streamed += kv_streamed
    return pl.pallas_call(
        functools.partial(_ffn_kernel, n_x=len(xs), attn_proj=attn_proj, shared_kv=shared_kv, layer=layer),
        out_shape=tuple(out_shape),
        grid=(ROW_TILES,),
        in_specs=in_specs,
        out_specs=tuple(out_specs),
        scratch_shapes=[pltpu.VMEM((TILE_ROWS, FFN_DIM), BF16)] + _FfnWeights.scratch_shapes(),
        compiler_params=_params(_vmem_limit(
            resident, streamed, _ACT_BYTES + _FfnWeights.SCRATCH_BYTES, 3 * _ROW_TILE_F32)),
        name="ffn_attn_proj" if attn_proj else "ffn",
    )(*args)


def _inproj_conv_kernel(x_ref, gm_ref, win_ref, cw_ref, pre_ref,
                        ytok_ref, qm_ref, tail_ref, us_ref, shift_ref):
    i = pl.program_id(0)
    hm = _rms(x_ref[...], gm_ref[...]).astype(BF16)
    qm_ref[...] = (_dot(hm, win_ref[:, 3 * CONV_DIM:].astype(BF16)) * MEM_Q_SCALE).astype(BF16)

    def chunk(cc, prompt):
        sl = slice(cc * CONV_CHUNK, (cc + 1) * CONV_CHUNK)
        c_gate = _dot(hm, win_ref[:, CONV_DIM + cc * CONV_CHUNK:CONV_DIM + (cc + 1) * CONV_CHUNK].astype(BF16))
        x_in = _dot(hm, win_ref[:, 2 * CONV_DIM + cc * CONV_CHUNK:2 * CONV_DIM + (cc + 1) * CONV_CHUNK].astype(BF16))
        u = c_gate * x_in
        if prompt:
            shift_ref[V7X_SUBLANES:, sl] = u
            u1 = shift_ref[V7X_SUBLANES - 1:V7X_SUBLANES - 1 + TILE_ROWS, sl]
            u2 = shift_ref[V7X_SUBLANES - 2:V7X_SUBLANES - 2 + TILE_ROWS, sl]
            last = u[TILE_ROWS - V7X_SUBLANES:, :]
            shift_ref[:V7X_SUBLANES, sl] = last
            tail_ref[0, :, sl] = last
        else:
            t = lax.broadcasted_iota(jnp.int32, (TILE_ROWS, CONV_CHUNK), 0) % DEC_SEQ
            p2 = pre_ref[:, sl]
            p1 = pltpu.roll(p2, TILE_ROWS - 1, axis=0)
            u1 = jnp.where(t == 0, p1, pltpu.roll(u, 1, axis=0))
            u2 = jnp.where(t < 2, p2, pltpu.roll(u, 2, axis=0))
            us_ref[:, sl] = u
            tail_ref[0, :, sl] = jnp.zeros((V7X_SUBLANES, CONV_CHUNK), F32)
        w = cw_ref[:, sl]
        conv = w[0:1] * u2 + w[1:2] * u1 + w[2:3] * u
        b_gate = _dot(hm, win_ref[:, sl].astype(BF16))
        ytok_ref[:, sl] = (b_gate * conv).astype(BF16)

    @pl.when(i < PROMPT_TILES)
    def _():
        @pl.when(i % TILES_PER_SEQ == 0)
        def _():
            shift_ref[:V7X_SUBLANES, :] = jnp.zeros((V7X_SUBLANES, CONV_DIM), F32)
        for cc in range(CONV_DIM // CONV_CHUNK):
            chunk(cc, True)

    @pl.when(i >= PROMPT_TILES)
    def _():
        for cc in range(CONV_DIM // CONV_CHUNK):
            chunk(cc, False)


def _inproj_conv(x1, gm, win, conv_w, prefix_rows, layer):
    sample_idx = lambda i: (jnp.maximum(i - PROMPT_TILES, 0), 0)
    win_bytes = D_MODEL * (3 * CONV_DIM + MEM_DIM) * 4
    return pl.pallas_call(
        _inproj_conv_kernel,
        out_shape=(jax.ShapeDtypeStruct((ROWS, CONV_DIM), BF16),
                   jax.ShapeDtypeStruct((ROWS, MEM_DIM), BF16),
                   jax.ShapeDtypeStruct((ROW_TILES, V7X_SUBLANES, CONV_DIM), F32),
                   jax.ShapeDtypeStruct((ROWS_SAMPLE, CONV_DIM), F32)),
        grid=(ROW_TILES,),
        in_specs=[_row_spec(D_MODEL), _resident((1, D_MODEL)),
                  _resident_layer((D_MODEL, 3 * CONV_DIM + MEM_DIM), layer),
                  _resident_layer((CONV_WIDTH, CONV_DIM), layer),
                  pl.BlockSpec((TILE_ROWS, CONV_DIM), sample_idx)],
        out_specs=(_row_spec(CONV_DIM), _row_spec(MEM_DIM),
                   pl.BlockSpec((1, V7X_SUBLANES, CONV_DIM), lambda i: (i, 0, 0)),
                   pl.BlockSpec((TILE_ROWS, CONV_DIM), sample_idx)),
        scratch_shapes=[pltpu.VMEM((TILE_ROWS + V7X_SUBLANES, CONV_DIM), F32)],
        compiler_params=_params(_vmem_limit(
            win_bytes, 3 * _ROW_TILE_F32 + TILE_ROWS * (CONV_DIM + MEM_DIM) * 2,
            _ROW_TILE_F32 + V7X_SUBLANES * CONV_DIM * 4, 6 * _ROW_TILE_F32)),
        name="inproj_conv",
    )(x1, gm.reshape(1, D_MODEL), win, conv_w, prefix_rows)


VZ_WIDTH = N_KV_HEADS * 2 * 2 * V7X_LANES


def _emit_shared_kv(x, g_ref, wk_ref, wv_ref, k_ref, v_ref, ktz_ref, vz_ref):
    hk = _rms(x, g_ref[...]).astype(BF16)
    k = _dot(hk, wk_ref[...].astype(BF16))
    k_ref[...] = k
    v = _dot(hk, wv_ref[...].astype(BF16))
    v_ref[...] = v
    eye = (lax.broadcasted_iota(jnp.int32, (KV_DIM, KV_DIM), 0)
           == lax.broadcasted_iota(jnp.int32, (KV_DIM, KV_DIM), 1)).astype(BF16)
    kt = _dot_nt(eye, k.astype(BF16))
    lo = lax.broadcasted_iota(jnp.int32, (TILE_ROWS, V7X_LANES), 1) < HALF_LANES
    hi = jnp.logical_not(lo)
    ones = (jnp.where(lo, 1.0, 0.0).astype(BF16), jnp.where(lo, 0.0, 1.0).astype(BF16))
    zero_k = jnp.zeros((HEAD_DIM, TILE_ROWS), BF16)
    for pair in range(N_KV_HEADS // 2):
        v_pair = v[:, pair * V7X_LANES:(pair + 1) * V7X_LANES]
        v_swap = pltpu.roll(v_pair, HALF_LANES, axis=1)
        for odd in range(2):
            h = 2 * pair + odd
            kth = kt[h * HEAD_DIM:(h + 1) * HEAD_DIM, :].astype(BF16)
            ktz_ref[h, 0] = jnp.concatenate([kth, zero_k], axis=0)
            ktz_ref[h, 1] = jnp.concatenate([zero_k, kth], axis=0)
            halves = (jnp.where(lo, v_swap if odd else v_pair, 0.0), jnp.where(hi, v_pair if odd else v_swap, 0.0))
            for e in range(2):
                c0 = (h * 2 + e) * 2 * V7X_LANES
                vz_ref[:, c0:c0 + 2 * V7X_LANES] = jnp.concatenate(
                    [halves[e].astype(BF16), ones[e]], axis=1)


def _shared_kv_specs():
    in_specs = [_resident((1, D_MODEL)), _resident((D_MODEL, KV_DIM)), _resident((D_MODEL, KV_DIM))]
    out_shape = [jax.ShapeDtypeStruct((ROWS, KV_DIM), F32), jax.ShapeDtypeStruct((ROWS, KV_DIM), F32),
                 jax.ShapeDtypeStruct((N_KV_HEADS, 2, V7X_LANES, ROWS), BF16),
                 jax.ShapeDtypeStruct((ROWS, VZ_WIDTH), BF16)]
    out_specs = [_row_spec(KV_DIM), _row_spec(KV_DIM),
                 pl.BlockSpec((N_KV_HEADS, 2, V7X_LANES, TILE_ROWS), lambda i: (0, 0, 0, i)),
                 _row_spec(VZ_WIDTH)]
    resident = 2 * D_MODEL * KV_DIM * 4
    streamed = TILE_ROWS * (2 * KV_DIM * 4 + 2 * KV_DIM * 2 + VZ_WIDTH * 2)
    return in_specs, out_shape, out_specs, resident, streamed


def _attn_prompt_kernel(*refs, swa):
    if swa:
        (qm_ref, mk_ref, mve_ref, qa_ref, ktp_ref, ktc_ref, vzp_ref, vzc_ref, bias_ref,
         ymem_ref, ytok_ref) = refs
    else:
        qm_ref, mk_ref, mve_ref, ymem_ref = refs

    head_cols = [slice(h * MEM_HEAD_DIM, (h + 1) * MEM_HEAD_DIM) for h in range(MEM_HEADS)]
    scores = [_dot_nt(qm_ref[:, sl], mk_ref[:, sl]) for sl in head_cols]
    probs = [jnp.exp2(s - jnp.max(s, axis=-1, keepdims=True)).astype(BF16) for s in scores]
    for h, sl in enumerate(head_cols):
        oe = _dot(probs[h], mve_ref[h])
        ymem_ref[:, sl] = (oe[:, :MEM_HEAD_DIM] / oe[:, MEM_HEAD_DIM:]).astype(BF16)
    if not swa:
        return

    no_prev = ((pl.program_id(0) % TILES_PER_SEQ) == 0).astype(jnp.int32)
    key0_col = lax.broadcasted_iota(jnp.int32, (V7X_LANES, WINDOW), 1) == 0
    key0_row = lax.broadcasted_iota(jnp.int32, (WINDOW, V7X_LANES), 0) == 0
    zero_kt = jnp.zeros((V7X_LANES, WINDOW), BF16)
    zero_v = jnp.zeros((WINDOW, V7X_LANES), BF16)

    def rows_of(n):
        return slice(n * WINDOW, (n + 1) * WINDOW)

    @functools.lru_cache(maxsize=None)
    def key_window(n, h, e):
        k_prev = ktp_ref[h, e] if n == 0 else ktc_ref[h, e, :, rows_of(n - 1)]
        k_prev = jnp.where(key0_col, zero_kt, k_prev)
        return jnp.concatenate([k_prev, ktc_ref[h, e, :, rows_of(n)]], axis=1)

    @functools.lru_cache(maxsize=None)
    def value_window(n, h, e):
        c0 = (h * 2 + e) * 2 * V7X_LANES
        cols = slice(c0, c0 + 2 * V7X_LANES)
        v_prev = vzp_ref[:, cols] if n == 0 else vzc_ref[rows_of(n - 1), cols]
        v_prev = jnp.concatenate(
            [jnp.where(key0_row, zero_v, v_prev[:, :V7X_LANES]), v_prev[:, V7X_LANES:]], axis=1)
        return jnp.concatenate([v_prev, vzc_ref[rows_of(n), cols]], axis=0)

    def slab_col(h, pr):
        return (h * (GROUP // 2) + pr) * V7X_LANES

    def head_scores(n, h, pr, e):
        j = h * GROUP + pr * 2 + e
        bias = bias_ref[no_prev, j] if n == 0 else bias_ref[0, j]
        c0 = slab_col(h, pr)
        return _dot(qa_ref[rows_of(n), c0:c0 + V7X_LANES], key_window(n, h, e)) + bias

    heads = [(n, h, pr, e) for n in range(BLOCKS_PER_TILE) for h in range(N_KV_HEADS)
             for pr in range(GROUP // 2) for e in range(2)]
    s_next = head_scores(*heads[0])
    acc = None
    for idx, (n, h, pr, e) in enumerate(heads):
        s = s_next
        if idx + 1 < len(heads):
            s_next = head_scores(*heads[idx + 1])
        p = jnp.exp2(s - jnp.max(s, axis=-1, keepdims=True)).astype(BF16)
        part = _dot(p, value_window(n, h, e))
        if e == 0:
            acc = part
        else:
            acc = acc + part
            c0 = slab_col(h, pr)
            ytok_ref[rows_of(n), c0:c0 + V7X_LANES] = (acc[:, :V7X_LANES] / acc[:, V7X_LANES:]).astype(BF16)


def _attn_prompt(qm, mk_bf, mv_ext, layer, swa_args=None):
    swa = swa_args is not None
    batch_of = lambda i: i // TILES_PER_SEQ
    in_specs = [_row_spec(MEM_DIM),
                pl.BlockSpec((None, None, MEM_TOKENS, MEM_DIM), lambda i: (layer, batch_of(i), 0, 0)),
                pl.BlockSpec((None, None, MEM_HEADS, MEM_TOKENS, 2 * MEM_HEAD_DIM),
                             lambda i: (layer, batch_of(i), 0, 0, 0))]
    args = [qm, mk_bf, mv_ext]
    out_shape = [jax.ShapeDtypeStruct((ROWS_PROMPT, MEM_DIM), BF16)]
    out_specs = [_row_spec(MEM_DIM)]
    streamed = TILE_ROWS * MEM_DIM * 4 + MEM_TOKENS * MEM_DIM * 2 * 3
    resident = 0
    if swa:
        qa, ktz, vz, bias_p, swa_layer = swa_args
        prev_blk = lambda i: jnp.maximum(i * BLOCKS_PER_TILE - 1, 0)
        in_specs += [_row_spec(ATTN_DIM),
                     pl.BlockSpec((N_KV_HEADS, 2, V7X_LANES, WINDOW), lambda i: (0, 0, 0, prev_blk(i))),
                     pl.BlockSpec((N_KV_HEADS, 2, V7X_LANES, TILE_ROWS), lambda i: (0, 0, 0, i)),
                     pl.BlockSpec((WINDOW, VZ_WIDTH), lambda i: (prev_blk(i), 0)),
                     pl.BlockSpec((TILE_ROWS, VZ_WIDTH), lambda i: (i, 0)),
                     _resident_layer((2, N_HEADS, WINDOW, 2 * WINDOW), swa_layer)]
        args += [qa, ktz, ktz, vz, vz, bias_p]
        out_shape += [jax.ShapeDtypeStruct((ROWS_PROMPT, ATTN_DIM), BF16)]
        out_specs += [_row_spec(ATTN_DIM)]
        resident = 2 * N_HEADS * WINDOW * 2 * WINDOW * 4
        streamed += 2 * TILE_ROWS * ATTN_DIM * 2 + (TILE_ROWS + WINDOW) * (2 * KV_DIM + VZ_WIDTH) * 2
    return pl.pallas_call(
        functools.partial(_attn_prompt_kernel, swa=swa),
        out_shape=tuple(out_shape),
        grid=(PROMPT_TILES,),
        in_specs=in_specs,
        out_specs=tuple(out_specs),
        compiler_params=_params(_vmem_limit(resident, streamed, 0, 8 * _ROW_TILE_F32)),
        name="attn_prompt_swa" if swa else "attn_prompt_mem",
    )(*args)


def _attn_sample_kernel(*refs, swa, emit_cache):
    if swa:
        (qm_ref, mk_ref, mv_ref, qa_ref, ck_ref, cv_ref, kn_ref, vn_ref, bias_ref,
         ymem_ref, ytok_ref) = refs[:11]
        if emit_cache:
            kc_out_ref, vc_out_ref = refs[11:]
            keep_cached = lax.broadcasted_iota(jnp.int32, (KV_DIM, WINDOW), 1) < WINDOW - DEC_SEQ
    else:
        qm_ref, mk_ref, mv_ref, ymem_ref = refs

    qm_all = qm_ref[...].astype(F32)
    mem_rows = MEM_HEADS * DEC_SEQ
    own_head = (lax.broadcasted_iota(jnp.int32, (mem_rows, MEM_TOKENS * MEM_HEADS), 1) % MEM_HEADS
                == lax.broadcasted_iota(jnp.int32, (mem_rows, MEM_TOKENS * MEM_HEADS), 0) // DEC_SEQ)
    if swa:
        qa_all = qa_ref[...].astype(F32)
        kn_all = kn_ref[...]
        vn_all = vn_ref[...]
        bias = bias_ref[...]
        key0 = lax.broadcasted_iota(jnp.int32, (KV_DIM, WINDOW), 1) == 0
        pad = jnp.zeros((WINDOW - DEC_SEQ, KV_DIM), F32)
        lo = lax.broadcasted_iota(jnp.int32, (DEC_SEQ, V7X_LANES), 1) < HALF_LANES
        hi = jnp.logical_not(lo)
        zero_slab = jnp.zeros((DEC_SEQ, V7X_LANES), F32)

    batch = range(SAMPLE_BATCH_BLOCK)
    rows_of = [slice(b * DEC_SEQ, (b + 1) * DEC_SEQ) for b in batch]
    q_heads = [jnp.concatenate([qm_all[rows_of[b], h * MEM_HEAD_DIM:(h + 1) * MEM_HEAD_DIM]
                                for h in range(MEM_HEADS)], axis=0) for b in batch]
    scores = [jnp.where(own_head, _dot_nt(q_heads[b], mk_ref[b]), NEG_INF) for b in batch]
    probs = [jnp.exp2(s - jnp.max(s, axis=-1, keepdims=True)) for s in scores]
    outs = [_dot(probs[b], mv_ref[b]) / jnp.sum(probs[b], axis=-1, keepdims=True) for b in batch]
    ymem_rows = [jnp.concatenate([o[h * DEC_SEQ:(h + 1) * DEC_SEQ] for h in range(MEM_HEADS)], axis=1)
                 for o in outs]

    if swa:
        def block_diag_queries(qb):
            groups = []
            for j in range(N_HEADS):
                kvh = j // GROUP
                slab = qb[:, (j // 2) * V7X_LANES:(j // 2 + 1) * V7X_LANES]
                if j % 2 != kvh % 2:
                    slab = pltpu.roll(slab, HALF_LANES, axis=1)
                slab = jnp.where(lo if kvh % 2 == 0 else hi, slab, 0.0)
                groups.append(jnp.concatenate([slab, zero_slab] if kvh // 2 == 0 else [zero_slab, slab], axis=1))
            return jnp.concatenate(groups, axis=0)

        def gather_heads(o_full):
            pairs = []
            for pair in range(N_HEADS // 2):
                acc = None
                for e in range(2):
                    j = pair * 2 + e
                    kvh = j // GROUP
                    slab = o_full[j * DEC_SEQ:(j + 1) * DEC_SEQ, (kvh // 2) * V7X_LANES:(kvh // 2 + 1) * V7X_LANES]
                    if e != kvh % 2:
                        slab = pltpu.roll(slab, HALF_LANES, axis=1)
                    slab = jnp.where(lo if e == 0 else hi, slab, 0.0)
                    acc = slab if acc is None else acc + slab
                pairs.append(acc)
            return jnp.concatenate(pairs, axis=1)

        q_bd = [block_diag_queries(qa_all[rows_of[b]]) for b in batch]
        kt_cache = [jnp.where(key0, 0.0, ck_ref[b]) for b in batch]
        vt_cache = [jnp.where(key0, 0.0, cv_ref[b]) for b in batch]
        k_new = [jnp.concatenate([kn_all[rows_of[b]], pad], axis=0) for b in batch]
        v_new = [jnp.concatenate([vn_all[rows_of[b]], pad], axis=0) for b in batch]
        if emit_cache:
            shift = WINDOW - DEC_SEQ
            block_pad = jnp.zeros((WINDOW - SAMPLE_BLOCK_ROWS, KV_DIM), F32)
            k_new_t = jnp.concatenate([kn_all, block_pad], axis=0).T
            v_new_t = jnp.concatenate([vn_all, block_pad], axis=0).T
            for b in batch:
                kc_out_ref[b] = jnp.where(keep_cached, pltpu.roll(ck_ref[b], shift, axis=1),
                                          pltpu.roll(k_new_t, shift - b * DEC_SEQ, axis=1))
                vc_out_ref[b] = jnp.where(keep_cached, pltpu.roll(cv_ref[b], shift, axis=1),
                                          pltpu.roll(v_new_t, shift - b * DEC_SEQ, axis=1))
        scores = [jnp.concatenate([_dot(q_bd[b], kt_cache[b]), _dot_nt(q_bd[b], k_new[b])], axis=1) + bias
                  for b in batch]
        probs = [jnp.exp2(s - jnp.max(s, axis=-1, keepdims=True)) for s in scores]
        o_full = [(_dot_nt(probs[b][:, :WINDOW], vt_cache[b]) + _dot(probs[b][:, WINDOW:], v_new[b]))
                  / jnp.sum(probs[b], axis=-1, keepdims=True) for b in batch]
        ytok_rows = [gather_heads(o) for o in o_full]

    ymem_ref[...] = jnp.concatenate(ymem_rows, axis=0).astype(BF16)
    if swa:
        ytok_ref[...] = jnp.concatenate(ytok_rows, axis=0).astype(BF16)


def _attn_sample(qm, cache_k, cache_v, layer, swa_args=None, emit_cache=False):
    swa = swa_args is not None
    assert swa or not emit_cache
    row0 = ROWS_PROMPT // SAMPLE_BLOCK_ROWS
    blk_rows = lambda width: pl.BlockSpec((SAMPLE_BLOCK_ROWS, width), lambda i: (row0 + i, 0))
    out_rows = lambda width: pl.BlockSpec((SAMPLE_BLOCK_ROWS, width), lambda i: (i, 0))
    cache_spec = pl.BlockSpec((None, SAMPLE_BATCH_BLOCK, MEM_TOKENS * MEM_HEADS, MEM_HEAD_DIM),
                              lambda i: (layer, i, 0, 0))
    in_specs = [blk_rows(MEM_DIM), cache_spec, cache_spec]
    args = [qm, cache_k, cache_v]
    out_shape = [jax.ShapeDtypeStruct((ROWS_SAMPLE, MEM_DIM), BF16)]
    out_specs = [out_rows(MEM_DIM)]
    streamed = 2 * SAMPLE_BATCH_BLOCK * MEM_TOKENS * MEM_DIM * 4 + SAMPLE_BLOCK_ROWS * MEM_DIM * 4
    if swa:
        qa, swa_k, swa_v, k_new, v_new, bias_s, swa_layer = swa_args
        swa_spec = pl.BlockSpec((SAMPLE_BATCH_BLOCK, KV_DIM, WINDOW), lambda i: (i, 0, 0))
        in_specs += [blk_rows(ATTN_DIM), swa_spec, swa_spec, out_rows(KV_DIM), out_rows(KV_DIM),
                     _resident_layer((N_HEADS * DEC_SEQ, SAMPLE_KEYS_PADDED), swa_layer)]
        args += [qa, swa_k, swa_v, k_new, v_new, bias_s]
        out_shape += [jax.ShapeDtypeStruct((ROWS_SAMPLE, ATTN_DIM), BF16)]
        out_specs += [out_rows(ATTN_DIM)]
        streamed += 2 * SAMPLE_BATCH_BLOCK * WINDOW * KV_DIM * 4 + SAMPLE_BLOCK_ROWS * (ATTN_DIM + KV_DIM) * 4
        if emit_cache:
            out_shape += [jax.ShapeDtypeStruct((DEC_BATCH, KV_DIM, WINDOW), F32)] * 2
            out_specs += [swa_spec, swa_spec]
            streamed += 2 * SAMPLE_BATCH_BLOCK * WINDOW * KV_DIM * 4
    return pl.pallas_call(
        functools.partial(_attn_sample_kernel, swa=swa, emit_cache=emit_cache),
        out_shape=tuple(out_shape),
        grid=(DEC_BATCH // SAMPLE_BATCH_BLOCK,),
        in_specs=in_specs,
        out_specs=tuple(out_specs),
        compiler_params=_params(_vmem_limit(1 << 20, streamed, 0, 6 * _ROW_TILE_F32)),
        name="attn_sample_swa" if swa else "attn_sample_mem",
    )(*args)


def _outffn_kernel(*refs, split_tok, final, layer):
    refs = list(refs)
    x1_ref = refs.pop(0)
    tok_refs = [refs.pop(0) for _ in range(2 if split_tok else 1)]
    ymp_ref, yms_ref, wo_ref, g_ref = [refs.pop(0) for _ in range(4)]
    w_hbm = [refs.pop(0) for _ in range(3)]
    gf_ref = refs.pop(0) if final else None
    out_refs = [refs.pop(0) for _ in range(2 if final else 1)]
    act_ref = refs.pop(0)
    w = _FfnWeights(layer, w_hbm, refs)
    tok_dim = wo_ref.shape[0] - MEM_DIM

    def body(load_weights):
        y_tok = _pick_group(*tok_refs) if split_tok else tok_refs[0][...]
        y_mem = _pick_group(ymp_ref, yms_ref)
        x2 = (x1_ref[...] + _dot(y_tok, wo_ref[:tok_dim, :].astype(BF16))
              + _dot(y_mem, wo_ref[tok_dim:, :].astype(BF16)))
        x3 = _ffn_half_step(x2, g_ref, w, act_ref, load_weights)
        if not final:
            out_refs[0][...] = x3
            return
        y = _rms(x3, gf_ref[...])

        @pl.when(pl.program_id(0) < PROMPT_TILES)
        def _():
            out_refs[0][...] = y

        @pl.when(pl.program_id(0) >= PROMPT_TILES)
        def _():
            out_refs[1][...] = y

    _first_step_loads(w, body)


def _outffn(x1, y_tok, ymem_p, ymem_s, wo, wo_layer, g, ffn_w, layer, final_gain=None):
    split_tok = isinstance(y_tok, tuple)
    final = final_gain is not None
    tok_dim = wo.shape[1] - MEM_DIM
    in_specs = [_row_spec(D_MODEL)]
    args = [x1]
    if split_tok:
        in_specs += _group_specs(tok_dim)
        args += list(y_tok)
    else:
        in_specs += [_row_spec(tok_dim)]
        args += [y_tok]
    in_specs += (_group_specs(MEM_DIM) + [_resident_layer((tok_dim + MEM_DIM, D_MODEL), wo_layer)]
                 + [_resident((1, D_MODEL))] + _FfnWeights.in_specs())
    args += [ymem_p, ymem_s, wo, g.reshape(1, D_MODEL)] + list(ffn_w)
    if final:
        in_specs += [_resident((1, D_MODEL))]
        args += [final_gain.reshape(1, D_MODEL)]
    resident = (tok_dim + MEM_DIM) * D_MODEL * 4
    streamed = 2 * _ROW_TILE_F32 + 2 * TILE_ROWS * (tok_dim + MEM_DIM) * 2
    if final:
        out_shape = [jax.ShapeDtypeStruct((ROWS_PROMPT, D_MODEL), F32),
                     jax.ShapeDtypeStruct((ROWS_SAMPLE, D_MODEL), F32)]
        out_specs = _group_specs(D_MODEL)
        streamed += _ROW_TILE_F32
    else:
        out_shape = [jax.ShapeDtypeStruct((ROWS, D_MODEL), F32)]
        out_specs = [_row_spec(D_MODEL)]
    return pl.pallas_call(
        functools.partial(_outffn_kernel, split_tok=split_tok, final=final, layer=layer),
        out_shape=tuple(out_shape),
        grid=(ROW_TILES,),
        in_specs=in_specs,
        out_specs=tuple(out_specs),
        scratch_shapes=[pltpu.VMEM((TILE_ROWS, FFN_DIM), BF16)] + _FfnWeights.scratch_shapes(),
        compiler_params=_params(_vmem_limit(
            resident, streamed, _ACT_BYTES + _FfnWeights.SCRATCH_BYTES, 3 * _ROW_TILE_F32)),
        name="outffn_final" if final else "outffn",
    )(*args)


def kernel(x_prompt, x_sample, state_conv, cache_swa_k, cache_swa_v, cache_mem_k, cache_mem_v, mem_prompt, ffn1_norm, ffn1_wg, ffn1_wu, ffn1_wd, mix_norm, w_in_a, conv_w, w_out_a, kv_norm, w_kv, w_in_b, attn_sinks, rel_bias, w_out_b, mem_norm, w_mem_kv, ffn2_norm, ffn2_wg, ffn2_wu, ffn2_wd, final_norm):
    ffn1 = (ffn1_wg, ffn1_wu, ffn1_wd)
    ffn2 = (ffn2_wg, ffn2_wu, ffn2_wd)
    wk = w_kv[:, :KV_DIM]
    wv = w_kv[:, KV_DIM:]

    mem_k, mem_v, mem_k_bf, mem_v_ext = _memkv(mem_prompt, mem_norm, w_mem_kv)
    bias_p, bias_s = _bias_tables(rel_bias, attn_sinks)
    swa_k_cache = cache_swa_k.transpose(0, 2, 3, 1).reshape(DEC_BATCH, KV_DIM, WINDOW)
    swa_v_cache = cache_swa_v.transpose(0, 2, 3, 1).reshape(DEC_BATCH, KV_DIM, WINDOW)
    mem_rows_shape = (DEPTH, DEC_BATCH, MEM_TOKENS * MEM_HEADS, MEM_HEAD_DIM)
    cache_k = cache_mem_k.reshape(mem_rows_shape)
    cache_v = cache_mem_v.reshape(mem_rows_shape)

    x = (x_prompt.reshape(ROWS_PROMPT, D_MODEL), x_sample.reshape(ROWS_SAMPLE, D_MODEL))
    tails, sample_us = [], []
    k_rows = v_rows = ktz = vz = k_new = v_new = None
    for l in range(DEPTH):
        last = l == DEPTH - 1
        if l < N_A_LAYERS:
            (x1,) = _ffn(x, ffn1_norm[l], ffn1, l)
            prefix_rows = jnp.pad(state_conv[l], ((0, 0), (0, DEC_SEQ - (CONV_WIDTH - 1)), (0, 0)))
            y_tok, qm, tail, us = _inproj_conv(x1, mix_norm[l], w_in_a, conv_w,
                                               prefix_rows.reshape(ROWS_SAMPLE, CONV_DIM), l)
            tails.append(tail)
            sample_us.append(us)
            (ymem_p,) = _attn_prompt(qm, mem_k_bf, mem_v_ext, l)
            (ymem_s,) = _attn_sample(qm, cache_k, cache_v, l)
            wo, wo_layer = w_out_a, l
        else:
            j = l - N_A_LAYERS
            if j == 0:
                x1, qa, qm, k_rows, v_rows, ktz, vz = _ffn(
                    x, ffn1_norm[l], ffn1, l, mix_norm[l], w_in_b, j,
                    shared_kv_w=(kv_norm, wk, wv))
                k_new = k_rows[ROWS_PROMPT:]
                v_new = v_rows[ROWS_PROMPT:]
            else:
                x1, qa, qm = _ffn(x, ffn1_norm[l], ffn1, l, mix_norm[l], w_in_b, j)
            ymem_p, ytok_p = _attn_prompt(qm, mem_k_bf, mem_v_ext, l, (qa, ktz, vz, bias_p, j))
            sample_out = _attn_sample(qm, cache_k, cache_v, l,
                                      (qa, swa_k_cache, swa_v_cache, k_new, v_new, bias_s, j), emit_cache=j == 0)
            ymem_s, ytok_s = sample_out[:2]
            if j == 0:
                new_k_cache, new_v_cache = sample_out[2:]
            y_tok = (ytok_p, ytok_s)
            wo, wo_layer = w_out_b, j
        if not last:
            (x,) = _outffn(x1, y_tok, ymem_p, ymem_s, wo, wo_layer, ffn2_norm[l], ffn2, l)
        else:
            y_prompt, y_sample = _outffn(x1, y_tok, ymem_p, ymem_s, wo, wo_layer, ffn2_norm[l], ffn2, l,
                                         final_gain=final_norm)

    keep = CONV_WIDTH - 1
    last_tiles = np.arange(BATCH) * TILES_PER_SEQ + TILES_PER_SEQ - 1
    conv_state_prompt = jnp.stack([t[last_tiles, V7X_SUBLANES - keep:, :] for t in tails])
    conv_state_sample = jnp.stack([u.reshape(DEC_BATCH, DEC_SEQ, CONV_DIM)[:, DEC_SEQ - keep:, :] for u in sample_us])
    k_tail = jnp.stack([k_rows[(b + 1) * SEQ - WINDOW:(b + 1) * SEQ] for b in range(BATCH)])
    v_tail = jnp.stack([v_rows[(b + 1) * SEQ - WINDOW:(b + 1) * SEQ] for b in range(BATCH)])
    swa_k_prompt = k_tail.reshape(BATCH, WINDOW, N_KV_HEADS, HEAD_DIM)
    swa_v_prompt = v_tail.reshape(BATCH, WINDOW, N_KV_HEADS, HEAD_DIM)
    to_cache_layout = lambda c: c.reshape(DEC_BATCH, N_KV_HEADS, HEAD_DIM, WINDOW).transpose(0, 3, 1, 2)
    swa_k_sample = to_cache_layout(new_k_cache)
    swa_v_sample = to_cache_layout(new_v_cache)
    mem_shape = (DEPTH, BATCH, MEM_TOKENS, MEM_HEADS, MEM_HEAD_DIM)
    return (y_prompt.reshape(BATCH, SEQ, D_MODEL), y_sample.reshape(DEC_BATCH, DEC_SEQ, D_MODEL),
            conv_state_prompt, conv_state_sample,
            swa_k_prompt, swa_v_prompt, swa_k_sample, swa_v_sample,
            mem_k.reshape(mem_shape), mem_v.reshape(mem_shape))
```

```python
import functools
import math

import numpy as np
import jax
import jax.numpy as jnp
from jax import lax
from jax.experimental import pallas as pl
from jax.experimental.pallas import tpu as pltpu

D_MODEL = 1024
BATCH = 2
SEQ = 8192
DEPTH = 4
DEC_BATCH = 128
DEC_SEQ = 8
N_A_LAYERS = DEPTH // 2
FFN_DIM = 2816
CONV_DIM = D_MODEL
CONV_WIDTH = 3
N_HEADS = 16
N_KV_HEADS = 4
HEAD_DIM = 64
GROUP = N_HEADS // N_KV_HEADS
ATTN_DIM = N_HEADS * HEAD_DIM
KV_DIM = N_KV_HEADS * HEAD_DIM
WINDOW = 128
REL_BUCKETS = 32
REL_MAX_DIST = 128
MEM_TOKENS = 256
MEM_HEADS = 4
MEM_HEAD_DIM = 128
MEM_DIM = MEM_HEADS * MEM_HEAD_DIM
RMS_EPS = 1e-5

F32 = jnp.float32
BF16 = jnp.bfloat16
NEG_INF = float("-inf")

V7X_LANES = 128
V7X_SUBLANES = 8
V7X_MXU_DIM = 256
V7X_VMEM_BYTES = 64 * 1024 * 1024

ROWS_PROMPT = BATCH * SEQ
ROWS_SAMPLE = DEC_BATCH * DEC_SEQ
ROWS = ROWS_PROMPT + ROWS_SAMPLE
TILE_ROWS = 512
PROMPT_TILES = ROWS_PROMPT // TILE_ROWS
SAMPLE_TILES = ROWS_SAMPLE // TILE_ROWS
ROW_TILES = PROMPT_TILES + SAMPLE_TILES
TILES_PER_SEQ = SEQ // TILE_ROWS
FFN_CHUNK = V7X_MXU_DIM
CONV_CHUNK = V7X_MXU_DIM
ATTN_TILE_ROWS = 1024
ATTN_PROMPT_TILES = ROWS_PROMPT // ATTN_TILE_ROWS
ATTN_TILES_PER_SEQ = SEQ // ATTN_TILE_ROWS
ATTN_BLOCKS_PER_TILE = ATTN_TILE_ROWS // WINDOW
SAMPLE_BATCH_BLOCK = 8
SAMPLE_BLOCK_ROWS = SAMPLE_BATCH_BLOCK * DEC_SEQ
SAMPLE_KEYS = WINDOW + DEC_SEQ
SAMPLE_KEYS_PADDED = 2 * WINDOW
HALF_LANES = V7X_LANES // 2
LOG2E = math.log2(math.e)
MEM_Q_SCALE = MEM_HEAD_DIM ** -0.5 * LOG2E
ATTN_Q_SCALE = HEAD_DIM ** -0.5 * LOG2E

assert HEAD_DIM == HALF_LANES and MEM_HEAD_DIM == V7X_LANES
assert ROWS_PROMPT % TILE_ROWS == 0 and ROWS_SAMPLE % TILE_ROWS == 0 and SEQ % TILE_ROWS == 0
assert FFN_DIM % FFN_CHUNK == 0 and ATTN_TILE_ROWS % TILE_ROWS == 0 and SEQ % ATTN_TILE_ROWS == 0


MIN_SCOPED_VMEM_BYTES = 48 * 1024 * 1024
SETUP_CALL_VMEM_BYTES = 32 * 1024 * 1024


def _vmem_limit(resident_bytes, streamed_bytes, scratch_bytes, temp_bytes):
    need = resident_bytes + 2 * streamed_bytes + scratch_bytes + temp_bytes
    assert need < V7X_VMEM_BYTES, need
    return max(int(need), MIN_SCOPED_VMEM_BYTES)


def _params(vmem_bytes, n_axes=1):
    return pltpu.CompilerParams(
        dimension_semantics=("arbitrary",) * n_axes, vmem_limit_bytes=vmem_bytes)


def _resident(shape):
    zeros = (0,) * len(shape)
    return pl.BlockSpec(shape, lambda *_: zeros, pipeline_mode=pl.Buffered(1))


def _resident_layer(shape, layer):
    idx = (layer,) + (0,) * len(shape)
    return pl.BlockSpec((None,) + tuple(shape), lambda *_: idx, pipeline_mode=pl.Buffered(1))


def _row_spec(width):
    return pl.BlockSpec((TILE_ROWS, width), lambda i: (i, 0))


def _group_specs(width):
    return [pl.BlockSpec((TILE_ROWS, width), lambda i: (jnp.minimum(i, PROMPT_TILES - 1), 0)),
            pl.BlockSpec((TILE_ROWS, width), lambda i: (jnp.maximum(i - PROMPT_TILES, 0), 0))]


def _pick_group(prompt_ref, sample_ref):
    return jnp.where(pl.program_id(0) < PROMPT_TILES, prompt_ref[...], sample_ref[...])


def _dot(a, b):
    return jnp.dot(a, b, preferred_element_type=F32)


def _dot_nt(a, b):
    return lax.dot_general(a, b, (((1,), (1,)), ((), ())), preferred_element_type=F32)


def _rms(x, g):
    return x * lax.rsqrt(jnp.mean(x * x, axis=-1, keepdims=True) + RMS_EPS) * g


FFN_CHUNKS = FFN_DIM // FFN_CHUNK
FFN_STAGE_SLOTS = 2


class _FfnWeights:
    def __init__(self, layer, hbm_refs, scratch_refs):
        self.layer = layer
        self.wg_hbm, self.wu_hbm, self.wd_hbm = hbm_refs
        self.wg, self.wu, self.wd, self.stage_in, self.stage_out, self.sems = scratch_refs

    @staticmethod
    def in_specs():
        return [pl.BlockSpec(memory_space=pl.ANY)] * 3

    @staticmethod
    def scratch_shapes():
        return [pltpu.VMEM((D_MODEL, FFN_DIM), BF16), pltpu.VMEM((D_MODEL, FFN_DIM), BF16),
                pltpu.VMEM((FFN_DIM, D_MODEL), BF16),
                pltpu.VMEM((2, FFN_STAGE_SLOTS, D_MODEL, FFN_CHUNK), F32),
                pltpu.VMEM((FFN_STAGE_SLOTS, FFN_CHUNK, D_MODEL), F32),
                pltpu.SemaphoreType.DMA((3, FFN_STAGE_SLOTS))]

    SCRATCH_BYTES = 3 * D_MODEL * FFN_DIM * 2 + 3 * FFN_STAGE_SLOTS * D_MODEL * FFN_CHUNK * 4

    def _copy(self, stream, c):
        slot = c % FFN_STAGE_SLOTS
        cols = pl.ds(c * FFN_CHUNK, FFN_CHUNK)
        if stream == 0:
            src, dst = self.wg_hbm.at[self.layer, :, cols], self.stage_in.at[0, slot]
        elif stream == 1:
            src, dst = self.wu_hbm.at[self.layer, :, cols], self.stage_in.at[1, slot]
        else:
            src, dst = self.wd_hbm.at[self.layer, cols, :], self.stage_out.at[slot]
        return pltpu.make_async_copy(src, dst, self.sems.at[stream, slot])

    def prime(self):
        for stream in range(3):
            for c in range(FFN_STAGE_SLOTS):
                self._copy(stream, c).start()

    def fetch(self, c):
        slot = c % FFN_STAGE_SLOTS
        sl = slice(c * FFN_CHUNK, (c + 1) * FFN_CHUNK)
        for stream in range(3):
            self._copy(stream, c).wait()
            if stream == 0:
                self.wg[:, sl] = self.stage_in[0, slot].astype(BF16)
            elif stream == 1:
                self.wu[:, sl] = self.stage_in[1, slot].astype(BF16)
            else:
                self.wd[sl, :] = self.stage_out[slot].astype(BF16)
            if c + FFN_STAGE_SLOTS < FFN_CHUNKS:
                self._copy(stream, c + FFN_STAGE_SLOTS).start()


def _ffn_half_step(x, g_ref, w, act_ref, load_weights):
    inv_rms = lax.rsqrt(jnp.mean(x * x, axis=-1, keepdims=True) + RMS_EPS)
    inv_rms_chunk = jnp.broadcast_to(inv_rms, (x.shape[0], FFN_CHUNK))
    h = (x * g_ref[...]).astype(BF16)
    for c in range(FFN_CHUNKS):
        if load_weights:
            w.fetch(c)
        sl = slice(c * FFN_CHUNK, (c + 1) * FFN_CHUNK)
        gate = _dot(h, w.wg[:, sl]) * inv_rms_chunk
        up = _dot(h, w.wu[:, sl])
        act_ref[:, sl] = (gate / (1.0 + jnp.exp(-gate)) * up).astype(BF16)
    return x + (0.5 * inv_rms) * _dot(act_ref[...], w.wd[...])


def _first_step_loads(w, body):
    @pl.when(pl.program_id(0) == 0)
    def _():
        w.prime()
        body(True)

    @pl.when(pl.program_id(0) > 0)
    def _():
        body(False)


_ROW_TILE_F32 = TILE_ROWS * D_MODEL * 4
_ACT_BYTES = TILE_ROWS * FFN_DIM * 2


def _t5_bucket_np(dist):
    n = np.maximum(dist, 0)
    exact = REL_BUCKETS // 2
    nf = np.maximum(n, 1).astype(np.float32)
    large = exact + (np.log(nf / np.float32(exact)) / np.float32(math.log(REL_MAX_DIST / exact))
                     * np.float32(REL_BUCKETS - exact)).astype(np.int32)
    large = np.minimum(large, REL_BUCKETS - 1)
    return np.where(n < exact, n, large).astype(np.int32)


def _bucket_tables():
    q = np.arange(WINDOW)[:, None]
    k = np.arange(2 * WINDOW)[None, :]
    dist = WINDOW + q - k
    prompt = np.where((dist >= 0) & (dist < WINDOW), _t5_bucket_np(dist), -1)
    t = (np.arange(N_HEADS * DEC_SEQ) % DEC_SEQ)[:, None]
    k = np.arange(SAMPLE_KEYS_PADDED)[None, :]
    dist = WINDOW + t - k
    ok = (dist >= 0) & (dist < WINDOW) & (k < SAMPLE_KEYS)
    sample = np.where(ok, _t5_bucket_np(dist), -1)
    return prompt.astype(np.int32), sample.astype(np.int32)


def _bias_kernel(rel_ref, sink_ref, bp_ref, bs_ref, op_ref, os_ref):
    layer = pl.program_id(0)
    j = pl.program_id(1)
    sink = sink_ref[layer, j] * LOG2E

    def build(bucket):
        acc = jnp.zeros(bucket.shape, F32)
        for b in range(REL_BUCKETS):
            acc = jnp.where(bucket == b, rel_ref[b, j], acc)
        return jnp.where(bucket < 0, NEG_INF, acc * LOG2E)

    col_p = lax.broadcasted_iota(jnp.int32, (WINDOW, 2 * WINDOW), 1)
    table = build(bp_ref[...])
    op_ref[0, 0, 0] = jnp.where(col_p == 0, sink, table)
    op_ref[0, 1, 0] = jnp.where(col_p == 0, sink, jnp.where(col_p < WINDOW, NEG_INF, table))
    col_s = lax.broadcasted_iota(jnp.int32, (DEC_SEQ, SAMPLE_KEYS_PADDED), 1)
    os_ref[0] = jnp.where(col_s == 0, sink, build(bs_ref[...]))


def _bias_tables(rel_bias, attn_sinks):
    bp, bs = _bucket_tables()
    n_layers = attn_sinks.shape[0]
    return pl.pallas_call(
        _bias_kernel,
        out_shape=(jax.ShapeDtypeStruct((n_layers, 2, N_HEADS, WINDOW, 2 * WINDOW), F32),
                   jax.ShapeDtypeStruct((n_layers, N_HEADS * DEC_SEQ, SAMPLE_KEYS_PADDED), F32)),
        grid=(n_layers, N_HEADS),
        in_specs=[pl.BlockSpec(memory_space=pltpu.SMEM), pl.BlockSpec(memory_space=pltpu.SMEM),
                  pl.BlockSpec((WINDOW, 2 * WINDOW), lambda l, j: (0, 0)),
                  pl.BlockSpec((DEC_SEQ, SAMPLE_KEYS_PADDED), lambda l, j: (j, 0))],
        out_specs=(pl.BlockSpec((1, 2, 1, WINDOW, 2 * WINDOW), lambda l, j: (l, 0, j, 0, 0)),
                   pl.BlockSpec((1, DEC_SEQ, SAMPLE_KEYS_PADDED), lambda l, j: (l, j, 0))),
        compiler_params=_params(SETUP_CALL_VMEM_BYTES, 2),
        name="bias_tables",
    )(rel_bias, attn_sinks, jnp.asarray(bp), jnp.asarray(bs))


def _memkv_kernel(m_ref, g_ref, w_ref, k_ref, v_ref, kb_ref, ve_ref):
    hn = _rms(m_ref[...], g_ref[...]).astype(BF16)
    kv = _dot(hn, w_ref[...].astype(BF16))
    k = kv[:, :MEM_DIM]
    v = kv[:, MEM_DIM:]
    k_ref[...] = k
    v_ref[...] = v
    kb_ref[...] = k.astype(BF16)
    ones = jnp.ones((MEM_TOKENS, MEM_HEAD_DIM), BF16)
    for h in range(MEM_HEADS):
        vh = v[:, h * MEM_HEAD_DIM:(h + 1) * MEM_HEAD_DIM].astype(BF16)
        ve_ref[h] = jnp.concatenate([vh, ones], axis=1)


def _memkv(mem_prompt, mem_norm, w_mem_kv):
    shp = (DEPTH, BATCH, MEM_TOKENS, MEM_DIM)
    blk = pl.BlockSpec((None, None, MEM_TOKENS, MEM_DIM), lambda l, b: (l, b, 0, 0))
    return pl.pallas_call(
        _memkv_kernel,
        out_shape=(jax.ShapeDtypeStruct(shp, F32), jax.ShapeDtypeStruct(shp, F32),
                   jax.ShapeDtypeStruct(shp, BF16),
                   jax.ShapeDtypeStruct((DEPTH, BATCH, MEM_HEADS, MEM_TOKENS, 2 * MEM_HEAD_DIM), BF16)),
        grid=(DEPTH, BATCH),
        in_specs=[pl.BlockSpec((None, MEM_TOKENS, D_MODEL), lambda l, b: (b, 0, 0)),
                  pl.BlockSpec((None, 1, D_MODEL), lambda l, b: (l, 0, 0)),
                  pl.BlockSpec((None, D_MODEL, 2 * MEM_DIM), lambda l, b: (l, 0, 0))],
        out_specs=(blk, blk, blk,
                   pl.BlockSpec((None, None, MEM_HEADS, MEM_TOKENS, 2 * MEM_HEAD_DIM),
                                lambda l, b: (l, b, 0, 0, 0))),
        compiler_params=_params(SETUP_CALL_VMEM_BYTES, 2),
        name="memkv",
    )(mem_prompt, mem_norm.reshape(DEPTH, 1, D_MODEL), w_mem_kv)


def _ffn_kernel(*refs, n_x, attn_proj, shared_kv, layer):
    refs = list(refs)
    x_refs = [refs.pop(0) for _ in range(n_x)]
    g_ref = refs.pop(0)
    w_hbm = [refs.pop(0) for _ in range(3)]
    if attn_proj:
        gm_ref, win_ref = refs.pop(0), refs.pop(0)
    kv_in = [refs.pop(0) for _ in range(3)] if shared_kv else None
    x1_ref = refs.pop(0)
    if attn_proj:
        qa_ref, qm_ref = refs.pop(0), refs.pop(0)
    kv_out = [refs.pop(0) for _ in range(4)] if shared_kv else None
    act_ref = refs.pop(0)
    w = _FfnWeights(layer, w_hbm, refs)

    def body(load_weights):
        x = _pick_group(*x_refs) if n_x == 2 else x_refs[0][...]
        if shared_kv:
            _emit_shared_kv(x, *kv_in, *kv_out)
        x1 = _ffn_half_step(x, g_ref, w, act_ref, load_weights)
        x1_ref[...] = x1
        if attn_proj:
            inv_rms = lax.rsqrt(jnp.mean(x1 * x1, axis=-1, keepdims=True) + RMS_EPS)
            hm = (x1 * gm_ref[...]).astype(BF16)
            qa_ref[...] = (_dot(hm, win_ref[:, :ATTN_DIM].astype(BF16)) * (inv_rms * ATTN_Q_SCALE)).astype(BF16)
            qm_ref[...] = (_dot(hm, win_ref[:, ATTN_DIM:].astype(BF16)) * (inv_rms * MEM_Q_SCALE)).astype(BF16)

    _first_step_loads(w, body)


def _ffn(x, g, ffn_w, layer, gm=None, win=None, win_layer=None, shared_kv_w=None):
    attn_proj = win is not None
    shared_kv = shared_kv_w is not None
    xs = list(x) if isinstance(x, tuple) else [x]
    in_specs = ((_group_specs(D_MODEL) if len(xs) == 2 else [_row_spec(D_MODEL)])
                + [_resident((1, D_MODEL))] + _FfnWeights.in_specs())
    args = xs + [g.reshape(1, D_MODEL)] + list(ffn_w)
    out_shape = [jax.ShapeDtypeStruct((ROWS, D_MODEL), F32)]
    out_specs = [_row_spec(D_MODEL)]
    resident = 0
    streamed = (1 + len(xs)) * _ROW_TILE_F32
    if attn_proj:
        in_specs += [_resident((1, D_MODEL)), _resident_layer((D_MODEL, ATTN_DIM + MEM_DIM), win_layer)]
        args += [gm.reshape(1, D_MODEL), win]
        out_shape += [jax.ShapeDtypeStruct((ROWS, ATTN_DIM), BF16), jax.ShapeDtypeStruct((ROWS, MEM_DIM), BF16)]
        out_specs += [_row_spec(ATTN_DIM), _row_spec(MEM_DIM)]
        resident += D_MODEL * (ATTN_DIM + MEM_DIM) * 4
        streamed += TILE_ROWS * (ATTN_DIM + MEM_DIM) * 2
    if shared_kv:
        kv_in_specs, kv_shape, kv_specs, kv_resident, kv_streamed = _shared_kv_specs()
        in_specs += kv_in_specs
        args += [shared_kv_w[0].reshape(1, D_MODEL)] + list(shared_kv_w[1:])
        out_shape += kv_shape
        out_specs += kv_specs
        resident += kv_resident
        streamed += kv_streamed
    return pl.pallas_call(
        functools.partial(_ffn_kernel, n_x=len(xs), attn_proj=attn_proj, shared_kv=shared_kv, layer=layer),
        out_shape=tuple(out_shape),
        grid=(ROW_TILES,),
        in_specs=in_specs,
        out_specs=tuple(out_specs),
        scratch_shapes=[pltpu.VMEM((TILE_ROWS, FFN_DIM), BF16)] + _FfnWeights.scratch_shapes(),
        compiler_params=_params(_vmem_limit(
            resident, streamed, _ACT_BYTES + _FfnWeights.SCRATCH_BYTES, 3 * _ROW_TILE_F32)),
        name="ffn_attn_proj" if attn_proj else "ffn",
    )(*args)


def _inproj_conv_kernel(x_ref, gm_ref, win_ref, cw_ref, pre_ref,
                        ytok_ref, qm_ref, tail_ref, us_ref, shift_ref):
    i = pl.program_id(0)
    hm = _rms(x_ref[...], gm_ref[...]).astype(BF16)
    qm_ref[...] = (_dot(hm, win_ref[:, 3 * CONV_DIM:].astype(BF16)) * MEM_Q_SCALE).astype(BF16)

    def chunk(cc, prompt):
        sl = slice(cc * CONV_CHUNK, (cc + 1) * CONV_CHUNK)
        c_gate = _dot(hm, win_ref[:, CONV_DIM + cc * CONV_CHUNK:CONV_DIM + (cc + 1) * CONV_CHUNK].astype(BF16))
        x_in = _dot(hm, win_ref[:, 2 * CONV_DIM + cc * CONV_CHUNK:2 * CONV_DIM + (cc + 1) * CONV_CHUNK].astype(BF16))
        u = c_gate * x_in
        if prompt:
            shift_ref[V7X_SUBLANES:, sl] = u
            u1 = shift_ref[V7X_SUBLANES - 1:V7X_SUBLANES - 1 + TILE_ROWS, sl]
            u2 = shift_ref[V7X_SUBLANES - 2:V7X_SUBLANES - 2 + TILE_ROWS, sl]
            last = u[TILE_ROWS - V7X_SUBLANES:, :]
            shift_ref[:V7X_SUBLANES, sl] = last
            tail_ref[0, :, sl] = last
        else:
            t = lax.broadcasted_iota(jnp.int32, (TILE_ROWS, CONV_CHUNK), 0) % DEC_SEQ
            p2 = pre_ref[:, sl]
            p1 = pltpu.roll(p2, TILE_ROWS - 1, axis=0)
            u1 = jnp.where(t == 0, p1, pltpu.roll(u, 1, axis=0))
            u2 = jnp.where(t < 2, p2, pltpu.roll(u, 2, axis=0))
            us_ref[:, sl] = u
            tail_ref[0, :, sl] = jnp.zeros((V7X_SUBLANES, CONV_CHUNK), F32)
        w = cw_ref[:, sl]
        conv = w[0:1] * u2 + w[1:2] * u1 + w[2:3] * u
        b_gate = _dot(hm, win_ref[:, sl].astype(BF16))
        ytok_ref[:, sl] = (b_gate * conv).astype(BF16)

    @pl.when(i < PROMPT_TILES)
    def _():
        @pl.when(i % TILES_PER_SEQ == 0)
        def _():
            shift_ref[:V7X_SUBLANES, :] = jnp.zeros((V7X_SUBLANES, CONV_DIM), F32)
        for cc in range(CONV_DIM // CONV_CHUNK):
            chunk(cc, True)

    @pl.when(i >= PROMPT_TILES)
    def _():
        for cc in range(CONV_DIM // CONV_CHUNK):
            chunk(cc, False)


def _inproj_conv(x1, gm, win, conv_w, prefix_rows, layer):
    sample_idx = lambda i: (jnp.maximum(i - PROMPT_TILES, 0), 0)
    win_bytes = D_MODEL * (3 * CONV_DIM + MEM_DIM) * 4
    return pl.pallas_call(
        _inproj_conv_kernel,
        out_shape=(jax.ShapeDtypeStruct((ROWS, CONV_DIM), BF16),
                   jax.ShapeDtypeStruct((ROWS, MEM_DIM), BF16),
                   jax.ShapeDtypeStruct((ROW_TILES, V7X_SUBLANES, CONV_DIM), F32),
                   jax.ShapeDtypeStruct((ROWS_SAMPLE, CONV_DIM), F32)),
        grid=(ROW_TILES,),
        in_specs=[_row_spec(D_MODEL), _resident((1, D_MODEL)),
                  _resident_layer((D_MODEL, 3 * CONV_DIM + MEM_DIM), layer),
                  _resident_layer((CONV_WIDTH, CONV_DIM), layer),
                  pl.BlockSpec((TILE_ROWS, CONV_DIM), sample_idx)],
        out_specs=(_row_spec(CONV_DIM), _row_spec(MEM_DIM),
                   pl.BlockSpec((1, V7X_SUBLANES, CONV_DIM), lambda i: (i, 0, 0)),
                   pl.BlockSpec((TILE_ROWS, CONV_DIM), sample_idx)),
        scratch_shapes=[pltpu.VMEM((TILE_ROWS + V7X_SUBLANES, CONV_DIM), F32)],
        compiler_params=_params(_vmem_limit(
            win_bytes, 3 * _ROW_TILE_F32 + TILE_ROWS * (CONV_DIM + MEM_DIM) * 2,
            _ROW_TILE_F32 + V7X_SUBLANES * CONV_DIM * 4, 6 * _ROW_TILE_F32)),
        name="inproj_conv",
    )(x1, gm.reshape(1, D_MODEL), win, conv_w, prefix_rows)


VZ_WIDTH = N_KV_HEADS * 2 * 2 * V7X_LANES


def _emit_shared_kv(x, g_ref, wk_ref, wv_ref, k_ref, v_ref, ktz_ref, vz_ref):
    hk = _rms(x, g_ref[...]).astype(BF16)
    k = _dot(hk, wk_ref[...].astype(BF16))
    k_ref[...] = k
    v = _dot(hk, wv_ref[...].astype(BF16))
    v_ref[...] = v
    eye = (lax.broadcasted_iota(jnp.int32, (KV_DIM, KV_DIM), 0)
           == lax.broadcasted_iota(jnp.int32, (KV_DIM, KV_DIM), 1)).astype(BF16)
    kt = _dot_nt(eye, k.astype(BF16))
    lo = lax.broadcasted_iota(jnp.int32, (TILE_ROWS, V7X_LANES), 1) < HALF_LANES
    hi = jnp.logical_not(lo)
    ones = (jnp.where(lo, 1.0, 0.0).astype(BF16), jnp.where(lo, 0.0, 1.0).astype(BF16))
    zero_k = jnp.zeros((HEAD_DIM, TILE_ROWS), BF16)
    for pair in range(N_KV_HEADS // 2):
        v_pair = v[:, pair * V7X_LANES:(pair + 1) * V7X_LANES]
        v_swap = pltpu.roll(v_pair, HALF_LANES, axis=1)
        for odd in range(2):
            h = 2 * pair + odd
            kth = kt[h * HEAD_DIM:(h + 1) * HEAD_DIM, :].astype(BF16)
            ktz_ref[h, 0] = jnp.concatenate([kth, zero_k], axis=0)
            ktz_ref[h, 1] = jnp.concatenate([zero_k, kth], axis=0)
            halves = (jnp.where(lo, v_swap if odd else v_pair, 0.0), jnp.where(hi, v_pair if odd else v_swap, 0.0))
            for e in range(2):
                c0 = (h * 2 + e) * 2 * V7X_LANES
                vz_ref[:, c0:c0 + 2 * V7X_LANES] = jnp.concatenate(
                    [halves[e].astype(BF16), ones[e]], axis=1)


def _shared_kv_specs():
    in_specs = [_resident((1, D_MODEL)), _resident((D_MODEL, KV_DIM)), _resident((D_MODEL, KV_DIM))]
    out_shape = [jax.ShapeDtypeStruct((ROWS, KV_DIM), F32), jax.ShapeDtypeStruct((ROWS, KV_DIM), F32),
                 jax.ShapeDtypeStruct((N_KV_HEADS, 2, V7X_LANES, ROWS), BF16),
                 jax.ShapeDtypeStruct((ROWS, VZ_WIDTH), BF16)]
    out_specs = [_row_spec(KV_DIM), _row_spec(KV_DIM),
                 pl.BlockSpec((N_KV_HEADS, 2, V7X_LANES, TILE_ROWS), lambda i: (0, 0, 0, i)),
                 _row_spec(VZ_WIDTH)]
    resident = 2 * D_MODEL * KV_DIM * 4
    streamed = TILE_ROWS * (2 * KV_DIM * 4 + 2 * KV_DIM * 2 + VZ_WIDTH * 2)
    return in_specs, out_shape, out_specs, resident, streamed


def _attn_prompt_kernel(*refs, swa):
    if swa:
        (qm_ref, mk_ref, mve_ref, qa_ref, ktp_ref, ktc_ref, vzp_ref, vzc_ref, bias_ref,
         ymem_ref, ytok_ref) = refs
    else:
        qm_ref, mk_ref, mve_ref, ymem_ref = refs

    head_cols = [slice(h * MEM_HEAD_DIM, (h + 1) * MEM_HEAD_DIM) for h in range(MEM_HEADS)]
    for sub in range(ATTN_TILE_ROWS // TILE_ROWS):
        sub_rows = slice(sub * TILE_ROWS, (sub + 1) * TILE_ROWS)
        scores = [_dot_nt(qm_ref[sub_rows, sl], mk_ref[:, sl]) for sl in head_cols]
        probs = [jnp.exp2(s - jnp.max(s, axis=-1, keepdims=True)).astype(BF16) for s in scores]
        for h, sl in enumerate(head_cols):
            oe = _dot(probs[h], mve_ref[h])
            ymem_ref[sub_rows, sl] = (oe[:, :MEM_HEAD_DIM] / oe[:, MEM_HEAD_DIM:]).astype(BF16)
    if not swa:
        return

    no_prev = ((pl.program_id(0) % ATTN_TILES_PER_SEQ) == 0).astype(jnp.int32)
    key0_col = lax.broadcasted_iota(jnp.int32, (V7X_LANES, WINDOW), 1) == 0
    key0_row = lax.broadcasted_iota(jnp.int32, (WINDOW, V7X_LANES), 0) == 0
    zero_kt = jnp.zeros((V7X_LANES, WINDOW), BF16)
    zero_v = jnp.zeros((WINDOW, V7X_LANES), BF16)

    def rows_of(n):
        return slice(n * WINDOW, (n + 1) * WINDOW)

    @functools.lru_cache(maxsize=None)
    def key_window(n, h, e):
        k_prev = ktp_ref[h, e] if n == 0 else ktc_ref[h, e, :, rows_of(n - 1)]
        k_prev = jnp.where(key0_col, zero_kt, k_prev)
        return jnp.concatenate([k_prev, ktc_ref[h, e, :, rows_of(n)]], axis=1)

    @functools.lru_cache(maxsize=None)
    def value_window(n, h, e):
        c0 = (h * 2 + e) * 2 * V7X_LANES
        cols = slice(c0, c0 + 2 * V7X_LANES)
        v_prev = vzp_ref[:, cols] if n == 0 else vzc_ref[rows_of(n - 1), cols]
        v_prev = jnp.concatenate(
            [jnp.where(key0_row, zero_v, v_prev[:, :V7X_LANES]), v_prev[:, V7X_LANES:]], axis=1)
        return jnp.concatenate([v_prev, vzc_ref[rows_of(n), cols]], axis=0)

    def slab_col(h, pr):
        return (h * (GROUP // 2) + pr) * V7X_LANES

    def head_scores(n, h, pr, e):
        j = h * GROUP + pr * 2 + e
        bias = bias_ref[no_prev, j] if n == 0 else bias_ref[0, j]
        c0 = slab_col(h, pr)
        return _dot(qa_ref[rows_of(n), c0:c0 + V7X_LANES], key_window(n, h, e)) + bias

    heads = [(n, h, pr, e) for n in range(ATTN_BLOCKS_PER_TILE) for h in range(N_KV_HEADS)
             for pr in range(GROUP // 2) for e in range(2)]
    s_next = head_scores(*heads[0])
    acc = None
    for idx, (n, h, pr, e) in enumerate(heads):
        s = s_next
        if idx + 1 < len(heads):
            s_next = head_scores(*heads[idx + 1])
        p = jnp.exp2(s - jnp.max(s, axis=-1, keepdims=True)).astype(BF16)
        part = _dot(p, value_window(n, h, e))
        if e == 0:
            acc = part
        else:
            acc = acc + part
            c0 = slab_col(h, pr)
            ytok_ref[rows_of(n), c0:c0 + V7X_LANES] = (acc[:, :V7X_LANES] / acc[:, V7X_LANES:]).astype(BF16)


def _attn_prompt(qm, mk_bf, mv_ext, layer, swa_args=None):
    swa = swa_args is not None
    batch_of = lambda i: i // ATTN_TILES_PER_SEQ
    row_spec = lambda width: pl.BlockSpec((ATTN_TILE_ROWS, width), lambda i: (i, 0))
    in_specs = [row_spec(MEM_DIM),
                pl.BlockSpec((None, None, MEM_TOKENS, MEM_DIM), lambda i: (layer, batch_of(i), 0, 0)),
                pl.BlockSpec((None, None, MEM_HEADS, MEM_TOKENS, 2 * MEM_HEAD_DIM),
                             lambda i: (layer, batch_of(i), 0, 0, 0))]
    args = [qm, mk_bf, mv_ext]
    out_shape = [jax.ShapeDtypeStruct((ROWS_PROMPT, MEM_DIM), BF16)]
    out_specs = [row_spec(MEM_DIM)]
    streamed = ATTN_TILE_ROWS * MEM_DIM * 4 + MEM_TOKENS * MEM_DIM * 2 * 3
    resident = 0
    if swa:
        qa, ktz, vz, bias_p, swa_layer = swa_args
        prev_blk = lambda i: jnp.maximum(i * ATTN_BLOCKS_PER_TILE - 1, 0)
        in_specs += [row_spec(ATTN_DIM),
                     pl.BlockSpec((N_KV_HEADS, 2, V7X_LANES, WINDOW), lambda i: (0, 0, 0, prev_blk(i))),
                     pl.BlockSpec((N_KV_HEADS, 2, V7X_LANES, ATTN_TILE_ROWS), lambda i: (0, 0, 0, i)),
                     pl.BlockSpec((WINDOW, VZ_WIDTH), lambda i: (prev_blk(i), 0)),
                     pl.BlockSpec((ATTN_TILE_ROWS, VZ_WIDTH), lambda i: (i, 0)),
                     _resident_layer((2, N_HEADS, WINDOW, 2 * WINDOW), swa_layer)]
        args += [qa, ktz, ktz, vz, vz, bias_p]
        out_shape += [jax.ShapeDtypeStruct((ROWS_PROMPT, ATTN_DIM), BF16)]
        out_specs += [row_spec(ATTN_DIM)]
        resident = 2 * N_HEADS * WINDOW * 2 * WINDOW * 4
        streamed += 2 * ATTN_TILE_ROWS * ATTN_DIM * 2 + (ATTN_TILE_ROWS + WINDOW) * (2 * KV_DIM + VZ_WIDTH) * 2
    return pl.pallas_call(
        functools.partial(_attn_prompt_kernel, swa=swa),
        out_shape=tuple(out_shape),
        grid=(ATTN_PROMPT_TILES,),
        in_specs=in_specs,
        out_specs=tuple(out_specs),
        compiler_params=_params(_vmem_limit(resident, streamed, 0, 8 * _ROW_TILE_F32)),
        name="attn_prompt_swa" if swa else "attn_prompt_mem",
    )(*args)


def _attn_sample_kernel(*refs, swa, emit_cache):
    if swa:
        (qm_ref, mk_ref, mv_ref, qa_ref, ck_ref, cv_ref, kn_ref, vn_ref, bias_ref,
         ymem_ref, ytok_ref) = refs[:11]
        if emit_cache:
            kc_out_ref, vc_out_ref = refs[11:]
            keep_cached = lax.broadcasted_iota(jnp.int32, (KV_DIM, WINDOW), 1) < WINDOW - DEC_SEQ
    else:
        qm_ref, mk_ref, mv_ref, ymem_ref = refs

    qm_all = qm_ref[...].astype(F32)
    mem_rows = MEM_HEADS * DEC_SEQ
    own_head = (lax.broadcasted_iota(jnp.int32, (mem_rows, MEM_TOKENS * MEM_HEADS), 1) % MEM_HEADS
                == lax.broadcasted_iota(jnp.int32, (mem_rows, MEM_TOKENS * MEM_HEADS), 0) // DEC_SEQ)
    if swa:
        qa_all = qa_ref[...].astype(F32)
        kn_all = kn_ref[...]
        vn_all = vn_ref[...]
        bias = bias_ref[...]
        key0 = lax.broadcasted_iota(jnp.int32, (KV_DIM, WINDOW), 1) == 0
        pad = jnp.zeros((WINDOW - DEC_SEQ, KV_DIM), F32)
        lo = lax.broadcasted_iota(jnp.int32, (DEC_SEQ, V7X_LANES), 1) < HALF_LANES
        hi = jnp.logical_not(lo)
        zero_slab = jnp.zeros((DEC_SEQ, V7X_LANES), F32)

    batch = range(SAMPLE_BATCH_BLOCK)
    rows_of = [slice(b * DEC_SEQ, (b + 1) * DEC_SEQ) for b in batch]
    q_heads = [jnp.concatenate([qm_all[rows_of[b], h * MEM_HEAD_DIM:(h + 1) * MEM_HEAD_DIM]
                                for h in range(MEM_HEADS)], axis=0) for b in batch]
    scores = [jnp.where(own_head, _dot_nt(q_heads[b], mk_ref[b]), NEG_INF) for b in batch]
    probs = [jnp.exp2(s - jnp.max(s, axis=-1, keepdims=True)) for s in scores]
    outs = [_dot(probs[b], mv_ref[b]) / jnp.sum(probs[b], axis=-1, keepdims=True) for b in batch]
    ymem_rows = [jnp.concatenate([o[h * DEC_SEQ:(h + 1) * DEC_SEQ] for h in range(MEM_HEADS)], axis=1)
                 for o in outs]

    if swa:
        def block_diag_queries(qb):
            groups = []
            for j in range(N_HEADS):
                kvh = j // GROUP
                slab = qb[:, (j // 2) * V7X_LANES:(j // 2 + 1) * V7X_LANES]
                if j % 2 != kvh % 2:
                    slab = pltpu.roll(slab, HALF_LANES, axis=1)
                slab = jnp.where(lo if kvh % 2 == 0 else hi, slab, 0.0)
                groups.append(jnp.concatenate([slab, zero_slab] if kvh // 2 == 0 else [zero_slab, slab], axis=1))
            return jnp.concatenate(groups, axis=0)

        def gather_heads(o_full):
            pairs = []
            for pair in range(N_HEADS // 2):
                acc = None
                for e in range(2):
                    j = pair * 2 + e
                    kvh = j // GROUP
                    slab = o_full[j * DEC_SEQ:(j + 1) * DEC_SEQ, (kvh // 2) * V7X_LANES:(kvh // 2 + 1) * V7X_LANES]
                    if e != kvh % 2:
                        slab = pltpu.roll(slab, HALF_LANES, axis=1)
                    slab = jnp.where(lo if e == 0 else hi, slab, 0.0)
                    acc = slab if acc is None else acc + slab
                pairs.append(acc)
            return jnp.concatenate(pairs, axis=1)

        q_bd = [block_diag_queries(qa_all[rows_of[b]]) for b in batch]
        kt_cache = [jnp.where(key0, 0.0, ck_ref[b]) for b in batch]
        vt_cache = [jnp.where(key0, 0.0, cv_ref[b]) for b in batch]
        k_new = [jnp.concatenate([kn_all[rows_of[b]], pad], axis=0) for b in batch]
        v_new = [jnp.concatenate([vn_all[rows_of[b]], pad], axis=0) for b in batch]
        if emit_cache:
            shift = WINDOW - DEC_SEQ
            block_pad = jnp.zeros((WINDOW - SAMPLE_BLOCK_ROWS, KV_DIM), F32)
            k_new_t = jnp.concatenate([kn_all, block_pad], axis=0).T
            v_new_t = jnp.concatenate([vn_all, block_pad], axis=0).T
            for b in batch:
                kc_out_ref[b] = jnp.where(keep_cached, pltpu.roll(ck_ref[b], shift, axis=1),
                                          pltpu.roll(k_new_t, shift - b * DEC_SEQ, axis=1))
                vc_out_ref[b] = jnp.where(keep_cached, pltpu.roll(cv_ref[b], shift, axis=1),
                                          pltpu.roll(v_new_t, shift - b * DEC_SEQ, axis=1))
        scores = [jnp.concatenate([_dot(q_bd[b], kt_cache[b]), _dot_nt(q_bd[b], k_new[b])], axis=1) + bias
                  for b in batch]
        probs = [jnp.exp2(s - jnp.max(s, axis=-1, keepdims=True)) for s in scores]
        o_full = [(_dot_nt(probs[b][:, :WINDOW], vt_cache[b]) + _dot(probs[b][:, WINDOW:], v_new[b]))
                  / jnp.sum(probs[b], axis=-1, keepdims=True) for b in batch]
        ytok_rows = [gather_heads(o) for o in o_full]

    ymem_ref[...] = jnp.concatenate(ymem_rows, axis=0).astype(BF16)
    if swa:
        ytok_ref[...] = jnp.concatenate(ytok_rows, axis=0).astype(BF16)


def _attn_sample(qm, cache_k, cache_v, layer, swa_args=None, emit_cache=False):
    swa = swa_args is not None
    assert swa or not emit_cache
    row0 = ROWS_PROMPT // SAMPLE_BLOCK_ROWS
    blk_rows = lambda width: pl.BlockSpec((SAMPLE_BLOCK_ROWS, width), lambda i: (row0 + i, 0))
    out_rows = lambda width: pl.BlockSpec((SAMPLE_BLOCK_ROWS, width), lambda i: (i, 0))
    cache_spec = pl.BlockSpec((None, SAMPLE_BATCH_BLOCK, MEM_TOKENS * MEM_HEADS, MEM_HEAD_DIM),
                              lambda i: (layer, i, 0, 0))
    in_specs = [blk_rows(MEM_DIM), cache_spec, cache_spec]
    args = [qm, cache_k, cache_v]
    out_shape = [jax.ShapeDtypeStruct((ROWS_SAMPLE, MEM_DIM), BF16)]
    out_specs = [out_rows(MEM_DIM)]
    streamed = 2 * SAMPLE_BATCH_BLOCK * MEM_TOKENS * MEM_DIM * 4 + SAMPLE_BLOCK_ROWS * MEM_DIM * 4
    if swa:
        qa, swa_k, swa_v, k_new, v_new, bias_s, swa_layer = swa_args
        swa_spec = pl.BlockSpec((SAMPLE_BATCH_BLOCK, KV_DIM, WINDOW), lambda i: (i, 0, 0))
        in_specs += [blk_rows(ATTN_DIM), swa_spec, swa_spec, out_rows(KV_DIM), out_rows(KV_DIM),
                     _resident_layer((N_HEADS * DEC_SEQ, SAMPLE_KEYS_PADDED), swa_layer)]
        args += [qa, swa_k, swa_v, k_new, v_new, bias_s]
        out_shape += [jax.ShapeDtypeStruct((ROWS_SAMPLE, ATTN_DIM), BF16)]
        out_specs += [out_rows(ATTN_DIM)]
        streamed += 2 * SAMPLE_BATCH_BLOCK * WINDOW * KV_DIM * 4 + SAMPLE_BLOCK_ROWS * (ATTN_DIM + KV_DIM) * 4
        if emit_cache:
            out_shape += [jax.ShapeDtypeStruct((DEC_BATCH, KV_DIM, WINDOW), F32)] * 2
            out_specs += [swa_spec, swa_spec]
            streamed += 2 * SAMPLE_BATCH_BLOCK * WINDOW * KV_DIM * 4
    return pl.pallas_call(
        functools.partial(_attn_sample_kernel, swa=swa, emit_cache=emit_cache),
        out_shape=tuple(out_shape),
        grid=(DEC_BATCH // SAMPLE_BATCH_BLOCK,),
        in_specs=in_specs,
        out_specs=tuple(out_specs),
        compiler_params=_params(_vmem_limit(1 << 20, streamed, 0, 6 * _ROW_TILE_F32)),
        name="attn_sample_swa" if swa else "attn_sample_mem",
    )(*args)


def _outffn_kernel(*refs, split_tok, final, layer):
    refs = list(refs)
    x1_ref = refs.pop(0)
    tok_refs = [refs.pop(0) for _ in range(2 if split_tok else 1)]
    ymp_ref, yms_ref, wo_ref, g_ref = [refs.pop(0) for _ in range(4)]
    w_hbm = [refs.pop(0) for _ in range(3)]
    gf_ref = refs.pop(0) if final else None
    out_refs = [refs.pop(0) for _ in range(2 if final else 1)]
    act_ref = refs.pop(0)
    w = _FfnWeights(layer, w_hbm, refs)
    tok_dim = wo_ref.shape[0] - MEM_DIM

    def body(load_weights):
        y_tok = _pick_group(*tok_refs) if split_tok else tok_refs[0][...]
        y_mem = _pick_group(ymp_ref, yms_ref)
        x2 = (x1_ref[...] + _dot(y_tok, wo_ref[:tok_dim, :].astype(BF16))
              + _dot(y_mem, wo_ref[tok_dim:, :].astype(BF16)))
        x3 = _ffn_half_step(x2, g_ref, w, act_ref, load_weights)
        if not final:
            out_refs[0][...] = x3
            return
        y = _rms(x3, gf_ref[...])

        @pl.when(pl.program_id(0) < PROMPT_TILES)
        def _():
            out_refs[0][...] = y

        @pl.when(pl.program_id(0) >= PROMPT_TILES)
        def _():
            out_refs[1][...] = y

    _first_step_loads(w, body)


def _outffn(x1, y_tok, ymem_p, ymem_s, wo, wo_layer, g, ffn_w, layer, final_gain=None):
    split_tok = isinstance(y_tok, tuple)
    final = final_gain is not None
    tok_dim = wo.shape[1] - MEM_DIM
    in_specs = [_row_spec(D_MODEL)]
    args = [x1]
    if split_tok:
        in_specs += _group_specs(tok_dim)
        args += list(y_tok)
    else:
        in_specs += [_row_spec(tok_dim)]
        args += [y_tok]
    in_specs += (_group_specs(MEM_DIM) + [_resident_layer((tok_dim + MEM_DIM, D_MODEL), wo_layer)]
                 + [_resident((1, D_MODEL))] + _FfnWeights.in_specs())
    args += [ymem_p, ymem_s, wo, g.reshape(1, D_MODEL)] + list(ffn_w)
    if final:
        in_specs += [_resident((1, D_MODEL))]
        args += [final_gain.reshape(1, D_MODEL)]
    resident = (tok_dim + MEM_DIM) * D_MODEL * 4
    streamed = 2 * _ROW_TILE_F32 + 2 * TILE_ROWS * (tok_dim + MEM_DIM) * 2
    if final:
        out_shape = [jax.ShapeDtypeStruct((ROWS_PROMPT, D_MODEL), F32),
                     jax.ShapeDtypeStruct((ROWS_SAMPLE, D_MODEL), F32)]
        out_specs = _group_specs(D_MODEL)
        streamed += _ROW_TILE_F32
    else:
        out_shape = [jax.ShapeDtypeStruct((ROWS, D_MODEL), F32)]
        out_specs = [_row_spec(D_MODEL)]
    return pl.pallas_call(
        functools.partial(_outffn_kernel, split_tok=split_tok, final=final, layer=layer),
        out_shape=tuple(out_shape),
        grid=(ROW_TILES,),
        in_specs=in_specs,
        out_specs=tuple(out_specs),
        scratch_shapes=[pltpu.VMEM((TILE_ROWS, FFN_DIM), BF16)] + _FfnWeights.scratch_shapes(),
        compiler_params=_params(_vmem_limit(
            resident, streamed, _ACT_BYTES + _FfnWeights.SCRATCH_BYTES, 3 * _ROW_TILE_F32)),
        name="outffn_final" if final else "outffn",
    )(*args)


def kernel(x_prompt, x_sample, state_conv, cache_swa_k, cache_swa_v, cache_mem_k, cache_mem_v, mem_prompt, ffn1_norm, ffn1_wg, ffn1_wu, ffn1_wd, mix_norm, w_in_a, conv_w, w_out_a, kv_norm, w_kv, w_in_b, attn_sinks, rel_bias, w_out_b, mem_norm, w_mem_kv, ffn2_norm, ffn2_wg, ffn2_wu, ffn2_wd, final_norm):
    ffn1 = (ffn1_wg, ffn1_wu, ffn1_wd)
    ffn2 = (ffn2_wg, ffn2_wu, ffn2_wd)
    wk = w_kv[:, :KV_DIM]
    wv = w_kv[:, KV_DIM:]

    mem_k, mem_v, mem_k_bf, mem_v_ext = _memkv(mem_prompt, mem_norm, w_mem_kv)
    bias_p, bias_s = _bias_tables(rel_bias, attn_sinks)
    swa_k_cache = cache_swa_k.transpose(0, 2, 3, 1).reshape(DEC_BATCH, KV_DIM, WINDOW)
    swa_v_cache = cache_swa_v.transpose(0, 2, 3, 1).reshape(DEC_BATCH, KV_DIM, WINDOW)
    mem_rows_shape = (DEPTH, DEC_BATCH, MEM_TOKENS * MEM_HEADS, MEM_HEAD_DIM)
    cache_k = cache_mem_k.reshape(mem_rows_shape)
    cache_v = cache_mem_v.reshape(mem_rows_shape)

    x = (x_prompt.reshape(ROWS_PROMPT, D_MODEL), x_sample.reshape(ROWS_SAMPLE, D_MODEL))
    tails, sample_us = [], []
    k_rows = v_rows = ktz = vz = k_new = v_new = None
    for l in range(DEPTH):
        last = l == DEPTH - 1
        if l < N_A_LAYERS:
            (x1,) = _ffn(x, ffn1_norm[l], ffn1, l)
            prefix_rows = jnp.pad(state_conv[l], ((0, 0), (0, DEC_SEQ - (CONV_WIDTH - 1)), (0, 0)))
            y_tok, qm, tail, us = _inproj_conv(x1, mix_norm[l], w_in_a, conv_w,
                                               prefix_rows.reshape(ROWS_SAMPLE, CONV_DIM), l)
            tails.append(tail)
            sample_us.append(us)
            (ymem_p,) = _attn_prompt(qm, mem_k_bf, mem_v_ext, l)
            (ymem_s,) = _attn_sample(qm, cache_k, cache_v, l)
            wo, wo_layer = w_out_a, l
        else:
            j = l - N_A_LAYERS
            if j == 0:
                x1, qa, qm, k_rows, v_rows, ktz, vz = _ffn(
                    x, ffn1_norm[l], ffn1, l, mix_norm[l], w_in_b, j,
                    shared_kv_w=(kv_norm, wk, wv))
                k_new = k_rows[ROWS_PROMPT:]
                v_new = v_rows[ROWS_PROMPT:]
            else:
                x1, qa, qm = _ffn(x, ffn1_norm[l], ffn1, l, mix_norm[l], w_in_b, j)
            ymem_p, ytok_p = _attn_prompt(qm, mem_k_bf, mem_v_ext, l, (qa, ktz, vz, bias_p, j))
            sample_out = _attn_sample(qm, cache_k, cache_v, l,
                                      (qa, swa_k_cache, swa_v_cache, k_new, v_new, bias_s, j), emit_cache=j == 0)
            ymem_s, ytok_s = sample_out[:2]
            if j == 0:
                new_k_cache, new_v_cache = sample_out[2:]
            y_tok = (ytok_p, ytok_s)
            wo, wo_layer = w_out_b, j
        if not last:
            (x,) = _outffn(x1, y_tok, ymem_p, ymem_s, wo, wo_layer, ffn2_norm[l], ffn2, l)
        else:
            y_prompt, y_sample = _outffn(x1, y_tok, ymem_p, ymem_s, wo, wo_layer, ffn2_norm[l], ffn2, l,
                                         final_gain=final_norm)

    keep = CONV_WIDTH - 1
    last_tiles = np.arange(BATCH) * TILES_PER_SEQ + TILES_PER_SEQ - 1
    conv_state_prompt = jnp.stack([t[last_tiles, V7X_SUBLANES - keep:, :] for t in tails])
    conv_state_sample = jnp.stack([u.reshape(DEC_BATCH, DEC_SEQ, CONV_DIM)[:, DEC_SEQ - keep:, :] for u in sample_us])
    k_tail = jnp.stack([k_rows[(b + 1) * SEQ - WINDOW:(b + 1) * SEQ] for b in range(BATCH)])
    v_tail = jnp.stack([v_rows[(b + 1) * SEQ - WINDOW:(b + 1) * SEQ] for b in range(BATCH)])
    swa_k_prompt = k_tail.reshape(BATCH, WINDOW, N_KV_HEADS, HEAD_DIM)
    swa_v_prompt = v_tail.reshape(BATCH, WINDOW, N_KV_HEADS, HEAD_DIM)
    to_cache_layout = lambda c: c.reshape(DEC_BATCH, N_KV_HEADS, HEAD_DIM, WINDOW).transpose(0, 3, 1, 2)
    swa_k_sample = to_cache_layout(new_k_cache)
    swa_v_sample = to_cache_layout(new_v_cache)
    mem_shape = (DEPTH, BATCH, MEM_TOKENS, MEM_HEADS, MEM_HEAD_DIM)
    return (y_prompt.reshape(BATCH, SEQ, D_MODEL), y_sample.reshape(DEC_BATCH, DEC_SEQ, D_MODEL),
            conv_state_prompt, conv_state_sample,
            swa_k_prompt, swa_v_prompt, swa_k_sample, swa_v_sample,
            mem_k.reshape(mem_shape), mem_v.reshape(mem_shape))
```
